```python
import math
import jax, jax.numpy as jnp
from jax import lax
import numpy as np

D_MODEL = 4096
BATCH = 1
SEQ = 8192
DEPTH = 1

HEAD_DIM = 128
MIX_WIDTH = D_MODEL
MIX_HEADS = MIX_WIDTH // HEAD_DIM
A_Q_HEADS = MIX_HEADS // 2
A_KV_HEADS = 4
A_WIDTH = A_Q_HEADS * HEAD_DIM
A_WINDOW = 128
B_Q_HEADS = MIX_HEADS - A_Q_HEADS
B_KV_HEADS = 4
B_WIDTH = B_Q_HEADS * HEAD_DIM
CMP_BLOCK = 32
CMP_STRIDE = 16
CMP_HIDDEN = 256
SLC_BLOCK = 64
SLC_TOPK = 16
B_WINDOW = 512
N_BRANCH = 3
QBLOCK = 128
N_BUCKETS = 32
MAX_DISTANCE = 128
D_FF = 11008
CONV_WIDTH = 3
EPS = 1e-6
NEG = -1e30
FORCE_SCORE = 1e6
SCALE = HEAD_DIM ** -0.5
IN_SIZES = [A_WIDTH, A_KV_HEADS * HEAD_DIM, A_KV_HEADS * HEAD_DIM,
            B_WIDTH] + [B_KV_HEADS * HEAD_DIM] * 6 + [N_BRANCH * B_Q_HEADS]
N_IN = A_WIDTH + 2 * A_KV_HEADS * HEAD_DIM + B_WIDTH + 6 * B_KV_HEADS * HEAD_DIM + N_BRANCH * B_Q_HEADS

kernel_name = "hymba_style_swa_sinks_nsa_convffn"


def rms_norm(x, g):
    xf = x.astype(jnp.float32)
    y = xf * lax.rsqrt(jnp.mean(xf * xf, axis=-1, keepdims=True) + EPS)
    return (y * g.astype(jnp.float32)).astype(x.dtype)


def t5_bucket(dist):
    n = jnp.maximum(dist, 0)
    max_exact = N_BUCKETS // 2
    nf = jnp.maximum(n, 1).astype(jnp.float32)
    large = max_exact + (jnp.log(nf / max_exact) / math.log(MAX_DISTANCE / max_exact)
                         * (N_BUCKETS - max_exact)).astype(jnp.int32)
    large = jnp.minimum(large, N_BUCKETS - 1)
    return jnp.where(n < max_exact, n, large)


def banded_attention(q, k, v, tab, window, sinks):
    B, S, H, D = q.shape
    G = k.shape[2]
    R = H // G
    nb = S // QBLOCK
    nback = -(-(window - 1) // QBLOCK)
    K = (nback + 1) * QBLOCK
    qb = q.reshape(B, nb, QBLOCK, G, R, D)

    def band(t):
        tb = t.reshape(B, nb, QBLOCK, G, D)
        parts = [jnp.pad(tb, ((0, 0), (j, 0), (0, 0), (0, 0), (0, 0)))[:, :nb]
                 for j in range(nback, -1, -1)]
        return jnp.concatenate(parts, axis=2)

    kb, vb = band(k), band(v)
    rel = (jnp.arange(QBLOCK)[:, None] + nback * QBLOCK) - jnp.arange(K)[None, :]
    kglob = jnp.arange(nb)[:, None] * QBLOCK - nback * QBLOCK + jnp.arange(K)[None, :]
    mask = ((rel >= 0) & (rel < window))[None] & (kglob >= 0)[:, None, :]
    bias = jnp.moveaxis(tab[t5_bucket(rel)], -1, 0).reshape(G, R, QBLOCK, K).astype(jnp.float32)
    logits = jnp.einsum('bcqgrd,bckgd->bcgrqk', qb, kb).astype(jnp.float32) * SCALE + bias
    logits = jnp.where(mask[:, None, None], logits, NEG)
    if sinks is not None:
        s = sinks.astype(jnp.float32).reshape(G, R)[:, :, None, None]
        m = jnp.maximum(jnp.max(logits, axis=-1, keepdims=True), s)
        e = jnp.exp(logits - m)
        p = e / (jnp.sum(e, axis=-1, keepdims=True) + jnp.exp(s - m))
    else:
        p = jax.nn.softmax(logits, axis=-1)
    out = jnp.einsum('bcgrqk,bckgd->bcqgrd', p.astype(v.dtype), vb)
    return out.reshape(B, S, H, D)


def compress(t, pos_emb, w1, b1, w2, b2):
    B, S, G, D = t.shape
    n = (S - CMP_BLOCK) // CMP_STRIDE + 1
    idx = np.arange(n)[:, None] * CMP_STRIDE + np.arange(CMP_BLOCK)[None, :]
    blocks = t[:, idx] + pos_emb[:, None, :]
    blocks = jnp.moveaxis(blocks, 3, 2).reshape(B, n, G, CMP_BLOCK * D)
    return jax.nn.gelu(blocks @ w1 + b1) @ w2 + b2


def compressed_attention(q, kc, vc, tab):
    B, S, H, D = q.shape
    N, G = kc.shape[1], kc.shape[2]
    R = H // G
    qg = q.reshape(B, S, G, R, D)
    end = jnp.arange(N) * CMP_STRIDE + CMP_BLOCK - 1
    dist = jnp.arange(S)[:, None] - end[None, :]
    valid = dist >= 0
    bias = jnp.moveaxis(tab[t5_bucket(dist)], -1, 0).reshape(G, R, S, N).astype(jnp.float32)
    logits = jnp.einsum('bsgrd,bngd->bgrsn', qg, kc).astype(jnp.float32) * SCALE + bias
    logits = jnp.where(valid, logits, NEG)
    has_any = jnp.any(valid, axis=-1)[:, None].astype(jnp.float32)
    p = jax.nn.softmax(logits, axis=-1) * has_any
    out = jnp.einsum('bgrsn,bngd->bsgrd', p.astype(vc.dtype), vc).reshape(B, S, H, D)
    return out, jnp.sum(p, axis=2)


def select_blocks(p_grp, S):
    n_slc = S // SLC_BLOCK
    N = p_grp.shape[-1]
    i = np.arange(N)[:, None]
    j = np.arange(n_slc)[None, :]
    overlap = ((i * CMP_STRIDE <= j * SLC_BLOCK + SLC_BLOCK - 1)
               & (i * CMP_STRIDE + CMP_BLOCK - 1 >= j * SLC_BLOCK)).astype(np.float32)
    scores = jnp.einsum('bgsn,nj->bgsj', p_grp, jnp.asarray(overlap, p_grp.dtype))
    cur = jnp.arange(S)[:, None] // SLC_BLOCK
    blk = jnp.arange(n_slc)[None, :]
    forced = (blk == 0) | (blk == cur) | (blk == cur - 1)
    scores = jnp.where(forced, FORCE_SCORE, jnp.where(blk <= cur, scores, -1.0))
    _, idx = lax.top_k(scores, min(SLC_TOPK, n_slc))
    return idx


def selected_attention(q, ks, vs, idx, tab):
    B, S, H, D = q.shape
    G = ks.shape[2]
    R = H // G
    k_sel = idx.shape[-1]
    n_slc = S // SLC_BLOCK
    nc = S // QBLOCK
    kblk = ks.reshape(B, n_slc, SLC_BLOCK, G, D).transpose(0, 3, 1, 2, 4)
    vblk = vs.reshape(B, n_slc, SLC_BLOCK, G, D).transpose(0, 3, 1, 2, 4)
    tab_gr = tab.T.reshape(G, R, N_BUCKETS)
    q_c = jnp.moveaxis(q.reshape(B, nc, QBLOCK, G, R, D), 1, 0)
    idx_c = jnp.moveaxis(idx.reshape(B, G, nc, QBLOCK, k_sel), 2, 0)
    starts = jnp.arange(nc) * QBLOCK
    bi = jnp.arange(B)[:, None, None, None]
    gi = jnp.arange(G)[None, :, None, None]
    offs = jnp.arange(SLC_BLOCK)
    g5 = jnp.arange(G)[None, :, None, None, None]
    r5 = jnp.arange(R)[None, None, :, None, None]

    def one_block(args):
        qc, ic, s0 = args
        kg = kblk[bi, gi, ic].reshape(B, G, QBLOCK, k_sel * SLC_BLOCK, D)
        vg = vblk[bi, gi, ic].reshape(B, G, QBLOCK, k_sel * SLC_BLOCK, D)
        tok = (ic[..., None] * SLC_BLOCK + offs).reshape(B, G, QBLOCK, k_sel * SLC_BLOCK)
        dist = (s0 + jnp.arange(QBLOCK))[None, None, :, None] - tok
        bias = tab_gr[g5, r5, t5_bucket(dist)[:, :, None]].astype(jnp.float32)
        logits = jnp.einsum('bqgrd,bgqkd->bgrqk', qc, kg).astype(jnp.float32) * SCALE + bias
        logits = jnp.where((dist >= 0)[:, :, None], logits, NEG)
        p = jax.nn.softmax(logits, axis=-1)
        return jnp.einsum('bgrqk,bgqkd->bqgrd', p.astype(vg.dtype), vg)

    out = lax.map(one_block, (q_c, idx_c, starts))
    return jnp.moveaxis(out, 0, 1).reshape(B, S, H, D)


def causal_dwconv(u, w, b):
    S = u.shape[1]
    up = jnp.pad(u, ((0, 0), (CONV_WIDTH - 1, 0), (0, 0)))
    y = b
    for j in range(CONV_WIDTH):
        y = y + up[:, j:j + S] * w[j]
    return y


def hybrid_layer(x, rel_bias, norm_mix_g, w_in, a_q_norm_g, a_k_norm_g, a_sinks, b_q_norm_g,
                 b_k_norm_g, cmp_pos_emb, cmp_w1, cmp_b1, cmp_w2, cmp_b2, out_norm_g, w_out,
                 norm_ffn_g, w_gate, w_up, conv_w, conv_b, w_down):
    B, S, _ = x.shape
    hn = rms_norm(x, norm_mix_g)
    proj = hn @ w_in
    splits = np.cumsum(IN_SIZES)[:-1].tolist()
    aq, ak, av, bq, bkc, bvc, bks, bvs, bkw, bvw, bg = jnp.split(proj, splits, axis=-1)

    def heads(t, n):
        return t.reshape(B, S, n, HEAD_DIM)

    tab_a = rel_bias[:, :A_Q_HEADS]
    tab_b = rel_bias[:, A_Q_HEADS:]

    o_a = banded_attention(rms_norm(heads(aq, A_Q_HEADS), a_q_norm_g),
                           rms_norm(heads(ak, A_KV_HEADS), a_k_norm_g),
                           heads(av, A_KV_HEADS), tab_a, A_WINDOW, a_sinks)

    q = rms_norm(heads(bq, B_Q_HEADS), b_q_norm_g)
    kc = rms_norm(compress(heads(bkc, B_KV_HEADS), cmp_pos_emb[0], cmp_w1[0], cmp_b1[0],
                           cmp_w2[0], cmp_b2[0]), b_k_norm_g[0])
    vc = compress(heads(bvc, B_KV_HEADS), cmp_pos_emb[1], cmp_w1[1], cmp_b1[1], cmp_w2[1], cmp_b2[1])
    o_cmp, p_grp = compressed_attention(q, kc, vc, tab_b)
    sel_idx = select_blocks(p_grp, S)
    o_slc = selected_attention(q, rms_norm(heads(bks, B_KV_HEADS), b_k_norm_g[1]),
                               heads(bvs, B_KV_HEADS), sel_idx, tab_b)
    o_win = banded_attention(q, rms_norm(heads(bkw, B_KV_HEADS), b_k_norm_g[2]),
                             heads(bvw, B_KV_HEADS), tab_b, B_WINDOW, None)
    gates = jax.nn.sigmoid(bg.astype(jnp.float32)).reshape(B, S, B_Q_HEADS, N_BRANCH).astype(x.dtype)
    o_b = gates[..., 0:1] * o_cmp + gates[..., 1:2] * o_slc + gates[..., 2:3] * o_win

    o_a = rms_norm(o_a.reshape(B, S, A_WIDTH), out_norm_g[:A_WIDTH])
    o_b = rms_norm(o_b.reshape(B, S, B_WIDTH), out_norm_g[A_WIDTH:])
    x = x + jnp.concatenate([o_a, o_b], axis=-1) @ w_out

    hf = rms_norm(x, norm_ffn_g)
    g = causal_dwconv(hf @ w_gate, conv_w, conv_b)
    return x + (jax.nn.silu(g) * (hf @ w_up)) @ w_down


def setup_inputs(seed: int = 0) -> dict:
    key = jax.random.key(seed)
    ks = jax.random.split(key, 24)
    f32 = jnp.float32
    L = DEPTH

    def nrm(k, shape, scale):
        return jax.random.normal(k, shape, f32) * scale

    def gain(k, shape):
        return 1.0 + 0.05 * jax.random.normal(k, shape, f32)

    return {
        "x": nrm(ks[0], (BATCH, SEQ, D_MODEL), 1.0),
        "rel_bias": nrm(ks[1], (N_BUCKETS, MIX_HEADS), 0.5),
        "norm_mix_g": gain(ks[2], (L, D_MODEL)),
        "w_in": nrm(ks[3], (L, D_MODEL, N_IN), D_MODEL ** -0.5),
        "a_q_norm_g": gain(ks[4], (L, HEAD_DIM)),
        "a_k_norm_g": gain(ks[5], (L, HEAD_DIM)),
        "a_sinks": nrm(ks[6], (L, A_Q_HEADS), 0.5),
        "b_q_norm_g": gain(ks[7], (L, HEAD_DIM)),
        "b_k_norm_g": gain(ks[8], (L, N_BRANCH, HEAD_DIM)),
        "cmp_pos_emb": nrm(ks[9], (L, 2, CMP_BLOCK, HEAD_DIM), 0.1),
        "cmp_w1": nrm(ks[10], (L, 2, CMP_BLOCK * HEAD_DIM, CMP_HIDDEN), (CMP_BLOCK * HEAD_DIM) ** -0.5),
        "cmp_b1": nrm(ks[11], (L, 2, CMP_HIDDEN), 0.02),
        "cmp_w2": nrm(ks[12], (L, 2, CMP_HIDDEN, HEAD_DIM), CMP_HIDDEN ** -0.5),
        "cmp_b2": nrm(ks[13], (L, 2, HEAD_DIM), 0.02),
        "out_norm_g": gain(ks[14], (L, MIX_WIDTH)),
        "w_out": nrm(ks[15], (L, MIX_WIDTH, D_MODEL), MIX_WIDTH ** -0.5),
        "norm_ffn_g": gain(ks[16], (L, D_MODEL)),
        "w_gate": nrm(ks[17], (L, D_MODEL, D_FF), D_MODEL ** -0.5),
        "w_up": nrm(ks[18], (L, D_MODEL, D_FF), D_MODEL ** -0.5),
        "conv_w": nrm(ks[19], (L, CONV_WIDTH, D_FF), CONV_WIDTH ** -0.5),
        "conv_b": nrm(ks[20], (L, D_FF), 0.02),
        "w_down": nrm(ks[21], (L, D_FF, D_MODEL), D_FF ** -0.5),
    }


def reference(x, rel_bias, norm_mix_g, w_in, a_q_norm_g, a_k_norm_g, a_sinks, b_q_norm_g,
              b_k_norm_g, cmp_pos_emb, cmp_w1, cmp_b1, cmp_w2, cmp_b2, out_norm_g, w_out,
              norm_ffn_g, w_gate, w_up, conv_w, conv_b, w_down):
    h = x
    for l in range(DEPTH):
        h = hybrid_layer(h, rel_bias, norm_mix_g[l], w_in[l], a_q_norm_g[l], a_k_norm_g[l],
                         a_sinks[l], b_q_norm_g[l], b_k_norm_g[l], cmp_pos_emb[l], cmp_w1[l],
                         cmp_b1[l], cmp_w2[l], cmp_b2[l], out_norm_g[l], w_out[l], norm_ffn_g[l],
                         w_gate[l], w_up[l], conv_w[l], conv_b[l], w_down[l])
    return h
```

```python
import functools
import math

import numpy as np
import jax
import jax.numpy as jnp
from jax import lax
from jax.experimental import pallas as pl
from jax.experimental.pallas import tpu as pltpu

F32 = jnp.float32
BF16 = jnp.bfloat16

HEAD_DIM = 128
A_Q_HEADS = 16
A_KV_HEADS = 4
B_Q_HEADS = 16
B_KV_HEADS = 4
GROUP = 4
A_WINDOW = 128
B_WINDOW = 512
CMP_BLOCK = 32
CMP_STRIDE = 16
SLC_BLOCK = 64
SLC_TOPK = 16
N_BRANCH = 3
N_BUCKETS = 32
MAX_DISTANCE = 128
EPS = 1e-6
NEG = -1e30
FORCE_SCORE = 1e6
SCALE = HEAD_DIM ** -0.5

QB = 128
CMP_WIN = 24
CMP_TAB = 40
VMEM_LIMIT = 56 * 1024 * 1024


def _cparams(sem):
    return pltpu.CompilerParams(dimension_semantics=sem, vmem_limit_bytes=VMEM_LIMIT)


def _t5_bucket_np(dist):
    n = np.maximum(dist, 0)
    max_exact = N_BUCKETS // 2
    nf = np.maximum(n, 1).astype(np.float32)
    large = max_exact + (np.log(nf / max_exact) / math.log(MAX_DISTANCE / max_exact)
                         * (N_BUCKETS - max_exact)).astype(np.int32)
    large = np.minimum(large, N_BUCKETS - 1)
    return np.where(n < max_exact, n, large).astype(np.int32)


def _near_bias_tables(tab):
    kk = np.arange(QB)[:, None]
    qq = np.arange(QB)[None, :]
    idx = np.stack([_t5_bucket_np(qq - kk), _t5_bucket_np(QB + qq - kk)])
    return jnp.take(tab.T.astype(F32), jnp.asarray(idx), axis=1)


def _cmp_bias_table(tab):
    npr = np.arange(CMP_TAB)[:, None] - 16
    qq = np.arange(QB)[None, :]
    idx = _t5_bucket_np(qq - CMP_STRIDE * npr - (CMP_BLOCK - 1))
    return jnp.take(tab.T.astype(F32), jnp.asarray(idx), axis=1)


def _rmsnorm_kernel(*refs, n_in):
    x_refs, g_ref, o_ref = refs[:n_in], refs[n_in], refs[n_in + 1]
    off = 0
    for x_ref in x_refs:
        x = x_ref[...]
        w = x.shape[-1]
        y = x * lax.rsqrt(jnp.mean(x * x, axis=-1, keepdims=True) + EPS)
        o_ref[:, off:off + w] = (y * g_ref[:, off:off + w]).astype(o_ref.dtype)
        off += w


def _rmsnorm(xs, gain, tr=256):
    S = xs[0].shape[0]
    widths = [x.shape[1] for x in xs]
    n = sum(widths)
    tr = min(tr, S)
    return pl.pallas_call(
        functools.partial(_rmsnorm_kernel, n_in=len(xs)),
        grid=(S // tr,),
        in_specs=[pl.BlockSpec((tr, w), lambda i: (i, 0)) for w in widths]
        + [pl.BlockSpec((1, n), lambda i: (0, 0))],
        out_specs=pl.BlockSpec((tr, n), lambda i: (i, 0)),
        out_shape=jax.ShapeDtypeStruct((S, n), BF16),
        compiler_params=_cparams(("arbitrary",)),
        name="rmsnorm",
    )(*xs, gain.reshape(1, n).astype(F32))


def _inproj_a_kernel(kind_ref, a_ref, w_ref, g_ref, o_ref):
    j = pl.program_id(1)
    acc = jnp.dot(a_ref[...], w_ref[...], preferred_element_type=F32)

    @pl.when(kind_ref[j] == 0)
    def _():
        o_ref[...] = acc.astype(o_ref.dtype)

    @pl.when(kind_ref[j] == 1)
    def _():
        g = g_ref[0]
        for h in range(acc.shape[1] // HEAD_DIM):
            sl = acc[:, h * HEAD_DIM:(h + 1) * HEAD_DIM]
            y = sl * lax.rsqrt(jnp.mean(sl * sl, axis=-1, keepdims=True) + EPS)
            o_ref[:, h * HEAD_DIM:(h + 1) * HEAD_DIM] = (y * g).astype(o_ref.dtype)


def _inproj_a(hn, w, kinds, gains, tm=1024, tn=512):
    S, D = hn.shape
    N = w.shape[1]
    tm = min(tm, S)
    grid_spec = pltpu.PrefetchScalarGridSpec(
        num_scalar_prefetch=1,
        grid=(S // tm, N // tn),
        in_specs=[pl.BlockSpec((tm, D), lambda i, j, k: (i, 0)),
                  pl.BlockSpec((D, tn), lambda i, j, k: (0, j)),
                  pl.BlockSpec((1, 1, HEAD_DIM), lambda i, j, k: (j, 0, 0))],
        out_specs=pl.BlockSpec((tm, tn), lambda i, j, k: (i, j)),
    )
    return pl.pallas_call(
        _inproj_a_kernel,
        grid_spec=grid_spec,
        out_shape=jax.ShapeDtypeStruct((S, N), BF16),
        compiler_params=_cparams(("arbitrary", "arbitrary")),
        name="inproj_heads",
    )(kinds, hn, w, gains)


def _inproj_b_kernel(a_ref, w_ref, o_ref, gate_ref):
    acc = jnp.dot(a_ref[...], w_ref[...], preferred_element_type=F32)
    n = o_ref.shape[1]
    o_ref[...] = acc[:, :n]
    z = acc[:, n:]
    gate_ref[...] = 1.0 / (1.0 + jnp.exp(-z))


def _inproj_b(hn, w, n_main, tm=512):
    S, D = hn.shape
    N = w.shape[1]
    tm = min(tm, S)
    return pl.pallas_call(
        _inproj_b_kernel,
        grid=(S // tm,),
        in_specs=[pl.BlockSpec((tm, D), lambda i: (i, 0)),
                  pl.BlockSpec((D, N), lambda i: (0, 0))],
        out_specs=[pl.BlockSpec((tm, n_main), lambda i: (i, 0)),
                   pl.BlockSpec((tm, N - n_main), lambda i: (i, 0))],
        out_shape=[jax.ShapeDtypeStruct((S, n_main), F32),
                   jax.ShapeDtypeStruct((S, N - n_main), F32)],
        compiler_params=_cparams(("arbitrary",)),
        name="inproj_cmp_gates",
    )(hn, w)


def _gelu_tanh(x):
    return 0.5 * x * (1.0 + jnp.tanh(math.sqrt(2.0 / math.pi) * (x + 0.044715 * (x * x * x))))


def _compress_kernel(tk_ref, tv_ref, pos_ref, w1_ref, b1_ref, w2_ref, b2_ref, gk_ref,
                     kc_ref, vct_ref):
    nc = tk_ref.shape[2]

    def mlp(t, kv):
        a0 = jnp.dot((t + pos_ref[kv, 0:1, :]).astype(BF16), w1_ref[kv, 0],
                     preferred_element_type=F32)
        a1 = jnp.dot((t + pos_ref[kv, 1:2, :]).astype(BF16), w1_ref[kv, 1],
                     preferred_element_type=F32)
        h = a0 + pltpu.roll(a1, nc - 1, 0) + b1_ref[kv]
        h = _gelu_tanh(h)
        return jnp.dot(h.astype(BF16), w2_ref[kv], preferred_element_type=F32) + b2_ref[kv]

    ck = mlp(tk_ref[0, 0], 0)
    ck = ck * lax.rsqrt(jnp.mean(ck * ck, axis=-1, keepdims=True) + EPS) * gk_ref[...]
    kc_ref[0] = ck.astype(kc_ref.dtype)
    cv = mlp(tv_ref[0, 0], 1)
    vct_ref[0] = cv.T.astype(vct_ref.dtype)


def _compress(tkv, pos, w1, b1, w2, b2, gk):
    _, G, NC, CW = tkv.shape
    hid = w1.shape[-1]
    return pl.pallas_call(
        _compress_kernel,
        grid=(G,),
        in_specs=[pl.BlockSpec((1, 1, NC, CW), lambda g: (0, g, 0, 0)),
                  pl.BlockSpec((1, 1, NC, CW), lambda g: (1, g, 0, 0)),
                  pl.BlockSpec((2, 2, CW), lambda g: (0, 0, 0)),
                  pl.BlockSpec((2, 2, CW, hid), lambda g: (0, 0, 0, 0)),
                  pl.BlockSpec((2, 1, hid), lambda g: (0, 0, 0)),
                  pl.BlockSpec((2, hid, HEAD_DIM), lambda g: (0, 0, 0)),
                  pl.BlockSpec((2, 1, HEAD_DIM), lambda g: (0, 0, 0)),
                  pl.BlockSpec((1, HEAD_DIM), lambda g: (0, 0))],
        out_specs=[pl.BlockSpec((1, NC, HEAD_DIM), lambda g: (g, 0, 0)),
                   pl.BlockSpec((1, HEAD_DIM, NC), lambda g: (g, 0, 0))],
        out_shape=[jax.ShapeDtypeStruct((G, NC, HEAD_DIM), BF16),
                   jax.ShapeDtypeStruct((G, HEAD_DIM, NC), BF16)],
        compiler_params=_cparams(("arbitrary",)),
        name="compress",
    )(tkv, tkv, pos, w1, b1, w2, b2, gk)


def _qk(k_t, q_r):
    return lax.dot_general(k_t, q_r, (((1,), (1,)), ((), ())), preferred_element_type=F32) * SCALE


def _init_state(m_ref, l_ref, acc_ref):
    m_ref[...] = jnp.full(m_ref.shape, NEG, F32)
    l_ref[...] = jnp.zeros(l_ref.shape, F32)
    acc_ref[...] = jnp.zeros(acc_ref.shape, F32)


def _online_update(m_ref, l_ref, acc_ref, r, s, vt):
    m_old = m_ref[r]
    m_new = jnp.maximum(m_old, jnp.max(s, axis=0, keepdims=True))
    alpha = jnp.exp(m_old - m_new)
    p = jnp.exp(s - m_new)
    l_ref[r] = alpha * l_ref[r] + jnp.sum(p, axis=0, keepdims=True)
    acc_ref[r] = acc_ref[r] * alpha + jnp.dot(vt, p.astype(BF16), preferred_element_type=F32)
    m_ref[r] = m_new


def _key_ge_query_iotas():
    kk = lax.broadcasted_iota(jnp.int32, (QB, QB), 0)
    qq = lax.broadcasted_iota(jnp.int32, (QB, QB), 1)
    return kk, qq


def _band_kernel(c31_ref, sink_ref, q_ref, k_ref, vt_ref, tab_ref, o_ref, m_ref, l_ref, acc_ref,
                 *, nback, use_sinks, transposed_out):
    g = pl.program_id(0)
    c = pl.program_id(1)
    _init_state(m_ref, l_ref, acc_ref)
    kk, qq = _key_ge_query_iotas()

    for delta in range(nback, -1, -1):
        @pl.when(c - delta >= 0)
        def _(delta=delta):
            kb = c - delta
            k_t = k_ref[pl.ds(pl.multiple_of(kb * QB, QB), QB), :]
            vt = vt_ref[0, kb]
            for r in range(GROUP):
                h = g * GROUP + r
                s = _qk(k_t, q_ref[:, r * HEAD_DIM:(r + 1) * HEAD_DIM])
                if delta == 0:
                    s = jnp.where(qq >= kk, s + tab_ref[r, 0], NEG)
                else:
                    s = s + (tab_ref[r, 1] if delta == 1 else c31_ref[h])
                    if delta == nback:
                        s = jnp.where(qq < kk, s, NEG)
                _online_update(m_ref, l_ref, acc_ref, r, s, vt)

    for r in range(GROUP):
        m = m_ref[r]
        l = l_ref[r]
        acc = acc_ref[r]
        if use_sinks:
            sk = sink_ref[g * GROUP + r]
            m_f = jnp.maximum(m, sk)
            a = jnp.exp(m - m_f)
            l = l * a + jnp.exp(sk - m_f)
            acc = acc * a
        o_t = acc * (1.0 / l)
        if transposed_out:
            o_ref[r * HEAD_DIM:(r + 1) * HEAD_DIM, :] = o_t
        else:
            o_ref[:, r * HEAD_DIM:(r + 1) * HEAD_DIM] = o_t.T


def _band_attention(proj, q_col, k_col, vt4, tabs, c31, sinks, window, transposed_out):
    S = proj.shape[0]
    G = vt4.shape[0]
    nkb = S // QB
    nback = -(-(window - 1) // QB)
    use_sinks = sinks is not None
    if sinks is None:
        sinks = jnp.zeros((G * GROUP,), F32)
    qw = GROUP * HEAD_DIM
    if transposed_out:
        out_shape = jax.ShapeDtypeStruct((G * qw, S), F32)
        out_spec = pl.BlockSpec((qw, QB), lambda g, c: (g, c))
    else:
        out_shape = jax.ShapeDtypeStruct((S, G * qw), F32)
        out_spec = pl.BlockSpec((QB, qw), lambda g, c: (c, g))
    smem = pl.BlockSpec(memory_space=pltpu.SMEM)
    return pl.pallas_call(
        functools.partial(_band_kernel, nback=nback, use_sinks=use_sinks,
                          transposed_out=transposed_out),
        grid=(G, nkb),
        in_specs=[smem, smem,
                  pl.BlockSpec((QB, qw), lambda g, c: (c, q_col + g)),
                  pl.BlockSpec((S, HEAD_DIM), lambda g, c: (0, k_col + g)),
                  pl.BlockSpec((1, nkb, HEAD_DIM, QB), lambda g, c: (g, 0, 0, 0)),
                  pl.BlockSpec((GROUP, 2, QB, QB), lambda g, c: (g, 0, 0, 0))],
        out_specs=out_spec,
        out_shape=out_shape,
        scratch_shapes=[pltpu.VMEM((GROUP, 1, QB), F32), pltpu.VMEM((GROUP, 1, QB), F32),
                        pltpu.VMEM((GROUP, HEAD_DIM, QB), F32)],
        compiler_params=_cparams(("arbitrary", "arbitrary")),
        name="band_attention_w%d" % window,
    )(c31, sinks.astype(F32), proj, proj, vt4, tabs)


def _cmp_select_kernel(c31_ref, q_ref, kc_ref, vct_ref, tab_ref, ovl_ref, ot_ref, sel_ref, s_ref,
                       *, n_valid, topk):
    g = pl.program_id(0)
    c = pl.program_id(1)
    nc = kc_ref.shape[1]
    nb = ovl_ref.shape[0]
    kc = kc_ref[0]
    vct = vct_ref[0]

    w0 = pl.multiple_of(jnp.maximum(8 * c - 16, 0), 8)
    toff = pl.multiple_of(w0 - 8 * c + 16, 8)
    rown = lax.broadcasted_iota(jnp.int32, (nc, QB), 0)
    wrow = lax.broadcasted_iota(jnp.int32, (CMP_WIN, QB), 0)
    wq = lax.broadcasted_iota(jnp.int32, (CMP_WIN, QB), 1)
    n_abs = w0 + wrow
    dist = (c * QB + wq) - (n_abs * CMP_STRIDE + (CMP_BLOCK - 1))
    valid_w = (dist >= 0) & (n_abs < n_valid)
    qpos = c * QB + lax.broadcasted_iota(jnp.int32, (1, QB), 1)
    has_any = (qpos >= CMP_BLOCK - 1).astype(F32)
    kc_w = kc_ref[0, pl.ds(w0, CMP_WIN), :]

    p_grp = jnp.zeros((nc, QB), F32)
    for r in range(GROUP):
        h = g * GROUP + r
        q_r = q_ref[:, r * HEAD_DIM:(r + 1) * HEAD_DIM]
        s_ref[...] = jnp.where(rown < w0, _qk(kc, q_r) + c31_ref[h], NEG)
        s_w = _qk(kc_w, q_r) + tab_ref[r, pl.ds(toff, CMP_WIN), :]
        s_ref[pl.ds(w0, CMP_WIN), :] = jnp.where(valid_w, s_w, NEG)
        s = s_ref[...]
        m = jnp.max(s, axis=0, keepdims=True)
        e = jnp.exp(s - m)
        p = e * (has_any / jnp.sum(e, axis=0, keepdims=True))
        ot_ref[r * HEAD_DIM:(r + 1) * HEAD_DIM, :] = jnp.dot(
            vct, p.astype(BF16), preferred_element_type=F32)
        p_grp = p_grp + p

    scores = jnp.dot(ovl_ref[...], p_grp.astype(BF16), preferred_element_type=F32)
    blk = lax.broadcasted_iota(jnp.int32, (nb, QB), 0)
    cur = lax.shift_right_logical(c * QB + lax.broadcasted_iota(jnp.int32, (nb, QB), 1),
                                  int(math.log2(SLC_BLOCK)))
    forced = (blk == 0) | (blk == cur) | (blk == cur - 1)
    work = jnp.where(forced, FORCE_SCORE, jnp.where(blk <= cur, scores, -1.0))
    blk_f = blk.astype(F32)
    sel = jnp.zeros((nb, QB), F32)
    for _ in range(topk):
        mx = jnp.max(work, axis=0, keepdims=True)
        first = jnp.min(jnp.where(work == mx, blk_f, float(nb)), axis=0, keepdims=True)
        hit = blk_f == first
        sel = jnp.where(hit, 1.0, sel)
        work = jnp.where(hit, -3e38, work)
    sel_ref[0] = sel


def _cmp_select(proj, q_col, kc, vct, tab, c31, ovl_t, n_valid):
    S = proj.shape[0]
    G, NC, _ = kc.shape
    NB = ovl_t.shape[0]
    qw = GROUP * HEAD_DIM
    smem = pl.BlockSpec(memory_space=pltpu.SMEM)
    return pl.pallas_call(
        functools.partial(_cmp_select_kernel, n_valid=n_valid, topk=min(SLC_TOPK, NB)),
        grid=(G, S // QB),
        in_specs=[smem,
                  pl.BlockSpec((QB, qw), lambda g, c: (c, q_col + g)),
                  pl.BlockSpec((1, NC, HEAD_DIM), lambda g, c: (g, 0, 0)),
                  pl.BlockSpec((1, HEAD_DIM, NC), lambda g, c: (g, 0, 0)),
                  pl.BlockSpec((GROUP, CMP_TAB, QB), lambda g, c: (g, 0, 0)),
                  pl.BlockSpec((NB, NC), lambda g, c: (0, 0))],
        out_specs=[pl.BlockSpec((qw, QB), lambda g, c: (g, c)),
                   pl.BlockSpec((1, NB, QB), lambda g, c: (g, 0, c))],
        out_shape=[jax.ShapeDtypeStruct((G * qw, S), F32),
                   jax.ShapeDtypeStruct((G, NB, S), F32)],
        scratch_shapes=[pltpu.VMEM((NC, QB), F32)],
        compiler_params=_cparams(("arbitrary", "arbitrary")),
        name="cmp_attention_select",
    )(c31, proj, kc, vct, tab, ovl_t)


def _sel_kernel(c31_ref, q_ref, k_ref, vt_ref, tab_ref, sel_ref, gt_ref, ocmp_ref, owin_ref,
                o_ref, m_ref, l_ref, acc_ref):
    g = pl.program_id(0)
    c = pl.program_id(1)
    _init_state(m_ref, l_ref, acc_ref)
    kk, qq = _key_ge_query_iotas()
    spb = QB // SLC_BLOCK

    def sel_mask(kb):
        rows = [sel_ref[0, pl.ds(kb * spb + t, 1), :] for t in range(spb)]
        return jnp.concatenate(
            [jnp.broadcast_to(row, (SLC_BLOCK, QB)) for row in rows], axis=0) > 0.5

    def block(kb, delta):
        k_t = k_ref[pl.ds(pl.multiple_of(kb * QB, QB), QB), :]
        vt = vt_ref[0, kb]
        msk = sel_mask(kb)
        if delta == 0:
            msk = msk & (qq >= kk)
        for r in range(GROUP):
            s = _qk(k_t, q_ref[:, r * HEAD_DIM:(r + 1) * HEAD_DIM])
            if delta is None:
                s = s + c31_ref[g * GROUP + r]
            else:
                s = s + tab_ref[r, delta]
            _online_update(m_ref, l_ref, acc_ref, r, jnp.where(msk, s, NEG), vt)

    def far_body(kb, carry):
        block(kb, None)
        return carry

    lax.fori_loop(0, jnp.maximum(c - 1, 0), far_body, 0)

    @pl.when(c >= 1)
    def _():
        block(c - 1, 1)

    block(c, 0)

    for r in range(GROUP):
        h = g * GROUP + r
        o_slc = acc_ref[r] * (1.0 / l_ref[r])
        rows = slice(r * HEAD_DIM, (r + 1) * HEAD_DIM)
        g_cmp = gt_ref[pl.ds(h * N_BRANCH + 0, 1), :]
        g_slc = gt_ref[pl.ds(h * N_BRANCH + 1, 1), :]
        g_win = gt_ref[pl.ds(h * N_BRANCH + 2, 1), :]
        o_t = g_cmp * ocmp_ref[rows, :] + g_slc * o_slc + g_win * owin_ref[rows, :]
        o_ref[:, rows] = o_t.T


def _sel_attention(proj, q_col, k_col, vt4, tabs, c31, sel_t, gates_t, ocmp_t, owin_t):
    S = proj.shape[0]
    G = vt4.shape[0]
    nkb = S // QB
    NB = sel_t.shape[1]
    qw = GROUP * HEAD_DIM
    smem = pl.BlockSpec(memory_space=pltpu.SMEM)
    return pl.pallas_call(
        _sel_kernel,
        grid=(G, nkb),
        in_specs=[smem,
                  pl.BlockSpec((QB, qw), lambda g, c: (c, q_col + g)),
                  pl.BlockSpec((S, HEAD_DIM), lambda g, c: (0, k_col + g)),
                  pl.BlockSpec((1, nkb, HEAD_DIM, QB), lambda g, c: (g, 0, 0, 0)),
                  pl.BlockSpec((GROUP, 2, QB, QB), lambda g, c: (g, 0, 0, 0)),
                  pl.BlockSpec((1, NB, QB), lambda g, c: (g, 0, c)),
                  pl.BlockSpec((gates_t.shape[0], QB), lambda g, c: (0, c)),
                  pl.BlockSpec((qw, QB), lambda g, c: (g, c)),
                  pl.BlockSpec((qw, QB), lambda g, c: (g, c))],
        out_specs=pl.BlockSpec((QB, qw), lambda g, c: (c, g)),
        out_shape=jax.ShapeDtypeStruct((S, G * qw), F32),
        scratch_shapes=[pltpu.VMEM((GROUP, 1, QB), F32), pltpu.VMEM((GROUP, 1, QB), F32),
                        pltpu.VMEM((GROUP, HEAD_DIM, QB), F32)],
        compiler_params=_cparams(("arbitrary", "arbitrary")),
        name="selected_attention_combine",
    )(c31, proj, proj, vt4, tabs, sel_t, gates_t, ocmp_t, owin_t)


def _outproj_kernel(a_ref, w_ref, x_ref, o_ref):
    o_ref[...] = x_ref[...] + jnp.dot(a_ref[...], w_ref[...], preferred_element_type=F32)


def _outproj(a, w, x, tm=1024, tn=512):
    S, D = a.shape
    N = w.shape[1]
    tm = min(tm, S)
    return pl.pallas_call(
        _outproj_kernel,
        grid=(S // tm, N // tn),
        in_specs=[pl.BlockSpec((tm, D), lambda i, j: (i, 0)),
                  pl.BlockSpec((D, tn), lambda i, j: (0, j)),
                  pl.BlockSpec((tm, tn), lambda i, j: (i, j))],
        out_specs=pl.BlockSpec((tm, tn), lambda i, j: (i, j)),
        out_shape=jax.ShapeDtypeStruct((S, N), F32),
        compiler_params=_cparams(("arbitrary", "arbitrary")),
        name="outproj_residual",
    )(a, w, x)


HALO = 16


def _ffn_a_kernel(halo_ref, a_ref, wg_ref, wu_ref, cw_ref, cb_ref, o_ref):
    i = pl.program_id(0)
    tm = a_ref.shape[0]
    a = a_ref[...]
    halo = halo_ref[...]
    halo = jnp.where(i > 0, halo, jnp.zeros_like(halo))
    gate = jnp.dot(a, wg_ref[...], preferred_element_type=F32)
    ghalo = jnp.dot(halo, wg_ref[...], preferred_element_type=F32)
    up = jnp.dot(a, wu_ref[...], preferred_element_type=F32)
    gext = jnp.concatenate([ghalo, gate], axis=0)
    g1 = gext[HALO - 1:HALO - 1 + tm]
    g2 = gext[HALO - 2:HALO - 2 + tm]
    y = cb_ref[...] + g2 * cw_ref[0:1, :]
    y = y + g1 * cw_ref[1:2, :]
    y = y + gate * cw_ref[2:3, :]
    act = y * (1.0 / (1.0 + jnp.exp(-y)))
    o_ref[...] = (act * up).astype(o_ref.dtype)


def _ffn_a(hf, wg, wu, cw, cb, tm=1024, tf=512):
    S, D = hf.shape
    Fp = wg.shape[1]
    tm = min(tm, S)
    hb = tm // HALO
    return pl.pallas_call(
        _ffn_a_kernel,
        grid=(S // tm, Fp // tf),
        in_specs=[pl.BlockSpec((HALO, D), lambda i, f: (jnp.maximum(i * hb - 1, 0), 0)),
                  pl.BlockSpec((tm, D), lambda i, f: (i, 0)),
                  pl.BlockSpec((D, tf), lambda i, f: (0, f)),
                  pl.BlockSpec((D, tf), lambda i, f: (0, f)),
                  pl.BlockSpec((cw.shape[0], tf), lambda i, f: (0, f)),
                  pl.BlockSpec((1, tf), lambda i, f: (0, f))],
        out_specs=pl.BlockSpec((tm, tf), lambda i, f: (i, f)),
        out_shape=jax.ShapeDtypeStruct((S, Fp), BF16),
        compiler_params=_cparams(("arbitrary", "arbitrary")),
        name="ffn_gate_up",
    )(hf, hf, wg, wu, cw, cb)


def _ffn_b_kernel(a_ref, w_ref, x_ref, o_ref, acc_ref):
    k = pl.program_id(2)
    prod = jnp.dot(a_ref[...], w_ref[...], preferred_element_type=F32)

    @pl.when(k == 0)
    def _():
        acc_ref[...] = x_ref[...] + prod

    @pl.when(k > 0)
    def _():
        acc_ref[...] += prod

    @pl.when(k == pl.num_programs(2) - 1)
    def _():
        o_ref[...] = acc_ref[...]


def _ffn_b(h, w, x, tm=1024, tn=1024):
    S, Fp = h.shape
    N = w.shape[1]
    tm = min(tm, S)
    tk = 1408 if Fp % 1408 == 0 else 512
    return pl.pallas_call(
        _ffn_b_kernel,
        grid=(S // tm, N // tn, Fp // tk),
        in_specs=[pl.BlockSpec((tm, tk), lambda i, j, k: (i, k)),
                  pl.BlockSpec((tk, tn), lambda i, j, k: (k, j)),
                  pl.BlockSpec((tm, tn), lambda i, j, k: (i, j))],
        out_specs=pl.BlockSpec((tm, tn), lambda i, j, k: (i, j)),
        out_shape=jax.ShapeDtypeStruct((S, N), F32),
        scratch_shapes=[pltpu.VMEM((tm, tn), F32)],
        compiler_params=_cparams(("arbitrary", "arbitrary", "arbitrary")),
        name="ffn_down_residual",
    )(h, w, x)


def _values_t(v, G):
    S = v.shape[0]
    return v.reshape(S // QB, QB, G, HEAD_DIM).transpose(2, 0, 3, 1)


def _layer(x, rel_bias, norm_mix_g, w_in, a_q_norm_g, a_k_norm_g, a_sinks, b_q_norm_g, b_k_norm_g,
           cmp_pos_emb, cmp_w1, cmp_b1, cmp_w2, cmp_b2, out_norm_g, w_out, norm_ffn_g, w_gate,
           w_up, conv_w, conv_b, w_down):
    S, D = x.shape
    aw = A_Q_HEADS * HEAD_DIM
    akv = A_KV_HEADS * HEAD_DIM
    bw = B_Q_HEADS * HEAD_DIM
    bkv = B_KV_HEADS * HEAD_DIM
    sizes = [aw, akv, akv, bw] + [bkv] * 6 + [N_BRANCH * B_Q_HEADS]
    offs = np.concatenate([[0], np.cumsum(sizes)]).tolist()
    seg = lambda k: w_in[:, offs[k]:offs[k + 1]]
    tn = 512
    order_a = [0, 1, 2, 3, 6, 7, 8, 9]
    w_a = jnp.concatenate([seg(k) for k in order_a], axis=1).astype(BF16)
    n_gate = sizes[10]
    w_b = jnp.concatenate([seg(4), seg(5), seg(10),
                           jnp.zeros((D, QB - n_gate), w_in.dtype)], axis=1).astype(BF16)
    one = jnp.ones((HEAD_DIM,), F32)
    seg_gain = {0: a_q_norm_g, 1: a_k_norm_g, 3: b_q_norm_g, 6: b_k_norm_g[1], 8: b_k_norm_g[2]}
    kinds, gains, col = [], [], {}
    c0 = 0
    for k in order_a:
        col[k] = c0
        for _ in range(sizes[k] // tn):
            kinds.append(1 if k in seg_gain else 0)
            gains.append(seg_gain.get(k, one))
        c0 += sizes[k]
    kinds = jnp.asarray(kinds, jnp.int32)
    gains = jnp.stack(gains).astype(F32).reshape(len(gains), 1, HEAD_DIM)

    hn = _rmsnorm([x], norm_mix_g)
    proj = _inproj_a(hn, w_a, kinds, gains, tn=tn)
    kv32, gates = _inproj_b(hn, w_b, 2 * bkv)

    tab_a = rel_bias[:, :A_Q_HEADS]
    tab_b = rel_bias[:, A_Q_HEADS:]
    near_a = _near_bias_tables(tab_a)
    near_b = _near_bias_tables(tab_b)
    c31_a = tab_a[N_BUCKETS - 1].astype(F32)
    c31_b = tab_b[N_BUCKETS - 1].astype(F32)
    qwid = GROUP * HEAD_DIM

    vt_a = _values_t(proj[:, col[2]:col[2] + akv], A_KV_HEADS)
    o_a = _band_attention(proj, col[0] // qwid, col[1] // HEAD_DIM, vt_a, near_a, c31_a, a_sinks,
                          A_WINDOW, transposed_out=False)

    NC = S // CMP_STRIDE
    n_cmp = (S - CMP_BLOCK) // CMP_STRIDE + 1
    tkv = kv32.reshape(NC, CMP_STRIDE, 2, B_KV_HEADS, HEAD_DIM).transpose(2, 3, 0, 1, 4)
    tkv = tkv.reshape(2, B_KV_HEADS, NC, CMP_STRIDE * HEAD_DIM)
    halves = CMP_BLOCK // CMP_STRIDE
    pos = cmp_pos_emb.reshape(2, halves, CMP_STRIDE * HEAD_DIM).astype(F32)
    w1 = cmp_w1.reshape(2, halves, CMP_STRIDE * HEAD_DIM, cmp_w1.shape[-1]).astype(BF16)
    kc, vct = _compress(tkv, pos, w1, cmp_b1[:, None, :].astype(F32), cmp_w2.astype(BF16),
                        cmp_b2[:, None, :].astype(F32), b_k_norm_g[0].reshape(1, HEAD_DIM).astype(F32))

    NB = S // SLC_BLOCK
    ii = np.arange(NC)[None, :]
    jj = np.arange(NB)[:, None]
    ovl_t = ((ii * CMP_STRIDE <= jj * SLC_BLOCK + SLC_BLOCK - 1)
             & (ii * CMP_STRIDE + CMP_BLOCK - 1 >= jj * SLC_BLOCK) & (ii < n_cmp))
    ovl_t = jnp.asarray(ovl_t.astype(np.float32), BF16)
    ocmp_t, sel_t = _cmp_select(proj, col[3] // qwid, kc, vct, _cmp_bias_table(tab_b), c31_b,
                                ovl_t, n_cmp)

    vt_w = _values_t(proj[:, col[9]:col[9] + bkv], B_KV_HEADS)
    owin_t = _band_attention(proj, col[3] // qwid, col[8] // HEAD_DIM, vt_w, near_b, c31_b, None,
                             B_WINDOW, transposed_out=True)
    vt_s = _values_t(proj[:, col[7]:col[7] + bkv], B_KV_HEADS)
    o_b = _sel_attention(proj, col[3] // qwid, col[6] // HEAD_DIM, vt_s, near_b, c31_b, sel_t,
                         gates.T, ocmp_t, owin_t)

    on = _rmsnorm([o_a, o_b], out_norm_g)
    x2 = _outproj(on, w_out.astype(BF16), x)

    F = w_gate.shape[1]
    tf = 512
    Fp = -(-F // tf) * tf
    padc = lambda w: jnp.pad(w, ((0, 0), (0, Fp - F)))
    hf = _rmsnorm([x2], norm_ffn_g)
    hmid = _ffn_a(hf, padc(w_gate).astype(BF16), padc(w_up).astype(BF16),
                  padc(conv_w).astype(F32), padc(conv_b[None, :]).astype(F32), tf=tf)
    return _ffn_b(hmid, jnp.pad(w_down, ((0, Fp - F), (0, 0))).astype(BF16), x2)


def kernel(x, rel_bias, norm_mix_g, w_in, a_q_norm_g, a_k_norm_g, a_sinks, b_q_norm_g, b_k_norm_g,
           cmp_pos_emb, cmp_w1, cmp_b1, cmp_w2, cmp_b2, out_norm_g, w_out, norm_ffn_g, w_gate, w_up,
           conv_w, conv_b, w_down):
    depth = w_in.shape[0]
    batch = x.shape[0]
    outs = []
    for b in range(batch):
        h = x[b]
        for l in range(depth):
            h = _layer(h, rel_bias, norm_mix_g[l], w_in[l], a_q_norm_g[l], a_k_norm_g[l], a_sinks[l],
                       b_q_norm_g[l], b_k_norm_g[l], cmp_pos_emb[l], cmp_w1[l], cmp_b1[l], cmp_w2[l],
                       cmp_b2[l], out_norm_g[l], w_out[l], norm_ffn_g[l], w_gate[l], w_up[l],
                       conv_w[l], conv_b[l], w_down[l])
        outs.append(h)
    return jnp.stack(outs)
```

```python
import functools
import math

import numpy as np
import jax
import jax.numpy as jnp
from jax import lax
from jax.experimental import pallas as pl
from jax.experimental.pallas import tpu as pltpu

F32 = jnp.float32
BF16 = jnp.bfloat16

HEAD_DIM = 128
A_Q_HEADS = 16
A_KV_HEADS = 4
B_Q_HEADS = 16
B_KV_HEADS = 4
GROUP = 4
A_WINDOW = 128
B_WINDOW = 512
CMP_BLOCK = 32
CMP_STRIDE = 16
SLC_BLOCK = 64
SLC_TOPK = 16
N_BRANCH = 3
N_BUCKETS = 32
MAX_DISTANCE = 128
EPS = 1e-6
NEG = -1e30
FORCE_SCORE = 1e6
SCALE = HEAD_DIM ** -0.5

QB = 128
CMP_WIN = 24
CMP_TAB = 40
VMEM_LIMIT = 56 * 1024 * 1024


def _cparams(sem):
    return pltpu.CompilerParams(dimension_semantics=sem, vmem_limit_bytes=VMEM_LIMIT)


def _t5_bucket_np(dist):
    n = np.maximum(dist, 0)
    max_exact = N_BUCKETS // 2
    nf = np.maximum(n, 1).astype(np.float32)
    large = max_exact + (np.log(nf / max_exact) / math.log(MAX_DISTANCE / max_exact)
                         * (N_BUCKETS - max_exact)).astype(np.int32)
    large = np.minimum(large, N_BUCKETS - 1)
    return np.where(n < max_exact, n, large).astype(np.int32)


def _near_bias_tables(tab):
    kk = np.arange(QB)[:, None]
    qq = np.arange(QB)[None, :]
    idx = np.stack([_t5_bucket_np(qq - kk), _t5_bucket_np(QB + qq - kk)])
    return jnp.take(tab.T.astype(F32), jnp.asarray(idx), axis=1)


def _cmp_bias_table(tab):
    npr = np.arange(CMP_TAB)[:, None] - 16
    qq = np.arange(QB)[None, :]
    idx = _t5_bucket_np(qq - CMP_STRIDE * npr - (CMP_BLOCK - 1))
    return jnp.take(tab.T.astype(F32), jnp.asarray(idx), axis=1)


def _rmsnorm_kernel(*refs, n_in):
    x_refs, g_ref, o_ref = refs[:n_in], refs[n_in], refs[n_in + 1]
    off = 0
    for x_ref in x_refs:
        x = x_ref[...]
        w = x.shape[-1]
        y = x * lax.rsqrt(jnp.mean(x * x, axis=-1, keepdims=True) + EPS)
        o_ref[:, off:off + w] = (y * g_ref[:, off:off + w]).astype(o_ref.dtype)
        off += w


def _rmsnorm(xs, gain, tr=256):
    S = xs[0].shape[0]
    widths = [x.shape[1] for x in xs]
    n = sum(widths)
    tr = min(tr, S)
    return pl.pallas_call(
        functools.partial(_rmsnorm_kernel, n_in=len(xs)),
        grid=(S // tr,),
        in_specs=[pl.BlockSpec((tr, w), lambda i: (i, 0)) for w in widths]
        + [pl.BlockSpec((1, n), lambda i: (0, 0))],
        out_specs=pl.BlockSpec((tr, n), lambda i: (i, 0)),
        out_shape=jax.ShapeDtypeStruct((S, n), BF16),
        compiler_params=_cparams(("arbitrary",)),
        name="rmsnorm",
    )(*xs, gain.reshape(1, n).astype(F32))


def _inproj_a_kernel(kind_ref, a_ref, w_ref, g_ref, o_ref):
    j = pl.program_id(1)
    acc = jnp.dot(a_ref[...], w_ref[...], preferred_element_type=F32)

    @pl.when(kind_ref[j] == 0)
    def _():
        o_ref[...] = acc.astype(o_ref.dtype)

    @pl.when(kind_ref[j] == 1)
    def _():
        g = g_ref[0]
        for h in range(acc.shape[1] // HEAD_DIM):
            sl = acc[:, h * HEAD_DIM:(h + 1) * HEAD_DIM]
            y = sl * lax.rsqrt(jnp.mean(sl * sl, axis=-1, keepdims=True) + EPS)
            o_ref[:, h * HEAD_DIM:(h + 1) * HEAD_DIM] = (y * g).astype(o_ref.dtype)


def _inproj_a(hn, w, kinds, gains, tm=1024, tn=512):
    S, D = hn.shape
    N = w.shape[1]
    tm = min(tm, S)
    grid_spec = pltpu.PrefetchScalarGridSpec(
        num_scalar_prefetch=1,
        grid=(S // tm, N // tn),
        in_specs=[pl.BlockSpec((tm, D), lambda i, j, k: (i, 0)),
                  pl.BlockSpec((D, tn), lambda i, j, k: (0, j)),
                  pl.BlockSpec((1, 1, HEAD_DIM), lambda i, j, k: (j, 0, 0))],
        out_specs=pl.BlockSpec((tm, tn), lambda i, j, k: (i, j)),
    )
    return pl.pallas_call(
        _inproj_a_kernel,
        grid_spec=grid_spec,
        out_shape=jax.ShapeDtypeStruct((S, N), BF16),
        compiler_params=_cparams(("arbitrary", "arbitrary")),
        name="inproj_heads",
    )(kinds, hn, w, gains)


def _inproj_b_kernel(a_ref, w_ref, o_ref, gate_ref):
    acc = jnp.dot(a_ref[...], w_ref[...], preferred_element_type=F32)
    n = o_ref.shape[1]
    o_ref[...] = acc[:, :n]
    z = acc[:, n:]
    gate_ref[...] = 1.0 / (1.0 + jnp.exp(-z))


def _inproj_b(hn, w, n_main, tm=512):
    S, D = hn.shape
    N = w.shape[1]
    tm = min(tm, S)
    return pl.pallas_call(
        _inproj_b_kernel,
        grid=(S // tm,),
        in_specs=[pl.BlockSpec((tm, D), lambda i: (i, 0)),
                  pl.BlockSpec((D, N), lambda i: (0, 0))],
        out_specs=[pl.BlockSpec((tm, n_main), lambda i: (i, 0)),
                   pl.BlockSpec((tm, N - n_main), lambda i: (i, 0))],
        out_shape=[jax.ShapeDtypeStruct((S, n_main), F32),
                   jax.ShapeDtypeStruct((S, N - n_main), F32)],
        compiler_params=_cparams(("arbitrary",)),
        name="inproj_cmp_gates",
    )(hn, w)


def _gelu_tanh(x):
    return 0.5 * x * (1.0 + jnp.tanh(math.sqrt(2.0 / math.pi) * (x + 0.044715 * (x * x * x))))


def _compress_kernel(tk_ref, tv_ref, pos_ref, w1_ref, b1_ref, w2_ref, b2_ref, gk_ref,
                     kc_ref, vct_ref):
    nc = tk_ref.shape[2]

    def mlp(t, kv):
        a0 = jnp.dot((t + pos_ref[kv, 0:1, :]).astype(BF16), w1_ref[kv, 0],
                     preferred_element_type=F32)
        a1 = jnp.dot((t + pos_ref[kv, 1:2, :]).astype(BF16), w1_ref[kv, 1],
                     preferred_element_type=F32)
        h = a0 + pltpu.roll(a1, nc - 1, 0) + b1_ref[kv]
        h = _gelu_tanh(h)
        return jnp.dot(h.astype(BF16), w2_ref[kv], preferred_element_type=F32) + b2_ref[kv]

    ck = mlp(tk_ref[0, 0], 0)
    ck = ck * lax.rsqrt(jnp.mean(ck * ck, axis=-1, keepdims=True) + EPS) * gk_ref[...]
    kc_ref[0] = ck.astype(kc_ref.dtype)
    cv = mlp(tv_ref[0, 0], 1)
    vct_ref[0] = cv.T.astype(vct_ref.dtype)


def _compress(tkv, pos, w1, b1, w2, b2, gk):
    _, G, NC, CW = tkv.shape
    hid = w1.shape[-1]
    return pl.pallas_call(
        _compress_kernel,
        grid=(G,),
        in_specs=[pl.BlockSpec((1, 1, NC, CW), lambda g: (0, g, 0, 0)),
                  pl.BlockSpec((1, 1, NC, CW), lambda g: (1, g, 0, 0)),
                  pl.BlockSpec((2, 2, CW), lambda g: (0, 0, 0)),
                  pl.BlockSpec((2, 2, CW, hid), lambda g: (0, 0, 0, 0)),
                  pl.BlockSpec((2, 1, hid), lambda g: (0, 0, 0)),
                  pl.BlockSpec((2, hid, HEAD_DIM), lambda g: (0, 0, 0)),
                  pl.BlockSpec((2, 1, HEAD_DIM), lambda g: (0, 0, 0)),
                  pl.BlockSpec((1, HEAD_DIM), lambda g: (0, 0))],
        out_specs=[pl.BlockSpec((1, NC, HEAD_DIM), lambda g: (g, 0, 0)),
                   pl.BlockSpec((1, HEAD_DIM, NC), lambda g: (g, 0, 0))],
        out_shape=[jax.ShapeDtypeStruct((G, NC, HEAD_DIM), BF16),
                   jax.ShapeDtypeStruct((G, HEAD_DIM, NC), BF16)],
        compiler_params=_cparams(("arbitrary",)),
        name="compress",
    )(tkv, tkv, pos, w1, b1, w2, b2, gk)


QW = GROUP * QB


def _scores(k_t, qt):
    return jnp.dot(k_t, qt, preferred_element_type=F32) * SCALE


def _init_state(m_ref, l_ref, acc_ref):
    m_ref[...] = jnp.full(m_ref.shape, NEG, F32)
    l_ref[...] = jnp.zeros(l_ref.shape, F32)
    acc_ref[...] = jnp.zeros(acc_ref.shape, F32)


def _online_update(m_ref, l_ref, acc_ref, s, vt):
    m_old = m_ref[...]
    m_new = jnp.maximum(m_old, jnp.max(s, axis=0, keepdims=True))
    alpha = jnp.exp(m_old - m_new)
    p = jnp.exp(s - m_new)
    l_ref[...] = alpha * l_ref[...] + jnp.sum(p, axis=0, keepdims=True)
    acc_ref[...] = acc_ref[...] * alpha + jnp.dot(vt, p.astype(BF16), preferred_element_type=F32)
    m_ref[...] = m_new


def _key_query_iotas():
    kk = lax.broadcasted_iota(jnp.int32, (QB, QW), 0)
    qq = lax.broadcasted_iota(jnp.int32, (QB, QW), 1) & (QB - 1)
    return kk, qq


def _load_kv(k_ref, vt_ref, kb0, n):
    k_t = k_ref[pl.ds(pl.multiple_of(kb0 * QB, QB), n * QB), :]
    if n == 1:
        return k_t, vt_ref[0, kb0]
    return k_t, jnp.concatenate([vt_ref[0, kb0 + t] for t in range(n)], axis=1)


def _store_heads(o_ref, o_t):
    for r in range(GROUP):
        o_ref[:, r * HEAD_DIM:(r + 1) * HEAD_DIM] = o_t[:, r * QB:(r + 1) * QB].T


def _stack_heads(a, G):
    a = a.reshape((G, GROUP) + a.shape[1:])
    a = jnp.moveaxis(a, 1, -2)
    return a.reshape(a.shape[:-2] + (QW,))


def _stack_queries(q, G):
    S = q.shape[0]
    q = q.reshape(S // QB, QB, G, GROUP, HEAD_DIM).transpose(2, 0, 4, 3, 1)
    return q.reshape(G, S // QB, HEAD_DIM, QW)


def _band_kernel(qt_ref, k_ref, vt_ref, tab_ref, c31_ref, sink_ref, o_ref, m_ref, l_ref, acc_ref,
                 *, nback, use_sinks, transposed_out):
    c = pl.program_id(1)
    _init_state(m_ref, l_ref, acc_ref)
    kk, qq = _key_query_iotas()
    qt = qt_ref[0, 0]
    c31 = c31_ref[0]

    def run(kb0, deltas):
        k_t, vt = _load_kv(k_ref, vt_ref, kb0, len(deltas))
        s = _scores(k_t, qt)
        parts = []
        for t, delta in enumerate(deltas):
            sb = s[t * QB:(t + 1) * QB]
            if delta == 0:
                sb = jnp.where(qq >= kk, sb + tab_ref[0, 0], NEG)
            else:
                sb = sb + (tab_ref[0, 1] if delta == 1 else c31)
                if delta == nback:
                    sb = jnp.where(qq < kk, sb, NEG)
            parts.append(sb)
        s = parts[0] if len(parts) == 1 else jnp.concatenate(parts, axis=0)
        _online_update(m_ref, l_ref, acc_ref, s, vt)

    @pl.when(c >= nback)
    def _():
        run(c - nback, list(range(nback, -1, -1)))

    @pl.when(c < nback)
    def _():
        for delta in range(nback - 1, -1, -1):
            @pl.when(c - delta >= 0)
            def _(delta=delta):
                run(c - delta, [delta])

    l = l_ref[...]
    acc = acc_ref[...]
    if use_sinks:
        m = m_ref[...]
        sk = sink_ref[0]
        m_f = jnp.maximum(m, sk)
        a = jnp.exp(m - m_f)
        l = l * a + jnp.exp(sk - m_f)
        acc = acc * a
    o_t = acc * (1.0 / l)
    if transposed_out:
        o_ref[0, 0] = o_t
    else:
        _store_heads(o_ref, o_t)


def _band_attention(qt, proj, k_col, vt4, tabs, c31, sinks, window, transposed_out):
    S = proj.shape[0]
    G, nkb = vt4.shape[0], vt4.shape[1]
    nback = -(-(window - 1) // QB)
    use_sinks = sinks is not None
    if sinks is None:
        sinks = jnp.zeros_like(c31)
    if transposed_out:
        out_shape = jax.ShapeDtypeStruct((G, nkb, HEAD_DIM, QW), F32)
        out_spec = pl.BlockSpec((1, 1, HEAD_DIM, QW), lambda g, c: (g, c, 0, 0))
    else:
        out_shape = jax.ShapeDtypeStruct((S, G * GROUP * HEAD_DIM), F32)
        out_spec = pl.BlockSpec((QB, GROUP * HEAD_DIM), lambda g, c: (c, g))
    row = pl.BlockSpec((1, 1, QW), lambda g, c: (g, 0, 0))
    return pl.pallas_call(
        functools.partial(_band_kernel, nback=nback, use_sinks=use_sinks,
                          transposed_out=transposed_out),
        grid=(G, nkb),
        in_specs=[pl.BlockSpec((1, 1, HEAD_DIM, QW), lambda g, c: (g, c, 0, 0)),
                  pl.BlockSpec((S, HEAD_DIM), lambda g, c: (0, k_col + g)),
                  pl.BlockSpec((1, nkb, HEAD_DIM, QB), lambda g, c: (g, 0, 0, 0)),
                  pl.BlockSpec((1, 2, QB, QW), lambda g, c: (g, 0, 0, 0)),
                  row, row],
        out_specs=out_spec,
        out_shape=out_shape,
        scratch_shapes=[pltpu.VMEM((1, QW), F32), pltpu.VMEM((1, QW), F32),
                        pltpu.VMEM((HEAD_DIM, QW), F32)],
        compiler_params=_cparams(("arbitrary", "arbitrary")),
        name="band_attention_w%d" % window,
    )(qt, proj, vt4, tabs, c31, sinks)


def _cmp_select_kernel(qt_ref, kc_ref, vct_ref, tab_ref, c31_ref, ovl_ref, ot_ref, sel_ref, s_ref,
                       *, n_valid, topk):
    c = pl.program_id(1)
    nc = kc_ref.shape[1]
    nb = ovl_ref.shape[0]
    qt = qt_ref[0, 0]

    w0 = pl.multiple_of(jnp.maximum(8 * c - 16, 0), 8)
    toff = pl.multiple_of(w0 - 8 * c + 16, 8)
    rown = lax.broadcasted_iota(jnp.int32, (nc, QW), 0)
    n_abs = w0 + lax.broadcasted_iota(jnp.int32, (CMP_WIN, QW), 0)
    wq = lax.broadcasted_iota(jnp.int32, (CMP_WIN, QW), 1) & (QB - 1)
    dist = (c * QB + wq) - (n_abs * CMP_STRIDE + (CMP_BLOCK - 1))
    valid_w = (dist >= 0) & (n_abs < n_valid)
    qpos = c * QB + (lax.broadcasted_iota(jnp.int32, (1, QW), 1) & (QB - 1))
    has_any = (qpos >= CMP_BLOCK - 1).astype(F32)

    s_ref[...] = jnp.where(rown < w0, _scores(kc_ref[0], qt) + c31_ref[0], NEG)
    s_w = _scores(kc_ref[0, pl.ds(w0, CMP_WIN), :], qt) + tab_ref[0, pl.ds(toff, CMP_WIN), :]
    s_ref[pl.ds(w0, CMP_WIN), :] = jnp.where(valid_w, s_w, NEG)
    s = s_ref[...]
    m = jnp.max(s, axis=0, keepdims=True)
    e = jnp.exp(s - m)
    p = e * (has_any / jnp.sum(e, axis=0, keepdims=True))
    ot_ref[0, 0] = jnp.dot(vct_ref[0], p.astype(BF16), preferred_element_type=F32)
    p_grp = p[:, 0:QB]
    for r in range(1, GROUP):
        p_grp = p_grp + p[:, r * QB:(r + 1) * QB]

    scores = jnp.dot(ovl_ref[...], p_grp.astype(BF16), preferred_element_type=F32)
    blk = lax.broadcasted_iota(jnp.int32, (nb, QB), 0)
    cur = lax.shift_right_logical(c * QB + lax.broadcasted_iota(jnp.int32, (nb, QB), 1),
                                  int(math.log2(SLC_BLOCK)))
    forced = (blk == 0) | (blk == cur) | (blk == cur - 1)
    work = jnp.where(forced, FORCE_SCORE, jnp.where(blk <= cur, scores, -1.0))
    blk_f = blk.astype(F32)
    sel = jnp.zeros((nb, QB), F32)
    for _ in range(topk):
        mx = jnp.max(work, axis=0, keepdims=True)
        first = jnp.min(jnp.where(work == mx, blk_f, float(nb)), axis=0, keepdims=True)
        hit = blk_f == first
        sel = jnp.where(hit, 1.0, sel)
        work = jnp.where(hit, -3e38, work)
    sel_ref[0] = sel


def _cmp_select(qt, kc, vct, tab, c31, ovl_t, n_valid):
    G, nkb = qt.shape[0], qt.shape[1]
    NC = kc.shape[1]
    NB = ovl_t.shape[0]
    return pl.pallas_call(
        functools.partial(_cmp_select_kernel, n_valid=n_valid, topk=min(SLC_TOPK, NB)),
        grid=(G, nkb),
        in_specs=[pl.BlockSpec((1, 1, HEAD_DIM, QW), lambda g, c: (g, c, 0, 0)),
                  pl.BlockSpec((1, NC, HEAD_DIM), lambda g, c: (g, 0, 0)),
                  pl.BlockSpec((1, HEAD_DIM, NC), lambda g, c: (g, 0, 0)),
                  pl.BlockSpec((1, CMP_TAB, QW), lambda g, c: (g, 0, 0)),
                  pl.BlockSpec((1, 1, QW), lambda g, c: (g, 0, 0)),
                  pl.BlockSpec((NB, NC), lambda g, c: (0, 0))],
        out_specs=[pl.BlockSpec((1, 1, HEAD_DIM, QW), lambda g, c: (g, c, 0, 0)),
                   pl.BlockSpec((1, NB, QB), lambda g, c: (g, 0, c))],
        out_shape=[jax.ShapeDtypeStruct((G, nkb, HEAD_DIM, QW), F32),
                   jax.ShapeDtypeStruct((G, NB, nkb * QB), F32)],
        scratch_shapes=[pltpu.VMEM((NC, QW), F32)],
        compiler_params=_cparams(("arbitrary", "arbitrary")),
        name="cmp_attention_select",
    )(qt, kc, vct, tab, c31, ovl_t)


FAR_BLOCKS = 4


def _sel_kernel(qt_ref, k_ref, vt_ref, tab_ref, c31_ref, sel_ref, gate_ref, ocmp_ref, owin_ref,
                o_ref, m_ref, l_ref, acc_ref):
    c = pl.program_id(1)
    _init_state(m_ref, l_ref, acc_ref)
    kk, qq = _key_query_iotas()
    qt = qt_ref[0, 0]
    c31 = c31_ref[0]
    spb = QB // SLC_BLOCK

    def sel_mask(kb0, n):
        pieces = []
        for t in range(n * spb):
            row = sel_ref[0, pl.ds(kb0 * spb + t, 1), :]
            row = jnp.concatenate([row] * GROUP, axis=1)
            pieces.append(jnp.broadcast_to(row, (SLC_BLOCK, QW)))
        return jnp.concatenate(pieces, axis=0) > 0.5

    def far(kb0, n):
        k_t, vt = _load_kv(k_ref, vt_ref, kb0, n)
        s = jnp.where(sel_mask(kb0, n), _scores(k_t, qt) + c31, NEG)
        _online_update(m_ref, l_ref, acc_ref, s, vt)

    def near(kb0, deltas):
        k_t, vt = _load_kv(k_ref, vt_ref, kb0, len(deltas))
        s = _scores(k_t, qt)
        parts = []
        for t, delta in enumerate(deltas):
            msk = sel_mask(kb0 + t, 1)
            if delta == 0:
                msk = msk & (qq >= kk)
            parts.append(jnp.where(msk, s[t * QB:(t + 1) * QB] + tab_ref[0, delta], NEG))
        s = parts[0] if len(parts) == 1 else jnp.concatenate(parts, axis=0)
        _online_update(m_ref, l_ref, acc_ref, s, vt)

    n_far = jnp.maximum(c - 1, 0)
    n_full = n_far // FAR_BLOCKS

    def full_body(i, carry):
        far(i * FAR_BLOCKS, FAR_BLOCKS)
        return carry

    def rest_body(kb, carry):
        far(kb, 1)
        return carry

    lax.fori_loop(0, n_full, full_body, 0)
    lax.fori_loop(n_full * FAR_BLOCKS, n_far, rest_body, 0)

    @pl.when(c >= 1)
    def _():
        near(c - 1, [1, 0])

    @pl.when(c == 0)
    def _():
        near(0, [0])

    o_slc = acc_ref[...] * (1.0 / l_ref[...])
    o_t = (gate_ref[0, 0, 0:1, :] * ocmp_ref[0, 0] + gate_ref[0, 0, 1:2, :] * o_slc
           + gate_ref[0, 0, 2:3, :] * owin_ref[0, 0])
    _store_heads(o_ref, o_t)


def _sel_attention(qt, proj, k_col, vt4, tabs, c31, sel_t, gates_s, ocmp_t, owin_t):
    S = proj.shape[0]
    G, nkb = vt4.shape[0], vt4.shape[1]
    NB = sel_t.shape[1]
    tile = pl.BlockSpec((1, 1, HEAD_DIM, QW), lambda g, c: (g, c, 0, 0))
    return pl.pallas_call(
        _sel_kernel,
        grid=(G, nkb),
        in_specs=[tile,
                  pl.BlockSpec((S, HEAD_DIM), lambda g, c: (0, k_col + g)),
                  pl.BlockSpec((1, nkb, HEAD_DIM, QB), lambda g, c: (g, 0, 0, 0)),
                  pl.BlockSpec((1, 2, QB, QW), lambda g, c: (g, 0, 0, 0)),
                  pl.BlockSpec((1, 1, QW), lambda g, c: (g, 0, 0)),
                  pl.BlockSpec((1, NB, QB), lambda g, c: (g, 0, c)),
                  pl.BlockSpec((1, 1, N_BRANCH, QW), lambda g, c: (g, c, 0, 0)),
                  tile, tile],
        out_specs=pl.BlockSpec((QB, GROUP * HEAD_DIM), lambda g, c: (c, g)),
        out_shape=jax.ShapeDtypeStruct((S, G * GROUP * HEAD_DIM), F32),
        scratch_shapes=[pltpu.VMEM((1, QW), F32), pltpu.VMEM((1, QW), F32),
                        pltpu.VMEM((HEAD_DIM, QW), F32)],
        compiler_params=_cparams(("arbitrary", "arbitrary")),
        name="selected_attention_combine",
    )(qt, proj, vt4, tabs, c31, sel_t, gates_s, ocmp_t, owin_t)


def _outproj_kernel(a_ref, w_ref, x_ref, o_ref):
    o_ref[...] = x_ref[...] + jnp.dot(a_ref[...], w_ref[...], preferred_element_type=F32)


def _outproj(a, w, x, tm=1024, tn=512):
    S, D = a.shape
    N = w.shape[1]
    tm = min(tm, S)
    return pl.pallas_call(
        _outproj_kernel,
        grid=(S // tm, N // tn),
        in_specs=[pl.BlockSpec((tm, D), lambda i, j: (i, 0)),
                  pl.BlockSpec((D, tn), lambda i, j: (0, j)),
                  pl.BlockSpec((tm, tn), lambda i, j: (i, j))],
        out_specs=pl.BlockSpec((tm, tn), lambda i, j: (i, j)),
        out_shape=jax.ShapeDtypeStruct((S, N), F32),
        compiler_params=_cparams(("arbitrary", "arbitrary")),
        name="outproj_residual",
    )(a, w, x)


HALO = 16


def _ffn_a_kernel(halo_ref, a_ref, wg_ref, wu_ref, cw_ref, cb_ref, o_ref):
    i = pl.program_id(0)
    tm = a_ref.shape[0]
    a = a_ref[...]
    halo = halo_ref[...]
    halo = jnp.where(i > 0, halo, jnp.zeros_like(halo))
    gate = jnp.dot(a, wg_ref[...], preferred_element_type=F32)
    ghalo = jnp.dot(halo, wg_ref[...], preferred_element_type=F32)
    up = jnp.dot(a, wu_ref[...], preferred_element_type=F32)
    gext = jnp.concatenate([ghalo, gate], axis=0)
    g1 = gext[HALO - 1:HALO - 1 + tm]
    g2 = gext[HALO - 2:HALO - 2 + tm]
    y = cb_ref[...] + g2 * cw_ref[0:1, :]
    y = y + g1 * cw_ref[1:2, :]
    y = y + gate * cw_ref[2:3, :]
    act = y * (1.0 / (1.0 + jnp.exp(-y)))
    o_ref[...] = (act * up).astype(o_ref.dtype)


def _ffn_a(hf, wg, wu, cw, cb, tm=1024, tf=512):
    S, D = hf.shape
    Fp = wg.shape[1]
    tm = min(tm, S)
    hb = tm // HALO
    return pl.pallas_call(
        _ffn_a_kernel,
        grid=(S // tm, Fp // tf),
        in_specs=[pl.BlockSpec((HALO, D), lambda i, f: (jnp.maximum(i * hb - 1, 0), 0)),
                  pl.BlockSpec((tm, D), lambda i, f: (i, 0)),
                  pl.BlockSpec((D, tf), lambda i, f: (0, f)),
                  pl.BlockSpec((D, tf), lambda i, f: (0, f)),
                  pl.BlockSpec((cw.shape[0], tf), lambda i, f: (0, f)),
                  pl.BlockSpec((1, tf), lambda i, f: (0, f))],
        out_specs=pl.BlockSpec((tm, tf), lambda i, f: (i, f)),
        out_shape=jax.ShapeDtypeStruct((S, Fp), BF16),
        compiler_params=_cparams(("arbitrary", "arbitrary")),
        name="ffn_gate_up",
    )(hf, hf, wg, wu, cw, cb)


def _ffn_b_kernel(a_ref, w_ref, x_ref, o_ref, acc_ref):
    k = pl.program_id(2)
    prod = jnp.dot(a_ref[...], w_ref[...], preferred_element_type=F32)

    @pl.when(k == 0)
    def _():
        acc_ref[...] = x_ref[...] + prod

    @pl.when(k > 0)
    def _():
        acc_ref[...] += prod

    @pl.when(k == pl.num_programs(2) - 1)
    def _():
        o_ref[...] = acc_ref[...]


def _ffn_b(h, w, x, tm=1024, tn=1024):
    S, Fp = h.shape
    N = w.shape[1]
    tm = min(tm, S)
    tk = 1408 if Fp % 1408 == 0 else 512
    return pl.pallas_call(
        _ffn_b_kernel,
        grid=(S // tm, N // tn, Fp // tk),
        in_specs=[pl.BlockSpec((tm, tk), lambda i, j, k: (i, k)),
                  pl.BlockSpec((tk, tn), lambda i, j, k: (k, j)),
                  pl.BlockSpec((tm, tn), lambda i, j, k: (i, j))],
        out_specs=pl.BlockSpec((tm, tn), lambda i, j, k: (i, j)),
        out_shape=jax.ShapeDtypeStruct((S, N), F32),
        scratch_shapes=[pltpu.VMEM((tm, tn), F32)],
        compiler_params=_cparams(("arbitrary", "arbitrary", "arbitrary")),
        name="ffn_down_residual",
    )(h, w, x)


def _values_t(v, G):
    S = v.shape[0]
    return v.reshape(S // QB, QB, G, HEAD_DIM).transpose(2, 0, 3, 1)


def _layer(x, rel_bias, norm_mix_g, w_in, a_q_norm_g, a_k_norm_g, a_sinks, b_q_norm_g, b_k_norm_g,
           cmp_pos_emb, cmp_w1, cmp_b1, cmp_w2, cmp_b2, out_norm_g, w_out, norm_ffn_g, w_gate,
           w_up, conv_w, conv_b, w_down):
    S, D = x.shape
    aw = A_Q_HEADS * HEAD_DIM
    akv = A_KV_HEADS * HEAD_DIM
    bw = B_Q_HEADS * HEAD_DIM
    bkv = B_KV_HEADS * HEAD_DIM
    sizes = [aw, akv, akv, bw] + [bkv] * 6 + [N_BRANCH * B_Q_HEADS]
    offs = np.concatenate([[0], np.cumsum(sizes)]).tolist()
    seg = lambda k: w_in[:, offs[k]:offs[k + 1]]
    tn = 512
    order_a = [0, 1, 2, 3, 6, 7, 8, 9]
    w_a = jnp.concatenate([seg(k) for k in order_a], axis=1).astype(BF16)
    n_gate = sizes[10]
    w_b = jnp.concatenate([seg(4), seg(5), seg(10),
                           jnp.zeros((D, QB - n_gate), w_in.dtype)], axis=1).astype(BF16)
    one = jnp.ones((HEAD_DIM,), F32)
    seg_gain = {0: a_q_norm_g, 1: a_k_norm_g, 3: b_q_norm_g, 6: b_k_norm_g[1], 8: b_k_norm_g[2]}
    kinds, gains, col = [], [], {}
    c0 = 0
    for k in order_a:
        col[k] = c0
        for _ in range(sizes[k] // tn):
            kinds.append(1 if k in seg_gain else 0)
            gains.append(seg_gain.get(k, one))
        c0 += sizes[k]
    kinds = jnp.asarray(kinds, jnp.int32)
    gains = jnp.stack(gains).astype(F32).reshape(len(gains), 1, HEAD_DIM)

    hn = _rmsnorm([x], norm_mix_g)
    proj = _inproj_a(hn, w_a, kinds, gains, tn=tn)
    kv32, gates = _inproj_b(hn, w_b, 2 * bkv)

    tab_a = rel_bias[:, :A_Q_HEADS]
    tab_b = rel_bias[:, A_Q_HEADS:]
    near_a = _stack_heads(_near_bias_tables(tab_a), A_KV_HEADS)
    near_b = _stack_heads(_near_bias_tables(tab_b), B_KV_HEADS)
    head_row = lambda v, G: _stack_heads(jnp.broadcast_to(v.astype(F32)[:, None, None],
                                                          (v.shape[0], 1, QB)), G)
    c31_a = head_row(tab_a[N_BUCKETS - 1], A_KV_HEADS)
    c31_b = head_row(tab_b[N_BUCKETS - 1], B_KV_HEADS)

    qt_a = _stack_queries(proj[:, col[0]:col[0] + aw], A_KV_HEADS)
    vt_a = _values_t(proj[:, col[2]:col[2] + akv], A_KV_HEADS)
    o_a = _band_attention(qt_a, proj, col[1] // HEAD_DIM, vt_a, near_a, c31_a,
                          head_row(a_sinks, A_KV_HEADS), A_WINDOW, transposed_out=False)

    NC = S // CMP_STRIDE
    n_cmp = (S - CMP_BLOCK) // CMP_STRIDE + 1
    tkv = kv32.reshape(NC, CMP_STRIDE, 2, B_KV_HEADS, HEAD_DIM).transpose(2, 3, 0, 1, 4)
    tkv = tkv.reshape(2, B_KV_HEADS, NC, CMP_STRIDE * HEAD_DIM)
    halves = CMP_BLOCK // CMP_STRIDE
    pos = cmp_pos_emb.reshape(2, halves, CMP_STRIDE * HEAD_DIM).astype(F32)
    w1 = cmp_w1.reshape(2, halves, CMP_STRIDE * HEAD_DIM, cmp_w1.shape[-1]).astype(BF16)
    kc, vct = _compress(tkv, pos, w1, cmp_b1[:, None, :].astype(F32), cmp_w2.astype(BF16),
                        cmp_b2[:, None, :].astype(F32), b_k_norm_g[0].reshape(1, HEAD_DIM).astype(F32))

    NB = S // SLC_BLOCK
    ii = np.arange(NC)[None, :]
    jj = np.arange(NB)[:, None]
    ovl_t = ((ii * CMP_STRIDE <= jj * SLC_BLOCK + SLC_BLOCK - 1)
             & (ii * CMP_STRIDE + CMP_BLOCK - 1 >= jj * SLC_BLOCK) & (ii < n_cmp))
    ovl_t = jnp.asarray(ovl_t.astype(np.float32), BF16)
    qt_b = _stack_queries(proj[:, col[3]:col[3] + bw], B_KV_HEADS)
    ocmp_t, sel_t = _cmp_select(qt_b, kc, vct, _stack_heads(_cmp_bias_table(tab_b), B_KV_HEADS),
                                c31_b, ovl_t, n_cmp)

    vt_w = _values_t(proj[:, col[9]:col[9] + bkv], B_KV_HEADS)
    owin_t = _band_attention(qt_b, proj, col[8] // HEAD_DIM, vt_w, near_b, c31_b, None,
                             B_WINDOW, transposed_out=True)
    vt_s = _values_t(proj[:, col[7]:col[7] + bkv], B_KV_HEADS)
    gates_s = gates[:, :n_gate].reshape(S // QB, QB, B_KV_HEADS, GROUP, N_BRANCH)
    gates_s = gates_s.transpose(2, 0, 4, 3, 1).reshape(B_KV_HEADS, S // QB, N_BRANCH, QW)
    o_b = _sel_attention(qt_b, proj, col[6] // HEAD_DIM, vt_s, near_b, c31_b, sel_t, gates_s,
                         ocmp_t, owin_t)

    on = _rmsnorm([o_a, o_b], out_norm_g)
    x2 = _outproj(on, w_out.astype(BF16), x)

    F = w_gate.shape[1]
    tf = 512
    Fp = -(-F // tf) * tf
    padc = lambda w: jnp.pad(w, ((0, 0), (0, Fp - F)))
    hf = _rmsnorm([x2], norm_ffn_g)
    hmid = _ffn_a(hf, padc(w_gate).astype(BF16), padc(w_up).astype(BF16),
                  padc(conv_w).astype(F32), padc(conv_b[None, :]).astype(F32), tf=tf)
    return _ffn_b(hmid, jnp.pad(w_down, ((0, Fp - F), (0, 0))).astype(BF16), x2)


def kernel(x, rel_bias, norm_mix_g, w_in, a_q_norm_g, a_k_norm_g, a_sinks, b_q_norm_g, b_k_norm_g,
           cmp_pos_emb, cmp_w1, cmp_b1, cmp_w2, cmp_b2, out_norm_g, w_out, norm_ffn_g, w_gate, w_up,
           conv_w, conv_b, w_down):
    depth = w_in.shape[0]
    batch = x.shape[0]
    outs = []
    for b in range(batch):
        h = x[b]
        for l in range(depth):
            h = _layer(h, rel_bias, norm_mix_g[l], w_in[l], a_q_norm_g[l], a_k_norm_g[l], a_sinks[l],
                       b_q_norm_g[l], b_k_norm_g[l], cmp_pos_emb[l], cmp_w1[l], cmp_b1[l], cmp_w2[l],
                       cmp_b2[l], out_norm_g[l], w_out[l], norm_ffn_g[l], w_gate[l], w_up[l],
                       conv_w[l], conv_b[l], w_down[l])
        outs.append(h)
    return jnp.stack(outs)
```

```python
import functools
import math

import numpy as np
import jax
import jax.numpy as jnp
from jax import lax
from jax.experimental import pallas as pl
from jax.experimental.pallas import tpu as pltpu

F32 = jnp.float32
BF16 = jnp.bfloat16

HEAD_DIM = 128
A_Q_HEADS = 16
A_KV_HEADS = 4
B_Q_HEADS = 16
B_KV_HEADS = 4
GROUP = 4
A_WINDOW = 128
B_WINDOW = 512
CMP_BLOCK = 32
CMP_STRIDE = 16
SLC_BLOCK = 64
SLC_TOPK = 16
N_BRANCH = 3
N_BUCKETS = 32
MAX_DISTANCE = 128
EPS = 1e-6
NEG = -1e30
FORCE_SCORE = 1e6
SCALE = HEAD_DIM ** -0.5

QB = 128
CMP_WIN = 24
CMP_TAB = 40
VMEM_LIMIT = 56 * 1024 * 1024


def _cparams(sem):
    return pltpu.CompilerParams(dimension_semantics=sem, vmem_limit_bytes=VMEM_LIMIT)


def _t5_bucket_np(dist):
    n = np.maximum(dist, 0)
    max_exact = N_BUCKETS // 2
    nf = np.maximum(n, 1).astype(np.float32)
    large = max_exact + (np.log(nf / max_exact) / math.log(MAX_DISTANCE / max_exact)
                         * (N_BUCKETS - max_exact)).astype(np.int32)
    large = np.minimum(large, N_BUCKETS - 1)
    return np.where(n < max_exact, n, large).astype(np.int32)


def _bias_from_buckets(tab, idx):
    onehot = (idx[None] == np.arange(N_BUCKETS).reshape((-1,) + (1,) * idx.ndim)).astype(np.float32)
    return jnp.einsum('bh,b...->h...', tab.astype(F32), jnp.asarray(onehot),
                      precision=lax.Precision.HIGHEST)


def _near_bias_tables(tab):
    kk = np.arange(QB)[:, None]
    qq = np.arange(QB)[None, :]
    return _bias_from_buckets(tab, np.stack([_t5_bucket_np(qq - kk), _t5_bucket_np(QB + qq - kk)]))


def _cmp_bias_table(tab):
    npr = np.arange(CMP_TAB)[:, None] - 16
    qq = np.arange(QB)[None, :]
    return _bias_from_buckets(tab, _t5_bucket_np(qq - CMP_STRIDE * npr - (CMP_BLOCK - 1)))


def _rmsnorm_kernel(*refs, n_in):
    x_refs, g_ref, o_ref = refs[:n_in], refs[n_in], refs[n_in + 1]
    off = 0
    for x_ref in x_refs:
        x = x_ref[...]
        w = x.shape[-1]
        y = x * lax.rsqrt(jnp.mean(x * x, axis=-1, keepdims=True) + EPS)
        o_ref[:, off:off + w] = (y * g_ref[:, off:off + w]).astype(o_ref.dtype)
        off += w


def _rmsnorm(xs, gain, tr=256):
    S = xs[0].shape[0]
    widths = [x.shape[1] for x in xs]
    n = sum(widths)
    tr = min(tr, S)
    return pl.pallas_call(
        functools.partial(_rmsnorm_kernel, n_in=len(xs)),
        grid=(S // tr,),
        in_specs=[pl.BlockSpec((tr, w), lambda i: (i, 0)) for w in widths]
        + [pl.BlockSpec((1, n), lambda i: (0, 0))],
        out_specs=pl.BlockSpec((tr, n), lambda i: (i, 0)),
        out_shape=jax.ShapeDtypeStruct((S, n), BF16),
        compiler_params=_cparams(("arbitrary",)),
        name="rmsnorm",
    )(*xs, gain.reshape(1, n).astype(F32))


def _inproj_a_kernel(kind_ref, tile_ref, a_ref, w_ref, g_ref, o_ref):
    j = pl.program_id(1)
    acc = jnp.dot(a_ref[...], w_ref[...].astype(BF16), preferred_element_type=F32)

    @pl.when(kind_ref[j] == 0)
    def _():
        o_ref[...] = acc.astype(o_ref.dtype)

    @pl.when(kind_ref[j] == 1)
    def _():
        g = g_ref[0]
        for h in range(acc.shape[1] // HEAD_DIM):
            sl = acc[:, h * HEAD_DIM:(h + 1) * HEAD_DIM]
            y = sl * lax.rsqrt(jnp.mean(sl * sl, axis=-1, keepdims=True) + EPS)
            o_ref[:, h * HEAD_DIM:(h + 1) * HEAD_DIM] = (y * g).astype(o_ref.dtype)


def _inproj_a(hn, w, kinds, tiles, gains, tm=1024, tn=512):
    S, D = hn.shape
    n_tiles = kinds.shape[0]
    tm = min(tm, S)
    grid_spec = pltpu.PrefetchScalarGridSpec(
        num_scalar_prefetch=2,
        grid=(S // tm, n_tiles),
        in_specs=[pl.BlockSpec((tm, D), lambda i, j, k, t: (i, 0)),
                  pl.BlockSpec((D, tn), lambda i, j, k, t: (0, t[j])),
                  pl.BlockSpec((1, 1, HEAD_DIM), lambda i, j, k, t: (j, 0, 0))],
        out_specs=pl.BlockSpec((tm, tn), lambda i, j, k, t: (i, j)),
    )
    return pl.pallas_call(
        _inproj_a_kernel,
        grid_spec=grid_spec,
        out_shape=jax.ShapeDtypeStruct((S, n_tiles * tn), BF16),
        compiler_params=_cparams(("arbitrary", "arbitrary")),
        name="inproj_heads",
    )(kinds, tiles, hn, w, gains)


def _inproj_b_kernel(a_ref, w_ref, o_ref, gate_ref):
    acc = jnp.dot(a_ref[...], w_ref[...], preferred_element_type=F32)
    n = o_ref.shape[1]
    o_ref[...] = acc[:, :n]
    z = acc[:, n:]
    gate_ref[...] = 1.0 / (1.0 + jnp.exp(-z))


def _inproj_b(hn, w, n_main, tm=512):
    S, D = hn.shape
    N = w.shape[1]
    tm = min(tm, S)
    return pl.pallas_call(
        _inproj_b_kernel,
        grid=(S // tm,),
        in_specs=[pl.BlockSpec((tm, D), lambda i: (i, 0)),
                  pl.BlockSpec((D, N), lambda i: (0, 0))],
        out_specs=[pl.BlockSpec((tm, n_main), lambda i: (i, 0)),
                   pl.BlockSpec((tm, N - n_main), lambda i: (i, 0))],
        out_shape=[jax.ShapeDtypeStruct((S, n_main), F32),
                   jax.ShapeDtypeStruct((S, N - n_main), F32)],
        compiler_params=_cparams(("arbitrary",)),
        name="inproj_cmp_gates",
    )(hn, w)


def _gelu_tanh(x):
    return 0.5 * x * (1.0 + jnp.tanh(math.sqrt(2.0 / math.pi) * (x + 0.044715 * (x * x * x))))


def _compress_kernel(tk_ref, tv_ref, pos_ref, w1_ref, b1_ref, w2_ref, b2_ref, gk_ref,
                     kc_ref, vct_ref):
    nc = kc_ref.shape[1]

    def mlp(t_ref, kv):
        a0 = jnp.zeros((nc, w1_ref.shape[-1]), F32)
        a1 = jnp.zeros((nc, w1_ref.shape[-1]), F32)
        for b in range(CMP_STRIDE):
            t = t_ref[pl.ds(b, nc, stride=CMP_STRIDE), :]
            lo, hi = b, CMP_STRIDE + b
            a0 = a0 + jnp.dot((t + pos_ref[kv, lo:lo + 1, :]).astype(BF16), w1_ref[kv, lo],
                              preferred_element_type=F32)
            a1 = a1 + jnp.dot((t + pos_ref[kv, hi:hi + 1, :]).astype(BF16), w1_ref[kv, hi],
                              preferred_element_type=F32)
        h = a0 + pltpu.roll(a1, nc - 1, 0) + b1_ref[kv]
        h = _gelu_tanh(h)
        return jnp.dot(h.astype(BF16), w2_ref[kv], preferred_element_type=F32) + b2_ref[kv]

    ck = mlp(tk_ref, 0)
    ck = ck * lax.rsqrt(jnp.mean(ck * ck, axis=-1, keepdims=True) + EPS) * gk_ref[...]
    kc_ref[0] = ck.astype(kc_ref.dtype)
    cv = mlp(tv_ref, 1)
    vct_ref[0] = cv.T.astype(vct_ref.dtype)


def _compress(kv32, G, pos, w1, b1, w2, b2, gk):
    S = kv32.shape[0]
    NC = S // CMP_STRIDE
    hid = w1.shape[-1]
    return pl.pallas_call(
        _compress_kernel,
        grid=(G,),
        in_specs=[pl.BlockSpec((S, HEAD_DIM), lambda g: (0, g)),
                  pl.BlockSpec((S, HEAD_DIM), lambda g: (0, G + g)),
                  pl.BlockSpec((2, CMP_BLOCK, HEAD_DIM), lambda g: (0, 0, 0)),
                  pl.BlockSpec((2, CMP_BLOCK, HEAD_DIM, hid), lambda g: (0, 0, 0, 0)),
                  pl.BlockSpec((2, 1, hid), lambda g: (0, 0, 0)),
                  pl.BlockSpec((2, hid, HEAD_DIM), lambda g: (0, 0, 0)),
                  pl.BlockSpec((2, 1, HEAD_DIM), lambda g: (0, 0, 0)),
                  pl.BlockSpec((1, HEAD_DIM), lambda g: (0, 0))],
        out_specs=[pl.BlockSpec((1, NC, HEAD_DIM), lambda g: (g, 0, 0)),
                   pl.BlockSpec((1, HEAD_DIM, NC), lambda g: (g, 0, 0))],
        out_shape=[jax.ShapeDtypeStruct((G, NC, HEAD_DIM), BF16),
                   jax.ShapeDtypeStruct((G, HEAD_DIM, NC), BF16)],
        compiler_params=_cparams(("arbitrary",)),
        name="compress",
    )(kv32, kv32, pos, w1, b1, w2, b2, gk)


QW = GROUP * QB


def _stacked_queries(q_ref):
    return jnp.concatenate([q_ref[:, r * HEAD_DIM:(r + 1) * HEAD_DIM] for r in range(GROUP)], axis=0)


LOG2E = math.log2(math.e)
NSPLIT = 2
HW = QW // NSPLIT


def _halves():
    return [slice(h * HW, (h + 1) * HW) for h in range(NSPLIT)]


def _scores(k_t, qs):
    s = lax.dot_general(k_t, qs, (((1,), (1,)), ((), ())), preferred_element_type=F32)
    return s * (SCALE * LOG2E)


def _rel_bias(tab, c31):
    return (tab - c31) * LOG2E


def _init_state(m_ref, l_ref, acc_ref):
    m_ref[...] = jnp.full(m_ref.shape, NEG, F32)
    l_ref[...] = jnp.zeros(l_ref.shape, F32)
    acc_ref[...] = jnp.zeros(acc_ref.shape, F32)


def _online_update(m_ref, l_ref, acc_ref, lanes, t, v_t):
    m_old = m_ref[:, lanes]
    m_new = jnp.maximum(m_old, jnp.max(t, axis=0, keepdims=True))
    alpha = jnp.exp2(m_old - m_new)
    p = jnp.exp2(t - m_new)
    l_ref[:, lanes] = alpha * l_ref[:, lanes] + jnp.sum(p, axis=0, keepdims=True)
    pv = lax.dot_general(v_t, p.astype(BF16), (((0,), (0,)), ((), ())), preferred_element_type=F32)
    acc_ref[:, lanes] = acc_ref[:, lanes] * alpha + pv
    m_ref[:, lanes] = m_new


def _key_query_iotas():
    kk = lax.broadcasted_iota(jnp.int32, (QB, HW), 0)
    qq = lax.broadcasted_iota(jnp.int32, (QB, HW), 1) & (QB - 1)
    return kk, qq


def _load_kv(k_ref, v_ref, kb0, n):
    rows = pl.ds(pl.multiple_of(kb0 * QB, QB), n * QB)
    return k_ref[rows, :], v_ref[rows, :]


def _store_heads(o_ref, o_t):
    for r in range(GROUP):
        o_ref[:, r * HEAD_DIM:(r + 1) * HEAD_DIM] = o_t[:, r * QB:(r + 1) * QB].T


def _stack_heads(a, G):
    a = a.reshape((G, GROUP) + a.shape[1:])
    a = jnp.moveaxis(a, 1, -2)
    return a.reshape(a.shape[:-2] + (QW,))


def _band_kernel(q_ref, k_ref, v_ref, tab_ref, c31_ref, sink_ref, o_ref, m_ref, l_ref, acc_ref,
                 *, nback, use_sinks, transposed_out):
    c = pl.program_id(1)
    _init_state(m_ref, l_ref, acc_ref)
    kk, qq = _key_query_iotas()
    qs = _stacked_queries(q_ref)
    c31 = c31_ref[0]

    def run(kb0, deltas):
        k_t, v_t = _load_kv(k_ref, v_ref, kb0, len(deltas))
        for lanes in _halves():
            s = _scores(k_t, qs[lanes])
            parts = []
            for t, delta in enumerate(deltas):
                sb = s[t * QB:(t + 1) * QB]
                if delta <= 1:
                    sb = sb + _rel_bias(tab_ref[0, delta, :, lanes], c31[:, lanes])
                if delta == 0:
                    sb = jnp.where(qq >= kk, sb, NEG)
                elif delta == nback:
                    sb = jnp.where(qq < kk, sb, NEG)
                parts.append(sb)
            s = parts[0] if len(parts) == 1 else jnp.concatenate(parts, axis=0)
            _online_update(m_ref, l_ref, acc_ref, lanes, s, v_t)

    @pl.when(c >= nback)
    def _():
        run(c - nback, list(range(nback, -1, -1)))

    @pl.when(c < nback)
    def _():
        for delta in range(nback - 1, -1, -1):
            @pl.when(c - delta >= 0)
            def _(delta=delta):
                run(c - delta, [delta])

    l = l_ref[...]
    acc = acc_ref[...]
    if use_sinks:
        m = m_ref[...]
        sk = _rel_bias(sink_ref[0], c31)
        m_f = jnp.maximum(m, sk)
        a = jnp.exp2(m - m_f)
        l = l * a + jnp.exp2(sk - m_f)
        acc = acc * a
    o_t = acc * (1.0 / l)
    if transposed_out:
        o_ref[0, 0] = o_t
    else:
        _store_heads(o_ref, o_t)


def _qkv_specs(S, q_col, k_col, v_col):
    return [pl.BlockSpec((QB, GROUP * HEAD_DIM), lambda g, c: (c, q_col + g)),
            pl.BlockSpec((S, HEAD_DIM), lambda g, c: (0, k_col + g)),
            pl.BlockSpec((S, HEAD_DIM), lambda g, c: (0, v_col + g))]


def _band_attention(proj, G, q_col, k_col, v_col, tabs, c31, sinks, window, transposed_out):
    S = proj.shape[0]
    nkb = S // QB
    nback = -(-(window - 1) // QB)
    use_sinks = sinks is not None
    if sinks is None:
        sinks = jnp.zeros_like(c31)
    if transposed_out:
        out_shape = jax.ShapeDtypeStruct((G, nkb, HEAD_DIM, QW), F32)
        out_spec = pl.BlockSpec((1, 1, HEAD_DIM, QW), lambda g, c: (g, c, 0, 0))
    else:
        out_shape = jax.ShapeDtypeStruct((S, G * GROUP * HEAD_DIM), F32)
        out_spec = pl.BlockSpec((QB, GROUP * HEAD_DIM), lambda g, c: (c, g))
    row = pl.BlockSpec((1, 1, QW), lambda g, c: (g, 0, 0))
    return pl.pallas_call(
        functools.partial(_band_kernel, nback=nback, use_sinks=use_sinks,
                          transposed_out=transposed_out),
        grid=(G, nkb),
        in_specs=_qkv_specs(S, q_col, k_col, v_col)
        + [pl.BlockSpec((1, 2, QB, QW), lambda g, c: (g, 0, 0, 0)), row, row],
        out_specs=out_spec,
        out_shape=out_shape,
        scratch_shapes=[pltpu.VMEM((1, QW), F32), pltpu.VMEM((1, QW), F32),
                        pltpu.VMEM((HEAD_DIM, QW), F32)],
        compiler_params=_cparams(("arbitrary", "arbitrary")),
        name="band_attention_w%d" % window,
    )(proj, proj, proj, tabs, c31, sinks)


def _cmp_select_kernel(q_ref, kc_ref, vct_ref, tab_ref, c31_ref, ovl_ref, ot_ref, sel_ref, s_ref,
                       *, n_valid, topk):
    c = pl.program_id(1)
    nc = kc_ref.shape[1]
    nb = ovl_ref.shape[0]
    qt = _stacked_queries(q_ref)

    w0 = pl.multiple_of(jnp.maximum(8 * c - 16, 0), 8)
    toff = pl.multiple_of(w0 - 8 * c + 16, 8)
    rown = lax.broadcasted_iota(jnp.int32, (nc, HW), 0)
    n_abs = w0 + lax.broadcasted_iota(jnp.int32, (CMP_WIN, HW), 0)
    wq = lax.broadcasted_iota(jnp.int32, (CMP_WIN, HW), 1) & (QB - 1)
    dist = (c * QB + wq) - (n_abs * CMP_STRIDE + (CMP_BLOCK - 1))
    valid_w = (dist >= 0) & (n_abs < n_valid)
    qpos = c * QB + (lax.broadcasted_iota(jnp.int32, (1, HW), 1) & (QB - 1))
    has_any = (qpos >= CMP_BLOCK - 1).astype(F32)
    kc_w = kc_ref[0, pl.ds(w0, CMP_WIN), :]

    p_heads = []
    for hv, lanes in enumerate(_halves()):
        s_ref[hv] = jnp.where(rown < w0, _scores(kc_ref[0], qt[lanes]), NEG)
        bias_w = _rel_bias(tab_ref[0, pl.ds(toff, CMP_WIN), lanes], c31_ref[0][:, lanes])
        s_ref[hv, pl.ds(w0, CMP_WIN), :] = jnp.where(valid_w, _scores(kc_w, qt[lanes]) + bias_w, NEG)
        s = s_ref[hv]
        m = jnp.max(s, axis=0, keepdims=True)
        e = jnp.exp2(s - m)
        p = e * (has_any / jnp.sum(e, axis=0, keepdims=True))
        ot_ref[0, 0, :, lanes] = jnp.dot(vct_ref[0], p.astype(BF16), preferred_element_type=F32)
        p_heads += [p[:, r * QB:(r + 1) * QB] for r in range(HW // QB)]
    p_grp = p_heads[0]
    for p_r in p_heads[1:]:
        p_grp = p_grp + p_r

    scores = jnp.dot(ovl_ref[...], p_grp.astype(BF16), preferred_element_type=F32)
    blk = lax.broadcasted_iota(jnp.int32, (nb, QB), 0)
    cur = lax.shift_right_logical(c * QB + lax.broadcasted_iota(jnp.int32, (nb, QB), 1),
                                  int(math.log2(SLC_BLOCK)))
    forced = (blk == 0) | (blk == cur) | (blk == cur - 1)
    work = jnp.where(forced, FORCE_SCORE, jnp.where(blk <= cur, scores, -1.0))
    blk_f = blk.astype(F32)
    sel = jnp.zeros((nb, QB), F32)
    for _ in range(topk):
        mx = jnp.max(work, axis=0, keepdims=True)
        first = jnp.min(jnp.where(work == mx, blk_f, float(nb)), axis=0, keepdims=True)
        hit = blk_f == first
        sel = jnp.where(hit, 1.0, sel)
        work = jnp.where(hit, -3e38, work)
    sel_ref[0] = sel


def _cmp_select(proj, q_col, kc, vct, tab, c31, ovl_t, n_valid):
    nkb = proj.shape[0] // QB
    G, NC = kc.shape[0], kc.shape[1]
    NB = ovl_t.shape[0]
    return pl.pallas_call(
        functools.partial(_cmp_select_kernel, n_valid=n_valid, topk=min(SLC_TOPK, NB)),
        grid=(G, nkb),
        in_specs=[pl.BlockSpec((QB, GROUP * HEAD_DIM), lambda g, c: (c, q_col + g)),
                  pl.BlockSpec((1, NC, HEAD_DIM), lambda g, c: (g, 0, 0)),
                  pl.BlockSpec((1, HEAD_DIM, NC), lambda g, c: (g, 0, 0)),
                  pl.BlockSpec((1, CMP_TAB, QW), lambda g, c: (g, 0, 0)),
                  pl.BlockSpec((1, 1, QW), lambda g, c: (g, 0, 0)),
                  pl.BlockSpec((NB, NC), lambda g, c: (0, 0))],
        out_specs=[pl.BlockSpec((1, 1, HEAD_DIM, QW), lambda g, c: (g, c, 0, 0)),
                   pl.BlockSpec((1, NB, QB), lambda g, c: (g, 0, c))],
        out_shape=[jax.ShapeDtypeStruct((G, nkb, HEAD_DIM, QW), F32),
                   jax.ShapeDtypeStruct((G, NB, nkb * QB), F32)],
        scratch_shapes=[pltpu.VMEM((NSPLIT, NC, HW), F32)],
        compiler_params=_cparams(("arbitrary", "arbitrary")),
        name="cmp_attention_select",
    )(proj, kc, vct, tab, c31, ovl_t)


FAR_BLOCKS = 4


def _sel_kernel(q_ref, k_ref, v_ref, tab_ref, c31_ref, sel_ref, gate_ref, ocmp_ref, owin_ref,
                o_ref, m_ref, l_ref, acc_ref):
    c = pl.program_id(1)
    _init_state(m_ref, l_ref, acc_ref)
    kk, qq = _key_query_iotas()
    qt = _stacked_queries(q_ref)
    c31 = c31_ref[0]
    spb = QB // SLC_BLOCK

    def sel_mask(kb0, n):
        pieces = []
        for t in range(n * spb):
            row = sel_ref[0, pl.ds(kb0 * spb + t, 1), :]
            row = jnp.concatenate([row] * (HW // QB), axis=1)
            pieces.append(jnp.broadcast_to(row, (SLC_BLOCK, HW)))
        return jnp.concatenate(pieces, axis=0) > 0.5

    def far(kb0, n):
        k_t, v_t = _load_kv(k_ref, v_ref, kb0, n)
        msk = sel_mask(kb0, n)
        scores = [_scores(k_t, qt[lanes]) for lanes in _halves()]
        for lanes, s in zip(_halves(), scores):
            _online_update(m_ref, l_ref, acc_ref, lanes, jnp.where(msk, s, NEG), v_t)

    def near(kb0, deltas):
        k_t, v_t = _load_kv(k_ref, v_ref, kb0, len(deltas))
        masks = []
        for t, delta in enumerate(deltas):
            msk = sel_mask(kb0 + t, 1)
            masks.append(msk & (qq >= kk) if delta == 0 else msk)
        for lanes in _halves():
            s = _scores(k_t, qt[lanes])
            parts = []
            for t, delta in enumerate(deltas):
                sb = s[t * QB:(t + 1) * QB] + _rel_bias(tab_ref[0, delta, :, lanes], c31[:, lanes])
                parts.append(jnp.where(masks[t], sb, NEG))
            s = parts[0] if len(parts) == 1 else jnp.concatenate(parts, axis=0)
            _online_update(m_ref, l_ref, acc_ref, lanes, s, v_t)

    n_far = jnp.maximum(c - 1, 0)
    n_full = n_far // FAR_BLOCKS

    def full_body(i, carry):
        far(i * FAR_BLOCKS, FAR_BLOCKS)
        return carry

    def rest_body(kb, carry):
        far(kb, 1)
        return carry

    lax.fori_loop(0, n_full, full_body, 0)
    lax.fori_loop(n_full * FAR_BLOCKS, n_far, rest_body, 0)

    @pl.when(c >= 1)
    def _():
        near(c - 1, [1, 0])

    @pl.when(c == 0)
    def _():
        near(0, [0])

    o_slc = acc_ref[...] * (1.0 / l_ref[...])
    o_t = (gate_ref[0, 0, 0:1, :] * ocmp_ref[0, 0] + gate_ref[0, 0, 1:2, :] * o_slc
           + gate_ref[0, 0, 2:3, :] * owin_ref[0, 0])
    _store_heads(o_ref, o_t)


def _sel_attention(proj, q_col, k_col, v_col, tabs, c31, sel_t, gates_s, ocmp_t, owin_t):
    S = proj.shape[0]
    G, NB = sel_t.shape[0], sel_t.shape[1]
    nkb = S // QB
    tile = pl.BlockSpec((1, 1, HEAD_DIM, QW), lambda g, c: (g, c, 0, 0))
    return pl.pallas_call(
        _sel_kernel,
        grid=(G, nkb),
        in_specs=_qkv_specs(S, q_col, k_col, v_col)
        + [pl.BlockSpec((1, 2, QB, QW), lambda g, c: (g, 0, 0, 0)),
                  pl.BlockSpec((1, 1, QW), lambda g, c: (g, 0, 0)),
                  pl.BlockSpec((1, NB, QB), lambda g, c: (g, 0, c)),
                  pl.BlockSpec((1, 1, N_BRANCH, QW), lambda g, c: (g, c, 0, 0)),
                  tile, tile],
        out_specs=pl.BlockSpec((QB, GROUP * HEAD_DIM), lambda g, c: (c, g)),
        out_shape=jax.ShapeDtypeStruct((S, G * GROUP * HEAD_DIM), F32),
        scratch_shapes=[pltpu.VMEM((1, QW), F32), pltpu.VMEM((1, QW), F32),
                        pltpu.VMEM((HEAD_DIM, QW), F32)],
        compiler_params=_cparams(("arbitrary", "arbitrary")),
        name="selected_attention_combine",
    )(proj, proj, proj, tabs, c31, sel_t, gates_s, ocmp_t, owin_t)


def _outproj_kernel(a_ref, w_ref, x_ref, o_ref):
    o_ref[...] = x_ref[...] + jnp.dot(a_ref[...], w_ref[...].astype(BF16),
                                      preferred_element_type=F32)


def _outproj(a, w, x, tm=1024, tn=512):
    S, D = a.shape
    N = w.shape[1]
    tm = min(tm, S)
    return pl.pallas_call(
        _outproj_kernel,
        grid=(S // tm, N // tn),
        in_specs=[pl.BlockSpec((tm, D), lambda i, j: (i, 0)),
                  pl.BlockSpec((D, tn), lambda i, j: (0, j)),
                  pl.BlockSpec((tm, tn), lambda i, j: (i, j))],
        out_specs=pl.BlockSpec((tm, tn), lambda i, j: (i, j)),
        out_shape=jax.ShapeDtypeStruct((S, N), F32),
        compiler_params=_cparams(("arbitrary", "arbitrary")),
        name="outproj_residual",
    )(a, w, x)


HALO = 16


def _ffn_a_kernel(halo_ref, a_ref, wg_ref, wu_ref, cw_ref, cb_ref, o_ref):
    i = pl.program_id(0)
    tm = a_ref.shape[0]
    a = a_ref[...]
    halo = halo_ref[...]
    halo = jnp.where(i > 0, halo, jnp.zeros_like(halo))
    wg = wg_ref[...].astype(BF16)
    gate = jnp.dot(a, wg, preferred_element_type=F32)
    ghalo = jnp.dot(halo, wg, preferred_element_type=F32)
    up = jnp.dot(a, wu_ref[...].astype(BF16), preferred_element_type=F32)
    gext = jnp.concatenate([ghalo, gate], axis=0)
    g1 = gext[HALO - 1:HALO - 1 + tm]
    g2 = gext[HALO - 2:HALO - 2 + tm]
    y = cb_ref[...] + g2 * cw_ref[0:1, :]
    y = y + g1 * cw_ref[1:2, :]
    y = y + gate * cw_ref[2:3, :]
    act = y * (1.0 / (1.0 + jnp.exp(-y)))
    o_ref[...] = (act * up).astype(o_ref.dtype)


def _ffn_a(hf, wg, wu, cw, cb, tm=1024, tf=256):
    S, D = hf.shape
    Fp = wg.shape[1]
    assert Fp % tf == 0
    tm = min(tm, S)
    hb = tm // HALO
    return pl.pallas_call(
        _ffn_a_kernel,
        grid=(S // tm, Fp // tf),
        in_specs=[pl.BlockSpec((HALO, D), lambda i, f: (jnp.maximum(i * hb - 1, 0), 0)),
                  pl.BlockSpec((tm, D), lambda i, f: (i, 0)),
                  pl.BlockSpec((D, tf), lambda i, f: (0, f)),
                  pl.BlockSpec((D, tf), lambda i, f: (0, f)),
                  pl.BlockSpec((cw.shape[0], tf), lambda i, f: (0, f)),
                  pl.BlockSpec((1, tf), lambda i, f: (0, f))],
        out_specs=pl.BlockSpec((tm, tf), lambda i, f: (i, f)),
        out_shape=jax.ShapeDtypeStruct((S, Fp), BF16),
        compiler_params=_cparams(("arbitrary", "arbitrary")),
        name="ffn_gate_up",
    )(hf, hf, wg, wu, cw, cb)


def _ffn_b_kernel(a_ref, w_ref, x_ref, o_ref):
    o_ref[...] = x_ref[...] + jnp.dot(a_ref[...], w_ref[...], preferred_element_type=F32)


def _ffn_b(h, w, x, tm=512, tn=512):
    S, F = h.shape
    N = w.shape[1]
    tm = min(tm, S)
    return pl.pallas_call(
        _ffn_b_kernel,
        grid=(S // tm, N // tn),
        in_specs=[pl.BlockSpec((tm, F), lambda i, j: (i, 0)),
                  pl.BlockSpec((F, tn), lambda i, j: (0, j)),
                  pl.BlockSpec((tm, tn), lambda i, j: (i, j))],
        out_specs=pl.BlockSpec((tm, tn), lambda i, j: (i, j)),
        out_shape=jax.ShapeDtypeStruct((S, N), F32),
        compiler_params=_cparams(("arbitrary", "arbitrary")),
        name="ffn_down_residual",
    )(h, w, x)


def _layer(x, rel_bias, norm_mix_g, w_in, a_q_norm_g, a_k_norm_g, a_sinks, b_q_norm_g, b_k_norm_g,
           cmp_pos_emb, cmp_w1, cmp_b1, cmp_w2, cmp_b2, out_norm_g, w_out, norm_ffn_g, w_gate,
           w_up, conv_w, conv_b, w_down):
    S, D = x.shape
    aw = A_Q_HEADS * HEAD_DIM
    akv = A_KV_HEADS * HEAD_DIM
    bw = B_Q_HEADS * HEAD_DIM
    bkv = B_KV_HEADS * HEAD_DIM
    sizes = [aw, akv, akv, bw] + [bkv] * 6 + [N_BRANCH * B_Q_HEADS]
    offs = np.concatenate([[0], np.cumsum(sizes)]).tolist()
    seg = lambda k: w_in[:, offs[k]:offs[k + 1]]
    tn = 512
    order_a = [0, 1, 2, 3, 6, 7, 8, 9]
    n_gate = sizes[10]
    w_b = jnp.concatenate([seg(4), seg(5), seg(10),
                           jnp.zeros((D, QB - n_gate), w_in.dtype)], axis=1).astype(BF16)
    one = jnp.ones((HEAD_DIM,), F32)
    seg_gain = {0: a_q_norm_g, 1: a_k_norm_g, 3: b_q_norm_g, 6: b_k_norm_g[1], 8: b_k_norm_g[2]}
    kinds, tiles, gains, col = [], [], [], {}
    c0 = 0
    for k in order_a:
        col[k] = c0
        for t in range(sizes[k] // tn):
            kinds.append(1 if k in seg_gain else 0)
            tiles.append(offs[k] // tn + t)
            gains.append(seg_gain.get(k, one))
        c0 += sizes[k]
    kinds = jnp.asarray(kinds, jnp.int32)
    tiles = jnp.asarray(tiles, jnp.int32)
    gains = jnp.stack(gains).astype(F32).reshape(len(gains), 1, HEAD_DIM)

    hn = _rmsnorm([x], norm_mix_g)
    proj = _inproj_a(hn, w_in, kinds, tiles, gains, tn=tn)
    kv32, gates = _inproj_b(hn, w_b, 2 * bkv)

    tab_a = rel_bias[:, :A_Q_HEADS]
    tab_b = rel_bias[:, A_Q_HEADS:]
    near_a = _stack_heads(_near_bias_tables(tab_a), A_KV_HEADS)
    near_b = _stack_heads(_near_bias_tables(tab_b), B_KV_HEADS)
    head_row = lambda v, G: _stack_heads(jnp.broadcast_to(v.astype(F32)[:, None, None],
                                                          (v.shape[0], 1, QB)), G)
    c31_a = head_row(tab_a[N_BUCKETS - 1], A_KV_HEADS)
    c31_b = head_row(tab_b[N_BUCKETS - 1], B_KV_HEADS)

    qcol = lambda k: col[k] // (GROUP * HEAD_DIM)
    hcol = lambda k: col[k] // HEAD_DIM
    o_a = _band_attention(proj, A_KV_HEADS, qcol(0), hcol(1), hcol(2), near_a, c31_a,
                          head_row(a_sinks, A_KV_HEADS), A_WINDOW, transposed_out=False)

    NC = S // CMP_STRIDE
    n_cmp = (S - CMP_BLOCK) // CMP_STRIDE + 1
    w1 = cmp_w1.reshape(2, CMP_BLOCK, HEAD_DIM, cmp_w1.shape[-1]).astype(BF16)
    kc, vct = _compress(kv32, B_KV_HEADS, cmp_pos_emb.astype(F32), w1, cmp_b1[:, None, :].astype(F32),
                        cmp_w2.astype(BF16), cmp_b2[:, None, :].astype(F32),
                        b_k_norm_g[0].reshape(1, HEAD_DIM).astype(F32))

    NB = S // SLC_BLOCK
    ii = np.arange(NC)[None, :]
    jj = np.arange(NB)[:, None]
    ovl_t = ((ii * CMP_STRIDE <= jj * SLC_BLOCK + SLC_BLOCK - 1)
             & (ii * CMP_STRIDE + CMP_BLOCK - 1 >= jj * SLC_BLOCK) & (ii < n_cmp))
    ovl_t = jnp.asarray(ovl_t.astype(np.float32), BF16)
    ocmp_t, sel_t = _cmp_select(proj, qcol(3), kc, vct,
                                _stack_heads(_cmp_bias_table(tab_b), B_KV_HEADS), c31_b, ovl_t, n_cmp)

    owin_t = _band_attention(proj, B_KV_HEADS, qcol(3), hcol(8), hcol(9), near_b, c31_b, None,
                             B_WINDOW, transposed_out=True)
    gates_s = gates[:, :n_gate].reshape(S // QB, QB, B_KV_HEADS, GROUP, N_BRANCH)
    gates_s = gates_s.transpose(2, 0, 4, 3, 1).reshape(B_KV_HEADS, S // QB, N_BRANCH, QW)
    o_b = _sel_attention(proj, qcol(3), hcol(6), hcol(7), near_b, c31_b, sel_t, gates_s,
                         ocmp_t, owin_t)

    on = _rmsnorm([o_a, o_b], out_norm_g)
    x2 = _outproj(on, w_out, x)

    hf = _rmsnorm([x2], norm_ffn_g)
    hmid = _ffn_a(hf, w_gate, w_up, conv_w.astype(F32), conv_b[None, :].astype(F32))
    return _ffn_b(hmid, w_down.astype(BF16), x2)


def kernel(x, rel_bias, norm_mix_g, w_in, a_q_norm_g, a_k_norm_g, a_sinks, b_q_norm_g, b_k_norm_g,
           cmp_pos_emb, cmp_w1, cmp_b1, cmp_w2, cmp_b2, out_norm_g, w_out, norm_ffn_g, w_gate, w_up,
           conv_w, conv_b, w_down):
    depth = w_in.shape[0]
    batch = x.shape[0]
    outs = []
    for b in range(batch):
        h = x[b]
        for l in range(depth):
            h = _layer(h, rel_bias, norm_mix_g[l], w_in[l], a_q_norm_g[l], a_k_norm_g[l], a_sinks[l],
                       b_q_norm_g[l], b_k_norm_g[l], cmp_pos_emb[l], cmp_w1[l], cmp_b1[l], cmp_w2[l],
                       cmp_b2[l], out_norm_g[l], w_out[l], norm_ffn_g[l], w_gate[l], w_up[l],
                       conv_w[l], conv_b[l], w_down[l])
        outs.append(h)
    return jnp.stack(outs)
```

```python
import functools
import math

import numpy as np
import jax
import jax.numpy as jnp
from jax import lax
from jax.experimental import pallas as pl
from jax.experimental.pallas import tpu as pltpu

F32 = jnp.float32
BF16 = jnp.bfloat16

HEAD_DIM = 128
A_Q_HEADS = 16
A_KV_HEADS = 4
B_Q_HEADS = 16
B_KV_HEADS = 4
GROUP = 4
A_WINDOW = 128
B_WINDOW = 512
CMP_BLOCK = 32
CMP_STRIDE = 16
SLC_BLOCK = 64
SLC_TOPK = 16
N_BRANCH = 3
N_BUCKETS = 32
MAX_DISTANCE = 128
EPS = 1e-6
NEG = -1e30
FORCE_SCORE = 1e6
SCALE = HEAD_DIM ** -0.5

QB = 128
CMP_WIN = 24
CMP_TAB = 40
VMEM_LIMIT = 56 * 1024 * 1024


def _cparams(sem, **kw):
    return pltpu.CompilerParams(dimension_semantics=sem, vmem_limit_bytes=VMEM_LIMIT, **kw)


def _t5_bucket_np(dist):
    n = np.maximum(dist, 0)
    max_exact = N_BUCKETS // 2
    nf = np.maximum(n, 1).astype(np.float32)
    large = max_exact + (np.log(nf / max_exact) / math.log(MAX_DISTANCE / max_exact)
                         * (N_BUCKETS - max_exact)).astype(np.int32)
    large = np.minimum(large, N_BUCKETS - 1)
    return np.where(n < max_exact, n, large).astype(np.int32)


def _bias_from_buckets(tab, idx):
    onehot = (idx[None] == np.arange(N_BUCKETS).reshape((-1,) + (1,) * idx.ndim)).astype(np.float32)
    return jnp.einsum('bh,b...->h...', tab.astype(F32), jnp.asarray(onehot),
                      precision=lax.Precision.HIGHEST)


def _near_bias_tables(tab):
    kk = np.arange(QB)[:, None]
    qq = np.arange(QB)[None, :]
    return _bias_from_buckets(tab, np.stack([_t5_bucket_np(qq - kk), _t5_bucket_np(QB + qq - kk)]))


def _cmp_bias_table(tab):
    npr = np.arange(CMP_TAB)[:, None] - 16
    qq = np.arange(QB)[None, :]
    return _bias_from_buckets(tab, _t5_bucket_np(qq - CMP_STRIDE * npr - (CMP_BLOCK - 1)))


def _rmsnorm_kernel(*refs, n_in):
    x_refs, g_ref, o_ref = refs[:n_in], refs[n_in], refs[n_in + 1]
    off = 0
    for x_ref in x_refs:
        x = x_ref[...]
        w = x.shape[-1]
        y = x * lax.rsqrt(jnp.mean(x * x, axis=-1, keepdims=True) + EPS)
        o_ref[:, off:off + w] = (y * g_ref[:, off:off + w]).astype(o_ref.dtype)
        off += w


def _rmsnorm(xs, gain, tr=256):
    S = xs[0].shape[0]
    widths = [x.shape[1] for x in xs]
    n = sum(widths)
    tr = min(tr, S)
    return pl.pallas_call(
        functools.partial(_rmsnorm_kernel, n_in=len(xs)),
        grid=(S // tr,),
        in_specs=[pl.BlockSpec((tr, w), lambda i: (i, 0)) for w in widths]
        + [pl.BlockSpec((1, n), lambda i: (0, 0))],
        out_specs=pl.BlockSpec((tr, n), lambda i: (i, 0)),
        out_shape=jax.ShapeDtypeStruct((S, n), BF16),
        compiler_params=_cparams(("arbitrary",)),
        name="rmsnorm",
    )(*xs, gain.reshape(1, n).astype(F32))


def _dot_wt(a, wt):
    return lax.dot_general(a, wt.astype(BF16), (((1,), (1,)), ((), ())), preferred_element_type=F32)


def _inproj_a_kernel(kind_ref, tile_ref, a_ref, w_ref, g_ref, o_ref):
    j = pl.program_id(1)
    acc = _dot_wt(a_ref[...], w_ref[...])

    @pl.when(kind_ref[j] == 0)
    def _():
        o_ref[...] = acc.astype(o_ref.dtype)

    @pl.when(kind_ref[j] == 1)
    def _():
        g = g_ref[0]
        for h in range(acc.shape[1] // HEAD_DIM):
            sl = acc[:, h * HEAD_DIM:(h + 1) * HEAD_DIM]
            y = sl * lax.rsqrt(jnp.mean(sl * sl, axis=-1, keepdims=True) + EPS)
            o_ref[:, h * HEAD_DIM:(h + 1) * HEAD_DIM] = (y * g).astype(o_ref.dtype)


def _inproj_a(hn, wt, kinds, tiles, gains, tm=1024, tn=512):
    S, D = hn.shape
    n_tiles = kinds.shape[0]
    tm = min(tm, S)
    grid_spec = pltpu.PrefetchScalarGridSpec(
        num_scalar_prefetch=2,
        grid=(S // tm, n_tiles),
        in_specs=[pl.BlockSpec((tm, D), lambda i, j, k, t: (i, 0)),
                  pl.BlockSpec((tn, D), lambda i, j, k, t: (t[j], 0)),
                  pl.BlockSpec((1, 1, HEAD_DIM), lambda i, j, k, t: (j, 0, 0))],
        out_specs=pl.BlockSpec((tm, tn), lambda i, j, k, t: (i, j)),
    )
    return pl.pallas_call(
        _inproj_a_kernel,
        grid_spec=grid_spec,
        out_shape=jax.ShapeDtypeStruct((S, n_tiles * tn), BF16),
        compiler_params=_cparams(("arbitrary", "arbitrary")),
        name="inproj_heads",
    )(kinds, tiles, hn, wt, gains)


def _inproj_b_kernel(a_ref, w_ref, wg_ref, o_ref, gate_ref):
    a = a_ref[...]
    o_ref[...] = _dot_wt(a, w_ref[...])

    @pl.when(pl.program_id(1) == 0)
    def _():
        gate_ref[...] = 1.0 / (1.0 + jnp.exp(-_dot_wt(a, wg_ref[...])))


def _inproj_b(hn, wt, tile0, n_tiles, wg_t, tm=1024, tn=512):
    S, D = hn.shape
    tm = min(tm, S)
    ng = wg_t.shape[0]
    return pl.pallas_call(
        _inproj_b_kernel,
        grid=(S // tm, n_tiles),
        in_specs=[pl.BlockSpec((tm, D), lambda i, j: (i, 0)),
                  pl.BlockSpec((tn, D), lambda i, j: (tile0 + j, 0)),
                  pl.BlockSpec((ng, D), lambda i, j: (0, 0))],
        out_specs=[pl.BlockSpec((tm, tn), lambda i, j: (i, j)),
                   pl.BlockSpec((tm, ng), lambda i, j: (i, 0))],
        out_shape=[jax.ShapeDtypeStruct((S, n_tiles * tn), F32),
                   jax.ShapeDtypeStruct((S, ng), F32)],
        compiler_params=_cparams(("arbitrary", "arbitrary")),
        name="inproj_cmp_gates",
    )(hn, wt, wg_t)


def _gelu_tanh(x):
    return 0.5 * x * (1.0 + jnp.tanh(math.sqrt(2.0 / math.pi) * (x + 0.044715 * (x * x * x))))


def _compress_kernel(tk_ref, tv_ref, pos_ref, w1_ref, b1_ref, w2_ref, b2_ref, gk_ref,
                     kc_ref, vct_ref):
    nc = kc_ref.shape[1]

    def mlp(t_ref, kv):
        a0 = jnp.zeros((nc, w1_ref.shape[-1]), F32)
        a1 = jnp.zeros((nc, w1_ref.shape[-1]), F32)
        for b in range(CMP_STRIDE):
            t = t_ref[pl.ds(b, nc, stride=CMP_STRIDE), :]
            lo, hi = b, CMP_STRIDE + b
            a0 = a0 + jnp.dot((t + pos_ref[kv, lo:lo + 1, :]).astype(BF16), w1_ref[kv, lo],
                              preferred_element_type=F32)
            a1 = a1 + jnp.dot((t + pos_ref[kv, hi:hi + 1, :]).astype(BF16), w1_ref[kv, hi],
                              preferred_element_type=F32)
        h = a0 + pltpu.roll(a1, nc - 1, 0) + b1_ref[kv]
        h = _gelu_tanh(h)
        return jnp.dot(h.astype(BF16), w2_ref[kv], preferred_element_type=F32) + b2_ref[kv]

    ck = mlp(tk_ref, 0)
    ck = ck * lax.rsqrt(jnp.mean(ck * ck, axis=-1, keepdims=True) + EPS) * gk_ref[...]
    kc_ref[0] = ck.astype(kc_ref.dtype)
    cv = mlp(tv_ref, 1)
    vct_ref[0] = cv.T.astype(vct_ref.dtype)


def _compress(kv32, G, pos, w1, b1, w2, b2, gk):
    S = kv32.shape[0]
    NC = S // CMP_STRIDE
    hid = w1.shape[-1]
    return pl.pallas_call(
        _compress_kernel,
        grid=(G,),
        in_specs=[pl.BlockSpec((S, HEAD_DIM), lambda g: (0, g)),
                  pl.BlockSpec((S, HEAD_DIM), lambda g: (0, G + g)),
                  pl.BlockSpec((2, CMP_BLOCK, HEAD_DIM), lambda g: (0, 0, 0)),
                  pl.BlockSpec((2, CMP_BLOCK, HEAD_DIM, hid), lambda g: (0, 0, 0, 0)),
                  pl.BlockSpec((2, 1, hid), lambda g: (0, 0, 0)),
                  pl.BlockSpec((2, hid, HEAD_DIM), lambda g: (0, 0, 0)),
                  pl.BlockSpec((2, 1, HEAD_DIM), lambda g: (0, 0, 0)),
                  pl.BlockSpec((1, HEAD_DIM), lambda g: (0, 0))],
        out_specs=[pl.BlockSpec((1, NC, HEAD_DIM), lambda g: (g, 0, 0)),
                   pl.BlockSpec((1, HEAD_DIM, NC), lambda g: (g, 0, 0))],
        out_shape=[jax.ShapeDtypeStruct((G, NC, HEAD_DIM), BF16),
                   jax.ShapeDtypeStruct((G, HEAD_DIM, NC), BF16)],
        compiler_params=_cparams(("arbitrary",)),
        name="compress",
    )(kv32, kv32, pos, w1, b1, w2, b2, gk)


QW = GROUP * QB


def _stacked_queries(q_ref):
    return jnp.concatenate([q_ref[:, r * HEAD_DIM:(r + 1) * HEAD_DIM] for r in range(GROUP)], axis=0)


LOG2E = math.log2(math.e)
KEY_SCALE = SCALE * LOG2E
NSPLIT = 1
HW = QW // NSPLIT


def _halves():
    return [slice(h * HW, (h + 1) * HW) for h in range(NSPLIT)]


def _scores(k_t, qs):
    return lax.dot_general(k_t, qs, (((1,), (1,)), ((), ())), preferred_element_type=F32)


def _rel_bias(tab, c31):
    return (tab - c31) * LOG2E


def _init_state(m_ref, l_ref, acc_ref):
    m_ref[...] = jnp.full(m_ref.shape, NEG, F32)
    l_ref[...] = jnp.zeros(l_ref.shape, F32)
    acc_ref[...] = jnp.zeros(acc_ref.shape, F32)


def _online_update(m_ref, l_ref, acc_ref, lanes, t, v_t):
    m_old = m_ref[:, lanes]
    m_new = jnp.maximum(m_old, jnp.max(t, axis=0, keepdims=True))
    alpha = jnp.exp2(m_old - m_new)
    p = jnp.exp2(t - m_new)
    l_ref[:, lanes] = alpha * l_ref[:, lanes] + jnp.sum(p, axis=0, keepdims=True)
    pv = lax.dot_general(v_t, p.astype(BF16), (((0,), (0,)), ((), ())), preferred_element_type=F32)
    acc_ref[:, lanes] = acc_ref[:, lanes] * alpha + pv
    m_ref[:, lanes] = m_new


def _key_query_iotas():
    kk = lax.broadcasted_iota(jnp.int32, (QB, HW), 0)
    qq = lax.broadcasted_iota(jnp.int32, (QB, HW), 1) & (QB - 1)
    return kk, qq


def _load_kv(k_ref, v_ref, kb0, n):
    rows = pl.ds(pl.multiple_of(kb0 * QB, QB), n * QB)
    return k_ref[rows, :], v_ref[rows, :]


def _store_heads(o_ref, o_t):
    for r in range(GROUP):
        o_ref[:, r * HEAD_DIM:(r + 1) * HEAD_DIM] = o_t[:, r * QB:(r + 1) * QB].T


def _stack_heads(a, G):
    a = a.reshape((G, GROUP) + a.shape[1:])
    a = jnp.moveaxis(a, 1, -2)
    return a.reshape(a.shape[:-2] + (QW,))


def _band_kernel(q_ref, k_ref, v_ref, tab_ref, c31_ref, sink_ref, o_ref, m_ref, l_ref, acc_ref,
                 *, nback, use_sinks, transposed_out):
    c = pl.program_id(1)
    _init_state(m_ref, l_ref, acc_ref)
    kk, qq = _key_query_iotas()
    qs = _stacked_queries(q_ref)
    c31 = c31_ref[0]

    def run(kb0, deltas):
        k_t, v_t = _load_kv(k_ref, v_ref, kb0, len(deltas))
        for lanes in _halves():
            s = _scores(k_t, qs[lanes])
            parts = []
            for t, delta in enumerate(deltas):
                sb = s[t * QB:(t + 1) * QB]
                if delta <= 1:
                    sb = sb + _rel_bias(tab_ref[0, delta, :, lanes], c31[:, lanes])
                if delta == 0:
                    sb = jnp.where(qq >= kk, sb, NEG)
                elif delta == nback:
                    sb = jnp.where(qq < kk, sb, NEG)
                parts.append(sb)
            s = parts[0] if len(parts) == 1 else jnp.concatenate(parts, axis=0)
            _online_update(m_ref, l_ref, acc_ref, lanes, s, v_t)

    @pl.when(c >= nback)
    def _():
        run(c - nback, list(range(nback, -1, -1)))

    @pl.when(c < nback)
    def _():
        for delta in range(nback - 1, -1, -1):
            @pl.when(c - delta >= 0)
            def _(delta=delta):
                run(c - delta, [delta])

    l = l_ref[...]
    acc = acc_ref[...]
    if use_sinks:
        m = m_ref[...]
        sk = _rel_bias(sink_ref[0], c31)
        m_f = jnp.maximum(m, sk)
        a = jnp.exp2(m - m_f)
        l = l * a + jnp.exp2(sk - m_f)
        acc = acc * a
    o_t = acc * (1.0 / l)
    if transposed_out:
        o_ref[0, 0] = o_t
    else:
        _store_heads(o_ref, o_t)


def _qkv_specs(S, q_col, k_col, v_col):
    return [pl.BlockSpec((QB, GROUP * HEAD_DIM), lambda g, c: (c, q_col + g)),
            pl.BlockSpec((S, HEAD_DIM), lambda g, c: (0, k_col + g)),
            pl.BlockSpec((S, HEAD_DIM), lambda g, c: (0, v_col + g))]


def _band_attention(proj, G, q_col, k_col, v_col, tabs, c31, sinks, window, transposed_out):
    S = proj.shape[0]
    nkb = S // QB
    nback = -(-(window - 1) // QB)
    use_sinks = sinks is not None
    if sinks is None:
        sinks = jnp.zeros_like(c31)
    if transposed_out:
        out_shape = jax.ShapeDtypeStruct((G, nkb, HEAD_DIM, QW), F32)
        out_spec = pl.BlockSpec((1, 1, HEAD_DIM, QW), lambda g, c: (g, c, 0, 0))
    else:
        out_shape = jax.ShapeDtypeStruct((S, G * GROUP * HEAD_DIM), F32)
        out_spec = pl.BlockSpec((QB, GROUP * HEAD_DIM), lambda g, c: (c, g))
    row = pl.BlockSpec((1, 1, QW), lambda g, c: (g, 0, 0))
    return pl.pallas_call(
        functools.partial(_band_kernel, nback=nback, use_sinks=use_sinks,
                          transposed_out=transposed_out),
        grid=(G, nkb),
        in_specs=_qkv_specs(S, q_col, k_col, v_col)
        + [pl.BlockSpec((1, 2, QB, QW), lambda g, c: (g, 0, 0, 0)), row, row],
        out_specs=out_spec,
        out_shape=out_shape,
        scratch_shapes=[pltpu.VMEM((1, QW), F32), pltpu.VMEM((1, QW), F32),
                        pltpu.VMEM((HEAD_DIM, QW), F32)],
        compiler_params=_cparams(("arbitrary", "arbitrary")),
        name="band_attention_w%d" % window,
    )(proj, proj, proj, tabs, c31, sinks)


def _cmp_select_kernel(q_ref, kc_ref, vct_ref, tab_ref, c31_ref, ovl_ref, ot_ref, sel_ref, s_ref,
                       *, n_valid, topk):
    c = pl.program_id(1)
    nc = kc_ref.shape[1]
    nb = ovl_ref.shape[0]
    qt = _stacked_queries(q_ref)

    w0 = pl.multiple_of(jnp.maximum(8 * c - 16, 0), 8)
    toff = pl.multiple_of(w0 - 8 * c + 16, 8)
    rown = lax.broadcasted_iota(jnp.int32, (nc, HW), 0)
    n_abs = w0 + lax.broadcasted_iota(jnp.int32, (CMP_WIN, HW), 0)
    wq = lax.broadcasted_iota(jnp.int32, (CMP_WIN, HW), 1) & (QB - 1)
    dist = (c * QB + wq) - (n_abs * CMP_STRIDE + (CMP_BLOCK - 1))
    valid_w = (dist >= 0) & (n_abs < n_valid)
    qpos = c * QB + (lax.broadcasted_iota(jnp.int32, (1, HW), 1) & (QB - 1))
    has_any = (qpos >= CMP_BLOCK - 1).astype(F32)
    kc_w = kc_ref[0, pl.ds(w0, CMP_WIN), :]

    p_heads = []
    for hv, lanes in enumerate(_halves()):
        s_ref[hv] = jnp.where(rown < w0, _scores(kc_ref[0], qt[lanes]), NEG)
        bias_w = _rel_bias(tab_ref[0, pl.ds(toff, CMP_WIN), lanes], c31_ref[0][:, lanes])
        s_ref[hv, pl.ds(w0, CMP_WIN), :] = jnp.where(valid_w, _scores(kc_w, qt[lanes]) + bias_w, NEG)
        s = s_ref[hv]
        m = jnp.max(s, axis=0, keepdims=True)
        e = jnp.exp2(s - m)
        p = e * (has_any / jnp.sum(e, axis=0, keepdims=True))
        ot_ref[0, 0, :, lanes] = jnp.dot(vct_ref[0], p.astype(BF16), preferred_element_type=F32)
        p_heads += [p[:, r * QB:(r + 1) * QB] for r in range(HW // QB)]
    p_grp = p_heads[0]
    for p_r in p_heads[1:]:
        p_grp = p_grp + p_r

    scores = jnp.dot(ovl_ref[...], p_grp.astype(BF16), preferred_element_type=F32)
    blk = lax.broadcasted_iota(jnp.int32, (nb, QB), 0)
    cur = lax.shift_right_logical(c * QB + lax.broadcasted_iota(jnp.int32, (nb, QB), 1),
                                  int(math.log2(SLC_BLOCK)))
    forced = (blk == 0) | (blk == cur) | (blk == cur - 1)
    work = jnp.where(forced, FORCE_SCORE, jnp.where(blk <= cur, scores, -1.0))
    blk_f = blk.astype(F32)
    sel = jnp.zeros((nb, QB), F32)
    for _ in range(topk):
        mx = jnp.max(work, axis=0, keepdims=True)
        first = jnp.min(jnp.where(work == mx, blk_f, float(nb)), axis=0, keepdims=True)
        hit = blk_f == first
        sel = jnp.where(hit, 1.0, sel)
        work = jnp.where(hit, -3e38, work)
    sel_ref[0] = sel


def _cmp_select(proj, q_col, kc, vct, tab, c31, ovl_t, n_valid):
    nkb = proj.shape[0] // QB
    G, NC = kc.shape[0], kc.shape[1]
    NB = ovl_t.shape[0]
    return pl.pallas_call(
        functools.partial(_cmp_select_kernel, n_valid=n_valid, topk=min(SLC_TOPK, NB)),
        grid=(G, nkb),
        in_specs=[pl.BlockSpec((QB, GROUP * HEAD_DIM), lambda g, c: (c, q_col + g)),
                  pl.BlockSpec((1, NC, HEAD_DIM), lambda g, c: (g, 0, 0)),
                  pl.BlockSpec((1, HEAD_DIM, NC), lambda g, c: (g, 0, 0)),
                  pl.BlockSpec((1, CMP_TAB, QW), lambda g, c: (g, 0, 0)),
                  pl.BlockSpec((1, 1, QW), lambda g, c: (g, 0, 0)),
                  pl.BlockSpec((NB, NC), lambda g, c: (0, 0))],
        out_specs=[pl.BlockSpec((1, 1, HEAD_DIM, QW), lambda g, c: (g, c, 0, 0)),
                   pl.BlockSpec((1, NB, QB), lambda g, c: (g, 0, c))],
        out_shape=[jax.ShapeDtypeStruct((G, nkb, HEAD_DIM, QW), F32),
                   jax.ShapeDtypeStruct((G, NB, nkb * QB), F32)],
        scratch_shapes=[pltpu.VMEM((NSPLIT, NC, HW), F32)],
        compiler_params=_cparams(("arbitrary", "arbitrary")),
        name="cmp_attention_select",
    )(proj, kc, vct, tab, c31, ovl_t)


FAR_BLOCKS = 4


def _sel_kernel(q_ref, k_ref, v_ref, tab_ref, c31_ref, sel_ref, gate_ref, ocmp_ref, owin_ref,
                o_ref, m_ref, l_ref, acc_ref, s_a, s_b, p_a, p_b, al_a, al_b):
    c = pl.program_id(1)
    _init_state(m_ref, l_ref, acc_ref)
    kk, qq = _key_query_iotas()
    qt = _stacked_queries(q_ref)
    c31 = c31_ref[0]
    spb = QB // SLC_BLOCK
    tk = FAR_BLOCKS * QB
    last_tile = k_ref.shape[0] // tk - 1
    n_far = jnp.maximum(c - 1, 0)
    n_tiles = (n_far + FAR_BLOCKS - 1) // FAR_BLOCKS

    def sel_mask(kb0, n, row_limit=None):
        pieces = []
        for t in range(n * spb):
            r = kb0 * spb + t
            row = sel_ref[0, pl.ds(r, 1), :]
            if row_limit is not None:
                row = jnp.where(r < row_limit, row, 0.0)
            row = jnp.concatenate([row] * (HW // QB), axis=1)
            pieces.append(jnp.broadcast_to(row, (SLC_BLOCK, HW)))
        return jnp.concatenate(pieces, axis=0) > 0.5

    def tile_rows(i):
        return pl.ds(pl.multiple_of(jnp.minimum(i, last_tile) * tk, tk), tk)

    def far_scores(i, s_buf):
        s_buf[...] = _scores(k_ref[tile_rows(i), :], qt)

    def far_softmax(i, s_buf, p_buf, al_buf):
        t = jnp.where(sel_mask(i * FAR_BLOCKS, FAR_BLOCKS, n_far * spb), s_buf[...], NEG)
        m_old = m_ref[...]
        m_new = jnp.maximum(m_old, jnp.max(t, axis=0, keepdims=True))
        alpha = jnp.exp2(m_old - m_new)
        p = jnp.exp2(t - m_new)
        l_ref[...] = alpha * l_ref[...] + jnp.sum(p, axis=0, keepdims=True)
        m_ref[...] = m_new
        p_buf[...] = p.astype(BF16)
        al_buf[...] = alpha

    def far_values(i, p_buf, al_buf):
        pv = lax.dot_general(v_ref[tile_rows(i), :], p_buf[...], (((0,), (0,)), ((), ())),
                             preferred_element_type=F32)
        acc_ref[...] = acc_ref[...] * al_buf[...] + pv

    far_scores(0, s_a)
    p_b[...] = jnp.zeros(p_b.shape, BF16)
    al_b[...] = jnp.ones(al_b.shape, F32)

    def pair_body(j, carry):
        i = 2 * j
        far_scores(i + 1, s_b)
        far_softmax(i, s_a, p_a, al_a)
        far_values(jnp.maximum(i - 1, 0), p_b, al_b)
        far_scores(i + 2, s_a)
        far_softmax(i + 1, s_b, p_b, al_b)
        far_values(i, p_a, al_a)
        return carry

    n_pairs = n_tiles // 2
    lax.fori_loop(0, n_pairs, pair_body, 0)
    far_values(jnp.maximum(2 * n_pairs - 1, 0), p_b, al_b)

    @pl.when(n_tiles > 2 * n_pairs)
    def _():
        far_softmax(n_tiles - 1, s_a, p_a, al_a)
        far_values(n_tiles - 1, p_a, al_a)

    def near(kb0, deltas):
        k_t, v_t = _load_kv(k_ref, v_ref, kb0, len(deltas))
        masks = []
        for t, delta in enumerate(deltas):
            msk = sel_mask(kb0 + t, 1)
            masks.append(msk & (qq >= kk) if delta == 0 else msk)
        for lanes in _halves():
            s = _scores(k_t, qt[lanes])
            parts = []
            for t, delta in enumerate(deltas):
                sb = s[t * QB:(t + 1) * QB] + _rel_bias(tab_ref[0, delta, :, lanes], c31[:, lanes])
                parts.append(jnp.where(masks[t], sb, NEG))
            s = parts[0] if len(parts) == 1 else jnp.concatenate(parts, axis=0)
            _online_update(m_ref, l_ref, acc_ref, lanes, s, v_t)

    @pl.when(c >= 1)
    def _():
        near(c - 1, [1, 0])

    @pl.when(c == 0)
    def _():
        near(0, [0])

    o_slc = acc_ref[...] * (1.0 / l_ref[...])
    o_t = (gate_ref[0, 0, 0:1, :] * ocmp_ref[0, 0] + gate_ref[0, 0, 1:2, :] * o_slc
           + gate_ref[0, 0, 2:3, :] * owin_ref[0, 0])
    _store_heads(o_ref, o_t)


def _sel_attention(proj, q_col, k_col, v_col, tabs, c31, sel_t, gates_s, ocmp_t, owin_t):
    S = proj.shape[0]
    G, NB = sel_t.shape[0], sel_t.shape[1]
    nkb = S // QB
    tile = pl.BlockSpec((1, 1, HEAD_DIM, QW), lambda g, c: (g, c, 0, 0))
    return pl.pallas_call(
        _sel_kernel,
        grid=(G, nkb),
        in_specs=_qkv_specs(S, q_col, k_col, v_col)
        + [pl.BlockSpec((1, 2, QB, QW), lambda g, c: (g, 0, 0, 0)),
                  pl.BlockSpec((1, 1, QW), lambda g, c: (g, 0, 0)),
                  pl.BlockSpec((1, NB, QB), lambda g, c: (g, 0, c)),
                  pl.BlockSpec((1, 1, N_BRANCH, QW), lambda g, c: (g, c, 0, 0)),
                  tile, tile],
        out_specs=pl.BlockSpec((QB, GROUP * HEAD_DIM), lambda g, c: (c, g)),
        out_shape=jax.ShapeDtypeStruct((S, G * GROUP * HEAD_DIM), F32),
        scratch_shapes=[pltpu.VMEM((1, QW), F32), pltpu.VMEM((1, QW), F32),
                        pltpu.VMEM((HEAD_DIM, QW), F32)]
        + [pltpu.VMEM((FAR_BLOCKS * QB, QW), F32)] * 2
        + [pltpu.VMEM((FAR_BLOCKS * QB, QW), BF16)] * 2
        + [pltpu.VMEM((1, QW), F32)] * 2,
        compiler_params=_cparams(("arbitrary", "arbitrary")),
        name="selected_attention_combine",
    )(proj, proj, proj, tabs, c31, sel_t, gates_s, ocmp_t, owin_t)


def _outproj_kernel(a_ref, w_ref, x_ref, o_ref):
    o_ref[...] = x_ref[...] + jnp.dot(a_ref[...], w_ref[...].astype(BF16),
                                      preferred_element_type=F32)


def _outproj(a, w, x, tm=1024, tn=512):
    S, D = a.shape
    N = w.shape[1]
    tm = min(tm, S)
    return pl.pallas_call(
        _outproj_kernel,
        grid=(S // tm, N // tn),
        in_specs=[pl.BlockSpec((tm, D), lambda i, j: (i, 0)),
                  pl.BlockSpec((D, tn), lambda i, j: (0, j)),
                  pl.BlockSpec((tm, tn), lambda i, j: (i, j))],
        out_specs=pl.BlockSpec((tm, tn), lambda i, j: (i, j)),
        out_shape=jax.ShapeDtypeStruct((S, N), F32),
        compiler_params=_cparams(("arbitrary", "arbitrary")),
        name="outproj_residual",
    )(a, w, x)


HALO = 16


def _ffn_a_kernel(halo_ref, a_ref, wg_ref, wu_ref, cw_ref, cb_ref, o_ref):
    i = pl.program_id(0)
    tm = a_ref.shape[0]
    a = a_ref[...]
    halo = halo_ref[...]
    halo = jnp.where(i > 0, halo, jnp.zeros_like(halo))
    wg = wg_ref[...].astype(BF16)
    gate = jnp.dot(a, wg, preferred_element_type=F32)
    ghalo = jnp.dot(halo, wg, preferred_element_type=F32)
    up = jnp.dot(a, wu_ref[...].astype(BF16), preferred_element_type=F32)
    gext = jnp.concatenate([ghalo, gate], axis=0)
    g1 = gext[HALO - 1:HALO - 1 + tm]
    g2 = gext[HALO - 2:HALO - 2 + tm]
    y = cb_ref[...] + g2 * cw_ref[0:1, :]
    y = y + g1 * cw_ref[1:2, :]
    y = y + gate * cw_ref[2:3, :]
    act = y * (1.0 / (1.0 + jnp.exp(-y)))
    o_ref[...] = (act * up).astype(o_ref.dtype)


def _ffn_a(hf, wg, wu, cw, cb, tm=1024, tf=256):
    S, D = hf.shape
    Fp = wg.shape[1]
    assert Fp % tf == 0
    tm = min(tm, S)
    hb = tm // HALO
    return pl.pallas_call(
        _ffn_a_kernel,
        grid=(S // tm, Fp // tf),
        in_specs=[pl.BlockSpec((HALO, D), lambda i, f: (jnp.maximum(i * hb - 1, 0), 0)),
                  pl.BlockSpec((tm, D), lambda i, f: (i, 0)),
                  pl.BlockSpec((D, tf), lambda i, f: (0, f)),
                  pl.BlockSpec((D, tf), lambda i, f: (0, f)),
                  pl.BlockSpec((cw.shape[0], tf), lambda i, f: (0, f)),
                  pl.BlockSpec((1, tf), lambda i, f: (0, f))],
        out_specs=pl.BlockSpec((tm, tf), lambda i, f: (i, f)),
        out_shape=jax.ShapeDtypeStruct((S, Fp), BF16),
        compiler_params=_cparams(("arbitrary", "arbitrary")),
        name="ffn_gate_up",
    )(hf, hf, wg, wu, cw, cb)


def _ffn_b_kernel(a_ref, w_ref, x_ref, o_ref):
    o_ref[...] = x_ref[...] + jnp.dot(a_ref[...], w_ref[...], preferred_element_type=F32)


def _ffn_b(h, w, x, tm=512, tn=512):
    S, F = h.shape
    N = w.shape[1]
    tm = min(tm, S)
    return pl.pallas_call(
        _ffn_b_kernel,
        grid=(S // tm, N // tn),
        in_specs=[pl.BlockSpec((tm, F), lambda i, j: (i, 0)),
                  pl.BlockSpec((F, tn), lambda i, j: (0, j)),
                  pl.BlockSpec((tm, tn), lambda i, j: (i, j))],
        out_specs=pl.BlockSpec((tm, tn), lambda i, j: (i, j)),
        out_shape=jax.ShapeDtypeStruct((S, N), F32),
        compiler_params=_cparams(("arbitrary", "arbitrary")),
        name="ffn_down_residual",
    )(h, w, x)


def _layer(x, rel_bias, norm_mix_g, w_in, a_q_norm_g, a_k_norm_g, a_sinks, b_q_norm_g, b_k_norm_g,
           cmp_pos_emb, cmp_w1, cmp_b1, cmp_w2, cmp_b2, out_norm_g, w_out, norm_ffn_g, w_gate,
           w_up, conv_w, conv_b, w_down):
    S, D = x.shape
    aw = A_Q_HEADS * HEAD_DIM
    akv = A_KV_HEADS * HEAD_DIM
    bw = B_Q_HEADS * HEAD_DIM
    bkv = B_KV_HEADS * HEAD_DIM
    sizes = [aw, akv, akv, bw] + [bkv] * 6 + [N_BRANCH * B_Q_HEADS]
    offs = np.concatenate([[0], np.cumsum(sizes)]).tolist()
    wt = jnp.swapaxes(w_in, 0, 1)
    tn = 512
    order_a = [0, 1, 2, 3, 6, 7, 8, 9]
    n_gate = sizes[10]
    wg_t = jnp.pad(wt[offs[10]:offs[10] + n_gate], ((0, QB - n_gate), (0, 0)))
    one = jnp.ones((HEAD_DIM,), F32)
    seg_gain = {0: a_q_norm_g, 1: a_k_norm_g * KEY_SCALE, 3: b_q_norm_g,
                6: b_k_norm_g[1] * KEY_SCALE, 8: b_k_norm_g[2] * KEY_SCALE}
    kinds, tiles, gains, col = [], [], [], {}
    c0 = 0
    for k in order_a:
        col[k] = c0
        for t in range(sizes[k] // tn):
            kinds.append(1 if k in seg_gain else 0)
            tiles.append(offs[k] // tn + t)
            gains.append(seg_gain.get(k, one))
        c0 += sizes[k]
    kinds = jnp.asarray(kinds, jnp.int32)
    tiles = jnp.asarray(tiles, jnp.int32)
    gains = jnp.stack(gains).astype(F32).reshape(len(gains), 1, HEAD_DIM)

    hn = _rmsnorm([x], norm_mix_g)
    proj = _inproj_a(hn, wt, kinds, tiles, gains, tn=tn)
    kv32, gates = _inproj_b(hn, wt, offs[4] // tn, 2 * bkv // tn, wg_t)

    tab_a = rel_bias[:, :A_Q_HEADS]
    tab_b = rel_bias[:, A_Q_HEADS:]
    near_a = _stack_heads(_near_bias_tables(tab_a), A_KV_HEADS)
    near_b = _stack_heads(_near_bias_tables(tab_b), B_KV_HEADS)
    head_row = lambda v, G: _stack_heads(jnp.broadcast_to(v.astype(F32)[:, None, None],
                                                          (v.shape[0], 1, QB)), G)
    c31_a = head_row(tab_a[N_BUCKETS - 1], A_KV_HEADS)
    c31_b = head_row(tab_b[N_BUCKETS - 1], B_KV_HEADS)

    qcol = lambda k: col[k] // (GROUP * HEAD_DIM)
    hcol = lambda k: col[k] // HEAD_DIM
    o_a = _band_attention(proj, A_KV_HEADS, qcol(0), hcol(1), hcol(2), near_a, c31_a,
                          head_row(a_sinks, A_KV_HEADS), A_WINDOW, transposed_out=False)

    NC = S // CMP_STRIDE
    n_cmp = (S - CMP_BLOCK) // CMP_STRIDE + 1
    w1 = cmp_w1.reshape(2, CMP_BLOCK, HEAD_DIM, cmp_w1.shape[-1]).astype(BF16)
    kc, vct = _compress(kv32, B_KV_HEADS, cmp_pos_emb.astype(F32), w1, cmp_b1[:, None, :].astype(F32),
                        cmp_w2.astype(BF16), cmp_b2[:, None, :].astype(F32),
                        (b_k_norm_g[0] * KEY_SCALE).reshape(1, HEAD_DIM).astype(F32))

    NB = S // SLC_BLOCK
    ii = np.arange(NC)[None, :]
    jj = np.arange(NB)[:, None]
    ovl_t = ((ii * CMP_STRIDE <= jj * SLC_BLOCK + SLC_BLOCK - 1)
             & (ii * CMP_STRIDE + CMP_BLOCK - 1 >= jj * SLC_BLOCK) & (ii < n_cmp))
    ovl_t = jnp.asarray(ovl_t.astype(np.float32), BF16)
    ocmp_t, sel_t = _cmp_select(proj, qcol(3), kc, vct,
                                _stack_heads(_cmp_bias_table(tab_b), B_KV_HEADS), c31_b, ovl_t, n_cmp)

    owin_t = _band_attention(proj, B_KV_HEADS, qcol(3), hcol(8), hcol(9), near_b, c31_b, None,
                             B_WINDOW, transposed_out=True)
    gates_s = gates[:, :n_gate].reshape(S // QB, QB, B_KV_HEADS, GROUP, N_BRANCH)
    gates_s = gates_s.transpose(2, 0, 4, 3, 1).reshape(B_KV_HEADS, S // QB, N_BRANCH, QW)
    o_b = _sel_attention(proj, qcol(3), hcol(6), hcol(7), near_b, c31_b, sel_t, gates_s,
                         ocmp_t, owin_t)

    on = _rmsnorm([o_a, o_b], out_norm_g)
    x2 = _outproj(on, w_out, x)

    hf = _rmsnorm([x2], norm_ffn_g)
    hmid = _ffn_a(hf, w_gate, w_up, conv_w.astype(F32), conv_b[None, :].astype(F32))
    return _ffn_b(hmid, w_down.astype(BF16), x2)


def kernel(x, rel_bias, norm_mix_g, w_in, a_q_norm_g, a_k_norm_g, a_sinks, b_q_norm_g, b_k_norm_g,
           cmp_pos_emb, cmp_w1, cmp_b1, cmp_w2, cmp_b2, out_norm_g, w_out, norm_ffn_g, w_gate, w_up,
           conv_w, conv_b, w_down):
    depth = w_in.shape[0]
    batch = x.shape[0]
    outs = []
    for b in range(batch):
        h = x[b]
        for l in range(depth):
            h = _layer(h, rel_bias, norm_mix_g[l], w_in[l], a_q_norm_g[l], a_k_norm_g[l], a_sinks[l],
                       b_q_norm_g[l], b_k_norm_g[l], cmp_pos_emb[l], cmp_w1[l], cmp_b1[l], cmp_w2[l],
                       cmp_b2[l], out_norm_g[l], w_out[l], norm_ffn_g[l], w_gate[l], w_up[l],
                       conv_w[l], conv_b[l], w_down[l])
        outs.append(h)
    return jnp.stack(outs)
```

```python
import functools
import math

import numpy as np
import jax
import jax.numpy as jnp
from jax import lax
from jax.experimental import pallas as pl
from jax.experimental.pallas import tpu as pltpu

F32 = jnp.float32
BF16 = jnp.bfloat16

HEAD_DIM = 128
A_Q_HEADS = 16
A_KV_HEADS = 4
B_Q_HEADS = 16
B_KV_HEADS = 4
GROUP = 4
A_WINDOW = 128
B_WINDOW = 512
CMP_BLOCK = 32
CMP_STRIDE = 16
SLC_BLOCK = 64
SLC_TOPK = 16
N_BRANCH = 3
N_BUCKETS = 32
MAX_DISTANCE = 128
EPS = 1e-6
NEG = -1e30
FORCE_SCORE = 1e6
SCALE = HEAD_DIM ** -0.5

QB = 128
CMP_WIN = 24
CMP_TAB = 40
VMEM_LIMIT = 56 * 1024 * 1024


def _cparams(sem, **kw):
    return pltpu.CompilerParams(dimension_semantics=sem, vmem_limit_bytes=VMEM_LIMIT, **kw)


def _t5_bucket_np(dist):
    n = np.maximum(dist, 0)
    max_exact = N_BUCKETS // 2
    nf = np.maximum(n, 1).astype(np.float32)
    large = max_exact + (np.log(nf / max_exact) / math.log(MAX_DISTANCE / max_exact)
                         * (N_BUCKETS - max_exact)).astype(np.int32)
    large = np.minimum(large, N_BUCKETS - 1)
    return np.where(n < max_exact, n, large).astype(np.int32)


def _bias_from_buckets(tab, idx):
    onehot = (idx[None] == np.arange(N_BUCKETS).reshape((-1,) + (1,) * idx.ndim)).astype(np.float32)
    return jnp.einsum('bh,b...->h...', tab.astype(F32), jnp.asarray(onehot),
                      precision=lax.Precision.HIGHEST)


def _near_bias_tables(tab):
    kk = np.arange(QB)[:, None]
    qq = np.arange(QB)[None, :]
    return _bias_from_buckets(tab, np.stack([_t5_bucket_np(qq - kk), _t5_bucket_np(QB + qq - kk)]))


def _cmp_bias_table(tab):
    npr = np.arange(CMP_TAB)[:, None] - 16
    qq = np.arange(QB)[None, :]
    return _bias_from_buckets(tab, _t5_bucket_np(qq - CMP_STRIDE * npr - (CMP_BLOCK - 1)))


def _rmsnorm_kernel(*refs, n_in):
    x_refs, g_ref, o_ref = refs[:n_in], refs[n_in], refs[n_in + 1]
    off = 0
    for x_ref in x_refs:
        x = x_ref[...]
        w = x.shape[-1]
        y = x * lax.rsqrt(jnp.mean(x * x, axis=-1, keepdims=True) + EPS)
        o_ref[:, off:off + w] = (y * g_ref[:, off:off + w]).astype(o_ref.dtype)
        off += w


def _rmsnorm(xs, gain, tr=256):
    S = xs[0].shape[0]
    widths = [x.shape[1] for x in xs]
    n = sum(widths)
    tr = min(tr, S)
    return pl.pallas_call(
        functools.partial(_rmsnorm_kernel, n_in=len(xs)),
        grid=(S // tr,),
        in_specs=[pl.BlockSpec((tr, w), lambda i: (i, 0)) for w in widths]
        + [pl.BlockSpec((1, n), lambda i: (0, 0))],
        out_specs=pl.BlockSpec((tr, n), lambda i: (i, 0)),
        out_shape=jax.ShapeDtypeStruct((S, n), BF16),
        compiler_params=_cparams(("arbitrary",)),
        name="rmsnorm",
    )(*xs, gain.reshape(1, n).astype(F32))


NORM_ROW_CHUNKS = 2


def _dot_wt(a, wt):
    return lax.dot_general(a, wt.astype(BF16), (((1,), (1,)), ((), ())), preferred_element_type=F32)


def _inproj_a_kernel(kind_ref, tile_ref, a_ref, w_ref, g_ref, o_ref, w_s):
    j = pl.program_id(0)

    @pl.when(pl.program_id(1) == 0)
    def _():
        w_s[...] = w_ref[...].astype(BF16)

    @pl.when(kind_ref[j] == 0)
    def _():
        o_ref[...] = _dot_wt(a_ref[...], w_s[...]).astype(o_ref.dtype)

    @pl.when(kind_ref[j] == 1)
    def _():
        g = g_ref[0]
        hm = a_ref.shape[0] // NORM_ROW_CHUNKS
        for c in range(NORM_ROW_CHUNKS):
            rows = slice(c * hm, (c + 1) * hm)
            acc = _dot_wt(a_ref[rows, :], w_s[...])
            for h in range(acc.shape[1] // HEAD_DIM):
                sl = acc[:, h * HEAD_DIM:(h + 1) * HEAD_DIM]
                y = sl * lax.rsqrt(jnp.mean(sl * sl, axis=-1, keepdims=True) + EPS)
                o_ref[rows, h * HEAD_DIM:(h + 1) * HEAD_DIM] = (y * g).astype(o_ref.dtype)


def _inproj_a(hn, wt, kinds, tiles, gains, tm=1024, tn=512):
    S, D = hn.shape
    n_tiles = kinds.shape[0]
    tm = min(tm, S)
    grid_spec = pltpu.PrefetchScalarGridSpec(
        num_scalar_prefetch=2,
        grid=(n_tiles, S // tm),
        in_specs=[pl.BlockSpec((tm, D), lambda j, i, k, t: (i, 0)),
                  pl.BlockSpec((tn, D), lambda j, i, k, t: (t[j], 0)),
                  pl.BlockSpec((1, 1, HEAD_DIM), lambda j, i, k, t: (j, 0, 0))],
        out_specs=pl.BlockSpec((tm, tn), lambda j, i, k, t: (i, j)),
        scratch_shapes=[pltpu.VMEM((tn, D), BF16)],
    )
    return pl.pallas_call(
        _inproj_a_kernel,
        grid_spec=grid_spec,
        out_shape=jax.ShapeDtypeStruct((S, n_tiles * tn), BF16),
        compiler_params=_cparams(("arbitrary", "arbitrary")),
        name="inproj_heads",
    )(kinds, tiles, hn, wt, gains)


def _inproj_b_kernel(a_ref, w_ref, wg_ref, o_ref, gate_ref):
    a = a_ref[...]
    o_ref[...] = _dot_wt(a, w_ref[...])

    @pl.when(pl.program_id(1) == 0)
    def _():
        gate_ref[...] = 1.0 / (1.0 + jnp.exp(-_dot_wt(a, wg_ref[...])))


def _inproj_b(hn, wt, tile0, n_tiles, wg_t, tm=1024, tn=512):
    S, D = hn.shape
    tm = min(tm, S)
    ng = wg_t.shape[0]
    return pl.pallas_call(
        _inproj_b_kernel,
        grid=(S // tm, n_tiles),
        in_specs=[pl.BlockSpec((tm, D), lambda i, j: (i, 0)),
                  pl.BlockSpec((tn, D), lambda i, j: (tile0 + j, 0)),
                  pl.BlockSpec((ng, D), lambda i, j: (0, 0))],
        out_specs=[pl.BlockSpec((tm, tn), lambda i, j: (i, j)),
                   pl.BlockSpec((tm, ng), lambda i, j: (i, 0))],
        out_shape=[jax.ShapeDtypeStruct((S, n_tiles * tn), F32),
                   jax.ShapeDtypeStruct((S, ng), F32)],
        compiler_params=_cparams(("arbitrary", "arbitrary")),
        name="inproj_cmp_gates",
    )(hn, wt, wg_t)


def _gelu_tanh(x):
    return 0.5 * x * (1.0 + jnp.tanh(math.sqrt(2.0 / math.pi) * (x + 0.044715 * (x * x * x))))


def _compress_kernel(tk_ref, tv_ref, pos_ref, w1_ref, b1_ref, w2_ref, b2_ref, gk_ref,
                     kc_ref, vct_ref):
    nc = kc_ref.shape[1]

    def mlp(t_ref, kv):
        a0 = jnp.zeros((nc, w1_ref.shape[-1]), F32)
        a1 = jnp.zeros((nc, w1_ref.shape[-1]), F32)
        for b in range(CMP_STRIDE):
            t = t_ref[pl.ds(b, nc, stride=CMP_STRIDE), :]
            lo, hi = b, CMP_STRIDE + b
            a0 = a0 + jnp.dot((t + pos_ref[kv, lo:lo + 1, :]).astype(BF16), w1_ref[kv, lo],
                              preferred_element_type=F32)
            a1 = a1 + jnp.dot((t + pos_ref[kv, hi:hi + 1, :]).astype(BF16), w1_ref[kv, hi],
                              preferred_element_type=F32)
        h = a0 + pltpu.roll(a1, nc - 1, 0) + b1_ref[kv]
        h = _gelu_tanh(h)
        return jnp.dot(h.astype(BF16), w2_ref[kv], preferred_element_type=F32) + b2_ref[kv]

    ck = mlp(tk_ref, 0)
    ck = ck * lax.rsqrt(jnp.mean(ck * ck, axis=-1, keepdims=True) + EPS) * gk_ref[...]
    kc_ref[0] = ck.astype(kc_ref.dtype)
    cv = mlp(tv_ref, 1)
    vct_ref[0] = cv.T.astype(vct_ref.dtype)


def _compress(kv32, G, pos, w1, b1, w2, b2, gk):
    S = kv32.shape[0]
    NC = S // CMP_STRIDE
    hid = w1.shape[-1]
    return pl.pallas_call(
        _compress_kernel,
        grid=(G,),
        in_specs=[pl.BlockSpec((S, HEAD_DIM), lambda g: (0, g)),
                  pl.BlockSpec((S, HEAD_DIM), lambda g: (0, G + g)),
                  pl.BlockSpec((2, CMP_BLOCK, HEAD_DIM), lambda g: (0, 0, 0)),
                  pl.BlockSpec((2, CMP_BLOCK, HEAD_DIM, hid), lambda g: (0, 0, 0, 0)),
                  pl.BlockSpec((2, 1, hid), lambda g: (0, 0, 0)),
                  pl.BlockSpec((2, hid, HEAD_DIM), lambda g: (0, 0, 0)),
                  pl.BlockSpec((2, 1, HEAD_DIM), lambda g: (0, 0, 0)),
                  pl.BlockSpec((1, HEAD_DIM), lambda g: (0, 0))],
        out_specs=[pl.BlockSpec((1, NC, HEAD_DIM), lambda g: (g, 0, 0)),
                   pl.BlockSpec((1, HEAD_DIM, NC), lambda g: (g, 0, 0))],
        out_shape=[jax.ShapeDtypeStruct((G, NC, HEAD_DIM), BF16),
                   jax.ShapeDtypeStruct((G, HEAD_DIM, NC), BF16)],
        compiler_params=_cparams(("arbitrary",)),
        name="compress",
    )(kv32, kv32, pos, w1, b1, w2, b2, gk)


QW = GROUP * QB


def _stacked_queries(q_ref):
    return jnp.concatenate([q_ref[:, r * HEAD_DIM:(r + 1) * HEAD_DIM] for r in range(GROUP)], axis=0)


LOG2E = math.log2(math.e)
KEY_SCALE = SCALE * LOG2E
QSUB = 2
NSPLIT = 1
HW = QW // NSPLIT


def _halves():
    return [slice(h * HW, (h + 1) * HW) for h in range(NSPLIT)]


def _scores(k_t, qs):
    return lax.dot_general(k_t, qs, (((1,), (1,)), ((), ())), preferred_element_type=F32)


def _rel_bias(tab, c31):
    return (tab - c31) * LOG2E


def _init_state(m_ref, l_ref, acc_ref):
    m_ref[...] = jnp.full(m_ref.shape, NEG, F32)
    l_ref[...] = jnp.zeros(l_ref.shape, F32)
    acc_ref[...] = jnp.zeros(acc_ref.shape, F32)


def _online_update(m_ref, l_ref, acc_ref, lanes, t, v_t):
    m_old = m_ref[:, lanes]
    m_new = jnp.maximum(m_old, jnp.max(t, axis=0, keepdims=True))
    alpha = jnp.exp2(m_old - m_new)
    p = jnp.exp2(t - m_new)
    l_ref[:, lanes] = alpha * l_ref[:, lanes] + jnp.sum(p, axis=0, keepdims=True)
    pv = lax.dot_general(v_t, p.astype(BF16), (((0,), (0,)), ((), ())), preferred_element_type=F32)
    acc_ref[:, lanes] = acc_ref[:, lanes] * alpha + pv
    m_ref[:, lanes] = m_new


def _key_query_iotas():
    kk = lax.broadcasted_iota(jnp.int32, (QB, HW), 0)
    qq = lax.broadcasted_iota(jnp.int32, (QB, HW), 1) & (QB - 1)
    return kk, qq


def _load_kv(k_ref, v_ref, kb0, n):
    rows = pl.ds(pl.multiple_of(kb0 * QB, QB), n * QB)
    return k_ref[rows, :], v_ref[rows, :]


def _store_heads(o_ref, o_t):
    for r in range(GROUP):
        o_ref[:, r * HEAD_DIM:(r + 1) * HEAD_DIM] = o_t[:, r * QB:(r + 1) * QB].T


def _stack_heads(a, G):
    a = a.reshape((G, GROUP) + a.shape[1:])
    a = jnp.moveaxis(a, 1, -2)
    return a.reshape(a.shape[:-2] + (QW,))


def _band_kernel(q_ref, k_ref, v_ref, tab_ref, c31_ref, sink_ref, o_ref, m_ref, l_ref, acc_ref,
                 *, nback, use_sinks, transposed_out):
    c0 = pl.program_id(1) * QSUB
    _init_state(m_ref, l_ref, acc_ref)
    kk, qq = _key_query_iotas()
    c31 = c31_ref[0]

    def masked_scores(sub, kb0, deltas):
        k_t, v_t = _load_kv(k_ref, v_ref, kb0, len(deltas))
        s = _scores(k_t, _stacked_queries(q_ref.at[sub * QB:(sub + 1) * QB]))
        parts = []
        for t, delta in enumerate(deltas):
            sb = s[t * QB:(t + 1) * QB]
            if delta <= 1:
                sb = sb + _rel_bias(tab_ref[0, delta], c31)
            if delta == 0:
                sb = jnp.where(qq >= kk, sb, NEG)
            elif delta == nback:
                sb = jnp.where(qq < kk, sb, NEG)
            parts.append(sb)
        return (parts[0] if len(parts) == 1 else jnp.concatenate(parts, axis=0)), v_t

    def update(sub, s, v_t):
        _online_update(m_ref.at[sub], l_ref.at[sub], acc_ref.at[sub], slice(None), s, v_t)

    @pl.when(c0 >= nback)
    def _():
        band = list(range(nback, -1, -1))
        tiles = [masked_scores(sub, c0 + sub - nback, band) for sub in range(QSUB)]
        for sub, (s, v_t) in enumerate(tiles):
            update(sub, s, v_t)

    @pl.when(c0 < nback)
    def _():
        for sub in range(QSUB):
            for delta in range(nback, -1, -1):
                @pl.when(c0 + sub - delta >= 0)
                def _(sub=sub, delta=delta):
                    update(sub, *masked_scores(sub, c0 + sub - delta, [delta]))

    for sub in range(QSUB):
        l = l_ref[sub]
        acc = acc_ref[sub]
        if use_sinks:
            m = m_ref[sub]
            sk = _rel_bias(sink_ref[0], c31)
            m_f = jnp.maximum(m, sk)
            a = jnp.exp2(m - m_f)
            l = l * a + jnp.exp2(sk - m_f)
            acc = acc * a
        o_t = acc * (1.0 / l)
        if transposed_out:
            o_ref[0, sub] = o_t
        else:
            _store_heads(o_ref.at[sub * QB:(sub + 1) * QB], o_t)


def _qkv_specs(S, q_col, k_col, v_col, q_rows=QB):
    return [pl.BlockSpec((q_rows, GROUP * HEAD_DIM), lambda g, c: (c, q_col + g)),
            pl.BlockSpec((S, HEAD_DIM), lambda g, c: (0, k_col + g)),
            pl.BlockSpec((S, HEAD_DIM), lambda g, c: (0, v_col + g))]


def _band_attention(proj, G, q_col, k_col, v_col, tabs, c31, sinks, window, transposed_out):
    S = proj.shape[0]
    nkb = S // QB
    nback = -(-(window - 1) // QB)
    use_sinks = sinks is not None
    if sinks is None:
        sinks = jnp.zeros_like(c31)
    if transposed_out:
        out_shape = jax.ShapeDtypeStruct((G, nkb, HEAD_DIM, QW), F32)
        out_spec = pl.BlockSpec((1, QSUB, HEAD_DIM, QW), lambda g, c: (g, c, 0, 0))
    else:
        out_shape = jax.ShapeDtypeStruct((S, G * GROUP * HEAD_DIM), F32)
        out_spec = pl.BlockSpec((QSUB * QB, GROUP * HEAD_DIM), lambda g, c: (c, g))
    row = pl.BlockSpec((1, 1, QW), lambda g, c: (g, 0, 0))
    return pl.pallas_call(
        functools.partial(_band_kernel, nback=nback, use_sinks=use_sinks,
                          transposed_out=transposed_out),
        grid=(G, nkb // QSUB),
        in_specs=_qkv_specs(S, q_col, k_col, v_col, QSUB * QB)
        + [pl.BlockSpec((1, 2, QB, QW), lambda g, c: (g, 0, 0, 0)), row, row],
        out_specs=out_spec,
        out_shape=out_shape,
        scratch_shapes=[pltpu.VMEM((QSUB, 1, QW), F32), pltpu.VMEM((QSUB, 1, QW), F32),
                        pltpu.VMEM((QSUB, HEAD_DIM, QW), F32)],
        compiler_params=_cparams(("arbitrary", "arbitrary")),
        name="band_attention_w%d" % window,
    )(proj, proj, proj, tabs, c31, sinks)


def _cmp_select_kernel(q_ref, kc_ref, vct_ref, tab_ref, c31_ref, ovl_ref, ot_ref, sel_ref, s_ref,
                       *, n_valid, topk):
    c0 = pl.program_id(1) * QSUB
    nc = kc_ref.shape[1]
    nb = ovl_ref.shape[0]
    rown = lax.broadcasted_iota(jnp.int32, (nc, QW), 0)
    wrow = lax.broadcasted_iota(jnp.int32, (CMP_WIN, QW), 0)
    wq = lax.broadcasted_iota(jnp.int32, (CMP_WIN, QW), 1) & (QB - 1)
    lane_q = lax.broadcasted_iota(jnp.int32, (1, QW), 1) & (QB - 1)

    for sub in range(QSUB):
        c = c0 + sub
        qs = _stacked_queries(q_ref.at[sub * QB:(sub + 1) * QB])
        w0 = pl.multiple_of(jnp.maximum(8 * c - 16, 0), 8)
        toff = pl.multiple_of(w0 - 8 * c + 16, 8)
        n_abs = w0 + wrow
        dist = (c * QB + wq) - (n_abs * CMP_STRIDE + (CMP_BLOCK - 1))
        valid_w = (dist >= 0) & (n_abs < n_valid)
        s_ref[sub] = jnp.where(rown < w0, _scores(kc_ref[0], qs), NEG)
        bias_w = _rel_bias(tab_ref[0, pl.ds(toff, CMP_WIN), :], c31_ref[0])
        s_w = _scores(kc_ref[0, pl.ds(w0, CMP_WIN), :], qs) + bias_w
        s_ref[sub, pl.ds(w0, CMP_WIN), :] = jnp.where(valid_w, s_w, NEG)

    p_groups = []
    for sub in range(QSUB):
        has_any = ((c0 + sub) * QB + lane_q >= CMP_BLOCK - 1).astype(F32)
        s = s_ref[sub]
        m = jnp.max(s, axis=0, keepdims=True)
        e = jnp.exp2(s - m)
        p = e * (has_any / jnp.sum(e, axis=0, keepdims=True))
        ot_ref[0, sub] = jnp.dot(vct_ref[0], p.astype(BF16), preferred_element_type=F32)
        p_grp = p[:, 0:QB]
        for r in range(1, GROUP):
            p_grp = p_grp + p[:, r * QB:(r + 1) * QB]
        p_groups.append(p_grp.astype(BF16))

    nq = QSUB * QB
    scores = jnp.dot(ovl_ref[...], jnp.concatenate(p_groups, axis=1), preferred_element_type=F32)
    blk = lax.broadcasted_iota(jnp.int32, (nb, nq), 0)
    cur = lax.shift_right_logical(c0 * QB + lax.broadcasted_iota(jnp.int32, (nb, nq), 1),
                                  int(math.log2(SLC_BLOCK)))
    forced = (blk == 0) | (blk == cur) | (blk == cur - 1)
    work = jnp.where(forced, FORCE_SCORE, jnp.where(blk <= cur, scores, -1.0))
    blk_f = blk.astype(F32)
    sel = jnp.zeros((nb, nq), F32)
    for _ in range(topk):
        mx = jnp.max(work, axis=0, keepdims=True)
        first = jnp.min(jnp.where(work == mx, blk_f, float(nb)), axis=0, keepdims=True)
        hit = blk_f == first
        sel = jnp.where(hit, 1.0, sel)
        work = jnp.where(hit, -3e38, work)
    sel_ref[0] = sel


def _cmp_select(proj, q_col, kc, vct, tab, c31, ovl_t, n_valid):
    nkb = proj.shape[0] // QB
    G, NC = kc.shape[0], kc.shape[1]
    NB = ovl_t.shape[0]
    return pl.pallas_call(
        functools.partial(_cmp_select_kernel, n_valid=n_valid, topk=min(SLC_TOPK, NB)),
        grid=(G, nkb // QSUB),
        in_specs=[pl.BlockSpec((QSUB * QB, GROUP * HEAD_DIM), lambda g, c: (c, q_col + g)),
                  pl.BlockSpec((1, NC, HEAD_DIM), lambda g, c: (g, 0, 0)),
                  pl.BlockSpec((1, HEAD_DIM, NC), lambda g, c: (g, 0, 0)),
                  pl.BlockSpec((1, CMP_TAB, QW), lambda g, c: (g, 0, 0)),
                  pl.BlockSpec((1, 1, QW), lambda g, c: (g, 0, 0)),
                  pl.BlockSpec((NB, NC), lambda g, c: (0, 0))],
        out_specs=[pl.BlockSpec((1, QSUB, HEAD_DIM, QW), lambda g, c: (g, c, 0, 0)),
                   pl.BlockSpec((1, NB, QSUB * QB), lambda g, c: (g, 0, c))],
        out_shape=[jax.ShapeDtypeStruct((G, nkb, HEAD_DIM, QW), F32),
                   jax.ShapeDtypeStruct((G, NB, nkb * QB), F32)],
        scratch_shapes=[pltpu.VMEM((QSUB, NC, QW), F32)],
        compiler_params=_cparams(("arbitrary", "arbitrary")),
        name="cmp_attention_select",
    )(proj, kc, vct, tab, c31, ovl_t)


FAR_BLOCKS = 4


def _sel_kernel(q_ref, k_ref, v_ref, tab_ref, c31_ref, sel_ref, gate_ref, ocmp_ref, owin_ref,
                o_ref, m_ref, l_ref, acc_ref, s_a, s_b, p_a, p_b, al_a, al_b):
    c = pl.program_id(1)
    _init_state(m_ref, l_ref, acc_ref)
    kk, qq = _key_query_iotas()
    qt = _stacked_queries(q_ref)
    c31 = c31_ref[0]
    spb = QB // SLC_BLOCK
    tk = FAR_BLOCKS * QB
    last_tile = k_ref.shape[0] // tk - 1
    n_far = jnp.maximum(c - 1, 0)
    n_tiles = (n_far + FAR_BLOCKS - 1) // FAR_BLOCKS

    def sel_mask(kb0, n, row_limit=None):
        pieces = []
        for t in range(n * spb):
            r = kb0 * spb + t
            row = sel_ref[0, pl.ds(r, 1), :]
            if row_limit is not None:
                row = jnp.where(r < row_limit, row, 0.0)
            row = jnp.concatenate([row] * (HW // QB), axis=1)
            pieces.append(jnp.broadcast_to(row, (SLC_BLOCK, HW)))
        return jnp.concatenate(pieces, axis=0) > 0.5

    def tile_rows(i):
        return pl.ds(pl.multiple_of(jnp.minimum(i, last_tile) * tk, tk), tk)

    def far_scores(i, s_buf):
        s_buf[...] = _scores(k_ref[tile_rows(i), :], qt)

    def far_softmax(i, s_buf, p_buf, al_buf):
        t = jnp.where(sel_mask(i * FAR_BLOCKS, FAR_BLOCKS, n_far * spb), s_buf[...], NEG)
        m_old = m_ref[...]
        m_new = jnp.maximum(m_old, jnp.max(t, axis=0, keepdims=True))
        alpha = jnp.exp2(m_old - m_new)
        p = jnp.exp2(t - m_new)
        l_ref[...] = alpha * l_ref[...] + jnp.sum(p, axis=0, keepdims=True)
        m_ref[...] = m_new
        p_buf[...] = p.astype(BF16)
        al_buf[...] = alpha

    def far_values(i, p_buf, al_buf):
        pv = lax.dot_general(v_ref[tile_rows(i), :], p_buf[...], (((0,), (0,)), ((), ())),
                             preferred_element_type=F32)
        acc_ref[...] = acc_ref[...] * al_buf[...] + pv

    far_scores(0, s_a)
    p_b[...] = jnp.zeros(p_b.shape, BF16)
    al_b[...] = jnp.ones(al_b.shape, F32)

    def pair_body(j, carry):
        i = 2 * j
        far_scores(i + 1, s_b)
        far_softmax(i, s_a, p_a, al_a)
        far_values(jnp.maximum(i - 1, 0), p_b, al_b)
        far_scores(i + 2, s_a)
        far_softmax(i + 1, s_b, p_b, al_b)
        far_values(i, p_a, al_a)
        return carry

    n_pairs = n_tiles // 2
    lax.fori_loop(0, n_pairs, pair_body, 0)
    far_values(jnp.maximum(2 * n_pairs - 1, 0), p_b, al_b)

    @pl.when(n_tiles > 2 * n_pairs)
    def _():
        far_softmax(n_tiles - 1, s_a, p_a, al_a)
        far_values(n_tiles - 1, p_a, al_a)

    def near(kb0, deltas):
        k_t, v_t = _load_kv(k_ref, v_ref, kb0, len(deltas))
        masks = []
        for t, delta in enumerate(deltas):
            msk = sel_mask(kb0 + t, 1)
            masks.append(msk & (qq >= kk) if delta == 0 else msk)
        for lanes in _halves():
            s = _scores(k_t, qt[lanes])
            parts = []
            for t, delta in enumerate(deltas):
                sb = s[t * QB:(t + 1) * QB] + _rel_bias(tab_ref[0, delta, :, lanes], c31[:, lanes])
                parts.append(jnp.where(masks[t], sb, NEG))
            s = parts[0] if len(parts) == 1 else jnp.concatenate(parts, axis=0)
            _online_update(m_ref, l_ref, acc_ref, lanes, s, v_t)

    @pl.when(c >= 1)
    def _():
        near(c - 1, [1, 0])

    @pl.when(c == 0)
    def _():
        near(0, [0])

    o_slc = acc_ref[...] * (1.0 / l_ref[...])
    o_t = (gate_ref[0, 0, 0:1, :] * ocmp_ref[0, 0] + gate_ref[0, 0, 1:2, :] * o_slc
           + gate_ref[0, 0, 2:3, :] * owin_ref[0, 0])
    _store_heads(o_ref, o_t)


def _sel_attention(proj, q_col, k_col, v_col, tabs, c31, sel_t, gates_s, ocmp_t, owin_t):
    S = proj.shape[0]
    G, NB = sel_t.shape[0], sel_t.shape[1]
    nkb = S // QB
    tile = pl.BlockSpec((1, 1, HEAD_DIM, QW), lambda g, c: (g, c, 0, 0))
    return pl.pallas_call(
        _sel_kernel,
        grid=(G, nkb),
        in_specs=_qkv_specs(S, q_col, k_col, v_col)
        + [pl.BlockSpec((1, 2, QB, QW), lambda g, c: (g, 0, 0, 0)),
                  pl.BlockSpec((1, 1, QW), lambda g, c: (g, 0, 0)),
                  pl.BlockSpec((1, NB, QB), lambda g, c: (g, 0, c)),
                  pl.BlockSpec((1, 1, N_BRANCH, QW), lambda g, c: (g, c, 0, 0)),
                  tile, tile],
        out_specs=pl.BlockSpec((QB, GROUP * HEAD_DIM), lambda g, c: (c, g)),
        out_shape=jax.ShapeDtypeStruct((S, G * GROUP * HEAD_DIM), F32),
        scratch_shapes=[pltpu.VMEM((1, QW), F32), pltpu.VMEM((1, QW), F32),
                        pltpu.VMEM((HEAD_DIM, QW), F32)]
        + [pltpu.VMEM((FAR_BLOCKS * QB, QW), F32)] * 2
        + [pltpu.VMEM((FAR_BLOCKS * QB, QW), BF16)] * 2
        + [pltpu.VMEM((1, QW), F32)] * 2,
        compiler_params=_cparams(("arbitrary", "arbitrary")),
        name="selected_attention_combine",
    )(proj, proj, proj, tabs, c31, sel_t, gates_s, ocmp_t, owin_t)


def _outproj_kernel(a_ref, w_ref, x_ref, o_ref, w_s):
    @pl.when(pl.program_id(1) == 0)
    def _():
        w_s[...] = w_ref[...].astype(BF16)

    o_ref[...] = x_ref[...] + jnp.dot(a_ref[...], w_s[...], preferred_element_type=F32)


def _outproj(a, w, x, tm=1024, tn=512):
    S, D = a.shape
    N = w.shape[1]
    tm = min(tm, S)
    return pl.pallas_call(
        _outproj_kernel,
        grid=(N // tn, S // tm),
        in_specs=[pl.BlockSpec((tm, D), lambda j, i: (i, 0)),
                  pl.BlockSpec((D, tn), lambda j, i: (0, j)),
                  pl.BlockSpec((tm, tn), lambda j, i: (i, j))],
        out_specs=pl.BlockSpec((tm, tn), lambda j, i: (i, j)),
        out_shape=jax.ShapeDtypeStruct((S, N), F32),
        scratch_shapes=[pltpu.VMEM((D, tn), BF16)],
        compiler_params=_cparams(("arbitrary", "arbitrary")),
        name="outproj_residual",
    )(a, w, x)


HALO = 16
FFN_ROW_CHUNKS = 2


def _ffn_a_kernel(halo_ref, a_ref, wg_ref, wu_ref, cw_ref, cb_ref, o_ref, wg_s, wu_s):
    i = pl.program_id(1)
    tm = a_ref.shape[0]

    @pl.when(i == 0)
    def _():
        wg_s[...] = wg_ref[...].astype(BF16)
        wu_s[...] = wu_ref[...].astype(BF16)

    halo = halo_ref[...]
    halo = jnp.where(i > 0, halo, jnp.zeros_like(halo))
    wg = wg_s[...]
    wu = wu_s[...]
    hm = tm // FFN_ROW_CHUNKS
    prev = jnp.dot(halo, wg, preferred_element_type=F32)
    for h in range(FFN_ROW_CHUNKS):
        a = a_ref[h * hm:(h + 1) * hm, :]
        gate = jnp.dot(a, wg, preferred_element_type=F32)
        up = jnp.dot(a, wu, preferred_element_type=F32)
        gext = jnp.concatenate([prev, gate], axis=0)
        g1 = gext[HALO - 1:HALO - 1 + hm]
        g2 = gext[HALO - 2:HALO - 2 + hm]
        y = cb_ref[...] + g2 * cw_ref[0:1, :]
        y = y + g1 * cw_ref[1:2, :]
        y = y + gate * cw_ref[2:3, :]
        act = y * (1.0 / (1.0 + jnp.exp(-y)))
        o_ref[h * hm:(h + 1) * hm, :] = (act * up).astype(o_ref.dtype)
        prev = gate[hm - HALO:]


def _ffn_a(hf, wg, wu, cw, cb, tm=1024, tf=256):
    S, D = hf.shape
    Fp = wg.shape[1]
    assert Fp % tf == 0
    tm = min(tm, S)
    hb = tm // HALO
    return pl.pallas_call(
        _ffn_a_kernel,
        grid=(Fp // tf, S // tm),
        in_specs=[pl.BlockSpec((HALO, D), lambda f, i: (jnp.maximum(i * hb - 1, 0), 0)),
                  pl.BlockSpec((tm, D), lambda f, i: (i, 0)),
                  pl.BlockSpec((D, tf), lambda f, i: (0, f)),
                  pl.BlockSpec((D, tf), lambda f, i: (0, f)),
                  pl.BlockSpec((cw.shape[0], tf), lambda f, i: (0, f)),
                  pl.BlockSpec((1, tf), lambda f, i: (0, f))],
        out_specs=pl.BlockSpec((tm, tf), lambda f, i: (i, f)),
        out_shape=jax.ShapeDtypeStruct((S, Fp), BF16),
        scratch_shapes=[pltpu.VMEM((D, tf), BF16), pltpu.VMEM((D, tf), BF16)],
        compiler_params=_cparams(("arbitrary", "arbitrary")),
        name="ffn_gate_up",
    )(hf, hf, wg, wu, cw, cb)


def _ffn_b_kernel(a_ref, w_ref, x_ref, o_ref):
    o_ref[...] = x_ref[...] + jnp.dot(a_ref[...], w_ref[...], preferred_element_type=F32)


def _ffn_b(h, w, x, tm=512, tn=512):
    S, F = h.shape
    N = w.shape[1]
    tm = min(tm, S)
    return pl.pallas_call(
        _ffn_b_kernel,
        grid=(S // tm, N // tn),
        in_specs=[pl.BlockSpec((tm, F), lambda i, j: (i, 0)),
                  pl.BlockSpec((F, tn), lambda i, j: (0, j)),
                  pl.BlockSpec((tm, tn), lambda i, j: (i, j))],
        out_specs=pl.BlockSpec((tm, tn), lambda i, j: (i, j)),
        out_shape=jax.ShapeDtypeStruct((S, N), F32),
        compiler_params=_cparams(("arbitrary", "arbitrary")),
        name="ffn_down_residual",
    )(h, w, x)


def _layer(x, rel_bias, norm_mix_g, w_in, a_q_norm_g, a_k_norm_g, a_sinks, b_q_norm_g, b_k_norm_g,
           cmp_pos_emb, cmp_w1, cmp_b1, cmp_w2, cmp_b2, out_norm_g, w_out, norm_ffn_g, w_gate,
           w_up, conv_w, conv_b, w_down):
    S, D = x.shape
    aw = A_Q_HEADS * HEAD_DIM
    akv = A_KV_HEADS * HEAD_DIM
    bw = B_Q_HEADS * HEAD_DIM
    bkv = B_KV_HEADS * HEAD_DIM
    sizes = [aw, akv, akv, bw] + [bkv] * 6 + [N_BRANCH * B_Q_HEADS]
    offs = np.concatenate([[0], np.cumsum(sizes)]).tolist()
    wt = jnp.swapaxes(w_in, 0, 1)
    tn = 512
    order_a = [0, 1, 2, 3, 6, 7, 8, 9]
    n_gate = sizes[10]
    wg_t = jnp.pad(wt[offs[10]:offs[10] + n_gate], ((0, QB - n_gate), (0, 0)))
    one = jnp.ones((HEAD_DIM,), F32)
    seg_gain = {0: a_q_norm_g, 1: a_k_norm_g * KEY_SCALE, 3: b_q_norm_g,
                6: b_k_norm_g[1] * KEY_SCALE, 8: b_k_norm_g[2] * KEY_SCALE}
    kinds, tiles, gains, col = [], [], [], {}
    c0 = 0
    for k in order_a:
        col[k] = c0
        for t in range(sizes[k] // tn):
            kinds.append(1 if k in seg_gain else 0)
            tiles.append(offs[k] // tn + t)
            gains.append(seg_gain.get(k, one))
        c0 += sizes[k]
    kinds = jnp.asarray(kinds, jnp.int32)
    tiles = jnp.asarray(tiles, jnp.int32)
    gains = jnp.stack(gains).astype(F32).reshape(len(gains), 1, HEAD_DIM)

    hn = _rmsnorm([x], norm_mix_g)
    proj = _inproj_a(hn, wt, kinds, tiles, gains, tn=tn)
    kv32, gates = _inproj_b(hn, wt, offs[4] // tn, 2 * bkv // tn, wg_t)

    tab_a = rel_bias[:, :A_Q_HEADS]
    tab_b = rel_bias[:, A_Q_HEADS:]
    near_a = _stack_heads(_near_bias_tables(tab_a), A_KV_HEADS)
    near_b = _stack_heads(_near_bias_tables(tab_b), B_KV_HEADS)
    head_row = lambda v, G: _stack_heads(jnp.broadcast_to(v.astype(F32)[:, None, None],
                                                          (v.shape[0], 1, QB)), G)
    c31_a = head_row(tab_a[N_BUCKETS - 1], A_KV_HEADS)
    c31_b = head_row(tab_b[N_BUCKETS - 1], B_KV_HEADS)

    qcol = lambda k: col[k] // (GROUP * HEAD_DIM)
    hcol = lambda k: col[k] // HEAD_DIM
    o_a = _band_attention(proj, A_KV_HEADS, qcol(0), hcol(1), hcol(2), near_a, c31_a,
                          head_row(a_sinks, A_KV_HEADS), A_WINDOW, transposed_out=False)

    NC = S // CMP_STRIDE
    n_cmp = (S - CMP_BLOCK) // CMP_STRIDE + 1
    w1 = cmp_w1.reshape(2, CMP_BLOCK, HEAD_DIM, cmp_w1.shape[-1]).astype(BF16)
    kc, vct = _compress(kv32, B_KV_HEADS, cmp_pos_emb.astype(F32), w1, cmp_b1[:, None, :].astype(F32),
                        cmp_w2.astype(BF16), cmp_b2[:, None, :].astype(F32),
                        (b_k_norm_g[0] * KEY_SCALE).reshape(1, HEAD_DIM).astype(F32))

    NB = S // SLC_BLOCK
    ii = np.arange(NC)[None, :]
    jj = np.arange(NB)[:, None]
    ovl_t = ((ii * CMP_STRIDE <= jj * SLC_BLOCK + SLC_BLOCK - 1)
             & (ii * CMP_STRIDE + CMP_BLOCK - 1 >= jj * SLC_BLOCK) & (ii < n_cmp))
    ovl_t = jnp.asarray(ovl_t.astype(np.float32), BF16)
    ocmp_t, sel_t = _cmp_select(proj, qcol(3), kc, vct,
                                _stack_heads(_cmp_bias_table(tab_b), B_KV_HEADS), c31_b, ovl_t, n_cmp)

    owin_t = _band_attention(proj, B_KV_HEADS, qcol(3), hcol(8), hcol(9), near_b, c31_b, None,
                             B_WINDOW, transposed_out=True)
    gates_s = gates[:, :n_gate].reshape(S // QB, QB, B_KV_HEADS, GROUP, N_BRANCH)
    gates_s = gates_s.transpose(2, 0, 4, 3, 1).reshape(B_KV_HEADS, S // QB, N_BRANCH, QW)
    o_b = _sel_attention(proj, qcol(3), hcol(6), hcol(7), near_b, c31_b, sel_t, gates_s,
                         ocmp_t, owin_t)

    on = _rmsnorm([o_a, o_b], out_norm_g)
    x2 = _outproj(on, w_out, x)

    hf = _rmsnorm([x2], norm_ffn_g)
    hmid = _ffn_a(hf, w_gate, w_up, conv_w.astype(F32), conv_b[None, :].astype(F32))
    return _ffn_b(hmid, w_down.astype(BF16), x2)


def kernel(x, rel_bias, norm_mix_g, w_in, a_q_norm_g, a_k_norm_g, a_sinks, b_q_norm_g, b_k_norm_g,
           cmp_pos_emb, cmp_w1, cmp_b1, cmp_w2, cmp_b2, out_norm_g, w_out, norm_ffn_g, w_gate, w_up,
           conv_w, conv_b, w_down):
    depth = w_in.shape[0]
    batch = x.shape[0]
    outs = []
    for b in range(batch):
        h = x[b]
        for l in range(depth):
            h = _layer(h, rel_bias, norm_mix_g[l], w_in[l], a_q_norm_g[l], a_k_norm_g[l], a_sinks[l],
                       b_q_norm_g[l], b_k_norm_g[l], cmp_pos_emb[l], cmp_w1[l], cmp_b1[l], cmp_w2[l],
                       cmp_b2[l], out_norm_g[l], w_out[l], norm_ffn_g[l], w_gate[l], w_up[l],
                       conv_w[l], conv_b[l], w_down[l])
        outs.append(h)
    return jnp.stack(outs)
```

```python
import functools
import math

import numpy as np
import jax
import jax.numpy as jnp
from jax import lax
from jax.experimental import pallas as pl
from jax.experimental.pallas import tpu as pltpu

F32 = jnp.float32
BF16 = jnp.bfloat16

HEAD_DIM = 128
A_Q_HEADS = 16
A_KV_HEADS = 4
B_Q_HEADS = 16
B_KV_HEADS = 4
GROUP = 4
A_WINDOW = 128
B_WINDOW = 512
CMP_BLOCK = 32
CMP_STRIDE = 16
SLC_BLOCK = 64
SLC_TOPK = 16
N_BRANCH = 3
N_BUCKETS = 32
MAX_DISTANCE = 128
EPS = 1e-6
NEG = -1e30
FORCE_SCORE = 1e6
SCALE = HEAD_DIM ** -0.5

QB = 128
CMP_WIN = 24
CMP_TAB = 40
VMEM_LIMIT = 56 * 1024 * 1024


def _cparams(sem, **kw):
    return pltpu.CompilerParams(dimension_semantics=sem, vmem_limit_bytes=VMEM_LIMIT, **kw)


def _t5_bucket_np(dist):
    n = np.maximum(dist, 0)
    max_exact = N_BUCKETS // 2
    nf = np.maximum(n, 1).astype(np.float32)
    large = max_exact + (np.log(nf / max_exact) / math.log(MAX_DISTANCE / max_exact)
                         * (N_BUCKETS - max_exact)).astype(np.int32)
    large = np.minimum(large, N_BUCKETS - 1)
    return np.where(n < max_exact, n, large).astype(np.int32)


def _bias_from_buckets(tab, idx):
    onehot = (idx[None] == np.arange(N_BUCKETS).reshape((-1,) + (1,) * idx.ndim)).astype(np.float32)
    return jnp.einsum('bh,b...->h...', tab.astype(F32), jnp.asarray(onehot),
                      precision=lax.Precision.HIGHEST)


def _near_bias_tables(tab):
    kk = np.arange(QB)[:, None]
    qq = np.arange(QB)[None, :]
    return _bias_from_buckets(tab, np.stack([_t5_bucket_np(qq - kk), _t5_bucket_np(QB + qq - kk)]))


def _cmp_bias_table(tab):
    npr = np.arange(CMP_TAB)[:, None] - 16
    qq = np.arange(QB)[None, :]
    return _bias_from_buckets(tab, _t5_bucket_np(qq - CMP_STRIDE * npr - (CMP_BLOCK - 1)))


def _rmsnorm_kernel(*refs, n_in):
    x_refs, g_ref, o_ref = refs[:n_in], refs[n_in], refs[n_in + 1]
    off = 0
    for x_ref in x_refs:
        x = x_ref[...]
        w = x.shape[-1]
        y = x * lax.rsqrt(jnp.mean(x * x, axis=-1, keepdims=True) + EPS)
        o_ref[:, off:off + w] = (y * g_ref[:, off:off + w]).astype(o_ref.dtype)
        off += w


def _rmsnorm(xs, gain, tr=256):
    S = xs[0].shape[0]
    widths = [x.shape[1] for x in xs]
    n = sum(widths)
    tr = min(tr, S)
    return pl.pallas_call(
        functools.partial(_rmsnorm_kernel, n_in=len(xs)),
        grid=(S // tr,),
        in_specs=[pl.BlockSpec((tr, w), lambda i: (i, 0)) for w in widths]
        + [pl.BlockSpec((1, n), lambda i: (0, 0))],
        out_specs=pl.BlockSpec((tr, n), lambda i: (i, 0)),
        out_shape=jax.ShapeDtypeStruct((S, n), BF16),
        compiler_params=_cparams(("arbitrary",)),
        name="rmsnorm",
    )(*xs, gain.reshape(1, n).astype(F32))


NORM_ROW_CHUNKS = 2


def _dot_wt(a, wt):
    return lax.dot_general(a, wt.astype(BF16), (((1,), (1,)), ((), ())), preferred_element_type=F32)


def _inproj_a_kernel(kind_ref, tile_ref, a_ref, w_ref, g_ref, o_ref, w_s):
    j = pl.program_id(0)

    @pl.when(pl.program_id(1) == 0)
    def _():
        w_s[...] = w_ref[...].astype(BF16)

    @pl.when(kind_ref[j] == 0)
    def _():
        o_ref[...] = _dot_wt(a_ref[...], w_s[...]).astype(o_ref.dtype)

    @pl.when(kind_ref[j] == 1)
    def _():
        g = g_ref[0]
        hm = a_ref.shape[0] // NORM_ROW_CHUNKS
        for c in range(NORM_ROW_CHUNKS):
            rows = slice(c * hm, (c + 1) * hm)
            acc = _dot_wt(a_ref[rows, :], w_s[...])
            for h in range(acc.shape[1] // HEAD_DIM):
                sl = acc[:, h * HEAD_DIM:(h + 1) * HEAD_DIM]
                y = sl * lax.rsqrt(jnp.mean(sl * sl, axis=-1, keepdims=True) + EPS)
                o_ref[rows, h * HEAD_DIM:(h + 1) * HEAD_DIM] = (y * g).astype(o_ref.dtype)


def _inproj_a(hn, wt, kinds, tiles, gains, tm=1024, tn=512):
    S, D = hn.shape
    n_tiles = kinds.shape[0]
    tm = min(tm, S)
    grid_spec = pltpu.PrefetchScalarGridSpec(
        num_scalar_prefetch=2,
        grid=(n_tiles, S // tm),
        in_specs=[pl.BlockSpec((tm, D), lambda j, i, k, t: (i, 0)),
                  pl.BlockSpec((tn, D), lambda j, i, k, t: (t[j], 0)),
                  pl.BlockSpec((1, 1, HEAD_DIM), lambda j, i, k, t: (j, 0, 0))],
        out_specs=pl.BlockSpec((tm, tn), lambda j, i, k, t: (i, j)),
        scratch_shapes=[pltpu.VMEM((tn, D), BF16)],
    )
    return pl.pallas_call(
        _inproj_a_kernel,
        grid_spec=grid_spec,
        out_shape=jax.ShapeDtypeStruct((S, n_tiles * tn), BF16),
        compiler_params=_cparams(("arbitrary", "arbitrary")),
        name="inproj_heads",
    )(kinds, tiles, hn, wt, gains)


def _inproj_b_kernel(a_ref, w_ref, wg_ref, o_ref, gate_ref):
    a = a_ref[...]
    o_ref[...] = _dot_wt(a, w_ref[...])

    @pl.when(pl.program_id(1) == 0)
    def _():
        gate_ref[...] = 1.0 / (1.0 + jnp.exp(-_dot_wt(a, wg_ref[...])))


def _inproj_b(hn, wt, tile0, n_tiles, wg_t, tm=1024, tn=512):
    S, D = hn.shape
    tm = min(tm, S)
    ng = wg_t.shape[0]
    return pl.pallas_call(
        _inproj_b_kernel,
        grid=(S // tm, n_tiles),
        in_specs=[pl.BlockSpec((tm, D), lambda i, j: (i, 0)),
                  pl.BlockSpec((tn, D), lambda i, j: (tile0 + j, 0)),
                  pl.BlockSpec((ng, D), lambda i, j: (0, 0))],
        out_specs=[pl.BlockSpec((tm, tn), lambda i, j: (i, j)),
                   pl.BlockSpec((tm, ng), lambda i, j: (i, 0))],
        out_shape=[jax.ShapeDtypeStruct((S, n_tiles * tn), F32),
                   jax.ShapeDtypeStruct((S, ng), F32)],
        compiler_params=_cparams(("arbitrary", "arbitrary")),
        name="inproj_cmp_gates",
    )(hn, wt, wg_t)


def _gelu_tanh(x):
    return 0.5 * x * (1.0 + jnp.tanh(math.sqrt(2.0 / math.pi) * (x + 0.044715 * (x * x * x))))


def _compress_kernel(tk_ref, tv_ref, pos_ref, w1_ref, b1_ref, w2_ref, b2_ref, gk_ref,
                     kc_ref, vct_ref):
    nc = kc_ref.shape[1]

    def mlp(t_ref, kv):
        a0 = jnp.zeros((nc, w1_ref.shape[-1]), F32)
        a1 = jnp.zeros((nc, w1_ref.shape[-1]), F32)
        for b in range(CMP_STRIDE):
            t = t_ref[pl.ds(b, nc, stride=CMP_STRIDE), :]
            lo, hi = b, CMP_STRIDE + b
            a0 = a0 + jnp.dot((t + pos_ref[kv, lo:lo + 1, :]).astype(BF16), w1_ref[kv, lo],
                              preferred_element_type=F32)
            a1 = a1 + jnp.dot((t + pos_ref[kv, hi:hi + 1, :]).astype(BF16), w1_ref[kv, hi],
                              preferred_element_type=F32)
        h = a0 + pltpu.roll(a1, nc - 1, 0) + b1_ref[kv]
        h = _gelu_tanh(h)
        return jnp.dot(h.astype(BF16), w2_ref[kv], preferred_element_type=F32) + b2_ref[kv]

    ck = mlp(tk_ref, 0)
    ck = ck * lax.rsqrt(jnp.mean(ck * ck, axis=-1, keepdims=True) + EPS) * gk_ref[...]
    kc_ref[0] = ck.astype(kc_ref.dtype)
    cv = mlp(tv_ref, 1)
    vct_ref[0] = cv.T.astype(vct_ref.dtype)


def _compress(kv32, G, pos, w1, b1, w2, b2, gk):
    S = kv32.shape[0]
    NC = S // CMP_STRIDE
    hid = w1.shape[-1]
    return pl.pallas_call(
        _compress_kernel,
        grid=(G,),
        in_specs=[pl.BlockSpec((S, HEAD_DIM), lambda g: (0, g)),
                  pl.BlockSpec((S, HEAD_DIM), lambda g: (0, G + g)),
                  pl.BlockSpec((2, CMP_BLOCK, HEAD_DIM), lambda g: (0, 0, 0)),
                  pl.BlockSpec((2, CMP_BLOCK, HEAD_DIM, hid), lambda g: (0, 0, 0, 0)),
                  pl.BlockSpec((2, 1, hid), lambda g: (0, 0, 0)),
                  pl.BlockSpec((2, hid, HEAD_DIM), lambda g: (0, 0, 0)),
                  pl.BlockSpec((2, 1, HEAD_DIM), lambda g: (0, 0, 0)),
                  pl.BlockSpec((1, HEAD_DIM), lambda g: (0, 0))],
        out_specs=[pl.BlockSpec((1, NC, HEAD_DIM), lambda g: (g, 0, 0)),
                   pl.BlockSpec((1, HEAD_DIM, NC), lambda g: (g, 0, 0))],
        out_shape=[jax.ShapeDtypeStruct((G, NC, HEAD_DIM), BF16),
                   jax.ShapeDtypeStruct((G, HEAD_DIM, NC), BF16)],
        compiler_params=_cparams(("arbitrary",)),
        name="compress",
    )(kv32, kv32, pos, w1, b1, w2, b2, gk)


QW = GROUP * QB


def _stacked_queries(q_ref):
    return jnp.concatenate([q_ref[:, r * HEAD_DIM:(r + 1) * HEAD_DIM] for r in range(GROUP)], axis=0)


LOG2E = math.log2(math.e)
KEY_SCALE = SCALE * LOG2E
QSUB = 2


def _scores(k_t, qs):
    return lax.dot_general(k_t, qs, (((1,), (1,)), ((), ())), preferred_element_type=F32)


def _rel_bias(tab, c31):
    return (tab - c31) * LOG2E


def _init_state(m_ref, l_ref, acc_ref):
    m_ref[...] = jnp.full(m_ref.shape, NEG, F32)
    l_ref[...] = jnp.zeros(l_ref.shape, F32)
    acc_ref[...] = jnp.zeros(acc_ref.shape, F32)


def _online_update(m_ref, l_ref, acc_ref, lanes, t, v_t):
    m_old = m_ref[:, lanes]
    m_new = jnp.maximum(m_old, jnp.max(t, axis=0, keepdims=True))
    alpha = jnp.exp2(m_old - m_new)
    p = jnp.exp2(t - m_new)
    l_ref[:, lanes] = alpha * l_ref[:, lanes] + jnp.sum(p, axis=0, keepdims=True)
    pv = lax.dot_general(v_t, p.astype(BF16), (((0,), (0,)), ((), ())), preferred_element_type=F32)
    acc_ref[:, lanes] = acc_ref[:, lanes] * alpha + pv
    m_ref[:, lanes] = m_new


def _key_query_iotas():
    kk = lax.broadcasted_iota(jnp.int32, (QB, QW), 0)
    qq = lax.broadcasted_iota(jnp.int32, (QB, QW), 1) & (QB - 1)
    return kk, qq


def _load_kv(k_ref, v_ref, kb0, n):
    rows = pl.ds(pl.multiple_of(kb0 * QB, QB), n * QB)
    return k_ref[rows, :], v_ref[rows, :]


def _store_heads(o_ref, o_t):
    for r in range(GROUP):
        o_ref[:, r * HEAD_DIM:(r + 1) * HEAD_DIM] = o_t[:, r * QB:(r + 1) * QB].T


def _stack_heads(a, G):
    a = a.reshape((G, GROUP) + a.shape[1:])
    a = jnp.moveaxis(a, 1, -2)
    return a.reshape(a.shape[:-2] + (QW,))


def _band_kernel(q_ref, k_ref, v_ref, tab_ref, c31_ref, sink_ref, o_ref, m_ref, l_ref, acc_ref,
                 *, nback, use_sinks, transposed_out):
    c0 = pl.program_id(1) * QSUB
    _init_state(m_ref, l_ref, acc_ref)
    kk, qq = _key_query_iotas()
    c31 = c31_ref[0]

    def masked_scores(sub, kb0, deltas):
        k_t, v_t = _load_kv(k_ref, v_ref, kb0, len(deltas))
        s = _scores(k_t, _stacked_queries(q_ref.at[sub * QB:(sub + 1) * QB]))
        parts = []
        for t, delta in enumerate(deltas):
            sb = s[t * QB:(t + 1) * QB]
            if delta <= 1:
                sb = sb + _rel_bias(tab_ref[0, delta], c31)
            if delta == 0:
                sb = jnp.where(qq >= kk, sb, NEG)
            elif delta == nback:
                sb = jnp.where(qq < kk, sb, NEG)
            parts.append(sb)
        return (parts[0] if len(parts) == 1 else jnp.concatenate(parts, axis=0)), v_t

    def update(sub, s, v_t):
        _online_update(m_ref.at[sub], l_ref.at[sub], acc_ref.at[sub], slice(None), s, v_t)

    @pl.when(c0 >= nback)
    def _():
        band = list(range(nback, -1, -1))
        tiles = [masked_scores(sub, c0 + sub - nback, band) for sub in range(QSUB)]
        for sub, (s, v_t) in enumerate(tiles):
            update(sub, s, v_t)

    @pl.when(c0 < nback)
    def _():
        for sub in range(QSUB):
            for delta in range(nback, -1, -1):
                @pl.when(c0 + sub - delta >= 0)
                def _(sub=sub, delta=delta):
                    update(sub, *masked_scores(sub, c0 + sub - delta, [delta]))

    for sub in range(QSUB):
        l = l_ref[sub]
        acc = acc_ref[sub]
        if use_sinks:
            m = m_ref[sub]
            sk = _rel_bias(sink_ref[0], c31)
            m_f = jnp.maximum(m, sk)
            a = jnp.exp2(m - m_f)
            l = l * a + jnp.exp2(sk - m_f)
            acc = acc * a
        o_t = acc * (1.0 / l)
        if transposed_out:
            o_ref[0, sub] = o_t
        else:
            _store_heads(o_ref.at[sub * QB:(sub + 1) * QB], o_t)


def _qkv_specs(S, q_col, k_col, v_col, q_rows=QB):
    return [pl.BlockSpec((q_rows, GROUP * HEAD_DIM), lambda g, c: (c, q_col + g)),
            pl.BlockSpec((S, HEAD_DIM), lambda g, c: (0, k_col + g)),
            pl.BlockSpec((S, HEAD_DIM), lambda g, c: (0, v_col + g))]


def _band_attention(proj, G, q_col, k_col, v_col, tabs, c31, sinks, window, transposed_out):
    S = proj.shape[0]
    nkb = S // QB
    nback = -(-(window - 1) // QB)
    use_sinks = sinks is not None
    if sinks is None:
        sinks = jnp.zeros_like(c31)
    if transposed_out:
        out_shape = jax.ShapeDtypeStruct((G, nkb, HEAD_DIM, QW), F32)
        out_spec = pl.BlockSpec((1, QSUB, HEAD_DIM, QW), lambda g, c: (g, c, 0, 0))
    else:
        out_shape = jax.ShapeDtypeStruct((S, G * GROUP * HEAD_DIM), F32)
        out_spec = pl.BlockSpec((QSUB * QB, GROUP * HEAD_DIM), lambda g, c: (c, g))
    row = pl.BlockSpec((1, 1, QW), lambda g, c: (g, 0, 0))
    return pl.pallas_call(
        functools.partial(_band_kernel, nback=nback, use_sinks=use_sinks,
                          transposed_out=transposed_out),
        grid=(G, nkb // QSUB),
        in_specs=_qkv_specs(S, q_col, k_col, v_col, QSUB * QB)
        + [pl.BlockSpec((1, 2, QB, QW), lambda g, c: (g, 0, 0, 0)), row, row],
        out_specs=out_spec,
        out_shape=out_shape,
        scratch_shapes=[pltpu.VMEM((QSUB, 1, QW), F32), pltpu.VMEM((QSUB, 1, QW), F32),
                        pltpu.VMEM((QSUB, HEAD_DIM, QW), F32)],
        compiler_params=_cparams(("arbitrary", "arbitrary")),
        name="band_attention_w%d" % window,
    )(proj, proj, proj, tabs, c31, sinks)


def _cmp_select_kernel(q_ref, kc_ref, vct_ref, tab_ref, c31_ref, ovl_ref, ot_ref, sel_ref, s_ref,
                       *, n_valid, topk):
    c0 = pl.program_id(1) * QSUB
    nc = kc_ref.shape[1]
    nb = ovl_ref.shape[0]
    rown = lax.broadcasted_iota(jnp.int32, (nc, QW), 0)
    wrow = lax.broadcasted_iota(jnp.int32, (CMP_WIN, QW), 0)
    wq = lax.broadcasted_iota(jnp.int32, (CMP_WIN, QW), 1) & (QB - 1)
    lane_q = lax.broadcasted_iota(jnp.int32, (1, QW), 1) & (QB - 1)

    for sub in range(QSUB):
        c = c0 + sub
        qs = _stacked_queries(q_ref.at[sub * QB:(sub + 1) * QB])
        w0 = pl.multiple_of(jnp.maximum(8 * c - 16, 0), 8)
        toff = pl.multiple_of(w0 - 8 * c + 16, 8)
        n_abs = w0 + wrow
        dist = (c * QB + wq) - (n_abs * CMP_STRIDE + (CMP_BLOCK - 1))
        valid_w = (dist >= 0) & (n_abs < n_valid)
        s_ref[sub] = jnp.where(rown < w0, _scores(kc_ref[0], qs), NEG)
        bias_w = _rel_bias(tab_ref[0, pl.ds(toff, CMP_WIN), :], c31_ref[0])
        s_w = _scores(kc_ref[0, pl.ds(w0, CMP_WIN), :], qs) + bias_w
        s_ref[sub, pl.ds(w0, CMP_WIN), :] = jnp.where(valid_w, s_w, NEG)

    p_groups = []
    for sub in range(QSUB):
        has_any = ((c0 + sub) * QB + lane_q >= CMP_BLOCK - 1).astype(F32)
        s = s_ref[sub]
        m = jnp.max(s, axis=0, keepdims=True)
        e = jnp.exp2(s - m)
        p = e * (has_any / jnp.sum(e, axis=0, keepdims=True))
        ot_ref[0, sub] = jnp.dot(vct_ref[0], p.astype(BF16), preferred_element_type=F32)
        p_grp = p[:, 0:QB]
        for r in range(1, GROUP):
            p_grp = p_grp + p[:, r * QB:(r + 1) * QB]
        p_groups.append(p_grp.astype(BF16))

    nq = QSUB * QB
    scores = jnp.dot(ovl_ref[...], jnp.concatenate(p_groups, axis=1), preferred_element_type=F32)
    blk = lax.broadcasted_iota(jnp.int32, (nb, nq), 0)
    cur = lax.shift_right_logical(c0 * QB + lax.broadcasted_iota(jnp.int32, (nb, nq), 1),
                                  int(math.log2(SLC_BLOCK)))
    forced = (blk == 0) | (blk == cur) | (blk == cur - 1)
    work = jnp.where(forced, FORCE_SCORE, jnp.where(blk <= cur, scores, -1.0))
    blk_f = blk.astype(F32)
    sel = jnp.zeros((nb, nq), F32)
    for _ in range(topk):
        mx = jnp.max(work, axis=0, keepdims=True)
        first = jnp.min(jnp.where(work == mx, blk_f, float(nb)), axis=0, keepdims=True)
        hit = blk_f == first
        sel = jnp.where(hit, 1.0, sel)
        work = jnp.where(hit, -3e38, work)
    sel_ref[0] = sel


def _cmp_select(proj, q_col, kc, vct, tab, c31, ovl_t, n_valid):
    nkb = proj.shape[0] // QB
    G, NC = kc.shape[0], kc.shape[1]
    NB = ovl_t.shape[0]
    return pl.pallas_call(
        functools.partial(_cmp_select_kernel, n_valid=n_valid, topk=min(SLC_TOPK, NB)),
        grid=(G, nkb // QSUB),
        in_specs=[pl.BlockSpec((QSUB * QB, GROUP * HEAD_DIM), lambda g, c: (c, q_col + g)),
                  pl.BlockSpec((1, NC, HEAD_DIM), lambda g, c: (g, 0, 0)),
                  pl.BlockSpec((1, HEAD_DIM, NC), lambda g, c: (g, 0, 0)),
                  pl.BlockSpec((1, CMP_TAB, QW), lambda g, c: (g, 0, 0)),
                  pl.BlockSpec((1, 1, QW), lambda g, c: (g, 0, 0)),
                  pl.BlockSpec((NB, NC), lambda g, c: (0, 0))],
        out_specs=[pl.BlockSpec((1, QSUB, HEAD_DIM, QW), lambda g, c: (g, c, 0, 0)),
                   pl.BlockSpec((1, NB, QSUB * QB), lambda g, c: (g, 0, c))],
        out_shape=[jax.ShapeDtypeStruct((G, nkb, HEAD_DIM, QW), F32),
                   jax.ShapeDtypeStruct((G, NB, nkb * QB), F32)],
        scratch_shapes=[pltpu.VMEM((QSUB, NC, QW), F32)],
        compiler_params=_cparams(("arbitrary", "arbitrary")),
        name="cmp_attention_select",
    )(proj, kc, vct, tab, c31, ovl_t)


FAR_BLOCKS = 4
SSUB = 1
SW = SSUB * QW


def _sel_kernel(q_ref, k_ref, v_ref, tab_ref, c31_ref, sel_ref, gate_ref, ocmp_ref, owin_ref,
                o_ref, m_ref, l_ref, acc_ref, s_a, s_b, p_a, p_b, al_a, al_b):
    c0 = pl.program_id(1) * SSUB
    _init_state(m_ref, l_ref, acc_ref)
    kk, qq = _key_query_iotas()
    qs = jnp.concatenate([_stacked_queries(q_ref.at[sub * QB:(sub + 1) * QB]) for sub in range(SSUB)],
                         axis=0)
    c31 = c31_ref[0]
    spb = QB // SLC_BLOCK
    tk = FAR_BLOCKS * QB
    last_tile = k_ref.shape[0] // tk - 1
    n_far = [jnp.maximum(c0 + sub - 1, 0) for sub in range(SSUB)]
    n_tiles = (n_far[-1] + FAR_BLOCKS - 1) // FAR_BLOCKS

    def sel_row(r, sub, row_limit=None):
        row = sel_ref[0, pl.ds(r, 1), :][:, sub * QB:(sub + 1) * QB]
        if row_limit is not None:
            row = jnp.where(r < row_limit, row, 0.0)
        return jnp.concatenate([row] * GROUP, axis=1)

    def far_logits(i, s_buf, t):
        return s_buf[t * SLC_BLOCK:(t + 1) * SLC_BLOCK, :]

    def tile_rows(i):
        return pl.ds(pl.multiple_of(jnp.minimum(i, last_tile) * tk, tk), tk)

    def far_scores(i, s_buf):
        s = _scores(k_ref[tile_rows(i), :], qs)
        r0 = jnp.minimum(i, last_tile) * (FAR_BLOCKS * spb)
        for t in range(FAR_BLOCKS * spb):
            row = jnp.concatenate([sel_row(r0 + t, sub, n_far[sub] * spb) for sub in range(SSUB)],
                                  axis=1)
            blk = slice(t * SLC_BLOCK, (t + 1) * SLC_BLOCK)
            s_buf[blk, :] = jnp.where(row > 0.5, s[blk], NEG)

    def far_softmax(i, s_buf, p_buf, al_buf):
        n_blk = FAR_BLOCKS * spb
        sub8 = lambda a: a.reshape(SLC_BLOCK // 8, 8, SW)
        m_part = jnp.full((8, SW), NEG, F32)
        for t in range(n_blk):
            m_part = jnp.maximum(m_part, jnp.max(sub8(far_logits(i, s_buf, t)), axis=0))
        m_old = m_ref[...]
        m_new = jnp.maximum(m_old, jnp.max(m_part, axis=0, keepdims=True))
        alpha = jnp.exp2(m_old - m_new)
        l_part = jnp.zeros((8, SW), F32)
        for t in range(n_blk):
            p = jnp.exp2(far_logits(i, s_buf, t) - m_new)
            l_part = l_part + jnp.sum(sub8(p), axis=0)
            p_buf[t * SLC_BLOCK:(t + 1) * SLC_BLOCK, :] = p.astype(BF16)
        l_ref[...] = alpha * l_ref[...] + jnp.sum(l_part, axis=0, keepdims=True)
        m_ref[...] = m_new
        al_buf[...] = alpha

    def far_values(i, p_buf, al_buf):
        pv = lax.dot_general(v_ref[tile_rows(i), :], p_buf[...], (((0,), (0,)), ((), ())),
                             preferred_element_type=F32)
        acc_ref[...] = acc_ref[...] * al_buf[...] + pv

    far_scores(0, s_a)
    far_scores(1, s_b)
    for p_buf, al_buf in ((p_a, al_a), (p_b, al_b)):
        p_buf[...] = jnp.zeros(p_buf.shape, BF16)
        al_buf[...] = jnp.ones(al_buf.shape, F32)

    def pair_body(j, carry):
        i = 2 * j
        far_values(jnp.maximum(i - 2, 0), p_a, al_a)
        far_values(jnp.maximum(i - 1, 0), p_b, al_b)
        far_softmax(i, s_a, p_a, al_a)
        far_softmax(i + 1, s_b, p_b, al_b)
        far_scores(i + 2, s_a)
        far_scores(i + 3, s_b)
        return carry

    n_pairs = n_tiles // 2
    lax.fori_loop(0, n_pairs, pair_body, 0)
    far_values(jnp.maximum(2 * n_pairs - 2, 0), p_a, al_a)
    far_values(jnp.maximum(2 * n_pairs - 1, 0), p_b, al_b)

    @pl.when(n_tiles > 2 * n_pairs)
    def _():
        far_softmax(n_tiles - 1, s_a, p_a, al_a)
        far_values(n_tiles - 1, p_a, al_a)

    def near_scores(sub, kb0, deltas):
        k_t, v_t = _load_kv(k_ref, v_ref, kb0, len(deltas))
        s = _scores(k_t, qs[sub * QW:(sub + 1) * QW])
        parts = []
        for t, delta in enumerate(deltas):
            rows = [jnp.broadcast_to(sel_row((kb0 + t) * spb + u, sub), (SLC_BLOCK, QW))
                    for u in range(spb)]
            msk = jnp.concatenate(rows, axis=0) > 0.5
            if delta == 0:
                msk = msk & (qq >= kk)
            sb = s[t * QB:(t + 1) * QB] + _rel_bias(tab_ref[0, delta], c31)
            parts.append(jnp.where(msk, sb, NEG))
        return (parts[0] if len(parts) == 1 else jnp.concatenate(parts, axis=0)), v_t

    def near_update(sub, s, v_t):
        _online_update(m_ref, l_ref, acc_ref, slice(sub * QW, (sub + 1) * QW), s, v_t)

    @pl.when(c0 >= 1)
    def _():
        tiles = [near_scores(sub, c0 + sub - 1, [1, 0]) for sub in range(SSUB)]
        for sub, (s, v_t) in enumerate(tiles):
            near_update(sub, s, v_t)

    @pl.when(c0 == 0)
    def _():
        near_update(0, *near_scores(0, 0, [0]))
        for sub in range(1, SSUB):
            near_update(sub, *near_scores(sub, sub - 1, [1, 0]))

    for sub in range(SSUB):
        lanes = slice(sub * QW, (sub + 1) * QW)
        o_slc = acc_ref[:, lanes] * (1.0 / l_ref[:, lanes])
        o_t = (gate_ref[0, sub, 0:1, :] * ocmp_ref[0, sub] + gate_ref[0, sub, 1:2, :] * o_slc
               + gate_ref[0, sub, 2:3, :] * owin_ref[0, sub])
        _store_heads(o_ref.at[sub * QB:(sub + 1) * QB], o_t)


def _sel_attention(proj, q_col, k_col, v_col, tabs, c31, sel_t, gates_s, ocmp_t, owin_t):
    S = proj.shape[0]
    G, NB = sel_t.shape[0], sel_t.shape[1]
    nkb = S // QB
    tile = pl.BlockSpec((1, SSUB, HEAD_DIM, QW), lambda g, c: (g, c, 0, 0))
    return pl.pallas_call(
        _sel_kernel,
        grid=(G, nkb // SSUB),
        in_specs=_qkv_specs(S, q_col, k_col, v_col, SSUB * QB)
        + [pl.BlockSpec((1, 2, QB, QW), lambda g, c: (g, 0, 0, 0)),
           pl.BlockSpec((1, 1, QW), lambda g, c: (g, 0, 0)),
           pl.BlockSpec((1, NB, SSUB * QB), lambda g, c: (g, 0, c)),
           pl.BlockSpec((1, SSUB, N_BRANCH, QW), lambda g, c: (g, c, 0, 0)),
           tile, tile],
        out_specs=pl.BlockSpec((SSUB * QB, GROUP * HEAD_DIM), lambda g, c: (c, g)),
        out_shape=jax.ShapeDtypeStruct((S, G * GROUP * HEAD_DIM), F32),
        scratch_shapes=[pltpu.VMEM((1, SW), F32), pltpu.VMEM((1, SW), F32),
                        pltpu.VMEM((HEAD_DIM, SW), F32)]
        + [pltpu.VMEM((FAR_BLOCKS * QB, SW), F32)] * 2
        + [pltpu.VMEM((FAR_BLOCKS * QB, SW), BF16)] * 2
        + [pltpu.VMEM((1, SW), F32)] * 2,
        compiler_params=_cparams(("arbitrary", "arbitrary")),
        name="selected_attention_combine",
    )(proj, proj, proj, tabs, c31, sel_t, gates_s, ocmp_t, owin_t)


def _outproj_kernel(a_ref, w_ref, x_ref, o_ref, w_s):
    @pl.when(pl.program_id(1) == 0)
    def _():
        w_s[...] = w_ref[...].astype(BF16)

    o_ref[...] = x_ref[...] + jnp.dot(a_ref[...], w_s[...], preferred_element_type=F32)


def _outproj(a, w, x, tm=1024, tn=512):
    S, D = a.shape
    N = w.shape[1]
    tm = min(tm, S)
    return pl.pallas_call(
        _outproj_kernel,
        grid=(N // tn, S // tm),
        in_specs=[pl.BlockSpec((tm, D), lambda j, i: (i, 0)),
                  pl.BlockSpec((D, tn), lambda j, i: (0, j)),
                  pl.BlockSpec((tm, tn), lambda j, i: (i, j))],
        out_specs=pl.BlockSpec((tm, tn), lambda j, i: (i, j)),
        out_shape=jax.ShapeDtypeStruct((S, N), F32),
        scratch_shapes=[pltpu.VMEM((D, tn), BF16)],
        compiler_params=_cparams(("arbitrary", "arbitrary")),
        name="outproj_residual",
    )(a, w, x)


HALO = 16
FFN_ROW_CHUNKS = 2


def _ffn_a_kernel(halo_ref, a_ref, wg_ref, wu_ref, cw_ref, cb_ref, o_ref):
    i = pl.program_id(0)
    tm = a_ref.shape[0]
    halo = halo_ref[...]
    halo = jnp.where(i > 0, halo, jnp.zeros_like(halo))
    wg = wg_ref[...].astype(BF16)
    wu = wu_ref[...].astype(BF16)
    hm = tm // FFN_ROW_CHUNKS
    prev = jnp.dot(halo, wg, preferred_element_type=F32)
    for h in range(FFN_ROW_CHUNKS):
        a = a_ref[h * hm:(h + 1) * hm, :]
        gate = jnp.dot(a, wg, preferred_element_type=F32)
        up = jnp.dot(a, wu, preferred_element_type=F32)
        gext = jnp.concatenate([prev, gate], axis=0)
        g1 = gext[HALO - 1:HALO - 1 + hm]
        g2 = gext[HALO - 2:HALO - 2 + hm]
        y = cb_ref[...] + g2 * cw_ref[0:1, :]
        y = y + g1 * cw_ref[1:2, :]
        y = y + gate * cw_ref[2:3, :]
        act = y * (1.0 / (1.0 + jnp.exp(-y)))
        o_ref[h * hm:(h + 1) * hm, :] = (act * up).astype(o_ref.dtype)
        prev = gate[hm - HALO:]


def _ffn_a(hf, wg, wu, cw, cb, tm=1024, tf=256):
    S, D = hf.shape
    Fp = wg.shape[1]
    assert Fp % tf == 0
    tm = min(tm, S)
    hb = tm // HALO
    return pl.pallas_call(
        _ffn_a_kernel,
        grid=(S // tm, Fp // tf),
        in_specs=[pl.BlockSpec((HALO, D), lambda i, f: (jnp.maximum(i * hb - 1, 0), 0)),
                  pl.BlockSpec((tm, D), lambda i, f: (i, 0)),
                  pl.BlockSpec((D, tf), lambda i, f: (0, f)),
                  pl.BlockSpec((D, tf), lambda i, f: (0, f)),
                  pl.BlockSpec((cw.shape[0], tf), lambda i, f: (0, f)),
                  pl.BlockSpec((1, tf), lambda i, f: (0, f))],
        out_specs=pl.BlockSpec((tm, tf), lambda i, f: (i, f)),
        out_shape=jax.ShapeDtypeStruct((S, Fp), BF16),
        compiler_params=_cparams(("arbitrary", "arbitrary")),
        name="ffn_gate_up",
    )(hf, hf, wg, wu, cw, cb)


def _ffn_b_kernel(a_ref, w_ref, x_ref, o_ref):
    o_ref[...] = x_ref[...] + jnp.dot(a_ref[...], w_ref[...], preferred_element_type=F32)


def _ffn_b(h, w, x, tm=512, tn=512):
    S, F = h.shape
    N = w.shape[1]
    tm = min(tm, S)
    return pl.pallas_call(
        _ffn_b_kernel,
        grid=(S // tm, N // tn),
        in_specs=[pl.BlockSpec((tm, F), lambda i, j: (i, 0)),
                  pl.BlockSpec((F, tn), lambda i, j: (0, j)),
                  pl.BlockSpec((tm, tn), lambda i, j: (i, j))],
        out_specs=pl.BlockSpec((tm, tn), lambda i, j: (i, j)),
        out_shape=jax.ShapeDtypeStruct((S, N), F32),
        compiler_params=_cparams(("arbitrary", "arbitrary")),
        name="ffn_down_residual",
    )(h, w, x)


def _layer(x, rel_bias, norm_mix_g, w_in, a_q_norm_g, a_k_norm_g, a_sinks, b_q_norm_g, b_k_norm_g,
           cmp_pos_emb, cmp_w1, cmp_b1, cmp_w2, cmp_b2, out_norm_g, w_out, norm_ffn_g, w_gate,
           w_up, conv_w, conv_b, w_down):
    S, D = x.shape
    aw = A_Q_HEADS * HEAD_DIM
    akv = A_KV_HEADS * HEAD_DIM
    bw = B_Q_HEADS * HEAD_DIM
    bkv = B_KV_HEADS * HEAD_DIM
    sizes = [aw, akv, akv, bw] + [bkv] * 6 + [N_BRANCH * B_Q_HEADS]
    offs = np.concatenate([[0], np.cumsum(sizes)]).tolist()
    wt = jnp.swapaxes(w_in, 0, 1)
    tn = 512
    order_a = [0, 1, 2, 3, 6, 7, 8, 9]
    n_gate = sizes[10]
    wg_t = jnp.pad(wt[offs[10]:offs[10] + n_gate], ((0, QB - n_gate), (0, 0)))
    one = jnp.ones((HEAD_DIM,), F32)
    seg_gain = {0: a_q_norm_g, 1: a_k_norm_g * KEY_SCALE, 3: b_q_norm_g,
                6: b_k_norm_g[1] * KEY_SCALE, 8: b_k_norm_g[2] * KEY_SCALE}
    kinds, tiles, gains, col = [], [], [], {}
    c0 = 0
    for k in order_a:
        col[k] = c0
        for t in range(sizes[k] // tn):
            kinds.append(1 if k in seg_gain else 0)
            tiles.append(offs[k] // tn + t)
            gains.append(seg_gain.get(k, one))
        c0 += sizes[k]
    kinds = jnp.asarray(kinds, jnp.int32)
    tiles = jnp.asarray(tiles, jnp.int32)
    gains = jnp.stack(gains).astype(F32).reshape(len(gains), 1, HEAD_DIM)

    hn = _rmsnorm([x], norm_mix_g)
    proj = _inproj_a(hn, wt, kinds, tiles, gains, tn=tn)
    kv32, gates = _inproj_b(hn, wt, offs[4] // tn, 2 * bkv // tn, wg_t)

    tab_a = rel_bias[:, :A_Q_HEADS]
    tab_b = rel_bias[:, A_Q_HEADS:]
    near_a = _stack_heads(_near_bias_tables(tab_a), A_KV_HEADS)
    near_b = _stack_heads(_near_bias_tables(tab_b), B_KV_HEADS)
    head_row = lambda v, G: _stack_heads(jnp.broadcast_to(v.astype(F32)[:, None, None],
                                                          (v.shape[0], 1, QB)), G)
    c31_a = head_row(tab_a[N_BUCKETS - 1], A_KV_HEADS)
    c31_b = head_row(tab_b[N_BUCKETS - 1], B_KV_HEADS)

    qcol = lambda k: col[k] // (GROUP * HEAD_DIM)
    hcol = lambda k: col[k] // HEAD_DIM
    o_a = _band_attention(proj, A_KV_HEADS, qcol(0), hcol(1), hcol(2), near_a, c31_a,
                          head_row(a_sinks, A_KV_HEADS), A_WINDOW, transposed_out=False)

    NC = S // CMP_STRIDE
    n_cmp = (S - CMP_BLOCK) // CMP_STRIDE + 1
    w1 = cmp_w1.reshape(2, CMP_BLOCK, HEAD_DIM, cmp_w1.shape[-1]).astype(BF16)
    kc, vct = _compress(kv32, B_KV_HEADS, cmp_pos_emb.astype(F32), w1, cmp_b1[:, None, :].astype(F32),
                        cmp_w2.astype(BF16), cmp_b2[:, None, :].astype(F32),
                        (b_k_norm_g[0] * KEY_SCALE).reshape(1, HEAD_DIM).astype(F32))

    NB = S // SLC_BLOCK
    ii = np.arange(NC)[None, :]
    jj = np.arange(NB)[:, None]
    ovl_t = ((ii * CMP_STRIDE <= jj * SLC_BLOCK + SLC_BLOCK - 1)
             & (ii * CMP_STRIDE + CMP_BLOCK - 1 >= jj * SLC_BLOCK) & (ii < n_cmp))
    ovl_t = jnp.asarray(ovl_t.astype(np.float32), BF16)
    ocmp_t, sel_t = _cmp_select(proj, qcol(3), kc, vct,
                                _stack_heads(_cmp_bias_table(tab_b), B_KV_HEADS), c31_b, ovl_t, n_cmp)

    owin_t = _band_attention(proj, B_KV_HEADS, qcol(3), hcol(8), hcol(9), near_b, c31_b, None,
                             B_WINDOW, transposed_out=True)
    gates_s = gates[:, :n_gate].reshape(S // QB, QB, B_KV_HEADS, GROUP, N_BRANCH)
    gates_s = gates_s.transpose(2, 0, 4, 3, 1).reshape(B_KV_HEADS, S // QB, N_BRANCH, QW)
    o_b = _sel_attention(proj, qcol(3), hcol(6), hcol(7), near_b, c31_b, sel_t, gates_s,
                         ocmp_t, owin_t)

    on = _rmsnorm([o_a, o_b], out_norm_g)
    x2 = _outproj(on, w_out, x)

    hf = _rmsnorm([x2], norm_ffn_g)
    hmid = _ffn_a(hf, w_gate, w_up, conv_w.astype(F32), conv_b[None, :].astype(F32))
    return _ffn_b(hmid, w_down.astype(BF16), x2)


def kernel(x, rel_bias, norm_mix_g, w_in, a_q_norm_g, a_k_norm_g, a_sinks, b_q_norm_g, b_k_norm_g,
           cmp_pos_emb, cmp_w1, cmp_b1, cmp_w2, cmp_b2, out_norm_g, w_out, norm_ffn_g, w_gate, w_up,
           conv_w, conv_b, w_down):
    depth = w_in.shape[0]
    batch = x.shape[0]
    outs = []
    for b in range(batch):
        h = x[b]
        for l in range(depth):
            h = _layer(h, rel_bias, norm_mix_g[l], w_in[l], a_q_norm_g[l], a_k_norm_g[l], a_sinks[l],
                       b_q_norm_g[l], b_k_norm_g[l], cmp_pos_emb[l], cmp_w1[l], cmp_b1[l], cmp_w2[l],
                       cmp_b2[l], out_norm_g[l], w_out[l], norm_ffn_g[l], w_gate[l], w_up[l],
                       conv_w[l], conv_b[l], w_down[l])
        outs.append(h)
    return jnp.stack(outs)
```

```python
import functools
import math

import numpy as np
import jax
import jax.numpy as jnp
from jax import lax
from jax.experimental import pallas as pl
from jax.experimental.pallas import tpu as pltpu

F32 = jnp.float32
BF16 = jnp.bfloat16

HEAD_DIM = 128
A_Q_HEADS = 16
A_KV_HEADS = 4
B_Q_HEADS = 16
B_KV_HEADS = 4
GROUP = 4
A_WINDOW = 128
B_WINDOW = 512
CMP_BLOCK = 32
CMP_STRIDE = 16
SLC_BLOCK = 64
SLC_TOPK = 16
N_BRANCH = 3
N_BUCKETS = 32
MAX_DISTANCE = 128
EPS = 1e-6
NEG = -1e30
FORCE_SCORE = 1e6
SCALE = HEAD_DIM ** -0.5

QB = 128
CMP_WIN = 24
CMP_TAB = 40
VMEM_LIMIT = 56 * 1024 * 1024


def _cparams(sem, **kw):
    return pltpu.CompilerParams(dimension_semantics=sem, vmem_limit_bytes=VMEM_LIMIT, **kw)


def _t5_bucket_np(dist):
    n = np.maximum(dist, 0)
    max_exact = N_BUCKETS // 2
    nf = np.maximum(n, 1).astype(np.float32)
    large = max_exact + (np.log(nf / max_exact) / math.log(MAX_DISTANCE / max_exact)
                         * (N_BUCKETS - max_exact)).astype(np.int32)
    large = np.minimum(large, N_BUCKETS - 1)
    return np.where(n < max_exact, n, large).astype(np.int32)


def _bias_from_buckets(tab, idx):
    onehot = (idx[None] == np.arange(N_BUCKETS).reshape((-1,) + (1,) * idx.ndim)).astype(np.float32)
    return jnp.einsum('bh,b...->h...', tab.astype(F32), jnp.asarray(onehot),
                      precision=lax.Precision.HIGHEST)


def _near_bias_tables(tab):
    kk = np.arange(QB)[:, None]
    qq = np.arange(QB)[None, :]
    return _bias_from_buckets(tab, np.stack([_t5_bucket_np(qq - kk), _t5_bucket_np(QB + qq - kk)]))


def _cmp_bias_table(tab):
    npr = np.arange(CMP_TAB)[:, None] - 16
    qq = np.arange(QB)[None, :]
    return _bias_from_buckets(tab, _t5_bucket_np(qq - CMP_STRIDE * npr - (CMP_BLOCK - 1)))


def _rmsnorm_kernel(*refs, n_in):
    x_refs, g_ref, o_ref = refs[:n_in], refs[n_in], refs[n_in + 1]
    off = 0
    for x_ref in x_refs:
        x = x_ref[...]
        w = x.shape[-1]
        y = x * lax.rsqrt(jnp.mean(x * x, axis=-1, keepdims=True) + EPS)
        o_ref[:, off:off + w] = (y * g_ref[:, off:off + w]).astype(o_ref.dtype)
        off += w


def _rmsnorm(xs, gain, tr=256):
    S = xs[0].shape[0]
    widths = [x.shape[1] for x in xs]
    n = sum(widths)
    tr = min(tr, S)
    return pl.pallas_call(
        functools.partial(_rmsnorm_kernel, n_in=len(xs)),
        grid=(S // tr,),
        in_specs=[pl.BlockSpec((tr, w), lambda i: (i, 0)) for w in widths]
        + [pl.BlockSpec((1, n), lambda i: (0, 0))],
        out_specs=pl.BlockSpec((tr, n), lambda i: (i, 0)),
        out_shape=jax.ShapeDtypeStruct((S, n), BF16),
        compiler_params=_cparams(("arbitrary",)),
        name="rmsnorm",
    )(*xs, gain.reshape(1, n).astype(F32))


NORM_ROW_CHUNKS = 2


def _dot_wt(a, wt):
    return lax.dot_general(a, wt.astype(BF16), (((1,), (1,)), ((), ())), preferred_element_type=F32)


def _inproj_a_kernel(kind_ref, tile_ref, a_ref, w_ref, g_ref, o_ref, w_s):
    j = pl.program_id(0)

    @pl.when(pl.program_id(1) == 0)
    def _():
        w_s[...] = w_ref[...].astype(BF16)

    @pl.when(kind_ref[j] == 0)
    def _():
        o_ref[...] = _dot_wt(a_ref[...], w_s[...]).astype(o_ref.dtype)

    @pl.when(kind_ref[j] == 1)
    def _():
        g = g_ref[0]
        hm = a_ref.shape[0] // NORM_ROW_CHUNKS
        for c in range(NORM_ROW_CHUNKS):
            rows = slice(c * hm, (c + 1) * hm)
            acc = _dot_wt(a_ref[rows, :], w_s[...])
            for h in range(acc.shape[1] // HEAD_DIM):
                sl = acc[:, h * HEAD_DIM:(h + 1) * HEAD_DIM]
                y = sl * lax.rsqrt(jnp.mean(sl * sl, axis=-1, keepdims=True) + EPS)
                o_ref[rows, h * HEAD_DIM:(h + 1) * HEAD_DIM] = (y * g).astype(o_ref.dtype)


def _inproj_a(hn, wt, kinds, tiles, gains, tm=1024, tn=512):
    S, D = hn.shape
    n_tiles = kinds.shape[0]
    tm = min(tm, S)
    grid_spec = pltpu.PrefetchScalarGridSpec(
        num_scalar_prefetch=2,
        grid=(n_tiles, S // tm),
        in_specs=[pl.BlockSpec((tm, D), lambda j, i, k, t: (i, 0)),
                  pl.BlockSpec((tn, D), lambda j, i, k, t: (t[j], 0)),
                  pl.BlockSpec((1, 1, HEAD_DIM), lambda j, i, k, t: (j, 0, 0))],
        out_specs=pl.BlockSpec((tm, tn), lambda j, i, k, t: (i, j)),
        scratch_shapes=[pltpu.VMEM((tn, D), BF16)],
    )
    return pl.pallas_call(
        _inproj_a_kernel,
        grid_spec=grid_spec,
        out_shape=jax.ShapeDtypeStruct((S, n_tiles * tn), BF16),
        compiler_params=_cparams(("arbitrary", "arbitrary")),
        name="inproj_heads",
    )(kinds, tiles, hn, wt, gains)


def _inproj_b_kernel(a_ref, w_ref, wg_ref, o_ref, gate_ref):
    a = a_ref[...]
    o_ref[...] = _dot_wt(a, w_ref[...])

    @pl.when(pl.program_id(1) == 0)
    def _():
        gate_ref[...] = 1.0 / (1.0 + jnp.exp(-_dot_wt(a, wg_ref[...])))


def _inproj_b(hn, wt, tile0, n_tiles, wg_t, tm=1024, tn=512):
    S, D = hn.shape
    tm = min(tm, S)
    ng = wg_t.shape[0]
    return pl.pallas_call(
        _inproj_b_kernel,
        grid=(S // tm, n_tiles),
        in_specs=[pl.BlockSpec((tm, D), lambda i, j: (i, 0)),
                  pl.BlockSpec((tn, D), lambda i, j: (tile0 + j, 0)),
                  pl.BlockSpec((ng, D), lambda i, j: (0, 0))],
        out_specs=[pl.BlockSpec((tm, tn), lambda i, j: (i, j)),
                   pl.BlockSpec((tm, ng), lambda i, j: (i, 0))],
        out_shape=[jax.ShapeDtypeStruct((S, n_tiles * tn), F32),
                   jax.ShapeDtypeStruct((S, ng), F32)],
        compiler_params=_cparams(("arbitrary", "arbitrary")),
        name="inproj_cmp_gates",
    )(hn, wt, wg_t)


def _gelu_tanh(x):
    return 0.5 * x * (1.0 + jnp.tanh(math.sqrt(2.0 / math.pi) * (x + 0.044715 * (x * x * x))))


def _compress_kernel(tk_ref, tv_ref, pos_ref, w1_ref, b1_ref, w2_ref, b2_ref, gk_ref,
                     kc_ref, vct_ref):
    nc = kc_ref.shape[1]

    def mlp(t_ref, kv):
        a0 = jnp.zeros((nc, w1_ref.shape[-1]), F32)
        a1 = jnp.zeros((nc, w1_ref.shape[-1]), F32)
        for b in range(CMP_STRIDE):
            t = t_ref[pl.ds(b, nc, stride=CMP_STRIDE), :]
            lo, hi = b, CMP_STRIDE + b
            a0 = a0 + jnp.dot((t + pos_ref[kv, lo:lo + 1, :]).astype(BF16), w1_ref[kv, lo],
                              preferred_element_type=F32)
            a1 = a1 + jnp.dot((t + pos_ref[kv, hi:hi + 1, :]).astype(BF16), w1_ref[kv, hi],
                              preferred_element_type=F32)
        h = a0 + pltpu.roll(a1, nc - 1, 0) + b1_ref[kv]
        h = _gelu_tanh(h)
        return jnp.dot(h.astype(BF16), w2_ref[kv], preferred_element_type=F32) + b2_ref[kv]

    ck = mlp(tk_ref, 0)
    ck = ck * lax.rsqrt(jnp.mean(ck * ck, axis=-1, keepdims=True) + EPS) * gk_ref[...]
    kc_ref[0] = ck.astype(kc_ref.dtype)
    cv = mlp(tv_ref, 1)
    vct_ref[0] = cv.T.astype(vct_ref.dtype)


def _compress(kv32, G, pos, w1, b1, w2, b2, gk):
    S = kv32.shape[0]
    NC = S // CMP_STRIDE
    hid = w1.shape[-1]
    return pl.pallas_call(
        _compress_kernel,
        grid=(G,),
        in_specs=[pl.BlockSpec((S, HEAD_DIM), lambda g: (0, g)),
                  pl.BlockSpec((S, HEAD_DIM), lambda g: (0, G + g)),
                  pl.BlockSpec((2, CMP_BLOCK, HEAD_DIM), lambda g: (0, 0, 0)),
                  pl.BlockSpec((2, CMP_BLOCK, HEAD_DIM, hid), lambda g: (0, 0, 0, 0)),
                  pl.BlockSpec((2, 1, hid), lambda g: (0, 0, 0)),
                  pl.BlockSpec((2, hid, HEAD_DIM), lambda g: (0, 0, 0)),
                  pl.BlockSpec((2, 1, HEAD_DIM), lambda g: (0, 0, 0)),
                  pl.BlockSpec((1, HEAD_DIM), lambda g: (0, 0))],
        out_specs=[pl.BlockSpec((1, NC, HEAD_DIM), lambda g: (g, 0, 0)),
                   pl.BlockSpec((1, HEAD_DIM, NC), lambda g: (g, 0, 0))],
        out_shape=[jax.ShapeDtypeStruct((G, NC, HEAD_DIM), BF16),
                   jax.ShapeDtypeStruct((G, HEAD_DIM, NC), BF16)],
        compiler_params=_cparams(("arbitrary",)),
        name="compress",
    )(kv32, kv32, pos, w1, b1, w2, b2, gk)


QW = GROUP * QB


def _stacked_queries(q_ref):
    return jnp.concatenate([q_ref[:, r * HEAD_DIM:(r + 1) * HEAD_DIM] for r in range(GROUP)], axis=0)


LOG2E = math.log2(math.e)
KEY_SCALE = SCALE * LOG2E
QSUB = 4


def _scores(k_t, qs):
    return lax.dot_general(k_t, qs, (((1,), (1,)), ((), ())), preferred_element_type=F32)


def _rel_bias(tab, c31):
    return (tab - c31) * LOG2E


def _init_state(m_ref, l_ref, acc_ref):
    m_ref[...] = jnp.full(m_ref.shape, NEG, F32)
    l_ref[...] = jnp.zeros(l_ref.shape, F32)
    acc_ref[...] = jnp.zeros(acc_ref.shape, F32)


def _online_update(m_ref, l_ref, acc_ref, lanes, t, v_t):
    m_old = m_ref[:, lanes]
    m_new = jnp.maximum(m_old, jnp.max(t, axis=0, keepdims=True))
    alpha = jnp.exp2(m_old - m_new)
    p = jnp.exp2(t - m_new)
    l_ref[:, lanes] = alpha * l_ref[:, lanes] + jnp.sum(p, axis=0, keepdims=True)
    pv = lax.dot_general(v_t, p.astype(BF16), (((0,), (0,)), ((), ())), preferred_element_type=F32)
    acc_ref[:, lanes] = acc_ref[:, lanes] * alpha + pv
    m_ref[:, lanes] = m_new


def _key_query_iotas():
    kk = lax.broadcasted_iota(jnp.int32, (QB, QW), 0)
    qq = lax.broadcasted_iota(jnp.int32, (QB, QW), 1) & (QB - 1)
    return kk, qq


def _load_kv(k_ref, v_ref, kb0, n):
    rows = pl.ds(pl.multiple_of(kb0 * QB, QB), n * QB)
    return k_ref[rows, :], v_ref[rows, :]


def _store_heads(o_ref, o_t):
    for r in range(GROUP):
        o_ref[:, r * HEAD_DIM:(r + 1) * HEAD_DIM] = o_t[:, r * QB:(r + 1) * QB].T


def _stack_heads(a, G):
    a = a.reshape((G, GROUP) + a.shape[1:])
    a = jnp.moveaxis(a, 1, -2)
    return a.reshape(a.shape[:-2] + (QW,))


def _band_kernel(q_ref, k_ref, v_ref, tab_ref, c31_ref, sink_ref, o_ref, m_ref, l_ref, acc_ref,
                 *, nback, use_sinks, transposed_out):
    c0 = pl.program_id(1) * QSUB
    _init_state(m_ref, l_ref, acc_ref)
    kk, qq = _key_query_iotas()
    c31 = c31_ref[0]

    def masked_scores(sub, kb0, deltas):
        k_t, v_t = _load_kv(k_ref, v_ref, kb0, len(deltas))
        s = _scores(k_t, _stacked_queries(q_ref.at[sub * QB:(sub + 1) * QB]))
        parts = []
        for t, delta in enumerate(deltas):
            sb = s[t * QB:(t + 1) * QB]
            if delta <= 1:
                sb = sb + _rel_bias(tab_ref[0, delta], c31)
            if delta == 0:
                sb = jnp.where(qq >= kk, sb, NEG)
            elif delta == nback:
                sb = jnp.where(qq < kk, sb, NEG)
            parts.append(sb)
        return (parts[0] if len(parts) == 1 else jnp.concatenate(parts, axis=0)), v_t

    def update(sub, s, v_t):
        _online_update(m_ref.at[sub], l_ref.at[sub], acc_ref.at[sub], slice(None), s, v_t)

    @pl.when(c0 >= nback)
    def _():
        band = list(range(nback, -1, -1))
        tiles = [masked_scores(sub, c0 + sub - nback, band) for sub in range(QSUB)]
        for sub, (s, v_t) in enumerate(tiles):
            update(sub, s, v_t)

    @pl.when(c0 < nback)
    def _():
        for sub in range(QSUB):
            for delta in range(nback, -1, -1):
                @pl.when(c0 + sub - delta >= 0)
                def _(sub=sub, delta=delta):
                    update(sub, *masked_scores(sub, c0 + sub - delta, [delta]))

    for sub in range(QSUB):
        l = l_ref[sub]
        acc = acc_ref[sub]
        if use_sinks:
            m = m_ref[sub]
            sk = _rel_bias(sink_ref[0], c31)
            m_f = jnp.maximum(m, sk)
            a = jnp.exp2(m - m_f)
            l = l * a + jnp.exp2(sk - m_f)
            acc = acc * a
        o_t = acc * (1.0 / l)
        if transposed_out:
            o_ref[0, sub] = o_t
        else:
            _store_heads(o_ref.at[sub * QB:(sub + 1) * QB], o_t)


def _qkv_specs(S, q_col, k_col, v_col, q_rows=QB):
    return [pl.BlockSpec((q_rows, GROUP * HEAD_DIM), lambda g, c: (c, q_col + g)),
            pl.BlockSpec((S, HEAD_DIM), lambda g, c: (0, k_col + g)),
            pl.BlockSpec((S, HEAD_DIM), lambda g, c: (0, v_col + g))]


def _band_attention(proj, G, q_col, k_col, v_col, tabs, c31, sinks, window, transposed_out):
    S = proj.shape[0]
    nkb = S // QB
    nback = -(-(window - 1) // QB)
    use_sinks = sinks is not None
    if sinks is None:
        sinks = jnp.zeros_like(c31)
    if transposed_out:
        out_shape = jax.ShapeDtypeStruct((G, nkb, HEAD_DIM, QW), F32)
        out_spec = pl.BlockSpec((1, QSUB, HEAD_DIM, QW), lambda g, c: (g, c, 0, 0))
    else:
        out_shape = jax.ShapeDtypeStruct((S, G * GROUP * HEAD_DIM), F32)
        out_spec = pl.BlockSpec((QSUB * QB, GROUP * HEAD_DIM), lambda g, c: (c, g))
    row = pl.BlockSpec((1, 1, QW), lambda g, c: (g, 0, 0))
    return pl.pallas_call(
        functools.partial(_band_kernel, nback=nback, use_sinks=use_sinks,
                          transposed_out=transposed_out),
        grid=(G, nkb // QSUB),
        in_specs=_qkv_specs(S, q_col, k_col, v_col, QSUB * QB)
        + [pl.BlockSpec((1, 2, QB, QW), lambda g, c: (g, 0, 0, 0)), row, row],
        out_specs=out_spec,
        out_shape=out_shape,
        scratch_shapes=[pltpu.VMEM((QSUB, 1, QW), F32), pltpu.VMEM((QSUB, 1, QW), F32),
                        pltpu.VMEM((QSUB, HEAD_DIM, QW), F32)],
        compiler_params=_cparams(("arbitrary", "arbitrary")),
        name="band_attention_w%d" % window,
    )(proj, proj, proj, tabs, c31, sinks)


def _cmp_select_kernel(q_ref, kc_ref, vct_ref, tab_ref, c31_ref, ovl_ref, ot_ref, sel_ref, s_ref,
                       *, n_valid, topk):
    c0 = pl.program_id(1) * QSUB
    nc = kc_ref.shape[1]
    nb = ovl_ref.shape[0]
    rown = lax.broadcasted_iota(jnp.int32, (nc, QW), 0)
    wrow = lax.broadcasted_iota(jnp.int32, (CMP_WIN, QW), 0)
    wq = lax.broadcasted_iota(jnp.int32, (CMP_WIN, QW), 1) & (QB - 1)
    lane_q = lax.broadcasted_iota(jnp.int32, (1, QW), 1) & (QB - 1)

    for sub in range(QSUB):
        c = c0 + sub
        qs = _stacked_queries(q_ref.at[sub * QB:(sub + 1) * QB])
        w0 = pl.multiple_of(jnp.maximum(8 * c - 16, 0), 8)
        toff = pl.multiple_of(w0 - 8 * c + 16, 8)
        n_abs = w0 + wrow
        dist = (c * QB + wq) - (n_abs * CMP_STRIDE + (CMP_BLOCK - 1))
        valid_w = (dist >= 0) & (n_abs < n_valid)
        s_ref[sub] = jnp.where(rown < w0, _scores(kc_ref[0], qs), NEG)
        bias_w = _rel_bias(tab_ref[0, pl.ds(toff, CMP_WIN), :], c31_ref[0])
        s_w = _scores(kc_ref[0, pl.ds(w0, CMP_WIN), :], qs) + bias_w
        s_ref[sub, pl.ds(w0, CMP_WIN), :] = jnp.where(valid_w, s_w, NEG)

    p_groups = []
    for sub in range(QSUB):
        has_any = ((c0 + sub) * QB + lane_q >= CMP_BLOCK - 1).astype(F32)
        s = s_ref[sub]
        m = jnp.max(s, axis=0, keepdims=True)
        e = jnp.exp2(s - m)
        p = e * (has_any / jnp.sum(e, axis=0, keepdims=True))
        ot_ref[0, sub] = jnp.dot(vct_ref[0], p.astype(BF16), preferred_element_type=F32)
        p_grp = p[:, 0:QB]
        for r in range(1, GROUP):
            p_grp = p_grp + p[:, r * QB:(r + 1) * QB]
        p_groups.append(p_grp.astype(BF16))

    nq = QSUB * QB
    scores = jnp.dot(ovl_ref[...], jnp.concatenate(p_groups, axis=1), preferred_element_type=F32)
    blk = lax.broadcasted_iota(jnp.int32, (nb, nq), 0)
    cur = lax.shift_right_logical(c0 * QB + lax.broadcasted_iota(jnp.int32, (nb, nq), 1),
                                  int(math.log2(SLC_BLOCK)))
    forced = (blk == 0) | (blk == cur) | (blk == cur - 1)
    work = jnp.where(forced, FORCE_SCORE, jnp.where(blk <= cur, scores, -1.0))
    blk_f = blk.astype(F32)
    sel = jnp.zeros((nb, nq), F32)
    for _ in range(topk):
        mx = jnp.max(work, axis=0, keepdims=True)
        first = jnp.min(jnp.where(work == mx, blk_f, float(nb)), axis=0, keepdims=True)
        hit = blk_f == first
        sel = jnp.where(hit, 1.0, sel)
        work = jnp.where(hit, -3e38, work)
    sel_ref[0] = sel


def _cmp_select(proj, q_col, kc, vct, tab, c31, ovl_t, n_valid):
    nkb = proj.shape[0] // QB
    G, NC = kc.shape[0], kc.shape[1]
    NB = ovl_t.shape[0]
    return pl.pallas_call(
        functools.partial(_cmp_select_kernel, n_valid=n_valid, topk=min(SLC_TOPK, NB)),
        grid=(G, nkb // QSUB),
        in_specs=[pl.BlockSpec((QSUB * QB, GROUP * HEAD_DIM), lambda g, c: (c, q_col + g)),
                  pl.BlockSpec((1, NC, HEAD_DIM), lambda g, c: (g, 0, 0)),
                  pl.BlockSpec((1, HEAD_DIM, NC), lambda g, c: (g, 0, 0)),
                  pl.BlockSpec((1, CMP_TAB, QW), lambda g, c: (g, 0, 0)),
                  pl.BlockSpec((1, 1, QW), lambda g, c: (g, 0, 0)),
                  pl.BlockSpec((NB, NC), lambda g, c: (0, 0))],
        out_specs=[pl.BlockSpec((1, QSUB, HEAD_DIM, QW), lambda g, c: (g, c, 0, 0)),
                   pl.BlockSpec((1, NB, QSUB * QB), lambda g, c: (g, 0, c))],
        out_shape=[jax.ShapeDtypeStruct((G, nkb, HEAD_DIM, QW), F32),
                   jax.ShapeDtypeStruct((G, NB, nkb * QB), F32)],
        scratch_shapes=[pltpu.VMEM((QSUB, NC, QW), F32)],
        compiler_params=_cparams(("arbitrary", "arbitrary")),
        name="cmp_attention_select",
    )(proj, kc, vct, tab, c31, ovl_t)


FAR_BLOCKS = 4
SSUB = 1
SW = SSUB * QW


def _sel_kernel(q_ref, k_ref, v_ref, tab_ref, c31_ref, sel_ref, gate_ref, ocmp_ref, owin_ref,
                o_ref, m_ref, l_ref, acc_ref, s_a, s_b, p_a, p_b, al_a, al_b):
    c0 = pl.program_id(1) * SSUB
    _init_state(m_ref, l_ref, acc_ref)
    kk, qq = _key_query_iotas()
    qs = jnp.concatenate([_stacked_queries(q_ref.at[sub * QB:(sub + 1) * QB]) for sub in range(SSUB)],
                         axis=0)
    c31 = c31_ref[0]
    spb = QB // SLC_BLOCK
    tk = FAR_BLOCKS * QB
    last_tile = k_ref.shape[0] // tk - 1
    n_far = [jnp.maximum(c0 + sub - 1, 0) for sub in range(SSUB)]
    n_tiles = (n_far[-1] + FAR_BLOCKS - 1) // FAR_BLOCKS

    def sel_row(r, sub, row_limit=None, r_load=None):
        row = sel_ref[0, pl.ds(r if r_load is None else r_load, 1), :][:, sub * QB:(sub + 1) * QB]
        if row_limit is not None:
            row = jnp.where(r < row_limit, row, 0.0)
        return jnp.concatenate([row] * GROUP, axis=1)

    def far_logits(i, s_buf, t):
        return s_buf[t * SLC_BLOCK:(t + 1) * SLC_BLOCK, :]

    def tile_rows(i):
        return pl.ds(pl.multiple_of(jnp.minimum(i, last_tile) * tk, tk), tk)

    def far_scores(i, s_buf):
        s = _scores(k_ref[tile_rows(i), :], qs)
        r0 = i * (FAR_BLOCKS * spb)
        r0_load = jnp.minimum(i, last_tile) * (FAR_BLOCKS * spb)
        for t in range(FAR_BLOCKS * spb):
            row = jnp.concatenate([sel_row(r0 + t, sub, n_far[sub] * spb, r0_load + t)
                                   for sub in range(SSUB)], axis=1)
            blk = slice(t * SLC_BLOCK, (t + 1) * SLC_BLOCK)
            s_buf[blk, :] = jnp.where(row > 0.5, s[blk], NEG)

    def far_softmax(i, s_buf, p_buf, al_buf):
        n_blk = FAR_BLOCKS * spb
        sub8 = lambda a: a.reshape(SLC_BLOCK // 8, 8, SW)
        m_part = jnp.full((8, SW), NEG, F32)
        for t in range(n_blk):
            m_part = jnp.maximum(m_part, jnp.max(sub8(far_logits(i, s_buf, t)), axis=0))
        m_old = m_ref[...]
        m_new = jnp.maximum(m_old, jnp.max(m_part, axis=0, keepdims=True))
        alpha = jnp.exp2(m_old - m_new)
        l_part = jnp.zeros((8, SW), F32)
        for t in range(n_blk):
            p = jnp.exp2(far_logits(i, s_buf, t) - m_new)
            l_part = l_part + jnp.sum(sub8(p), axis=0)
            p_buf[t * SLC_BLOCK:(t + 1) * SLC_BLOCK, :] = p.astype(BF16)
        l_ref[...] = alpha * l_ref[...] + jnp.sum(l_part, axis=0, keepdims=True)
        m_ref[...] = m_new
        al_buf[...] = alpha

    def far_values(i, p_buf, al_buf):
        pv = lax.dot_general(v_ref[tile_rows(i), :], p_buf[...], (((0,), (0,)), ((), ())),
                             preferred_element_type=F32)
        acc_ref[...] = acc_ref[...] * al_buf[...] + pv

    far_scores(0, s_a)
    far_scores(1, s_b)
    for p_buf, al_buf in ((p_a, al_a), (p_b, al_b)):
        p_buf[...] = jnp.zeros(p_buf.shape, BF16)
        al_buf[...] = jnp.ones(al_buf.shape, F32)

    def pair_body(j, carry):
        i = 2 * j
        far_values(jnp.maximum(i - 2, 0), p_a, al_a)
        far_values(jnp.maximum(i - 1, 0), p_b, al_b)
        far_softmax(i, s_a, p_a, al_a)
        far_softmax(i + 1, s_b, p_b, al_b)
        far_scores(i + 2, s_a)
        far_scores(i + 3, s_b)
        return carry

    n_pairs = (n_tiles + 1) // 2
    lax.fori_loop(0, n_pairs, pair_body, 0)
    far_values(jnp.maximum(2 * n_pairs - 2, 0), p_a, al_a)
    far_values(jnp.maximum(2 * n_pairs - 1, 0), p_b, al_b)

    def near_scores(sub):
        c = c0 + sub
        blocks = [(jnp.maximum(c - 1, 0), 1, c * spb), (c, 0, None)]
        rows = [pl.ds(pl.multiple_of(kb * QB, QB), QB) for kb, _, _ in blocks]
        k_t = jnp.concatenate([k_ref[r, :] for r in rows], axis=0)
        v_t = jnp.concatenate([v_ref[r, :] for r in rows], axis=0)
        s = _scores(k_t, qs[sub * QW:(sub + 1) * QW])
        parts = []
        for t, (kb, delta, limit) in enumerate(blocks):
            msk = jnp.concatenate(
                [jnp.broadcast_to(sel_row(kb * spb + u, sub, limit), (SLC_BLOCK, QW)) for u in range(spb)],
                axis=0) > 0.5
            if delta == 0:
                msk = msk & (qq >= kk)
            sb = s[t * QB:(t + 1) * QB] + _rel_bias(tab_ref[0, delta], c31)
            parts.append(jnp.where(msk, sb, NEG))
        return jnp.concatenate(parts, axis=0), v_t

    tiles = [near_scores(sub) for sub in range(SSUB)]
    for sub, (s, v_t) in enumerate(tiles):
        _online_update(m_ref, l_ref, acc_ref, slice(sub * QW, (sub + 1) * QW), s, v_t)

    for sub in range(SSUB):
        lanes = slice(sub * QW, (sub + 1) * QW)
        o_slc = acc_ref[:, lanes] * (1.0 / l_ref[:, lanes])
        o_t = (gate_ref[0, sub, 0:1, :] * ocmp_ref[0, sub] + gate_ref[0, sub, 1:2, :] * o_slc
               + gate_ref[0, sub, 2:3, :] * owin_ref[0, sub])
        _store_heads(o_ref.at[sub * QB:(sub + 1) * QB], o_t)


def _sel_attention(proj, q_col, k_col, v_col, tabs, c31, sel_t, gates_s, ocmp_t, owin_t):
    S = proj.shape[0]
    G, NB = sel_t.shape[0], sel_t.shape[1]
    nkb = S // QB
    tile = pl.BlockSpec((1, SSUB, HEAD_DIM, QW), lambda g, c: (g, c, 0, 0))
    return pl.pallas_call(
        _sel_kernel,
        grid=(G, nkb // SSUB),
        in_specs=_qkv_specs(S, q_col, k_col, v_col, SSUB * QB)
        + [pl.BlockSpec((1, 2, QB, QW), lambda g, c: (g, 0, 0, 0)),
           pl.BlockSpec((1, 1, QW), lambda g, c: (g, 0, 0)),
           pl.BlockSpec((1, NB, SSUB * QB), lambda g, c: (g, 0, c)),
           pl.BlockSpec((1, SSUB, N_BRANCH, QW), lambda g, c: (g, c, 0, 0)),
           tile, tile],
        out_specs=pl.BlockSpec((SSUB * QB, GROUP * HEAD_DIM), lambda g, c: (c, g)),
        out_shape=jax.ShapeDtypeStruct((S, G * GROUP * HEAD_DIM), F32),
        scratch_shapes=[pltpu.VMEM((1, SW), F32), pltpu.VMEM((1, SW), F32),
                        pltpu.VMEM((HEAD_DIM, SW), F32)]
        + [pltpu.VMEM((FAR_BLOCKS * QB, SW), F32)] * 2
        + [pltpu.VMEM((FAR_BLOCKS * QB, SW), BF16)] * 2
        + [pltpu.VMEM((1, SW), F32)] * 2,
        compiler_params=_cparams(("arbitrary", "arbitrary")),
        name="selected_attention_combine",
    )(proj, proj, proj, tabs, c31, sel_t, gates_s, ocmp_t, owin_t)


def _outproj_kernel(a_ref, w_ref, x_ref, o_ref, w_s):
    @pl.when(pl.program_id(1) == 0)
    def _():
        w_s[...] = w_ref[...].astype(BF16)

    o_ref[...] = x_ref[...] + jnp.dot(a_ref[...], w_s[...], preferred_element_type=F32)


def _outproj(a, w, x, tm=1024, tn=512):
    S, D = a.shape
    N = w.shape[1]
    tm = min(tm, S)
    return pl.pallas_call(
        _outproj_kernel,
        grid=(N // tn, S // tm),
        in_specs=[pl.BlockSpec((tm, D), lambda j, i: (i, 0)),
                  pl.BlockSpec((D, tn), lambda j, i: (0, j)),
                  pl.BlockSpec((tm, tn), lambda j, i: (i, j))],
        out_specs=pl.BlockSpec((tm, tn), lambda j, i: (i, j)),
        out_shape=jax.ShapeDtypeStruct((S, N), F32),
        scratch_shapes=[pltpu.VMEM((D, tn), BF16)],
        compiler_params=_cparams(("arbitrary", "arbitrary")),
        name="outproj_residual",
    )(a, w, x)


HALO = 16
FFN_ROW_CHUNKS = 2


def _ffn_a_kernel(halo_ref, a_ref, wg_ref, wu_ref, cw_ref, cb_ref, o_ref):
    i = pl.program_id(0)
    tm = a_ref.shape[0]
    halo = halo_ref[...]
    halo = jnp.where(i > 0, halo, jnp.zeros_like(halo))
    wg = wg_ref[...].astype(BF16)
    wu = wu_ref[...].astype(BF16)
    hm = tm // FFN_ROW_CHUNKS
    prev = jnp.dot(halo, wg, preferred_element_type=F32)
    for h in range(FFN_ROW_CHUNKS):
        a = a_ref[h * hm:(h + 1) * hm, :]
        gate = jnp.dot(a, wg, preferred_element_type=F32)
        up = jnp.dot(a, wu, preferred_element_type=F32)
        gext = jnp.concatenate([prev, gate], axis=0)
        g1 = gext[HALO - 1:HALO - 1 + hm]
        g2 = gext[HALO - 2:HALO - 2 + hm]
        y = cb_ref[...] + g2 * cw_ref[0:1, :]
        y = y + g1 * cw_ref[1:2, :]
        y = y + gate * cw_ref[2:3, :]
        act = y * (1.0 / (1.0 + jnp.exp(-y)))
        o_ref[h * hm:(h + 1) * hm, :] = (act * up).astype(o_ref.dtype)
        prev = gate[hm - HALO:]


def _ffn_a(hf, wg, wu, cw, cb, tm=1024, tf=256):
    S, D = hf.shape
    Fp = wg.shape[1]
    assert Fp % tf == 0
    tm = min(tm, S)
    hb = tm // HALO
    return pl.pallas_call(
        _ffn_a_kernel,
        grid=(S // tm, Fp // tf),
        in_specs=[pl.BlockSpec((HALO, D), lambda i, f: (jnp.maximum(i * hb - 1, 0), 0)),
                  pl.BlockSpec((tm, D), lambda i, f: (i, 0)),
                  pl.BlockSpec((D, tf), lambda i, f: (0, f)),
                  pl.BlockSpec((D, tf), lambda i, f: (0, f)),
                  pl.BlockSpec((cw.shape[0], tf), lambda i, f: (0, f)),
                  pl.BlockSpec((1, tf), lambda i, f: (0, f))],
        out_specs=pl.BlockSpec((tm, tf), lambda i, f: (i, f)),
        out_shape=jax.ShapeDtypeStruct((S, Fp), BF16),
        compiler_params=_cparams(("arbitrary", "arbitrary")),
        name="ffn_gate_up",
    )(hf, hf, wg, wu, cw, cb)


def _ffn_b_kernel(a_ref, w_ref, x_ref, o_ref):
    o_ref[...] = x_ref[...] + jnp.dot(a_ref[...], w_ref[...], preferred_element_type=F32)


def _ffn_b(h, w, x, tm=512, tn=512):
    S, F = h.shape
    N = w.shape[1]
    tm = min(tm, S)
    return pl.pallas_call(
        _ffn_b_kernel,
        grid=(S // tm, N // tn),
        in_specs=[pl.BlockSpec((tm, F), lambda i, j: (i, 0)),
                  pl.BlockSpec((F, tn), lambda i, j: (0, j)),
                  pl.BlockSpec((tm, tn), lambda i, j: (i, j))],
        out_specs=pl.BlockSpec((tm, tn), lambda i, j: (i, j)),
        out_shape=jax.ShapeDtypeStruct((S, N), F32),
        compiler_params=_cparams(("arbitrary", "arbitrary")),
        name="ffn_down_residual",
    )(h, w, x)


def _layer(x, rel_bias, norm_mix_g, w_in, a_q_norm_g, a_k_norm_g, a_sinks, b_q_norm_g, b_k_norm_g,
           cmp_pos_emb, cmp_w1, cmp_b1, cmp_w2, cmp_b2, out_norm_g, w_out, norm_ffn_g, w_gate,
           w_up, conv_w, conv_b, w_down):
    S, D = x.shape
    aw = A_Q_HEADS * HEAD_DIM
    akv = A_KV_HEADS * HEAD_DIM
    bw = B_Q_HEADS * HEAD_DIM
    bkv = B_KV_HEADS * HEAD_DIM
    sizes = [aw, akv, akv, bw] + [bkv] * 6 + [N_BRANCH * B_Q_HEADS]
    offs = np.concatenate([[0], np.cumsum(sizes)]).tolist()
    wt = jnp.swapaxes(w_in, 0, 1)
    tn = 512
    order_a = [0, 1, 2, 3, 6, 7, 8, 9]
    n_gate = sizes[10]
    wg_t = jnp.pad(wt[offs[10]:offs[10] + n_gate], ((0, QB - n_gate), (0, 0)))
    one = jnp.ones((HEAD_DIM,), F32)
    seg_gain = {0: a_q_norm_g, 1: a_k_norm_g * KEY_SCALE, 3: b_q_norm_g,
                6: b_k_norm_g[1] * KEY_SCALE, 8: b_k_norm_g[2] * KEY_SCALE}
    kinds, tiles, gains, col = [], [], [], {}
    c0 = 0
    for k in order_a:
        col[k] = c0
        for t in range(sizes[k] // tn):
            kinds.append(1 if k in seg_gain else 0)
            tiles.append(offs[k] // tn + t)
            gains.append(seg_gain.get(k, one))
        c0 += sizes[k]
    kinds = jnp.asarray(kinds, jnp.int32)
    tiles = jnp.asarray(tiles, jnp.int32)
    gains = jnp.stack(gains).astype(F32).reshape(len(gains), 1, HEAD_DIM)

    hn = _rmsnorm([x], norm_mix_g)
    proj = _inproj_a(hn, wt, kinds, tiles, gains, tn=tn)
    kv32, gates = _inproj_b(hn, wt, offs[4] // tn, 2 * bkv // tn, wg_t)

    tab_a = rel_bias[:, :A_Q_HEADS]
    tab_b = rel_bias[:, A_Q_HEADS:]
    near_a = _stack_heads(_near_bias_tables(tab_a), A_KV_HEADS)
    near_b = _stack_heads(_near_bias_tables(tab_b), B_KV_HEADS)
    head_row = lambda v, G: _stack_heads(jnp.broadcast_to(v.astype(F32)[:, None, None],
                                                          (v.shape[0], 1, QB)), G)
    c31_a = head_row(tab_a[N_BUCKETS - 1], A_KV_HEADS)
    c31_b = head_row(tab_b[N_BUCKETS - 1], B_KV_HEADS)

    qcol = lambda k: col[k] // (GROUP * HEAD_DIM)
    hcol = lambda k: col[k] // HEAD_DIM
    o_a = _band_attention(proj, A_KV_HEADS, qcol(0), hcol(1), hcol(2), near_a, c31_a,
                          head_row(a_sinks, A_KV_HEADS), A_WINDOW, transposed_out=False)

    NC = S // CMP_STRIDE
    n_cmp = (S - CMP_BLOCK) // CMP_STRIDE + 1
    w1 = cmp_w1.reshape(2, CMP_BLOCK, HEAD_DIM, cmp_w1.shape[-1]).astype(BF16)
    kc, vct = _compress(kv32, B_KV_HEADS, cmp_pos_emb.astype(F32), w1, cmp_b1[:, None, :].astype(F32),
                        cmp_w2.astype(BF16), cmp_b2[:, None, :].astype(F32),
                        (b_k_norm_g[0] * KEY_SCALE).reshape(1, HEAD_DIM).astype(F32))

    NB = S // SLC_BLOCK
    ii = np.arange(NC)[None, :]
    jj = np.arange(NB)[:, None]
    ovl_t = ((ii * CMP_STRIDE <= jj * SLC_BLOCK + SLC_BLOCK - 1)
             & (ii * CMP_STRIDE + CMP_BLOCK - 1 >= jj * SLC_BLOCK) & (ii < n_cmp))
    ovl_t = jnp.asarray(ovl_t.astype(np.float32), BF16)
    ocmp_t, sel_t = _cmp_select(proj, qcol(3), kc, vct,
                                _stack_heads(_cmp_bias_table(tab_b), B_KV_HEADS), c31_b, ovl_t, n_cmp)

    owin_t = _band_attention(proj, B_KV_HEADS, qcol(3), hcol(8), hcol(9), near_b, c31_b, None,
                             B_WINDOW, transposed_out=True)
    gates_s = gates[:, :n_gate].reshape(S // QB, QB, B_KV_HEADS, GROUP, N_BRANCH)
    gates_s = gates_s.transpose(2, 0, 4, 3, 1).reshape(B_KV_HEADS, S // QB, N_BRANCH, QW)
    o_b = _sel_attention(proj, qcol(3), hcol(6), hcol(7), near_b, c31_b, sel_t, gates_s,
                         ocmp_t, owin_t)

    on = _rmsnorm([o_a, o_b], out_norm_g)
    x2 = _outproj(on, w_out, x)

    hf = _rmsnorm([x2], norm_ffn_g)
    hmid = _ffn_a(hf, w_gate, w_up, conv_w.astype(F32), conv_b[None, :].astype(F32))
    return _ffn_b(hmid, w_down.astype(BF16), x2)


def kernel(x, rel_bias, norm_mix_g, w_in, a_q_norm_g, a_k_norm_g, a_sinks, b_q_norm_g, b_k_norm_g,
           cmp_pos_emb, cmp_w1, cmp_b1, cmp_w2, cmp_b2, out_norm_g, w_out, norm_ffn_g, w_gate, w_up,
           conv_w, conv_b, w_down):
    depth = w_in.shape[0]
    batch = x.shape[0]
    outs = []
    for b in range(batch):
        h = x[b]
        for l in range(depth):
            h = _layer(h, rel_bias, norm_mix_g[l], w_in[l], a_q_norm_g[l], a_k_norm_g[l], a_sinks[l],
                       b_q_norm_g[l], b_k_norm_g[l], cmp_pos_emb[l], cmp_w1[l], cmp_b1[l], cmp_w2[l],
                       cmp_b2[l], out_norm_g[l], w_out[l], norm_ffn_g[l], w_gate[l], w_up[l],
                       conv_w[l], conv_b[l], w_down[l])
        outs.append(h)
    return jnp.stack(outs)
```

```python
import functools
import math

import numpy as np
import jax
import jax.numpy as jnp
from jax import lax
from jax.experimental import pallas as pl
from jax.experimental.pallas import tpu as pltpu

F32 = jnp.float32
BF16 = jnp.bfloat16

HEAD_DIM = 128
A_Q_HEADS = 16
A_KV_HEADS = 4
B_Q_HEADS = 16
B_KV_HEADS = 4
GROUP = 4
A_WINDOW = 128
B_WINDOW = 512
CMP_BLOCK = 32
CMP_STRIDE = 16
SLC_BLOCK = 64
SLC_TOPK = 16
N_BRANCH = 3
N_BUCKETS = 32
MAX_DISTANCE = 128
EPS = 1e-6
NEG = -1e30
FORCE_SCORE = 1e6
SCALE = HEAD_DIM ** -0.5

QB = 128
CMP_WIN = 24
CMP_TAB = 40
VMEM_LIMIT = 56 * 1024 * 1024


def _cparams(sem, **kw):
    return pltpu.CompilerParams(dimension_semantics=sem, vmem_limit_bytes=VMEM_LIMIT, **kw)


def _t5_bucket_np(dist):
    n = np.maximum(dist, 0)
    max_exact = N_BUCKETS // 2
    nf = np.maximum(n, 1).astype(np.float32)
    large = max_exact + (np.log(nf / max_exact) / math.log(MAX_DISTANCE / max_exact)
                         * (N_BUCKETS - max_exact)).astype(np.int32)
    large = np.minimum(large, N_BUCKETS - 1)
    return np.where(n < max_exact, n, large).astype(np.int32)


def _bias_from_buckets(tab, idx):
    onehot = (idx[None] == np.arange(N_BUCKETS).reshape((-1,) + (1,) * idx.ndim)).astype(np.float32)
    return jnp.einsum('bh,b...->h...', tab.astype(F32), jnp.asarray(onehot),
                      precision=lax.Precision.HIGHEST)


def _near_bias_tables(tab):
    kk = np.arange(QB)[:, None]
    qq = np.arange(QB)[None, :]
    return _bias_from_buckets(tab, np.stack([_t5_bucket_np(qq - kk), _t5_bucket_np(QB + qq - kk)]))


def _cmp_bias_table(tab):
    npr = np.arange(CMP_TAB)[:, None] - 16
    qq = np.arange(QB)[None, :]
    return _bias_from_buckets(tab, _t5_bucket_np(qq - CMP_STRIDE * npr - (CMP_BLOCK - 1)))


def _rmsnorm_kernel(*refs, n_in):
    x_refs, g_ref, o_ref = refs[:n_in], refs[n_in], refs[n_in + 1]
    off = 0
    for x_ref in x_refs:
        x = x_ref[...]
        w = x.shape[-1]
        y = x * lax.rsqrt(jnp.mean(x * x, axis=-1, keepdims=True) + EPS)
        o_ref[:, off:off + w] = (y * g_ref[:, off:off + w]).astype(o_ref.dtype)
        off += w


def _rmsnorm(xs, gain, tr=256):
    S = xs[0].shape[0]
    widths = [x.shape[1] for x in xs]
    n = sum(widths)
    tr = min(tr, S)
    return pl.pallas_call(
        functools.partial(_rmsnorm_kernel, n_in=len(xs)),
        grid=(S // tr,),
        in_specs=[pl.BlockSpec((tr, w), lambda i: (i, 0)) for w in widths]
        + [pl.BlockSpec((1, n), lambda i: (0, 0))],
        out_specs=pl.BlockSpec((tr, n), lambda i: (i, 0)),
        out_shape=jax.ShapeDtypeStruct((S, n), BF16),
        compiler_params=_cparams(("arbitrary",)),
        name="rmsnorm",
    )(*xs, gain.reshape(1, n).astype(F32))


NORM_ROW_CHUNKS = 2


def _dot_wt(a, wt):
    return lax.dot_general(a, wt.astype(BF16), (((1,), (1,)), ((), ())), preferred_element_type=F32)


def _inproj_a_kernel(kind_ref, tile_ref, a_ref, w_ref, g_ref, o_ref, w_s):
    j = pl.program_id(0)

    @pl.when(pl.program_id(1) == 0)
    def _():
        w_s[...] = w_ref[...].astype(BF16)

    @pl.when(kind_ref[j] == 0)
    def _():
        o_ref[...] = _dot_wt(a_ref[...], w_s[...]).astype(o_ref.dtype)

    @pl.when(kind_ref[j] == 1)
    def _():
        g = g_ref[0]
        hm = a_ref.shape[0] // NORM_ROW_CHUNKS
        for c in range(NORM_ROW_CHUNKS):
            rows = slice(c * hm, (c + 1) * hm)
            acc = _dot_wt(a_ref[rows, :], w_s[...])
            for h in range(acc.shape[1] // HEAD_DIM):
                sl = acc[:, h * HEAD_DIM:(h + 1) * HEAD_DIM]
                y = sl * lax.rsqrt(jnp.mean(sl * sl, axis=-1, keepdims=True) + EPS)
                o_ref[rows, h * HEAD_DIM:(h + 1) * HEAD_DIM] = (y * g).astype(o_ref.dtype)


def _inproj_a(hn, wt, kinds, tiles, gains, tm=1024, tn=512):
    S, D = hn.shape
    n_tiles = kinds.shape[0]
    tm = min(tm, S)
    grid_spec = pltpu.PrefetchScalarGridSpec(
        num_scalar_prefetch=2,
        grid=(n_tiles, S // tm),
        in_specs=[pl.BlockSpec((tm, D), lambda j, i, k, t: (i, 0)),
                  pl.BlockSpec((tn, D), lambda j, i, k, t: (t[j], 0)),
                  pl.BlockSpec((1, 1, HEAD_DIM), lambda j, i, k, t: (j, 0, 0))],
        out_specs=pl.BlockSpec((tm, tn), lambda j, i, k, t: (i, j)),
        scratch_shapes=[pltpu.VMEM((tn, D), BF16)],
    )
    return pl.pallas_call(
        _inproj_a_kernel,
        grid_spec=grid_spec,
        out_shape=jax.ShapeDtypeStruct((S, n_tiles * tn), BF16),
        compiler_params=_cparams(("arbitrary", "arbitrary")),
        name="inproj_heads",
    )(kinds, tiles, hn, wt, gains)


def _inproj_b_kernel(a_ref, w_ref, wg_ref, o_ref, gate_ref):
    a = a_ref[...]
    o_ref[...] = _dot_wt(a, w_ref[...])

    @pl.when(pl.program_id(1) == 0)
    def _():
        gate_ref[...] = 1.0 / (1.0 + jnp.exp(-_dot_wt(a, wg_ref[...])))


def _inproj_b(hn, wt, tile0, n_tiles, wg_t, tm=1024, tn=512):
    S, D = hn.shape
    tm = min(tm, S)
    ng = wg_t.shape[0]
    return pl.pallas_call(
        _inproj_b_kernel,
        grid=(S // tm, n_tiles),
        in_specs=[pl.BlockSpec((tm, D), lambda i, j: (i, 0)),
                  pl.BlockSpec((tn, D), lambda i, j: (tile0 + j, 0)),
                  pl.BlockSpec((ng, D), lambda i, j: (0, 0))],
        out_specs=[pl.BlockSpec((tm, tn), lambda i, j: (i, j)),
                   pl.BlockSpec((tm, ng), lambda i, j: (i, 0))],
        out_shape=[jax.ShapeDtypeStruct((S, n_tiles * tn), F32),
                   jax.ShapeDtypeStruct((S, ng), F32)],
        compiler_params=_cparams(("arbitrary", "arbitrary")),
        name="inproj_cmp_gates",
    )(hn, wt, wg_t)


def _gelu_tanh(x):
    return 0.5 * x * (1.0 + jnp.tanh(math.sqrt(2.0 / math.pi) * (x + 0.044715 * (x * x * x))))


def _compress_kernel(tk_ref, tv_ref, pos_ref, w1_ref, b1_ref, w2_ref, b2_ref, gk_ref,
                     kc_ref, vct_ref):
    nc = kc_ref.shape[1]

    def mlp(t_ref, kv):
        a0 = jnp.zeros((nc, w1_ref.shape[-1]), F32)
        a1 = jnp.zeros((nc, w1_ref.shape[-1]), F32)
        for b in range(CMP_STRIDE):
            t = t_ref[pl.ds(b, nc, stride=CMP_STRIDE), :]
            lo, hi = b, CMP_STRIDE + b
            a0 = a0 + jnp.dot((t + pos_ref[kv, lo:lo + 1, :]).astype(BF16), w1_ref[kv, lo],
                              preferred_element_type=F32)
            a1 = a1 + jnp.dot((t + pos_ref[kv, hi:hi + 1, :]).astype(BF16), w1_ref[kv, hi],
                              preferred_element_type=F32)
        h = a0 + pltpu.roll(a1, nc - 1, 0) + b1_ref[kv]
        h = _gelu_tanh(h)
        return jnp.dot(h.astype(BF16), w2_ref[kv], preferred_element_type=F32) + b2_ref[kv]

    ck = mlp(tk_ref, 0)
    ck = ck * lax.rsqrt(jnp.mean(ck * ck, axis=-1, keepdims=True) + EPS) * gk_ref[...]
    kc_ref[0] = ck.astype(kc_ref.dtype)
    cv = mlp(tv_ref, 1)
    vct_ref[0] = cv.T.astype(vct_ref.dtype)


def _compress(kv32, G, pos, w1, b1, w2, b2, gk):
    S = kv32.shape[0]
    NC = S // CMP_STRIDE
    hid = w1.shape[-1]
    return pl.pallas_call(
        _compress_kernel,
        grid=(G,),
        in_specs=[pl.BlockSpec((S, HEAD_DIM), lambda g: (0, g)),
                  pl.BlockSpec((S, HEAD_DIM), lambda g: (0, G + g)),
                  pl.BlockSpec((2, CMP_BLOCK, HEAD_DIM), lambda g: (0, 0, 0)),
                  pl.BlockSpec((2, CMP_BLOCK, HEAD_DIM, hid), lambda g: (0, 0, 0, 0)),
                  pl.BlockSpec((2, 1, hid), lambda g: (0, 0, 0)),
                  pl.BlockSpec((2, hid, HEAD_DIM), lambda g: (0, 0, 0)),
                  pl.BlockSpec((2, 1, HEAD_DIM), lambda g: (0, 0, 0)),
                  pl.BlockSpec((1, HEAD_DIM), lambda g: (0, 0))],
        out_specs=[pl.BlockSpec((1, NC, HEAD_DIM), lambda g: (g, 0, 0)),
                   pl.BlockSpec((1, HEAD_DIM, NC), lambda g: (g, 0, 0))],
        out_shape=[jax.ShapeDtypeStruct((G, NC, HEAD_DIM), BF16),
                   jax.ShapeDtypeStruct((G, HEAD_DIM, NC), BF16)],
        compiler_params=_cparams(("arbitrary",)),
        name="compress",
    )(kv32, kv32, pos, w1, b1, w2, b2, gk)


QW = GROUP * QB


def _stacked_queries(q_ref):
    return jnp.concatenate([q_ref[:, r * HEAD_DIM:(r + 1) * HEAD_DIM] for r in range(GROUP)], axis=0)


LOG2E = math.log2(math.e)
KEY_SCALE = SCALE * LOG2E
QSUB = 4


def _scores(k_t, qs):
    return lax.dot_general(k_t, qs, (((1,), (1,)), ((), ())), preferred_element_type=F32)


def _rel_bias(tab, c31):
    return (tab - c31) * LOG2E


def _init_state(m_ref, l_ref, acc_ref):
    m_ref[...] = jnp.full(m_ref.shape, NEG, F32)
    l_ref[...] = jnp.zeros(l_ref.shape, F32)
    acc_ref[...] = jnp.zeros(acc_ref.shape, F32)


def _online_update(m_ref, l_ref, acc_ref, lanes, t, v_t):
    m_old = m_ref[:, lanes]
    m_new = jnp.maximum(m_old, jnp.max(t, axis=0, keepdims=True))
    alpha = jnp.exp2(m_old - m_new)
    p = jnp.exp2(t - m_new)
    l_ref[:, lanes] = alpha * l_ref[:, lanes] + jnp.sum(p, axis=0, keepdims=True)
    pv = lax.dot_general(v_t, p.astype(BF16), (((0,), (0,)), ((), ())), preferred_element_type=F32)
    acc_ref[:, lanes] = acc_ref[:, lanes] * alpha + pv
    m_ref[:, lanes] = m_new


def _key_query_iotas():
    kk = lax.broadcasted_iota(jnp.int32, (QB, QW), 0)
    qq = lax.broadcasted_iota(jnp.int32, (QB, QW), 1) & (QB - 1)
    return kk, qq


def _load_kv(k_ref, v_ref, kb0, n):
    rows = pl.ds(pl.multiple_of(kb0 * QB, QB), n * QB)
    return k_ref[rows, :], v_ref[rows, :]


def _store_heads(o_ref, o_t):
    for r in range(GROUP):
        o_ref[:, r * HEAD_DIM:(r + 1) * HEAD_DIM] = o_t[:, r * QB:(r + 1) * QB].T


def _stack_heads(a, G):
    a = a.reshape((G, GROUP) + a.shape[1:])
    a = jnp.moveaxis(a, 1, -2)
    return a.reshape(a.shape[:-2] + (QW,))


def _band_kernel(q_ref, k_ref, v_ref, tab_ref, c31_ref, sink_ref, o_ref, m_ref, l_ref, acc_ref,
                 *, nback, use_sinks, transposed_out):
    c0 = pl.program_id(1) * QSUB
    _init_state(m_ref, l_ref, acc_ref)
    kk, qq = _key_query_iotas()
    c31 = c31_ref[0]

    def masked_scores(sub, kb0, deltas):
        k_t, v_t = _load_kv(k_ref, v_ref, kb0, len(deltas))
        s = _scores(k_t, _stacked_queries(q_ref.at[sub * QB:(sub + 1) * QB]))
        parts = []
        for t, delta in enumerate(deltas):
            sb = s[t * QB:(t + 1) * QB]
            if delta <= 1:
                sb = sb + _rel_bias(tab_ref[0, delta], c31)
            if delta == 0:
                sb = jnp.where(qq >= kk, sb, NEG)
            elif delta == nback:
                sb = jnp.where(qq < kk, sb, NEG)
            parts.append(sb)
        return (parts[0] if len(parts) == 1 else jnp.concatenate(parts, axis=0)), v_t

    def update(sub, s, v_t):
        _online_update(m_ref.at[sub], l_ref.at[sub], acc_ref.at[sub], slice(None), s, v_t)

    @pl.when(c0 >= nback)
    def _():
        band = list(range(nback, -1, -1))
        tiles = [masked_scores(sub, c0 + sub - nback, band) for sub in range(QSUB)]
        for sub, (s, v_t) in enumerate(tiles):
            update(sub, s, v_t)

    @pl.when(c0 < nback)
    def _():
        for sub in range(QSUB):
            for delta in range(nback, -1, -1):
                @pl.when(c0 + sub - delta >= 0)
                def _(sub=sub, delta=delta):
                    update(sub, *masked_scores(sub, c0 + sub - delta, [delta]))

    for sub in range(QSUB):
        l = l_ref[sub]
        acc = acc_ref[sub]
        if use_sinks:
            m = m_ref[sub]
            sk = _rel_bias(sink_ref[0], c31)
            m_f = jnp.maximum(m, sk)
            a = jnp.exp2(m - m_f)
            l = l * a + jnp.exp2(sk - m_f)
            acc = acc * a
        o_t = acc * (1.0 / l)
        if transposed_out:
            o_ref[0, sub] = o_t
        else:
            _store_heads(o_ref.at[sub * QB:(sub + 1) * QB], o_t)


def _qkv_specs(S, q_col, k_col, v_col, q_rows=QB):
    return [pl.BlockSpec((q_rows, GROUP * HEAD_DIM), lambda g, c: (c, q_col + g)),
            pl.BlockSpec((S, HEAD_DIM), lambda g, c: (0, k_col + g)),
            pl.BlockSpec((S, HEAD_DIM), lambda g, c: (0, v_col + g))]


def _band_attention(proj, G, q_col, k_col, v_col, tabs, c31, sinks, window, transposed_out):
    S = proj.shape[0]
    nkb = S // QB
    nback = -(-(window - 1) // QB)
    use_sinks = sinks is not None
    if sinks is None:
        sinks = jnp.zeros_like(c31)
    if transposed_out:
        out_shape = jax.ShapeDtypeStruct((G, nkb, HEAD_DIM, QW), F32)
        out_spec = pl.BlockSpec((1, QSUB, HEAD_DIM, QW), lambda g, c: (g, c, 0, 0))
    else:
        out_shape = jax.ShapeDtypeStruct((S, G * GROUP * HEAD_DIM), F32)
        out_spec = pl.BlockSpec((QSUB * QB, GROUP * HEAD_DIM), lambda g, c: (c, g))
    row = pl.BlockSpec((1, 1, QW), lambda g, c: (g, 0, 0))
    return pl.pallas_call(
        functools.partial(_band_kernel, nback=nback, use_sinks=use_sinks,
                          transposed_out=transposed_out),
        grid=(G, nkb // QSUB),
        in_specs=_qkv_specs(S, q_col, k_col, v_col, QSUB * QB)
        + [pl.BlockSpec((1, 2, QB, QW), lambda g, c: (g, 0, 0, 0)), row, row],
        out_specs=out_spec,
        out_shape=out_shape,
        scratch_shapes=[pltpu.VMEM((QSUB, 1, QW), F32), pltpu.VMEM((QSUB, 1, QW), F32),
                        pltpu.VMEM((QSUB, HEAD_DIM, QW), F32)],
        compiler_params=_cparams(("arbitrary", "arbitrary")),
        name="band_attention_w%d" % window,
    )(proj, proj, proj, tabs, c31, sinks)


def _cmp_select_kernel(q_ref, kc_ref, vct_ref, tab_ref, c31_ref, ovl_ref, ot_ref, sel_ref, s_ref,
                       *, n_valid, topk):
    c0 = pl.program_id(1) * QSUB
    nc = kc_ref.shape[1]
    nb = ovl_ref.shape[0]
    rown = lax.broadcasted_iota(jnp.int32, (nc, QW), 0)
    wrow = lax.broadcasted_iota(jnp.int32, (CMP_WIN, QW), 0)
    wq = lax.broadcasted_iota(jnp.int32, (CMP_WIN, QW), 1) & (QB - 1)
    lane_q = lax.broadcasted_iota(jnp.int32, (1, QW), 1) & (QB - 1)

    for sub in range(QSUB):
        c = c0 + sub
        qs = _stacked_queries(q_ref.at[sub * QB:(sub + 1) * QB])
        w0 = pl.multiple_of(jnp.maximum(8 * c - 16, 0), 8)
        toff = pl.multiple_of(w0 - 8 * c + 16, 8)
        n_abs = w0 + wrow
        dist = (c * QB + wq) - (n_abs * CMP_STRIDE + (CMP_BLOCK - 1))
        valid_w = (dist >= 0) & (n_abs < n_valid)
        s_ref[sub] = jnp.where(rown < w0, _scores(kc_ref[0], qs), NEG)
        bias_w = _rel_bias(tab_ref[0, pl.ds(toff, CMP_WIN), :], c31_ref[0])
        s_w = _scores(kc_ref[0, pl.ds(w0, CMP_WIN), :], qs) + bias_w
        s_ref[sub, pl.ds(w0, CMP_WIN), :] = jnp.where(valid_w, s_w, NEG)

    p_groups = []
    for sub in range(QSUB):
        has_any = ((c0 + sub) * QB + lane_q >= CMP_BLOCK - 1).astype(F32)
        s = s_ref[sub]
        m = jnp.max(s, axis=0, keepdims=True)
        e = jnp.exp2(s - m)
        p = e * (has_any / jnp.sum(e, axis=0, keepdims=True))
        ot_ref[0, sub] = jnp.dot(vct_ref[0], p.astype(BF16), preferred_element_type=F32)
        p_grp = p[:, 0:QB]
        for r in range(1, GROUP):
            p_grp = p_grp + p[:, r * QB:(r + 1) * QB]
        p_groups.append(p_grp.astype(BF16))

    nq = QSUB * QB
    scores = jnp.dot(ovl_ref[...], jnp.concatenate(p_groups, axis=1), preferred_element_type=F32)
    blk = lax.broadcasted_iota(jnp.int32, (nb, nq), 0)
    cur = lax.shift_right_logical(c0 * QB + lax.broadcasted_iota(jnp.int32, (nb, nq), 1),
                                  int(math.log2(SLC_BLOCK)))
    forced = (blk == 0) | (blk == cur) | (blk == cur - 1)
    work = jnp.where(forced, FORCE_SCORE, jnp.where(blk <= cur, scores, -1.0))
    blk_f = blk.astype(F32)
    sel = jnp.zeros((nb, nq), F32)
    for _ in range(topk):
        mx = jnp.max(work, axis=0, keepdims=True)
        first = jnp.min(jnp.where(work == mx, blk_f, float(nb)), axis=0, keepdims=True)
        hit = blk_f == first
        sel = jnp.where(hit, 1.0, sel)
        work = jnp.where(hit, -3e38, work)
    sel_ref[0] = sel


def _cmp_select(proj, q_col, kc, vct, tab, c31, ovl_t, n_valid):
    nkb = proj.shape[0] // QB
    G, NC = kc.shape[0], kc.shape[1]
    NB = ovl_t.shape[0]
    return pl.pallas_call(
        functools.partial(_cmp_select_kernel, n_valid=n_valid, topk=min(SLC_TOPK, NB)),
        grid=(G, nkb // QSUB),
        in_specs=[pl.BlockSpec((QSUB * QB, GROUP * HEAD_DIM), lambda g, c: (c, q_col + g)),
                  pl.BlockSpec((1, NC, HEAD_DIM), lambda g, c: (g, 0, 0)),
                  pl.BlockSpec((1, HEAD_DIM, NC), lambda g, c: (g, 0, 0)),
                  pl.BlockSpec((1, CMP_TAB, QW), lambda g, c: (g, 0, 0)),
                  pl.BlockSpec((1, 1, QW), lambda g, c: (g, 0, 0)),
                  pl.BlockSpec((NB, NC), lambda g, c: (0, 0))],
        out_specs=[pl.BlockSpec((1, QSUB, HEAD_DIM, QW), lambda g, c: (g, c, 0, 0)),
                   pl.BlockSpec((1, NB, QSUB * QB), lambda g, c: (g, 0, c))],
        out_shape=[jax.ShapeDtypeStruct((G, nkb, HEAD_DIM, QW), F32),
                   jax.ShapeDtypeStruct((G, NB, nkb * QB), F32)],
        scratch_shapes=[pltpu.VMEM((QSUB, NC, QW), F32)],
        compiler_params=_cparams(("arbitrary", "arbitrary")),
        name="cmp_attention_select",
    )(proj, kc, vct, tab, c31, ovl_t)


FAR_BLOCKS = 4


def _sel_block(c, q_ref, sel_row, gate_ref, ocmp_ref, owin_ref, o_ref, k_ref, v_ref, tab_ref, c31,
               m_ref, l_ref, acc_ref, s_a, s_b, p_a, p_b, al_a, al_b):
    _init_state(m_ref, l_ref, acc_ref)
    kk, qq = _key_query_iotas()
    qs = _stacked_queries(q_ref)
    spb = QB // SLC_BLOCK
    tk = FAR_BLOCKS * QB
    last_tile = k_ref.shape[0] // tk - 1
    n_far = jnp.maximum(c - 1, 0)
    n_tiles = (n_far + FAR_BLOCKS - 1) // FAR_BLOCKS

    def head_row(r, row_limit=None, r_load=None):
        row = sel_row(r if r_load is None else r_load)
        if row_limit is not None:
            row = jnp.where(r < row_limit, row, 0.0)
        return jnp.concatenate([row] * GROUP, axis=1)

    def tile_rows(i):
        return pl.ds(pl.multiple_of(jnp.minimum(i, last_tile) * tk, tk), tk)

    def far_scores(i, s_buf):
        s = _scores(k_ref[tile_rows(i), :], qs)
        r0 = i * (FAR_BLOCKS * spb)
        r0_load = jnp.minimum(i, last_tile) * (FAR_BLOCKS * spb)
        for t in range(FAR_BLOCKS * spb):
            row = head_row(r0 + t, n_far * spb, r0_load + t)
            blk = slice(t * SLC_BLOCK, (t + 1) * SLC_BLOCK)
            s_buf[blk, :] = jnp.where(row > 0.5, s[blk], NEG)

    def far_softmax(s_buf, p_buf, al_buf):
        blocks = [slice(t * SLC_BLOCK, (t + 1) * SLC_BLOCK) for t in range(FAR_BLOCKS * spb)]
        sub8 = lambda a: a.reshape(SLC_BLOCK // 8, 8, QW)
        m_part = jnp.full((8, QW), NEG, F32)
        for blk in blocks:
            m_part = jnp.maximum(m_part, jnp.max(sub8(s_buf[blk, :]), axis=0))
        m_old = m_ref[...]
        m_new = jnp.maximum(m_old, jnp.max(m_part, axis=0, keepdims=True))
        alpha = jnp.exp2(m_old - m_new)
        l_part = jnp.zeros((8, QW), F32)
        for blk in blocks:
            p = jnp.exp2(s_buf[blk, :] - m_new)
            l_part = l_part + jnp.sum(sub8(p), axis=0)
            p_buf[blk, :] = p.astype(BF16)
        l_ref[...] = alpha * l_ref[...] + jnp.sum(l_part, axis=0, keepdims=True)
        m_ref[...] = m_new
        al_buf[...] = alpha

    def far_values(i, p_buf, al_buf):
        pv = lax.dot_general(v_ref[tile_rows(i), :], p_buf[...], (((0,), (0,)), ((), ())),
                             preferred_element_type=F32)
        acc_ref[...] = acc_ref[...] * al_buf[...] + pv

    far_scores(0, s_a)
    far_scores(1, s_b)
    for p_buf, al_buf in ((p_a, al_a), (p_b, al_b)):
        p_buf[...] = jnp.zeros(p_buf.shape, BF16)
        al_buf[...] = jnp.ones(al_buf.shape, F32)

    def pair_body(j, carry):
        i = 2 * j
        far_values(jnp.maximum(i - 2, 0), p_a, al_a)
        far_values(jnp.maximum(i - 1, 0), p_b, al_b)
        far_softmax(s_a, p_a, al_a)
        far_softmax(s_b, p_b, al_b)
        far_scores(i + 2, s_a)
        far_scores(i + 3, s_b)
        return carry

    n_pairs = (n_tiles + 1) // 2
    lax.fori_loop(0, n_pairs, pair_body, 0)
    far_values(jnp.maximum(2 * n_pairs - 2, 0), p_a, al_a)
    far_values(jnp.maximum(2 * n_pairs - 1, 0), p_b, al_b)

    blocks = [(jnp.maximum(c - 1, 0), 1, c * spb), (c, 0, None)]
    rows = [pl.ds(pl.multiple_of(kb * QB, QB), QB) for kb, _, _ in blocks]
    k_t = jnp.concatenate([k_ref[r, :] for r in rows], axis=0)
    v_t = jnp.concatenate([v_ref[r, :] for r in rows], axis=0)
    s = _scores(k_t, qs)
    parts = []
    for t, (kb, delta, limit) in enumerate(blocks):
        msk = jnp.concatenate(
            [jnp.broadcast_to(head_row(kb * spb + u, limit), (SLC_BLOCK, QW)) for u in range(spb)],
            axis=0) > 0.5
        if delta == 0:
            msk = msk & (qq >= kk)
        sb = s[t * QB:(t + 1) * QB] + _rel_bias(tab_ref[0, delta], c31)
        parts.append(jnp.where(msk, sb, NEG))
    _online_update(m_ref, l_ref, acc_ref, slice(None), jnp.concatenate(parts, axis=0), v_t)

    o_slc = acc_ref[...] * (1.0 / l_ref[...])
    o_t = gate_ref[0:1, :] * ocmp_ref[...] + gate_ref[1:2, :] * o_slc + gate_ref[2:3, :] * owin_ref[...]
    _store_heads(o_ref, o_t)


def _sel_kernel(q_ref, k_ref, v_ref, tab_ref, c31_ref, sel_ref, gate_ref, ocmp_ref, owin_ref, o_ref,
                *scratch):
    _sel_block(pl.program_id(1), q_ref, lambda r: sel_ref[0, pl.ds(r, 1), :],
               gate_ref.at[0, 0], ocmp_ref.at[0, 0], owin_ref.at[0, 0], o_ref,
               k_ref, v_ref, tab_ref, c31_ref[0], *scratch)


def _sel_attention(proj, q_col, k_col, v_col, tabs, c31, sel_t, gates_s, ocmp_t, owin_t):
    S = proj.shape[0]
    G, NB = sel_t.shape[0], sel_t.shape[1]
    nkb = S // QB
    nq = 1
    tile = pl.BlockSpec((1, nq, HEAD_DIM, QW), lambda g, c: (g, c, 0, 0))
    return pl.pallas_call(
        _sel_kernel,
        grid=(G, nkb // nq),
        in_specs=_qkv_specs(S, q_col, k_col, v_col, nq * QB)
        + [pl.BlockSpec((1, 2, QB, QW), lambda g, c: (g, 0, 0, 0)),
           pl.BlockSpec((1, 1, QW), lambda g, c: (g, 0, 0)),
           pl.BlockSpec((1, NB, nq * QB), lambda g, c: (g, 0, c)),
           pl.BlockSpec((1, nq, N_BRANCH, QW), lambda g, c: (g, c, 0, 0)),
           tile, tile],
        out_specs=pl.BlockSpec((nq * QB, GROUP * HEAD_DIM), lambda g, c: (c, g)),
        out_shape=jax.ShapeDtypeStruct((S, G * GROUP * HEAD_DIM), F32),
        scratch_shapes=[pltpu.VMEM((1, QW), F32), pltpu.VMEM((1, QW), F32),
                        pltpu.VMEM((HEAD_DIM, QW), F32)]
        + [pltpu.VMEM((FAR_BLOCKS * QB, QW), F32)] * 2
        + [pltpu.VMEM((FAR_BLOCKS * QB, QW), BF16)] * 2
        + [pltpu.VMEM((1, QW), F32)] * 2,
        compiler_params=_cparams(("arbitrary", "arbitrary")),
        name="selected_attention_combine",
    )(proj, proj, proj, tabs, c31, sel_t, gates_s, ocmp_t, owin_t)


def _outproj_kernel(a_ref, w_ref, x_ref, o_ref, w_s):
    @pl.when(pl.program_id(1) == 0)
    def _():
        w_s[...] = w_ref[...].astype(BF16)

    o_ref[...] = x_ref[...] + jnp.dot(a_ref[...], w_s[...], preferred_element_type=F32)


def _outproj(a, w, x, tm=1024, tn=512):
    S, D = a.shape
    N = w.shape[1]
    tm = min(tm, S)
    return pl.pallas_call(
        _outproj_kernel,
        grid=(N // tn, S // tm),
        in_specs=[pl.BlockSpec((tm, D), lambda j, i: (i, 0)),
                  pl.BlockSpec((D, tn), lambda j, i: (0, j)),
                  pl.BlockSpec((tm, tn), lambda j, i: (i, j))],
        out_specs=pl.BlockSpec((tm, tn), lambda j, i: (i, j)),
        out_shape=jax.ShapeDtypeStruct((S, N), F32),
        scratch_shapes=[pltpu.VMEM((D, tn), BF16)],
        compiler_params=_cparams(("arbitrary", "arbitrary")),
        name="outproj_residual",
    )(a, w, x)


HALO = 16
FFN_ROW_CHUNKS = 2


def _ffn_a_kernel(halo_ref, a_ref, wg_ref, wu_ref, cw_ref, cb_ref, o_ref, g_s):
    i = pl.program_id(0)
    tm = a_ref.shape[0]
    halo = halo_ref[...]
    halo = jnp.where(i > 0, halo, jnp.zeros_like(halo))
    wg = wg_ref[...].astype(BF16)
    wu = wu_ref[...].astype(BF16)
    hm = tm // FFN_ROW_CHUNKS
    for h in range(FFN_ROW_CHUNKS):
        a = a_ref[h * hm:(h + 1) * hm, :]
        g = g_s.at[h]
        if h == 0:
            g[...] = jnp.dot(jnp.concatenate([halo, a], axis=0), wg, preferred_element_type=F32)
        else:
            g[0:HALO, :] = g_s[h - 1, hm:hm + HALO, :]
            g[HALO:, :] = jnp.dot(a, wg, preferred_element_type=F32)
        up = jnp.dot(a, wu, preferred_element_type=F32)
        gext = g[...]
        g1 = pltpu.roll(gext, 1, 0)
        g2 = pltpu.roll(g1, 1, 0)
        y = cb_ref[...] + g2[HALO:] * cw_ref[0:1, :]
        y = y + g1[HALO:] * cw_ref[1:2, :]
        y = y + gext[HALO:] * cw_ref[2:3, :]
        act = y * (1.0 / (1.0 + jnp.exp(-y)))
        o_ref[h * hm:(h + 1) * hm, :] = (act * up).astype(o_ref.dtype)


def _ffn_a(hf, wg, wu, cw, cb, tm=1024, tf=256):
    S, D = hf.shape
    Fp = wg.shape[1]
    assert Fp % tf == 0
    tm = min(tm, S)
    hb = tm // HALO
    return pl.pallas_call(
        _ffn_a_kernel,
        grid=(S // tm, Fp // tf),
        in_specs=[pl.BlockSpec((HALO, D), lambda i, f: (jnp.maximum(i * hb - 1, 0), 0)),
                  pl.BlockSpec((tm, D), lambda i, f: (i, 0)),
                  pl.BlockSpec((D, tf), lambda i, f: (0, f)),
                  pl.BlockSpec((D, tf), lambda i, f: (0, f)),
                  pl.BlockSpec((cw.shape[0], tf), lambda i, f: (0, f)),
                  pl.BlockSpec((1, tf), lambda i, f: (0, f))],
        out_specs=pl.BlockSpec((tm, tf), lambda i, f: (i, f)),
        out_shape=jax.ShapeDtypeStruct((S, Fp), BF16),
        scratch_shapes=[pltpu.VMEM((FFN_ROW_CHUNKS, HALO + tm // FFN_ROW_CHUNKS, tf), F32)],
        compiler_params=_cparams(("arbitrary", "arbitrary")),
        name="ffn_gate_up",
    )(hf, hf, wg, wu, cw, cb)


def _ffn_b_kernel(a_ref, w_ref, x_ref, o_ref):
    o_ref[...] = x_ref[...] + jnp.dot(a_ref[...], w_ref[...], preferred_element_type=F32)


def _ffn_b(h, w, x, tm=512, tn=512):
    S, F = h.shape
    N = w.shape[1]
    tm = min(tm, S)
    return pl.pallas_call(
        _ffn_b_kernel,
        grid=(S // tm, N // tn),
        in_specs=[pl.BlockSpec((tm, F), lambda i, j: (i, 0)),
                  pl.BlockSpec((F, tn), lambda i, j: (0, j)),
                  pl.BlockSpec((tm, tn), lambda i, j: (i, j))],
        out_specs=pl.BlockSpec((tm, tn), lambda i, j: (i, j)),
        out_shape=jax.ShapeDtypeStruct((S, N), F32),
        compiler_params=_cparams(("arbitrary", "arbitrary")),
        name="ffn_down_residual",
    )(h, w, x)


def _layer(x, rel_bias, norm_mix_g, w_in, a_q_norm_g, a_k_norm_g, a_sinks, b_q_norm_g, b_k_norm_g,
           cmp_pos_emb, cmp_w1, cmp_b1, cmp_w2, cmp_b2, out_norm_g, w_out, norm_ffn_g, w_gate,
           w_up, conv_w, conv_b, w_down):
    S, D = x.shape
    aw = A_Q_HEADS * HEAD_DIM
    akv = A_KV_HEADS * HEAD_DIM
    bw = B_Q_HEADS * HEAD_DIM
    bkv = B_KV_HEADS * HEAD_DIM
    sizes = [aw, akv, akv, bw] + [bkv] * 6 + [N_BRANCH * B_Q_HEADS]
    offs = np.concatenate([[0], np.cumsum(sizes)]).tolist()
    wt = jnp.swapaxes(w_in, 0, 1)
    tn = 512
    order_a = [0, 1, 2, 3, 6, 7, 8, 9]
    n_gate = sizes[10]
    wg_t = jnp.pad(wt[offs[10]:offs[10] + n_gate], ((0, QB - n_gate), (0, 0)))
    one = jnp.ones((HEAD_DIM,), F32)
    seg_gain = {0: a_q_norm_g, 1: a_k_norm_g * KEY_SCALE, 3: b_q_norm_g,
                6: b_k_norm_g[1] * KEY_SCALE, 8: b_k_norm_g[2] * KEY_SCALE}
    kinds, tiles, gains, col = [], [], [], {}
    c0 = 0
    for k in order_a:
        col[k] = c0
        for t in range(sizes[k] // tn):
            kinds.append(1 if k in seg_gain else 0)
            tiles.append(offs[k] // tn + t)
            gains.append(seg_gain.get(k, one))
        c0 += sizes[k]
    kinds = jnp.asarray(kinds, jnp.int32)
    tiles = jnp.asarray(tiles, jnp.int32)
    gains = jnp.stack(gains).astype(F32).reshape(len(gains), 1, HEAD_DIM)

    hn = _rmsnorm([x], norm_mix_g)
    proj = _inproj_a(hn, wt, kinds, tiles, gains, tn=tn)
    kv32, gates = _inproj_b(hn, wt, offs[4] // tn, 2 * bkv // tn, wg_t)

    tab_a = rel_bias[:, :A_Q_HEADS]
    tab_b = rel_bias[:, A_Q_HEADS:]
    near_a = _stack_heads(_near_bias_tables(tab_a), A_KV_HEADS)
    near_b = _stack_heads(_near_bias_tables(tab_b), B_KV_HEADS)
    head_row = lambda v, G: _stack_heads(jnp.broadcast_to(v.astype(F32)[:, None, None],
                                                          (v.shape[0], 1, QB)), G)
    c31_a = head_row(tab_a[N_BUCKETS - 1], A_KV_HEADS)
    c31_b = head_row(tab_b[N_BUCKETS - 1], B_KV_HEADS)

    qcol = lambda k: col[k] // (GROUP * HEAD_DIM)
    hcol = lambda k: col[k] // HEAD_DIM
    o_a = _band_attention(proj, A_KV_HEADS, qcol(0), hcol(1), hcol(2), near_a, c31_a,
                          head_row(a_sinks, A_KV_HEADS), A_WINDOW, transposed_out=False)

    NC = S // CMP_STRIDE
    n_cmp = (S - CMP_BLOCK) // CMP_STRIDE + 1
    w1 = cmp_w1.reshape(2, CMP_BLOCK, HEAD_DIM, cmp_w1.shape[-1]).astype(BF16)
    kc, vct = _compress(kv32, B_KV_HEADS, cmp_pos_emb.astype(F32), w1, cmp_b1[:, None, :].astype(F32),
                        cmp_w2.astype(BF16), cmp_b2[:, None, :].astype(F32),
                        (b_k_norm_g[0] * KEY_SCALE).reshape(1, HEAD_DIM).astype(F32))

    NB = S // SLC_BLOCK
    ii = np.arange(NC)[None, :]
    jj = np.arange(NB)[:, None]
    ovl_t = ((ii * CMP_STRIDE <= jj * SLC_BLOCK + SLC_BLOCK - 1)
             & (ii * CMP_STRIDE + CMP_BLOCK - 1 >= jj * SLC_BLOCK) & (ii < n_cmp))
    ovl_t = jnp.asarray(ovl_t.astype(np.float32), BF16)
    ocmp_t, sel_t = _cmp_select(proj, qcol(3), kc, vct,
                                _stack_heads(_cmp_bias_table(tab_b), B_KV_HEADS), c31_b, ovl_t, n_cmp)

    owin_t = _band_attention(proj, B_KV_HEADS, qcol(3), hcol(8), hcol(9), near_b, c31_b, None,
                             B_WINDOW, transposed_out=True)
    gates_s = gates[:, :n_gate].reshape(S // QB, QB, B_KV_HEADS, GROUP, N_BRANCH)
    gates_s = gates_s.transpose(2, 0, 4, 3, 1).reshape(B_KV_HEADS, S // QB, N_BRANCH, QW)
    o_b = _sel_attention(proj, qcol(3), hcol(6), hcol(7), near_b, c31_b, sel_t, gates_s,
                         ocmp_t, owin_t)

    on = _rmsnorm([o_a, o_b], out_norm_g)
    x2 = _outproj(on, w_out, x)

    hf = _rmsnorm([x2], norm_ffn_g)
    hmid = _ffn_a(hf, w_gate, w_up, conv_w.astype(F32), conv_b[None, :].astype(F32))
    return _ffn_b(hmid, w_down.astype(BF16), x2)


def kernel(x, rel_bias, norm_mix_g, w_in, a_q_norm_g, a_k_norm_g, a_sinks, b_q_norm_g, b_k_norm_g,
           cmp_pos_emb, cmp_w1, cmp_b1, cmp_w2, cmp_b2, out_norm_g, w_out, norm_ffn_g, w_gate, w_up,
           conv_w, conv_b, w_down):
    depth = w_in.shape[0]
    batch = x.shape[0]
    outs = []
    for b in range(batch):
        h = x[b]
        for l in range(depth):
            h = _layer(h, rel_bias, norm_mix_g[l], w_in[l], a_q_norm_g[l], a_k_norm_g[l], a_sinks[l],
                       b_q_norm_g[l], b_k_norm_g[l], cmp_pos_emb[l], cmp_w1[l], cmp_b1[l], cmp_w2[l],
                       cmp_b2[l], out_norm_g[l], w_out[l], norm_ffn_g[l], w_gate[l], w_up[l],
                       conv_w[l], conv_b[l], w_down[l])
        outs.append(h)
    return jnp.stack(outs)
```

```python
import functools
import math

import numpy as np
import jax
import jax.numpy as jnp
from jax import lax
from jax.experimental import pallas as pl
from jax.experimental.pallas import tpu as pltpu

F32 = jnp.float32
BF16 = jnp.bfloat16

HEAD_DIM = 128
A_Q_HEADS = 16
A_KV_HEADS = 4
B_Q_HEADS = 16
B_KV_HEADS = 4
GROUP = 4
A_WINDOW = 128
B_WINDOW = 512
CMP_BLOCK = 32
CMP_STRIDE = 16
SLC_BLOCK = 64
SLC_TOPK = 16
N_BRANCH = 3
N_BUCKETS = 32
MAX_DISTANCE = 128
EPS = 1e-6
NEG = -1e30
FORCE_SCORE = 1e6
SCALE = HEAD_DIM ** -0.5

QB = 128
CMP_WIN = 24
CMP_TAB = 40
VMEM_LIMIT = 56 * 1024 * 1024


def _cparams(sem, **kw):
    return pltpu.CompilerParams(dimension_semantics=sem, vmem_limit_bytes=VMEM_LIMIT, **kw)


def _t5_bucket_np(dist):
    n = np.maximum(dist, 0)
    max_exact = N_BUCKETS // 2
    nf = np.maximum(n, 1).astype(np.float32)
    large = max_exact + (np.log(nf / max_exact) / math.log(MAX_DISTANCE / max_exact)
                         * (N_BUCKETS - max_exact)).astype(np.int32)
    large = np.minimum(large, N_BUCKETS - 1)
    return np.where(n < max_exact, n, large).astype(np.int32)


def _bias_from_buckets(tab, idx):
    onehot = (idx[None] == np.arange(N_BUCKETS).reshape((-1,) + (1,) * idx.ndim)).astype(np.float32)
    return jnp.einsum('bh,b...->h...', tab.astype(F32), jnp.asarray(onehot),
                      precision=lax.Precision.HIGHEST)


def _near_bias_tables(tab):
    kk = np.arange(QB)[:, None]
    qq = np.arange(QB)[None, :]
    return _bias_from_buckets(tab, np.stack([_t5_bucket_np(qq - kk), _t5_bucket_np(QB + qq - kk)]))


def _cmp_bias_table(tab):
    npr = np.arange(CMP_TAB)[:, None] - 16
    qq = np.arange(QB)[None, :]
    return _bias_from_buckets(tab, _t5_bucket_np(qq - CMP_STRIDE * npr - (CMP_BLOCK - 1)))


def _rmsnorm_kernel(*refs, n_in):
    x_refs, g_ref, o_ref = refs[:n_in], refs[n_in], refs[n_in + 1]
    off = 0
    for x_ref in x_refs:
        x = x_ref[...]
        w = x.shape[-1]
        y = x * lax.rsqrt(jnp.mean(x * x, axis=-1, keepdims=True) + EPS)
        o_ref[:, off:off + w] = (y * g_ref[:, off:off + w]).astype(o_ref.dtype)
        off += w


def _rmsnorm(xs, gain, tr=256):
    S = xs[0].shape[0]
    widths = [x.shape[1] for x in xs]
    n = sum(widths)
    tr = min(tr, S)
    return pl.pallas_call(
        functools.partial(_rmsnorm_kernel, n_in=len(xs)),
        grid=(S // tr,),
        in_specs=[pl.BlockSpec((tr, w), lambda i: (i, 0)) for w in widths]
        + [pl.BlockSpec((1, n), lambda i: (0, 0))],
        out_specs=pl.BlockSpec((tr, n), lambda i: (i, 0)),
        out_shape=jax.ShapeDtypeStruct((S, n), BF16),
        compiler_params=_cparams(("arbitrary",)),
        name="rmsnorm",
    )(*xs, gain.reshape(1, n).astype(F32))


NORM_ROW_CHUNKS = 2


def _dot_wt(a, wt):
    return lax.dot_general(a, wt.astype(BF16), (((1,), (1,)), ((), ())), preferred_element_type=F32)


def _inproj_a_kernel(kind_ref, tile_ref, a_ref, w_ref, g_ref, o_ref, w_s):
    j = pl.program_id(0)

    @pl.when(pl.program_id(1) == 0)
    def _():
        w_s[...] = w_ref[...].astype(BF16)

    @pl.when(kind_ref[j] == 0)
    def _():
        o_ref[...] = _dot_wt(a_ref[...], w_s[...]).astype(o_ref.dtype)

    @pl.when(kind_ref[j] == 1)
    def _():
        g = g_ref[0]
        hm = a_ref.shape[0] // NORM_ROW_CHUNKS
        for c in range(NORM_ROW_CHUNKS):
            rows = slice(c * hm, (c + 1) * hm)
            acc = _dot_wt(a_ref[rows, :], w_s[...])
            for h in range(acc.shape[1] // HEAD_DIM):
                sl = acc[:, h * HEAD_DIM:(h + 1) * HEAD_DIM]
                y = sl * lax.rsqrt(jnp.mean(sl * sl, axis=-1, keepdims=True) + EPS)
                o_ref[rows, h * HEAD_DIM:(h + 1) * HEAD_DIM] = (y * g).astype(o_ref.dtype)


def _inproj_a(hn, wt, kinds, tiles, gains, tm=1024, tn=512):
    S, D = hn.shape
    n_tiles = kinds.shape[0]
    tm = min(tm, S)
    grid_spec = pltpu.PrefetchScalarGridSpec(
        num_scalar_prefetch=2,
        grid=(n_tiles, S // tm),
        in_specs=[pl.BlockSpec((tm, D), lambda j, i, k, t: (i, 0)),
                  pl.BlockSpec((tn, D), lambda j, i, k, t: (t[j], 0)),
                  pl.BlockSpec((1, 1, HEAD_DIM), lambda j, i, k, t: (j, 0, 0))],
        out_specs=pl.BlockSpec((tm, tn), lambda j, i, k, t: (i, j)),
        scratch_shapes=[pltpu.VMEM((tn, D), BF16)],
    )
    return pl.pallas_call(
        _inproj_a_kernel,
        grid_spec=grid_spec,
        out_shape=jax.ShapeDtypeStruct((S, n_tiles * tn), BF16),
        compiler_params=_cparams(("arbitrary", "arbitrary")),
        name="inproj_heads",
    )(kinds, tiles, hn, wt, gains)


def _inproj_b_kernel(a_ref, w_ref, wg_ref, o_ref, gate_ref):
    a = a_ref[...]
    o_ref[...] = _dot_wt(a, w_ref[...])

    @pl.when(pl.program_id(1) == 0)
    def _():
        gate_ref[...] = 1.0 / (1.0 + jnp.exp(-_dot_wt(a, wg_ref[...])))


def _inproj_b(hn, wt, tile0, n_tiles, wg_t, tm=1024, tn=512):
    S, D = hn.shape
    tm = min(tm, S)
    ng = wg_t.shape[0]
    return pl.pallas_call(
        _inproj_b_kernel,
        grid=(S // tm, n_tiles),
        in_specs=[pl.BlockSpec((tm, D), lambda i, j: (i, 0)),
                  pl.BlockSpec((tn, D), lambda i, j: (tile0 + j, 0)),
                  pl.BlockSpec((ng, D), lambda i, j: (0, 0))],
        out_specs=[pl.BlockSpec((tm, tn), lambda i, j: (i, j)),
                   pl.BlockSpec((tm, ng), lambda i, j: (i, 0))],
        out_shape=[jax.ShapeDtypeStruct((S, n_tiles * tn), F32),
                   jax.ShapeDtypeStruct((S, ng), F32)],
        compiler_params=_cparams(("arbitrary", "arbitrary")),
        name="inproj_cmp_gates",
    )(hn, wt, wg_t)


def _gelu_tanh(x):
    return 0.5 * x * (1.0 + jnp.tanh(math.sqrt(2.0 / math.pi) * (x + 0.044715 * (x * x * x))))


def _compress_kernel(tk_ref, tv_ref, pos_ref, w1_ref, b1_ref, w2_ref, b2_ref, gk_ref,
                     kc_ref, vct_ref):
    nc = kc_ref.shape[1]

    def mlp(t_ref, kv):
        a0 = jnp.zeros((nc, w1_ref.shape[-1]), F32)
        a1 = jnp.zeros((nc, w1_ref.shape[-1]), F32)
        for b in range(CMP_STRIDE):
            t = t_ref[pl.ds(b, nc, stride=CMP_STRIDE), :]
            lo, hi = b, CMP_STRIDE + b
            a0 = a0 + jnp.dot((t + pos_ref[kv, lo:lo + 1, :]).astype(BF16), w1_ref[kv, lo],
                              preferred_element_type=F32)
            a1 = a1 + jnp.dot((t + pos_ref[kv, hi:hi + 1, :]).astype(BF16), w1_ref[kv, hi],
                              preferred_element_type=F32)
        h = a0 + pltpu.roll(a1, nc - 1, 0) + b1_ref[kv]
        h = _gelu_tanh(h)
        return jnp.dot(h.astype(BF16), w2_ref[kv], preferred_element_type=F32) + b2_ref[kv]

    ck = mlp(tk_ref, 0)
    ck = ck * lax.rsqrt(jnp.mean(ck * ck, axis=-1, keepdims=True) + EPS) * gk_ref[...]
    kc_ref[0] = ck.astype(kc_ref.dtype)
    cv = mlp(tv_ref, 1)
    vct_ref[0] = cv.T.astype(vct_ref.dtype)


def _compress(kv32, G, pos, w1, b1, w2, b2, gk):
    S = kv32.shape[0]
    NC = S // CMP_STRIDE
    hid = w1.shape[-1]
    return pl.pallas_call(
        _compress_kernel,
        grid=(G,),
        in_specs=[pl.BlockSpec((S, HEAD_DIM), lambda g: (0, g)),
                  pl.BlockSpec((S, HEAD_DIM), lambda g: (0, G + g)),
                  pl.BlockSpec((2, CMP_BLOCK, HEAD_DIM), lambda g: (0, 0, 0)),
                  pl.BlockSpec((2, CMP_BLOCK, HEAD_DIM, hid), lambda g: (0, 0, 0, 0)),
                  pl.BlockSpec((2, 1, hid), lambda g: (0, 0, 0)),
                  pl.BlockSpec((2, hid, HEAD_DIM), lambda g: (0, 0, 0)),
                  pl.BlockSpec((2, 1, HEAD_DIM), lambda g: (0, 0, 0)),
                  pl.BlockSpec((1, HEAD_DIM), lambda g: (0, 0))],
        out_specs=[pl.BlockSpec((1, NC, HEAD_DIM), lambda g: (g, 0, 0)),
                   pl.BlockSpec((1, HEAD_DIM, NC), lambda g: (g, 0, 0))],
        out_shape=[jax.ShapeDtypeStruct((G, NC, HEAD_DIM), BF16),
                   jax.ShapeDtypeStruct((G, HEAD_DIM, NC), BF16)],
        compiler_params=_cparams(("arbitrary",)),
        name="compress",
    )(kv32, kv32, pos, w1, b1, w2, b2, gk)


QW = GROUP * QB


def _stacked_queries(q_ref):
    return jnp.concatenate([q_ref[:, r * HEAD_DIM:(r + 1) * HEAD_DIM] for r in range(GROUP)], axis=0)


LOG2E = math.log2(math.e)
KEY_SCALE = SCALE * LOG2E
QSUB = 4


def _scores(k_t, qs):
    return lax.dot_general(k_t, qs, (((1,), (1,)), ((), ())), preferred_element_type=F32)


def _rel_bias(tab, c31):
    return (tab - c31) * LOG2E


def _init_state(m_ref, l_ref, acc_ref):
    m_ref[...] = jnp.full(m_ref.shape, NEG, F32)
    l_ref[...] = jnp.zeros(l_ref.shape, F32)
    acc_ref[...] = jnp.zeros(acc_ref.shape, F32)


def _online_update(m_ref, l_ref, acc_ref, lanes, t, v_t):
    m_old = m_ref[:, lanes]
    m_new = jnp.maximum(m_old, jnp.max(t, axis=0, keepdims=True))
    alpha = jnp.exp2(m_old - m_new)
    p = jnp.exp2(t - m_new)
    l_ref[:, lanes] = alpha * l_ref[:, lanes] + jnp.sum(p, axis=0, keepdims=True)
    pv = lax.dot_general(v_t, p.astype(BF16), (((0,), (0,)), ((), ())), preferred_element_type=F32)
    acc_ref[:, lanes] = acc_ref[:, lanes] * alpha + pv
    m_ref[:, lanes] = m_new


def _key_query_iotas():
    kk = lax.broadcasted_iota(jnp.int32, (QB, QW), 0)
    qq = lax.broadcasted_iota(jnp.int32, (QB, QW), 1) & (QB - 1)
    return kk, qq


def _load_kv(k_ref, v_ref, kb0, n):
    rows = pl.ds(pl.multiple_of(kb0 * QB, QB), n * QB)
    return k_ref[rows, :], v_ref[rows, :]


def _store_heads(o_ref, o_t):
    for r in range(GROUP):
        o_ref[:, r * HEAD_DIM:(r + 1) * HEAD_DIM] = o_t[:, r * QB:(r + 1) * QB].T


def _stack_heads(a, G):
    a = a.reshape((G, GROUP) + a.shape[1:])
    a = jnp.moveaxis(a, 1, -2)
    return a.reshape(a.shape[:-2] + (QW,))


def _band_kernel(q_ref, k_ref, v_ref, tab_ref, c31_ref, sink_ref, o_ref, m_ref, l_ref, acc_ref,
                 *, nback, use_sinks, transposed_out):
    c0 = pl.program_id(1) * QSUB
    _init_state(m_ref, l_ref, acc_ref)
    kk, qq = _key_query_iotas()
    c31 = c31_ref[0]

    def masked_scores(sub, kb0, deltas):
        k_t, v_t = _load_kv(k_ref, v_ref, kb0, len(deltas))
        s = _scores(k_t, _stacked_queries(q_ref.at[sub * QB:(sub + 1) * QB]))
        parts = []
        for t, delta in enumerate(deltas):
            sb = s[t * QB:(t + 1) * QB]
            if delta <= 1:
                sb = sb + _rel_bias(tab_ref[0, delta], c31)
            if delta == 0:
                sb = jnp.where(qq >= kk, sb, NEG)
            elif delta == nback:
                sb = jnp.where(qq < kk, sb, NEG)
            parts.append(sb)
        return (parts[0] if len(parts) == 1 else jnp.concatenate(parts, axis=0)), v_t

    def update(sub, s, v_t):
        _online_update(m_ref.at[sub], l_ref.at[sub], acc_ref.at[sub], slice(None), s, v_t)

    @pl.when(c0 >= nback)
    def _():
        band = list(range(nback, -1, -1))
        tiles = [masked_scores(sub, c0 + sub - nback, band) for sub in range(QSUB)]
        for sub, (s, v_t) in enumerate(tiles):
            update(sub, s, v_t)

    @pl.when(c0 < nback)
    def _():
        for sub in range(QSUB):
            for delta in range(nback, -1, -1):
                @pl.when(c0 + sub - delta >= 0)
                def _(sub=sub, delta=delta):
                    update(sub, *masked_scores(sub, c0 + sub - delta, [delta]))

    for sub in range(QSUB):
        l = l_ref[sub]
        acc = acc_ref[sub]
        if use_sinks:
            m = m_ref[sub]
            sk = _rel_bias(sink_ref[0], c31)
            m_f = jnp.maximum(m, sk)
            a = jnp.exp2(m - m_f)
            l = l * a + jnp.exp2(sk - m_f)
            acc = acc * a
        o_t = acc * (1.0 / l)
        if transposed_out:
            o_ref[0, sub] = o_t
        else:
            _store_heads(o_ref.at[sub * QB:(sub + 1) * QB], o_t)


def _qkv_specs(S, q_col, k_col, v_col, q_rows=QB):
    return [pl.BlockSpec((q_rows, GROUP * HEAD_DIM), lambda g, c: (c, q_col + g)),
            pl.BlockSpec((S, HEAD_DIM), lambda g, c: (0, k_col + g)),
            pl.BlockSpec((S, HEAD_DIM), lambda g, c: (0, v_col + g))]


def _band_attention(proj, G, q_col, k_col, v_col, tabs, c31, sinks, window, transposed_out):
    S = proj.shape[0]
    nkb = S // QB
    nback = -(-(window - 1) // QB)
    use_sinks = sinks is not None
    if sinks is None:
        sinks = jnp.zeros_like(c31)
    if transposed_out:
        out_shape = jax.ShapeDtypeStruct((G, nkb, HEAD_DIM, QW), F32)
        out_spec = pl.BlockSpec((1, QSUB, HEAD_DIM, QW), lambda g, c: (g, c, 0, 0))
    else:
        out_shape = jax.ShapeDtypeStruct((S, G * GROUP * HEAD_DIM), F32)
        out_spec = pl.BlockSpec((QSUB * QB, GROUP * HEAD_DIM), lambda g, c: (c, g))
    row = pl.BlockSpec((1, 1, QW), lambda g, c: (g, 0, 0))
    return pl.pallas_call(
        functools.partial(_band_kernel, nback=nback, use_sinks=use_sinks,
                          transposed_out=transposed_out),
        grid=(G, nkb // QSUB),
        in_specs=_qkv_specs(S, q_col, k_col, v_col, QSUB * QB)
        + [pl.BlockSpec((1, 2, QB, QW), lambda g, c: (g, 0, 0, 0)), row, row],
        out_specs=out_spec,
        out_shape=out_shape,
        scratch_shapes=[pltpu.VMEM((QSUB, 1, QW), F32), pltpu.VMEM((QSUB, 1, QW), F32),
                        pltpu.VMEM((QSUB, HEAD_DIM, QW), F32)],
        compiler_params=_cparams(("arbitrary", "arbitrary")),
        name="band_attention_w%d" % window,
    )(proj, proj, proj, tabs, c31, sinks)


def _cmp_select_kernel(q_ref, kc_ref, vct_ref, tab_ref, c31_ref, ovl_ref, ot_ref, sel_ref, s_ref,
                       *, n_valid, topk):
    c0 = pl.program_id(1) * QSUB
    nc = kc_ref.shape[1]
    nb = ovl_ref.shape[0]
    rown = lax.broadcasted_iota(jnp.int32, (nc, QW), 0)
    wrow = lax.broadcasted_iota(jnp.int32, (CMP_WIN, QW), 0)
    wq = lax.broadcasted_iota(jnp.int32, (CMP_WIN, QW), 1) & (QB - 1)
    lane_q = lax.broadcasted_iota(jnp.int32, (1, QW), 1) & (QB - 1)

    for sub in range(QSUB):
        c = c0 + sub
        qs = _stacked_queries(q_ref.at[sub * QB:(sub + 1) * QB])
        w0 = pl.multiple_of(jnp.maximum(8 * c - 16, 0), 8)
        toff = pl.multiple_of(w0 - 8 * c + 16, 8)
        n_abs = w0 + wrow
        dist = (c * QB + wq) - (n_abs * CMP_STRIDE + (CMP_BLOCK - 1))
        valid_w = (dist >= 0) & (n_abs < n_valid)
        s_ref[sub] = jnp.where(rown < w0, _scores(kc_ref[0], qs), NEG)
        bias_w = _rel_bias(tab_ref[0, pl.ds(toff, CMP_WIN), :], c31_ref[0])
        s_w = _scores(kc_ref[0, pl.ds(w0, CMP_WIN), :], qs) + bias_w
        s_ref[sub, pl.ds(w0, CMP_WIN), :] = jnp.where(valid_w, s_w, NEG)

    p_groups = []
    for sub in range(QSUB):
        has_any = ((c0 + sub) * QB + lane_q >= CMP_BLOCK - 1).astype(F32)
        s = s_ref[sub]
        m = jnp.max(s, axis=0, keepdims=True)
        e = jnp.exp2(s - m)
        p = e * (has_any / jnp.sum(e, axis=0, keepdims=True))
        ot_ref[0, sub] = jnp.dot(vct_ref[0], p.astype(BF16), preferred_element_type=F32)
        p_grp = p[:, 0:QB]
        for r in range(1, GROUP):
            p_grp = p_grp + p[:, r * QB:(r + 1) * QB]
        p_groups.append(p_grp.astype(BF16))

    nq = QSUB * QB
    scores = jnp.dot(ovl_ref[...], jnp.concatenate(p_groups, axis=1), preferred_element_type=F32)
    blk = lax.broadcasted_iota(jnp.int32, (nb, nq), 0)
    cur = lax.shift_right_logical(c0 * QB + lax.broadcasted_iota(jnp.int32, (nb, nq), 1),
                                  int(math.log2(SLC_BLOCK)))
    forced = (blk == 0) | (blk == cur) | (blk == cur - 1)
    work = jnp.where(forced, FORCE_SCORE, jnp.where(blk <= cur, scores, -1.0))
    blk_f = blk.astype(F32)
    sel = jnp.zeros((nb, nq), F32)
    for _ in range(topk):
        mx = jnp.max(work, axis=0, keepdims=True)
        first = jnp.min(jnp.where(work == mx, blk_f, float(nb)), axis=0, keepdims=True)
        hit = blk_f == first
        sel = jnp.where(hit, 1.0, sel)
        work = jnp.where(hit, -3e38, work)
    sel_ref[0] = sel


def _cmp_select(proj, q_col, kc, vct, tab, c31, ovl_t, n_valid):
    nkb = proj.shape[0] // QB
    G, NC = kc.shape[0], kc.shape[1]
    NB = ovl_t.shape[0]
    return pl.pallas_call(
        functools.partial(_cmp_select_kernel, n_valid=n_valid, topk=min(SLC_TOPK, NB)),
        grid=(G, nkb // QSUB),
        in_specs=[pl.BlockSpec((QSUB * QB, GROUP * HEAD_DIM), lambda g, c: (c, q_col + g)),
                  pl.BlockSpec((1, NC, HEAD_DIM), lambda g, c: (g, 0, 0)),
                  pl.BlockSpec((1, HEAD_DIM, NC), lambda g, c: (g, 0, 0)),
                  pl.BlockSpec((1, CMP_TAB, QW), lambda g, c: (g, 0, 0)),
                  pl.BlockSpec((1, 1, QW), lambda g, c: (g, 0, 0)),
                  pl.BlockSpec((NB, NC), lambda g, c: (0, 0))],
        out_specs=[pl.BlockSpec((1, QSUB, HEAD_DIM, QW), lambda g, c: (g, c, 0, 0)),
                   pl.BlockSpec((1, NB, QSUB * QB), lambda g, c: (g, 0, c))],
        out_shape=[jax.ShapeDtypeStruct((G, nkb, HEAD_DIM, QW), F32),
                   jax.ShapeDtypeStruct((G, NB, nkb * QB), F32)],
        scratch_shapes=[pltpu.VMEM((QSUB, NC, QW), F32)],
        compiler_params=_cparams(("arbitrary", "arbitrary")),
        name="cmp_attention_select",
    )(proj, kc, vct, tab, c31, ovl_t)


FAR_BLOCKS = 4


def _sel_block(c, q_ref, sel_row, gate_ref, ocmp_ref, owin_ref, o_ref, k_ref, v_ref, tab_ref, c31,
               m_ref, l_ref, acc_ref, s_a, s_b, p_a, p_b, al_a, al_b):
    _init_state(m_ref, l_ref, acc_ref)
    kk, qq = _key_query_iotas()
    qs = _stacked_queries(q_ref)
    spb = QB // SLC_BLOCK
    tk = FAR_BLOCKS * QB
    last_tile = k_ref.shape[0] // tk - 1
    n_far = jnp.maximum(c - 1, 0)
    n_tiles = (n_far + FAR_BLOCKS - 1) // FAR_BLOCKS

    def head_row(r, row_limit=None, r_load=None):
        row = sel_row(r if r_load is None else r_load)
        if row_limit is not None:
            row = jnp.where(r < row_limit, row, 0.0)
        return jnp.concatenate([row] * GROUP, axis=1)

    def tile_rows(i):
        return pl.ds(pl.multiple_of(jnp.minimum(i, last_tile) * tk, tk), tk)

    def far_scores(i, s_buf):
        s = _scores(k_ref[tile_rows(i), :], qs)
        r0 = i * (FAR_BLOCKS * spb)
        r0_load = jnp.minimum(i, last_tile) * (FAR_BLOCKS * spb)
        for t in range(FAR_BLOCKS * spb):
            row = head_row(r0 + t, n_far * spb, r0_load + t)
            blk = slice(t * SLC_BLOCK, (t + 1) * SLC_BLOCK)
            s_buf[blk, :] = jnp.where(row > 0.5, s[blk], NEG)

    def far_softmax(s_buf, p_buf, al_buf):
        blocks = [slice(t * SLC_BLOCK, (t + 1) * SLC_BLOCK) for t in range(FAR_BLOCKS * spb)]
        sub8 = lambda a: a.reshape(SLC_BLOCK // 8, 8, QW)
        m_part = jnp.full((8, QW), NEG, F32)
        for blk in blocks:
            m_part = jnp.maximum(m_part, jnp.max(sub8(s_buf[blk, :]), axis=0))
        m_old = m_ref[...]
        m_new = jnp.maximum(m_old, jnp.max(m_part, axis=0, keepdims=True))
        alpha = jnp.exp2(m_old - m_new)
        l_part = jnp.zeros((8, QW), F32)
        for blk in blocks:
            p = jnp.exp2(s_buf[blk, :] - m_new)
            l_part = l_part + jnp.sum(sub8(p), axis=0)
            p_buf[blk, :] = p.astype(BF16)
        l_ref[...] = alpha * l_ref[...] + jnp.sum(l_part, axis=0, keepdims=True)
        m_ref[...] = m_new
        al_buf[...] = alpha

    def far_values(i, p_buf, al_buf):
        pv = lax.dot_general(v_ref[tile_rows(i), :], p_buf[...], (((0,), (0,)), ((), ())),
                             preferred_element_type=F32)
        acc_ref[...] = acc_ref[...] * al_buf[...] + pv

    far_scores(0, s_a)
    far_scores(1, s_b)
    for p_buf, al_buf in ((p_a, al_a), (p_b, al_b)):
        p_buf[...] = jnp.zeros(p_buf.shape, BF16)
        al_buf[...] = jnp.ones(al_buf.shape, F32)

    def pair_body(j, carry):
        i = 2 * j
        far_values(jnp.maximum(i - 2, 0), p_a, al_a)
        far_values(jnp.maximum(i - 1, 0), p_b, al_b)
        far_softmax(s_a, p_a, al_a)
        far_softmax(s_b, p_b, al_b)
        far_scores(i + 2, s_a)
        far_scores(i + 3, s_b)
        return carry

    n_pairs = (n_tiles + 1) // 2
    lax.fori_loop(0, n_pairs, pair_body, 0)
    far_values(jnp.maximum(2 * n_pairs - 2, 0), p_a, al_a)
    far_values(jnp.maximum(2 * n_pairs - 1, 0), p_b, al_b)

    blocks = [(jnp.maximum(c - 1, 0), 1, c * spb), (c, 0, None)]
    rows = [pl.ds(pl.multiple_of(kb * QB, QB), QB) for kb, _, _ in blocks]
    k_t = jnp.concatenate([k_ref[r, :] for r in rows], axis=0)
    v_t = jnp.concatenate([v_ref[r, :] for r in rows], axis=0)
    s = _scores(k_t, qs)
    parts = []
    for t, (kb, delta, limit) in enumerate(blocks):
        msk = jnp.concatenate(
            [jnp.broadcast_to(head_row(kb * spb + u, limit), (SLC_BLOCK, QW)) for u in range(spb)],
            axis=0) > 0.5
        if delta == 0:
            msk = msk & (qq >= kk)
        sb = s[t * QB:(t + 1) * QB] + _rel_bias(tab_ref[0, delta], c31)
        parts.append(jnp.where(msk, sb, NEG))
    _online_update(m_ref, l_ref, acc_ref, slice(None), jnp.concatenate(parts, axis=0), v_t)

    o_slc = acc_ref[...] * (1.0 / l_ref[...])
    o_t = gate_ref[0:1, :] * ocmp_ref[...] + gate_ref[1:2, :] * o_slc + gate_ref[2:3, :] * owin_ref[...]
    _store_heads(o_ref, o_t)


def _sel_kernel(q_ref, k_ref, v_ref, tab_ref, c31_ref, sel_ref, gate_ref, ocmp_ref, owin_ref, o_ref,
                *scratch):
    _sel_block(pl.program_id(1), q_ref, lambda r: sel_ref[0, pl.ds(r, 1), :],
               gate_ref.at[0, 0], ocmp_ref.at[0, 0], owin_ref.at[0, 0], o_ref,
               k_ref, v_ref, tab_ref, c31_ref[0], *scratch)


def _sel_attention(proj, q_col, k_col, v_col, tabs, c31, sel_t, gates_s, ocmp_t, owin_t):
    S = proj.shape[0]
    G, NB = sel_t.shape[0], sel_t.shape[1]
    nkb = S // QB
    nq = 1
    tile = pl.BlockSpec((1, nq, HEAD_DIM, QW), lambda g, c: (g, c, 0, 0))
    return pl.pallas_call(
        _sel_kernel,
        grid=(G, nkb // nq),
        in_specs=_qkv_specs(S, q_col, k_col, v_col, nq * QB)
        + [pl.BlockSpec((1, 2, QB, QW), lambda g, c: (g, 0, 0, 0)),
           pl.BlockSpec((1, 1, QW), lambda g, c: (g, 0, 0)),
           pl.BlockSpec((1, NB, nq * QB), lambda g, c: (g, 0, c)),
           pl.BlockSpec((1, nq, N_BRANCH, QW), lambda g, c: (g, c, 0, 0)),
           tile, tile],
        out_specs=pl.BlockSpec((nq * QB, GROUP * HEAD_DIM), lambda g, c: (c, g)),
        out_shape=jax.ShapeDtypeStruct((S, G * GROUP * HEAD_DIM), F32),
        scratch_shapes=[pltpu.VMEM((1, QW), F32), pltpu.VMEM((1, QW), F32),
                        pltpu.VMEM((HEAD_DIM, QW), F32)]
        + [pltpu.VMEM((FAR_BLOCKS * QB, QW), F32)] * 2
        + [pltpu.VMEM((FAR_BLOCKS * QB, QW), BF16)] * 2
        + [pltpu.VMEM((1, QW), F32)] * 2,
        compiler_params=_cparams(("arbitrary", "arbitrary")),
        name="selected_attention_combine",
    )(proj, proj, proj, tabs, c31, sel_t, gates_s, ocmp_t, owin_t)


def _outproj_kernel(a_ref, w_ref, x_ref, o_ref, w_s):
    @pl.when(pl.program_id(1) == 0)
    def _():
        w_s[...] = w_ref[...].astype(BF16)

    o_ref[...] = x_ref[...] + jnp.dot(a_ref[...], w_s[...], preferred_element_type=F32)


def _outproj(a, w, x, tm=1024, tn=512):
    S, D = a.shape
    N = w.shape[1]
    tm = min(tm, S)
    return pl.pallas_call(
        _outproj_kernel,
        grid=(N // tn, S // tm),
        in_specs=[pl.BlockSpec((tm, D), lambda j, i: (i, 0)),
                  pl.BlockSpec((D, tn), lambda j, i: (0, j)),
                  pl.BlockSpec((tm, tn), lambda j, i: (i, j))],
        out_specs=pl.BlockSpec((tm, tn), lambda j, i: (i, j)),
        out_shape=jax.ShapeDtypeStruct((S, N), F32),
        scratch_shapes=[pltpu.VMEM((D, tn), BF16)],
        compiler_params=_cparams(("arbitrary", "arbitrary")),
        name="outproj_residual",
    )(a, w, x)


HALO = 16
FFN_ROW_CHUNKS = 2


def _ffn_a_kernel(halo_ref, a_ref, wg_ref, wu_ref, cw_ref, cb_ref, o_ref, g_s):
    i = pl.program_id(0)
    tm = a_ref.shape[0]
    halo = halo_ref[...]
    halo = jnp.where(i > 0, halo, jnp.zeros_like(halo))
    wg = wg_ref[...].astype(BF16)
    wu = wu_ref[...].astype(BF16)
    hm = tm // FFN_ROW_CHUNKS
    for h in range(FFN_ROW_CHUNKS):
        a = a_ref[h * hm:(h + 1) * hm, :]
        g = g_s.at[h]
        if h == 0:
            g[...] = jnp.dot(jnp.concatenate([halo, a], axis=0), wg, preferred_element_type=F32)
        else:
            g[0:HALO, :] = g_s[h - 1, hm:hm + HALO, :]
            g[HALO:, :] = jnp.dot(a, wg, preferred_element_type=F32)
        up = jnp.dot(a, wu, preferred_element_type=F32)
        gext = g[...]
        g1 = pltpu.roll(gext, 1, 0)
        g2 = pltpu.roll(g1, 1, 0)
        y = cb_ref[...] + g2[HALO:] * cw_ref[0:1, :]
        y = y + g1[HALO:] * cw_ref[1:2, :]
        y = y + gext[HALO:] * cw_ref[2:3, :]
        act = y * (1.0 / (1.0 + jnp.exp(-y)))
        o_ref[h * hm:(h + 1) * hm, :] = (act * up).astype(o_ref.dtype)


def _ffn_a(hf, wg, wu, cw, cb, tm=2048, tf=256):
    S, D = hf.shape
    Fp = wg.shape[1]
    assert Fp % tf == 0
    tm = min(tm, S)
    hb = tm // HALO
    return pl.pallas_call(
        _ffn_a_kernel,
        grid=(S // tm, Fp // tf),
        in_specs=[pl.BlockSpec((HALO, D), lambda i, f: (jnp.maximum(i * hb - 1, 0), 0)),
                  pl.BlockSpec((tm, D), lambda i, f: (i, 0), pipeline_mode=pl.Buffered(1)),
                  pl.BlockSpec((D, tf), lambda i, f: (0, f)),
                  pl.BlockSpec((D, tf), lambda i, f: (0, f)),
                  pl.BlockSpec((cw.shape[0], tf), lambda i, f: (0, f)),
                  pl.BlockSpec((1, tf), lambda i, f: (0, f))],
        out_specs=pl.BlockSpec((tm, tf), lambda i, f: (i, f)),
        out_shape=jax.ShapeDtypeStruct((S, Fp), BF16),
        scratch_shapes=[pltpu.VMEM((FFN_ROW_CHUNKS, HALO + tm // FFN_ROW_CHUNKS, tf), F32)],
        compiler_params=_cparams(("arbitrary", "arbitrary")),
        name="ffn_gate_up",
    )(hf, hf, wg, wu, cw, cb)


def _ffn_b_kernel(a_ref, w_ref, x_ref, o_ref):
    o_ref[...] = x_ref[...] + jnp.dot(a_ref[...], w_ref[...], preferred_element_type=F32)


def _ffn_b(h, w, x, tm=512, tn=512):
    S, F = h.shape
    N = w.shape[1]
    tm = min(tm, S)
    return pl.pallas_call(
        _ffn_b_kernel,
        grid=(S // tm, N // tn),
        in_specs=[pl.BlockSpec((tm, F), lambda i, j: (i, 0)),
                  pl.BlockSpec((F, tn), lambda i, j: (0, j)),
                  pl.BlockSpec((tm, tn), lambda i, j: (i, j))],
        out_specs=pl.BlockSpec((tm, tn), lambda i, j: (i, j)),
        out_shape=jax.ShapeDtypeStruct((S, N), F32),
        compiler_params=_cparams(("arbitrary", "arbitrary")),
        name="ffn_down_residual",
    )(h, w, x)


def _layer(x, rel_bias, norm_mix_g, w_in, a_q_norm_g, a_k_norm_g, a_sinks, b_q_norm_g, b_k_norm_g,
           cmp_pos_emb, cmp_w1, cmp_b1, cmp_w2, cmp_b2, out_norm_g, w_out, norm_ffn_g, w_gate,
           w_up, conv_w, conv_b, w_down):
    S, D = x.shape
    aw = A_Q_HEADS * HEAD_DIM
    akv = A_KV_HEADS * HEAD_DIM
    bw = B_Q_HEADS * HEAD_DIM
    bkv = B_KV_HEADS * HEAD_DIM
    sizes = [aw, akv, akv, bw] + [bkv] * 6 + [N_BRANCH * B_Q_HEADS]
    offs = np.concatenate([[0], np.cumsum(sizes)]).tolist()
    wt = jnp.swapaxes(w_in, 0, 1)
    tn = 512
    order_a = [0, 1, 2, 3, 6, 7, 8, 9]
    n_gate = sizes[10]
    wg_t = jnp.pad(wt[offs[10]:offs[10] + n_gate], ((0, QB - n_gate), (0, 0)))
    one = jnp.ones((HEAD_DIM,), F32)
    seg_gain = {0: a_q_norm_g, 1: a_k_norm_g * KEY_SCALE, 3: b_q_norm_g,
                6: b_k_norm_g[1] * KEY_SCALE, 8: b_k_norm_g[2] * KEY_SCALE}
    kinds, tiles, gains, col = [], [], [], {}
    c0 = 0
    for k in order_a:
        col[k] = c0
        for t in range(sizes[k] // tn):
            kinds.append(1 if k in seg_gain else 0)
            tiles.append(offs[k] // tn + t)
            gains.append(seg_gain.get(k, one))
        c0 += sizes[k]
    kinds = jnp.asarray(kinds, jnp.int32)
    tiles = jnp.asarray(tiles, jnp.int32)
    gains = jnp.stack(gains).astype(F32).reshape(len(gains), 1, HEAD_DIM)

    hn = _rmsnorm([x], norm_mix_g)
    proj = _inproj_a(hn, wt, kinds, tiles, gains, tn=tn)
    kv32, gates = _inproj_b(hn, wt, offs[4] // tn, 2 * bkv // tn, wg_t)

    tab_a = rel_bias[:, :A_Q_HEADS]
    tab_b = rel_bias[:, A_Q_HEADS:]
    near_a = _stack_heads(_near_bias_tables(tab_a), A_KV_HEADS)
    near_b = _stack_heads(_near_bias_tables(tab_b), B_KV_HEADS)
    head_row = lambda v, G: _stack_heads(jnp.broadcast_to(v.astype(F32)[:, None, None],
                                                          (v.shape[0], 1, QB)), G)
    c31_a = head_row(tab_a[N_BUCKETS - 1], A_KV_HEADS)
    c31_b = head_row(tab_b[N_BUCKETS - 1], B_KV_HEADS)

    qcol = lambda k: col[k] // (GROUP * HEAD_DIM)
    hcol = lambda k: col[k] // HEAD_DIM
    o_a = _band_attention(proj, A_KV_HEADS, qcol(0), hcol(1), hcol(2), near_a, c31_a,
                          head_row(a_sinks, A_KV_HEADS), A_WINDOW, transposed_out=False)

    NC = S // CMP_STRIDE
    n_cmp = (S - CMP_BLOCK) // CMP_STRIDE + 1
    w1 = cmp_w1.reshape(2, CMP_BLOCK, HEAD_DIM, cmp_w1.shape[-1]).astype(BF16)
    kc, vct = _compress(kv32, B_KV_HEADS, cmp_pos_emb.astype(F32), w1, cmp_b1[:, None, :].astype(F32),
                        cmp_w2.astype(BF16), cmp_b2[:, None, :].astype(F32),
                        (b_k_norm_g[0] * KEY_SCALE).reshape(1, HEAD_DIM).astype(F32))

    NB = S // SLC_BLOCK
    ii = np.arange(NC)[None, :]
    jj = np.arange(NB)[:, None]
    ovl_t = ((ii * CMP_STRIDE <= jj * SLC_BLOCK + SLC_BLOCK - 1)
             & (ii * CMP_STRIDE + CMP_BLOCK - 1 >= jj * SLC_BLOCK) & (ii < n_cmp))
    ovl_t = jnp.asarray(ovl_t.astype(np.float32), BF16)
    ocmp_t, sel_t = _cmp_select(proj, qcol(3), kc, vct,
                                _stack_heads(_cmp_bias_table(tab_b), B_KV_HEADS), c31_b, ovl_t, n_cmp)

    owin_t = _band_attention(proj, B_KV_HEADS, qcol(3), hcol(8), hcol(9), near_b, c31_b, None,
                             B_WINDOW, transposed_out=True)
    gates_s = gates[:, :n_gate].reshape(S // QB, QB, B_KV_HEADS, GROUP, N_BRANCH)
    gates_s = gates_s.transpose(2, 0, 4, 3, 1).reshape(B_KV_HEADS, S // QB, N_BRANCH, QW)
    o_b = _sel_attention(proj, qcol(3), hcol(6), hcol(7), near_b, c31_b, sel_t, gates_s,
                         ocmp_t, owin_t)

    on = _rmsnorm([o_a, o_b], out_norm_g)
    x2 = _outproj(on, w_out, x)

    hf = _rmsnorm([x2], norm_ffn_g)
    hmid = _ffn_a(hf, w_gate, w_up, conv_w.astype(F32), conv_b[None, :].astype(F32))
    return _ffn_b(hmid, w_down.astype(BF16), x2)


def kernel(x, rel_bias, norm_mix_g, w_in, a_q_norm_g, a_k_norm_g, a_sinks, b_q_norm_g, b_k_norm_g,
           cmp_pos_emb, cmp_w1, cmp_b1, cmp_w2, cmp_b2, out_norm_g, w_out, norm_ffn_g, w_gate, w_up,
           conv_w, conv_b, w_down):
    depth = w_in.shape[0]
    batch = x.shape[0]
    outs = []
    for b in range(batch):
        h = x[b]
        for l in range(depth):
            h = _layer(h, rel_bias, norm_mix_g[l], w_in[l], a_q_norm_g[l], a_k_norm_g[l], a_sinks[l],
                       b_q_norm_g[l], b_k_norm_g[l], cmp_pos_emb[l], cmp_w1[l], cmp_b1[l], cmp_w2[l],
                       cmp_b2[l], out_norm_g[l], w_out[l], norm_ffn_g[l], w_gate[l], w_up[l],
                       conv_w[l], conv_b[l], w_down[l])
        outs.append(h)
    return jnp.stack(outs)
```

```python
import functools
import math

import numpy as np
import jax
import jax.numpy as jnp
from jax import lax
from jax.experimental import pallas as pl
from jax.experimental.pallas import tpu as pltpu

F32 = jnp.float32
BF16 = jnp.bfloat16

HEAD_DIM = 128
A_Q_HEADS = 16
A_KV_HEADS = 4
B_Q_HEADS = 16
B_KV_HEADS = 4
GROUP = 4
A_WINDOW = 128
B_WINDOW = 512
CMP_BLOCK = 32
CMP_STRIDE = 16
SLC_BLOCK = 64
SLC_TOPK = 16
N_BRANCH = 3
N_BUCKETS = 32
MAX_DISTANCE = 128
EPS = 1e-6
NEG = -1e30
FORCE_SCORE = 1e6
SCALE = HEAD_DIM ** -0.5

QB = 128
CMP_WIN = 24
CMP_TAB = 40
VMEM_LIMIT = 56 * 1024 * 1024


def _cparams(sem, **kw):
    return pltpu.CompilerParams(dimension_semantics=sem, vmem_limit_bytes=VMEM_LIMIT, **kw)


def _t5_bucket_np(dist):
    n = np.maximum(dist, 0)
    max_exact = N_BUCKETS // 2
    nf = np.maximum(n, 1).astype(np.float32)
    large = max_exact + (np.log(nf / max_exact) / math.log(MAX_DISTANCE / max_exact)
                         * (N_BUCKETS - max_exact)).astype(np.int32)
    large = np.minimum(large, N_BUCKETS - 1)
    return np.where(n < max_exact, n, large).astype(np.int32)


def _bias_from_buckets(tab, idx):
    onehot = (idx[None] == np.arange(N_BUCKETS).reshape((-1,) + (1,) * idx.ndim)).astype(np.float32)
    return jnp.einsum('bh,b...->h...', tab.astype(F32), jnp.asarray(onehot),
                      precision=lax.Precision.HIGHEST)


def _near_bias_tables(tab):
    kk = np.arange(QB)[:, None]
    qq = np.arange(QB)[None, :]
    return _bias_from_buckets(tab, np.stack([_t5_bucket_np(qq - kk), _t5_bucket_np(QB + qq - kk)]))


def _cmp_bias_table(tab):
    npr = np.arange(CMP_TAB)[:, None] - 16
    qq = np.arange(QB)[None, :]
    return _bias_from_buckets(tab, _t5_bucket_np(qq - CMP_STRIDE * npr - (CMP_BLOCK - 1)))


def _rmsnorm_kernel(*refs, n_in):
    x_refs, g_ref, o_ref = refs[:n_in], refs[n_in], refs[n_in + 1]
    off = 0
    for x_ref in x_refs:
        x = x_ref[...]
        w = x.shape[-1]
        y = x * lax.rsqrt(jnp.mean(x * x, axis=-1, keepdims=True) + EPS)
        o_ref[:, off:off + w] = (y * g_ref[:, off:off + w]).astype(o_ref.dtype)
        off += w


def _rmsnorm(xs, gain, tr=256):
    S = xs[0].shape[0]
    widths = [x.shape[1] for x in xs]
    n = sum(widths)
    tr = min(tr, S)
    return pl.pallas_call(
        functools.partial(_rmsnorm_kernel, n_in=len(xs)),
        grid=(S // tr,),
        in_specs=[pl.BlockSpec((tr, w), lambda i: (i, 0)) for w in widths]
        + [pl.BlockSpec((1, n), lambda i: (0, 0))],
        out_specs=pl.BlockSpec((tr, n), lambda i: (i, 0)),
        out_shape=jax.ShapeDtypeStruct((S, n), BF16),
        compiler_params=_cparams(("arbitrary",)),
        name="rmsnorm",
    )(*xs, gain.reshape(1, n).astype(F32))


NORM_ROW_CHUNKS = 2


def _dot_wt(a, wt):
    return lax.dot_general(a, wt.astype(BF16), (((1,), (1,)), ((), ())), preferred_element_type=F32)


def _inproj_a_kernel(kind_ref, tile_ref, a_ref, w_ref, g_ref, o_ref, w_s):
    j = pl.program_id(0)

    @pl.when(pl.program_id(1) == 0)
    def _():
        w_s[...] = w_ref[...].astype(BF16)

    @pl.when(kind_ref[j] == 0)
    def _():
        o_ref[...] = _dot_wt(a_ref[...], w_s[...]).astype(o_ref.dtype)

    @pl.when(kind_ref[j] == 1)
    def _():
        g = g_ref[0]
        hm = a_ref.shape[0] // NORM_ROW_CHUNKS
        for c in range(NORM_ROW_CHUNKS):
            rows = slice(c * hm, (c + 1) * hm)
            acc = _dot_wt(a_ref[rows, :], w_s[...])
            for h in range(acc.shape[1] // HEAD_DIM):
                sl = acc[:, h * HEAD_DIM:(h + 1) * HEAD_DIM]
                y = sl * lax.rsqrt(jnp.mean(sl * sl, axis=-1, keepdims=True) + EPS)
                o_ref[rows, h * HEAD_DIM:(h + 1) * HEAD_DIM] = (y * g).astype(o_ref.dtype)


def _inproj_a(hn, wt, kinds, tiles, gains, tm=1024, tn=512):
    S, D = hn.shape
    n_tiles = kinds.shape[0]
    tm = min(tm, S)
    grid_spec = pltpu.PrefetchScalarGridSpec(
        num_scalar_prefetch=2,
        grid=(n_tiles, S // tm),
        in_specs=[pl.BlockSpec((tm, D), lambda j, i, k, t: (i, 0)),
                  pl.BlockSpec((tn, D), lambda j, i, k, t: (t[j], 0)),
                  pl.BlockSpec((1, 1, HEAD_DIM), lambda j, i, k, t: (j, 0, 0))],
        out_specs=pl.BlockSpec((tm, tn), lambda j, i, k, t: (i, j)),
        scratch_shapes=[pltpu.VMEM((tn, D), BF16)],
    )
    return pl.pallas_call(
        _inproj_a_kernel,
        grid_spec=grid_spec,
        out_shape=jax.ShapeDtypeStruct((S, n_tiles * tn), BF16),
        compiler_params=_cparams(("arbitrary", "arbitrary")),
        name="inproj_heads",
    )(kinds, tiles, hn, wt, gains)


def _inproj_b_kernel(x_ref, g_ref, w_ref, wg_ref, hn_ref, o_ref, gate_ref):
    @pl.when(pl.program_id(1) == 0)
    def _():
        x = x_ref[...]
        y = x * lax.rsqrt(jnp.mean(x * x, axis=-1, keepdims=True) + EPS)
        hn_ref[...] = (y * g_ref[...]).astype(hn_ref.dtype)
        gate_ref[...] = 1.0 / (1.0 + jnp.exp(-_dot_wt(hn_ref[...], wg_ref[...])))

    o_ref[...] = _dot_wt(hn_ref[...], w_ref[...])


def _inproj_b(x, gain, wt, tile0, n_tiles, wg_t, tm=512, tn=512):
    S, D = x.shape
    tm = min(tm, S)
    ng = wg_t.shape[0]
    return pl.pallas_call(
        _inproj_b_kernel,
        grid=(S // tm, n_tiles),
        in_specs=[pl.BlockSpec((tm, D), lambda i, j: (i, 0)),
                  pl.BlockSpec((1, D), lambda i, j: (0, 0)),
                  pl.BlockSpec((tn, D), lambda i, j: (tile0 + j, 0)),
                  pl.BlockSpec((ng, D), lambda i, j: (0, 0))],
        out_specs=[pl.BlockSpec((tm, D), lambda i, j: (i, 0)),
                   pl.BlockSpec((tm, tn), lambda i, j: (i, j)),
                   pl.BlockSpec((tm, ng), lambda i, j: (i, 0))],
        out_shape=[jax.ShapeDtypeStruct((S, D), BF16),
                   jax.ShapeDtypeStruct((S, n_tiles * tn), F32),
                   jax.ShapeDtypeStruct((S, ng), F32)],
        compiler_params=_cparams(("arbitrary", "arbitrary")),
        name="norm_inproj_cmp_gates",
    )(x, gain.reshape(1, D).astype(F32), wt, wg_t)


def _gelu_tanh(x):
    return 0.5 * x * (1.0 + jnp.tanh(math.sqrt(2.0 / math.pi) * (x + 0.044715 * (x * x * x))))


def _compress_kernel(tk_ref, tv_ref, pos_ref, w1_ref, b1_ref, w2_ref, b2_ref, gk_ref,
                     kc_ref, vct_ref):
    nc = kc_ref.shape[1]

    def mlp(t_ref, kv):
        a0 = jnp.zeros((nc, w1_ref.shape[-1]), F32)
        a1 = jnp.zeros((nc, w1_ref.shape[-1]), F32)
        for b in range(CMP_STRIDE):
            t = t_ref[pl.ds(b, nc, stride=CMP_STRIDE), :]
            lo, hi = b, CMP_STRIDE + b
            a0 = a0 + jnp.dot((t + pos_ref[kv, lo:lo + 1, :]).astype(BF16), w1_ref[kv, lo],
                              preferred_element_type=F32)
            a1 = a1 + jnp.dot((t + pos_ref[kv, hi:hi + 1, :]).astype(BF16), w1_ref[kv, hi],
                              preferred_element_type=F32)
        h = a0 + pltpu.roll(a1, nc - 1, 0) + b1_ref[kv]
        h = _gelu_tanh(h)
        return jnp.dot(h.astype(BF16), w2_ref[kv], preferred_element_type=F32) + b2_ref[kv]

    ck = mlp(tk_ref, 0)
    ck = ck * lax.rsqrt(jnp.mean(ck * ck, axis=-1, keepdims=True) + EPS) * gk_ref[...]
    kc_ref[0] = ck.astype(kc_ref.dtype)
    cv = mlp(tv_ref, 1)
    vct_ref[0] = cv.T.astype(vct_ref.dtype)


def _compress(kv32, G, pos, w1, b1, w2, b2, gk):
    S = kv32.shape[0]
    NC = S // CMP_STRIDE
    hid = w1.shape[-1]
    return pl.pallas_call(
        _compress_kernel,
        grid=(G,),
        in_specs=[pl.BlockSpec((S, HEAD_DIM), lambda g: (0, g)),
                  pl.BlockSpec((S, HEAD_DIM), lambda g: (0, G + g)),
                  pl.BlockSpec((2, CMP_BLOCK, HEAD_DIM), lambda g: (0, 0, 0)),
                  pl.BlockSpec((2, CMP_BLOCK, HEAD_DIM, hid), lambda g: (0, 0, 0, 0)),
                  pl.BlockSpec((2, 1, hid), lambda g: (0, 0, 0)),
                  pl.BlockSpec((2, hid, HEAD_DIM), lambda g: (0, 0, 0)),
                  pl.BlockSpec((2, 1, HEAD_DIM), lambda g: (0, 0, 0)),
                  pl.BlockSpec((1, HEAD_DIM), lambda g: (0, 0))],
        out_specs=[pl.BlockSpec((1, NC, HEAD_DIM), lambda g: (g, 0, 0)),
                   pl.BlockSpec((1, HEAD_DIM, NC), lambda g: (g, 0, 0))],
        out_shape=[jax.ShapeDtypeStruct((G, NC, HEAD_DIM), BF16),
                   jax.ShapeDtypeStruct((G, HEAD_DIM, NC), BF16)],
        compiler_params=_cparams(("arbitrary",)),
        name="compress",
    )(kv32, kv32, pos, w1, b1, w2, b2, gk)


QW = GROUP * QB


def _stacked_queries(q_ref):
    return jnp.concatenate([q_ref[:, r * HEAD_DIM:(r + 1) * HEAD_DIM] for r in range(GROUP)], axis=0)


LOG2E = math.log2(math.e)
KEY_SCALE = SCALE * LOG2E
QSUB = 4


def _scores(k_t, qs):
    return lax.dot_general(k_t, qs, (((1,), (1,)), ((), ())), preferred_element_type=F32)


def _rel_bias(tab, c31):
    return (tab - c31) * LOG2E


def _init_state(m_ref, l_ref, acc_ref):
    m_ref[...] = jnp.full(m_ref.shape, NEG, F32)
    l_ref[...] = jnp.zeros(l_ref.shape, F32)
    acc_ref[...] = jnp.zeros(acc_ref.shape, F32)


def _online_update(m_ref, l_ref, acc_ref, lanes, t, v_t):
    m_old = m_ref[:, lanes]
    m_new = jnp.maximum(m_old, jnp.max(t, axis=0, keepdims=True))
    alpha = jnp.exp2(m_old - m_new)
    p = jnp.exp2(t - m_new)
    l_ref[:, lanes] = alpha * l_ref[:, lanes] + jnp.sum(p, axis=0, keepdims=True)
    pv = lax.dot_general(v_t, p.astype(BF16), (((0,), (0,)), ((), ())), preferred_element_type=F32)
    acc_ref[:, lanes] = acc_ref[:, lanes] * alpha + pv
    m_ref[:, lanes] = m_new


def _key_query_iotas():
    kk = lax.broadcasted_iota(jnp.int32, (QB, QW), 0)
    qq = lax.broadcasted_iota(jnp.int32, (QB, QW), 1) & (QB - 1)
    return kk, qq


def _load_kv(k_ref, v_ref, kb0, n):
    rows = pl.ds(pl.multiple_of(kb0 * QB, QB), n * QB)
    return k_ref[rows, :], v_ref[rows, :]


def _store_heads(o_ref, o_t):
    for r in range(GROUP):
        o_ref[:, r * HEAD_DIM:(r + 1) * HEAD_DIM] = o_t[:, r * QB:(r + 1) * QB].T


def _stack_heads(a, G):
    a = a.reshape((G, GROUP) + a.shape[1:])
    a = jnp.moveaxis(a, 1, -2)
    return a.reshape(a.shape[:-2] + (QW,))


def _band_kernel(q_ref, k_ref, v_ref, tab_ref, c31_ref, sink_ref, o_ref, m_ref, l_ref, acc_ref,
                 *, nback, use_sinks, transposed_out):
    c0 = pl.program_id(1) * QSUB
    _init_state(m_ref, l_ref, acc_ref)
    kk, qq = _key_query_iotas()
    c31 = c31_ref[0]

    def masked_scores(sub, kb0, deltas):
        k_t, v_t = _load_kv(k_ref, v_ref, kb0, len(deltas))
        s = _scores(k_t, _stacked_queries(q_ref.at[sub * QB:(sub + 1) * QB]))
        parts = []
        for t, delta in enumerate(deltas):
            sb = s[t * QB:(t + 1) * QB]
            if delta <= 1:
                sb = sb + _rel_bias(tab_ref[0, delta], c31)
            if delta == 0:
                sb = jnp.where(qq >= kk, sb, NEG)
            elif delta == nback:
                sb = jnp.where(qq < kk, sb, NEG)
            parts.append(sb)
        return (parts[0] if len(parts) == 1 else jnp.concatenate(parts, axis=0)), v_t

    def update(sub, s, v_t):
        _online_update(m_ref.at[sub], l_ref.at[sub], acc_ref.at[sub], slice(None), s, v_t)

    @pl.when(c0 >= nback)
    def _():
        band = list(range(nback, -1, -1))
        tiles = [masked_scores(sub, c0 + sub - nback, band) for sub in range(QSUB)]
        for sub, (s, v_t) in enumerate(tiles):
            update(sub, s, v_t)

    @pl.when(c0 < nback)
    def _():
        for sub in range(QSUB):
            for delta in range(nback, -1, -1):
                @pl.when(c0 + sub - delta >= 0)
                def _(sub=sub, delta=delta):
                    update(sub, *masked_scores(sub, c0 + sub - delta, [delta]))

    for sub in range(QSUB):
        l = l_ref[sub]
        acc = acc_ref[sub]
        if use_sinks:
            m = m_ref[sub]
            sk = _rel_bias(sink_ref[0], c31)
            m_f = jnp.maximum(m, sk)
            a = jnp.exp2(m - m_f)
            l = l * a + jnp.exp2(sk - m_f)
            acc = acc * a
        o_t = acc * (1.0 / l)
        if transposed_out:
            o_ref[0, sub] = o_t
        else:
            _store_heads(o_ref.at[sub * QB:(sub + 1) * QB], o_t)


def _qkv_specs(S, q_col, k_col, v_col, q_rows=QB):
    return [pl.BlockSpec((q_rows, GROUP * HEAD_DIM), lambda g, c: (c, q_col + g)),
            pl.BlockSpec((S, HEAD_DIM), lambda g, c: (0, k_col + g)),
            pl.BlockSpec((S, HEAD_DIM), lambda g, c: (0, v_col + g))]


def _band_attention(proj, G, q_col, k_col, v_col, tabs, c31, sinks, window, transposed_out):
    S = proj.shape[0]
    nkb = S // QB
    nback = -(-(window - 1) // QB)
    use_sinks = sinks is not None
    if sinks is None:
        sinks = jnp.zeros_like(c31)
    if transposed_out:
        out_shape = jax.ShapeDtypeStruct((G, nkb, HEAD_DIM, QW), F32)
        out_spec = pl.BlockSpec((1, QSUB, HEAD_DIM, QW), lambda g, c: (g, c, 0, 0))
    else:
        out_shape = jax.ShapeDtypeStruct((S, G * GROUP * HEAD_DIM), F32)
        out_spec = pl.BlockSpec((QSUB * QB, GROUP * HEAD_DIM), lambda g, c: (c, g))
    row = pl.BlockSpec((1, 1, QW), lambda g, c: (g, 0, 0))
    return pl.pallas_call(
        functools.partial(_band_kernel, nback=nback, use_sinks=use_sinks,
                          transposed_out=transposed_out),
        grid=(G, nkb // QSUB),
        in_specs=_qkv_specs(S, q_col, k_col, v_col, QSUB * QB)
        + [pl.BlockSpec((1, 2, QB, QW), lambda g, c: (g, 0, 0, 0)), row, row],
        out_specs=out_spec,
        out_shape=out_shape,
        scratch_shapes=[pltpu.VMEM((QSUB, 1, QW), F32), pltpu.VMEM((QSUB, 1, QW), F32),
                        pltpu.VMEM((QSUB, HEAD_DIM, QW), F32)],
        compiler_params=_cparams(("arbitrary", "arbitrary")),
        name="band_attention_w%d" % window,
    )(proj, proj, proj, tabs, c31, sinks)


def _cmp_select_kernel(q_ref, kc_ref, vct_ref, tab_ref, c31_ref, ovl_ref, ot_ref, sel_ref, s_ref,
                       *, n_valid, topk):
    c0 = pl.program_id(1) * QSUB
    nc = kc_ref.shape[1]
    nb = ovl_ref.shape[0]
    rown = lax.broadcasted_iota(jnp.int32, (nc, QW), 0)
    wrow = lax.broadcasted_iota(jnp.int32, (CMP_WIN, QW), 0)
    wq = lax.broadcasted_iota(jnp.int32, (CMP_WIN, QW), 1) & (QB - 1)
    lane_q = lax.broadcasted_iota(jnp.int32, (1, QW), 1) & (QB - 1)

    for sub in range(QSUB):
        c = c0 + sub
        qs = _stacked_queries(q_ref.at[sub * QB:(sub + 1) * QB])
        w0 = pl.multiple_of(jnp.maximum(8 * c - 16, 0), 8)
        toff = pl.multiple_of(w0 - 8 * c + 16, 8)
        n_abs = w0 + wrow
        dist = (c * QB + wq) - (n_abs * CMP_STRIDE + (CMP_BLOCK - 1))
        valid_w = (dist >= 0) & (n_abs < n_valid)
        s_ref[sub] = jnp.where(rown < w0, _scores(kc_ref[0], qs), NEG)
        bias_w = _rel_bias(tab_ref[0, pl.ds(toff, CMP_WIN), :], c31_ref[0])
        s_w = _scores(kc_ref[0, pl.ds(w0, CMP_WIN), :], qs) + bias_w
        s_ref[sub, pl.ds(w0, CMP_WIN), :] = jnp.where(valid_w, s_w, NEG)

    p_groups = []
    for sub in range(QSUB):
        has_any = ((c0 + sub) * QB + lane_q >= CMP_BLOCK - 1).astype(F32)
        s = s_ref[sub]
        m = jnp.max(s, axis=0, keepdims=True)
        e = jnp.exp2(s - m)
        p = e * (has_any / jnp.sum(e, axis=0, keepdims=True))
        ot_ref[0, sub] = jnp.dot(vct_ref[0], p.astype(BF16), preferred_element_type=F32)
        p_grp = p[:, 0:QB]
        for r in range(1, GROUP):
            p_grp = p_grp + p[:, r * QB:(r + 1) * QB]
        p_groups.append(p_grp.astype(BF16))

    nq = QSUB * QB
    scores = jnp.dot(ovl_ref[...], jnp.concatenate(p_groups, axis=1), preferred_element_type=F32)
    blk = lax.broadcasted_iota(jnp.int32, (nb, nq), 0)
    cur = lax.shift_right_logical(c0 * QB + lax.broadcasted_iota(jnp.int32, (nb, nq), 1),
                                  int(math.log2(SLC_BLOCK)))
    forced = (blk == 0) | (blk == cur) | (blk == cur - 1)
    work = jnp.where(forced, FORCE_SCORE, jnp.where(blk <= cur, scores, -1.0))
    blk_f = blk.astype(F32)
    sel = jnp.zeros((nb, nq), F32)
    for _ in range(topk):
        mx = jnp.max(work, axis=0, keepdims=True)
        first = jnp.min(jnp.where(work == mx, blk_f, float(nb)), axis=0, keepdims=True)
        hit = blk_f == first
        sel = jnp.where(hit, 1.0, sel)
        work = jnp.where(hit, -3e38, work)
    sel_ref[0] = sel


def _cmp_select(proj, q_col, kc, vct, tab, c31, ovl_t, n_valid):
    nkb = proj.shape[0] // QB
    G, NC = kc.shape[0], kc.shape[1]
    NB = ovl_t.shape[0]
    return pl.pallas_call(
        functools.partial(_cmp_select_kernel, n_valid=n_valid, topk=min(SLC_TOPK, NB)),
        grid=(G, nkb // QSUB),
        in_specs=[pl.BlockSpec((QSUB * QB, GROUP * HEAD_DIM), lambda g, c: (c, q_col + g)),
                  pl.BlockSpec((1, NC, HEAD_DIM), lambda g, c: (g, 0, 0)),
                  pl.BlockSpec((1, HEAD_DIM, NC), lambda g, c: (g, 0, 0)),
                  pl.BlockSpec((1, CMP_TAB, QW), lambda g, c: (g, 0, 0)),
                  pl.BlockSpec((1, 1, QW), lambda g, c: (g, 0, 0)),
                  pl.BlockSpec((NB, NC), lambda g, c: (0, 0))],
        out_specs=[pl.BlockSpec((1, QSUB, HEAD_DIM, QW), lambda g, c: (g, c, 0, 0)),
                   pl.BlockSpec((1, NB, QSUB * QB), lambda g, c: (g, 0, c))],
        out_shape=[jax.ShapeDtypeStruct((G, nkb, HEAD_DIM, QW), F32),
                   jax.ShapeDtypeStruct((G, NB, nkb * QB), F32)],
        scratch_shapes=[pltpu.VMEM((QSUB, NC, QW), F32)],
        compiler_params=_cparams(("arbitrary", "arbitrary")),
        name="cmp_attention_select",
    )(proj, kc, vct, tab, c31, ovl_t)


FAR_BLOCKS = 4


def _sel_block(c, q_ref, sel_row, gate_ref, ocmp_ref, owin_ref, o_ref, k_ref, v_ref, tab_ref, c31,
               m_ref, l_ref, acc_ref, s_a, s_b, p_a, p_b, al_a, al_b, s_n):
    _init_state(m_ref, l_ref, acc_ref)
    kk, qq = _key_query_iotas()
    qs = _stacked_queries(q_ref)
    spb = QB // SLC_BLOCK
    tk = FAR_BLOCKS * QB
    last_tile = k_ref.shape[0] // tk - 1
    n_far = jnp.maximum(c - 1, 0)
    n_tiles = (n_far + FAR_BLOCKS - 1) // FAR_BLOCKS

    def head_row(r, row_limit=None, r_load=None):
        row = sel_row(r if r_load is None else r_load)
        if row_limit is not None:
            row = jnp.where(r < row_limit, row, 0.0)
        return jnp.concatenate([row] * GROUP, axis=1)

    def tile_rows(i):
        return pl.ds(pl.multiple_of(jnp.minimum(i, last_tile) * tk, tk), tk)

    def far_scores(i, s_buf):
        s = _scores(k_ref[tile_rows(i), :], qs)
        r0 = i * (FAR_BLOCKS * spb)
        r0_load = jnp.minimum(i, last_tile) * (FAR_BLOCKS * spb)
        for t in range(FAR_BLOCKS * spb):
            row = head_row(r0 + t, n_far * spb, r0_load + t)
            blk = slice(t * SLC_BLOCK, (t + 1) * SLC_BLOCK)
            s_buf[blk, :] = jnp.where(row > 0.5, s[blk], NEG)

    def far_softmax(s_buf, p_buf, al_buf):
        blocks = [slice(t * SLC_BLOCK, (t + 1) * SLC_BLOCK) for t in range(FAR_BLOCKS * spb)]
        sub8 = lambda a: a.reshape(SLC_BLOCK // 8, 8, QW)
        m_part = jnp.full((8, QW), NEG, F32)
        for blk in blocks:
            m_part = jnp.maximum(m_part, jnp.max(sub8(s_buf[blk, :]), axis=0))
        m_old = m_ref[...]
        m_new = jnp.maximum(m_old, jnp.max(m_part, axis=0, keepdims=True))
        alpha = jnp.exp2(m_old - m_new)
        l_part = jnp.zeros((8, QW), F32)
        for blk in blocks:
            p = jnp.exp2(s_buf[blk, :] - m_new)
            l_part = l_part + jnp.sum(sub8(p), axis=0)
            p_buf[blk, :] = p.astype(BF16)
        l_ref[...] = alpha * l_ref[...] + jnp.sum(l_part, axis=0, keepdims=True)
        m_ref[...] = m_new
        al_buf[...] = alpha

    def far_values(i, p_buf, al_buf):
        pv = lax.dot_general(v_ref[tile_rows(i), :], p_buf[...], (((0,), (0,)), ((), ())),
                             preferred_element_type=F32)
        acc_ref[...] = acc_ref[...] * al_buf[...] + pv

    far_scores(0, s_a)
    far_scores(1, s_b)
    for p_buf, al_buf in ((p_a, al_a), (p_b, al_b)):
        p_buf[...] = jnp.zeros(p_buf.shape, BF16)
        al_buf[...] = jnp.ones(al_buf.shape, F32)

    blocks = [(jnp.maximum(c - 1, 0), 1, c * spb), (c, 0, None)]
    near_rows = [pl.ds(pl.multiple_of(kb * QB, QB), QB) for kb, _, _ in blocks]
    s = _scores(jnp.concatenate([k_ref[r, :] for r in near_rows], axis=0), qs)
    for t, (kb, delta, limit) in enumerate(blocks):
        msk = jnp.concatenate(
            [jnp.broadcast_to(head_row(kb * spb + u, limit), (SLC_BLOCK, QW)) for u in range(spb)],
            axis=0) > 0.5
        if delta == 0:
            msk = msk & (qq >= kk)
        sb = s[t * QB:(t + 1) * QB] + _rel_bias(tab_ref[0, delta], c31)
        s_n[t * QB:(t + 1) * QB, :] = jnp.where(msk, sb, NEG)

    def pair_body(j, carry):
        i = 2 * j
        far_values(jnp.maximum(i - 2, 0), p_a, al_a)
        far_values(jnp.maximum(i - 1, 0), p_b, al_b)
        far_softmax(s_a, p_a, al_a)
        far_softmax(s_b, p_b, al_b)
        far_scores(i + 2, s_a)
        far_scores(i + 3, s_b)
        return carry

    n_pairs = (n_tiles + 1) // 2
    lax.fori_loop(0, n_pairs, pair_body, 0)
    far_values(jnp.maximum(2 * n_pairs - 2, 0), p_a, al_a)
    far_values(jnp.maximum(2 * n_pairs - 1, 0), p_b, al_b)

    v_near = jnp.concatenate([v_ref[r, :] for r in near_rows], axis=0)
    _online_update(m_ref, l_ref, acc_ref, slice(None), s_n[...], v_near)

    o_slc = acc_ref[...] * (1.0 / l_ref[...])
    o_t = gate_ref[0:1, :] * ocmp_ref[...] + gate_ref[1:2, :] * o_slc + gate_ref[2:3, :] * owin_ref[...]
    _store_heads(o_ref, o_t)


def _sel_kernel(q_ref, k_ref, v_ref, tab_ref, c31_ref, sel_ref, gate_ref, ocmp_ref, owin_ref, o_ref,
                *scratch):
    _sel_block(pl.program_id(1), q_ref, lambda r: sel_ref[0, pl.ds(r, 1), :],
               gate_ref.at[0, 0], ocmp_ref.at[0, 0], owin_ref.at[0, 0], o_ref,
               k_ref, v_ref, tab_ref, c31_ref[0], *scratch)


def _sel_attention(proj, q_col, k_col, v_col, tabs, c31, sel_t, gates_s, ocmp_t, owin_t):
    S = proj.shape[0]
    G, NB = sel_t.shape[0], sel_t.shape[1]
    nkb = S // QB
    nq = 1
    tile = pl.BlockSpec((1, nq, HEAD_DIM, QW), lambda g, c: (g, c, 0, 0))
    return pl.pallas_call(
        _sel_kernel,
        grid=(G, nkb // nq),
        in_specs=_qkv_specs(S, q_col, k_col, v_col, nq * QB)
        + [pl.BlockSpec((1, 2, QB, QW), lambda g, c: (g, 0, 0, 0)),
           pl.BlockSpec((1, 1, QW), lambda g, c: (g, 0, 0)),
           pl.BlockSpec((1, NB, nq * QB), lambda g, c: (g, 0, c)),
           pl.BlockSpec((1, nq, N_BRANCH, QW), lambda g, c: (g, c, 0, 0)),
           tile, tile],
        out_specs=pl.BlockSpec((nq * QB, GROUP * HEAD_DIM), lambda g, c: (c, g)),
        out_shape=jax.ShapeDtypeStruct((S, G * GROUP * HEAD_DIM), F32),
        scratch_shapes=[pltpu.VMEM((1, QW), F32), pltpu.VMEM((1, QW), F32),
                        pltpu.VMEM((HEAD_DIM, QW), F32)]
        + [pltpu.VMEM((FAR_BLOCKS * QB, QW), F32)] * 2
        + [pltpu.VMEM((FAR_BLOCKS * QB, QW), BF16)] * 2
        + [pltpu.VMEM((1, QW), F32)] * 2
        + [pltpu.VMEM((2 * QB, QW), F32)],
        compiler_params=_cparams(("arbitrary", "arbitrary")),
        name="selected_attention_combine",
    )(proj, proj, proj, tabs, c31, sel_t, gates_s, ocmp_t, owin_t)


def _outproj_kernel(a_ref, w_ref, x_ref, o_ref, w_s):
    @pl.when(pl.program_id(1) == 0)
    def _():
        w_s[...] = w_ref[...].astype(BF16)

    o_ref[...] = x_ref[...] + jnp.dot(a_ref[...], w_s[...], preferred_element_type=F32)


def _outproj(a, w, x, tm=1024, tn=512):
    S, D = a.shape
    N = w.shape[1]
    tm = min(tm, S)
    return pl.pallas_call(
        _outproj_kernel,
        grid=(N // tn, S // tm),
        in_specs=[pl.BlockSpec((tm, D), lambda j, i: (i, 0)),
                  pl.BlockSpec((D, tn), lambda j, i: (0, j)),
                  pl.BlockSpec((tm, tn), lambda j, i: (i, j))],
        out_specs=pl.BlockSpec((tm, tn), lambda j, i: (i, j)),
        out_shape=jax.ShapeDtypeStruct((S, N), F32),
        scratch_shapes=[pltpu.VMEM((D, tn), BF16)],
        compiler_params=_cparams(("arbitrary", "arbitrary")),
        name="outproj_residual",
    )(a, w, x)


HALO = 16
FFN_ROW_CHUNKS = 2


def _ffn_a_kernel(halo_ref, a_ref, wg_ref, wu_ref, cw_ref, cb_ref, o_ref, g_s):
    i = pl.program_id(0)
    tm = a_ref.shape[0]
    halo = halo_ref[...]
    halo = jnp.where(i > 0, halo, jnp.zeros_like(halo))
    wg = wg_ref[...].astype(BF16)
    wu = wu_ref[...].astype(BF16)
    hm = tm // FFN_ROW_CHUNKS
    for h in range(FFN_ROW_CHUNKS):
        a = a_ref[h * hm:(h + 1) * hm, :]
        g = g_s.at[h]
        if h == 0:
            g[...] = jnp.dot(jnp.concatenate([halo, a], axis=0), wg, preferred_element_type=F32)
        else:
            g[0:HALO, :] = g_s[h - 1, hm:hm + HALO, :]
            g[HALO:, :] = jnp.dot(a, wg, preferred_element_type=F32)
        up = jnp.dot(a, wu, preferred_element_type=F32)
        gext = g[...]
        g1 = pltpu.roll(gext, 1, 0)
        g2 = pltpu.roll(g1, 1, 0)
        y = cb_ref[...] + g2[HALO:] * cw_ref[0:1, :]
        y = y + g1[HALO:] * cw_ref[1:2, :]
        y = y + gext[HALO:] * cw_ref[2:3, :]
        act = y * (1.0 / (1.0 + jnp.exp(-y)))
        o_ref[h * hm:(h + 1) * hm, :] = (act * up).astype(o_ref.dtype)


def _ffn_a(hf, wg, wu, cw, cb, tm=2048, tf=256):
    S, D = hf.shape
    Fp = wg.shape[1]
    assert Fp % tf == 0
    tm = min(tm, S)
    hb = tm // HALO
    return pl.pallas_call(
        _ffn_a_kernel,
        grid=(S // tm, Fp // tf),
        in_specs=[pl.BlockSpec((HALO, D), lambda i, f: (jnp.maximum(i * hb - 1, 0), 0)),
                  pl.BlockSpec((tm, D), lambda i, f: (i, 0), pipeline_mode=pl.Buffered(1)),
                  pl.BlockSpec((D, tf), lambda i, f: (0, f)),
                  pl.BlockSpec((D, tf), lambda i, f: (0, f)),
                  pl.BlockSpec((cw.shape[0], tf), lambda i, f: (0, f)),
                  pl.BlockSpec((1, tf), lambda i, f: (0, f))],
        out_specs=pl.BlockSpec((tm, tf), lambda i, f: (i, f)),
        out_shape=jax.ShapeDtypeStruct((S, Fp), BF16),
        scratch_shapes=[pltpu.VMEM((FFN_ROW_CHUNKS, HALO + tm // FFN_ROW_CHUNKS, tf), F32)],
        compiler_params=_cparams(("arbitrary", "arbitrary")),
        name="ffn_gate_up",
    )(hf, hf, wg, wu, cw, cb)


def _ffn_b_kernel(a_ref, w_ref, x_ref, o_ref):
    o_ref[...] = x_ref[...] + jnp.dot(a_ref[...], w_ref[...], preferred_element_type=F32)


def _ffn_b(h, w, x, tm=512, tn=512):
    S, F = h.shape
    N = w.shape[1]
    tm = min(tm, S)
    return pl.pallas_call(
        _ffn_b_kernel,
        grid=(S // tm, N // tn),
        in_specs=[pl.BlockSpec((tm, F), lambda i, j: (i, 0)),
                  pl.BlockSpec((F, tn), lambda i, j: (0, j)),
                  pl.BlockSpec((tm, tn), lambda i, j: (i, j))],
        out_specs=pl.BlockSpec((tm, tn), lambda i, j: (i, j)),
        out_shape=jax.ShapeDtypeStruct((S, N), F32),
        compiler_params=_cparams(("arbitrary", "arbitrary")),
        name="ffn_down_residual",
    )(h, w, x)


def _layer(x, rel_bias, norm_mix_g, w_in, a_q_norm_g, a_k_norm_g, a_sinks, b_q_norm_g, b_k_norm_g,
           cmp_pos_emb, cmp_w1, cmp_b1, cmp_w2, cmp_b2, out_norm_g, w_out, norm_ffn_g, w_gate,
           w_up, conv_w, conv_b, w_down):
    S, D = x.shape
    aw = A_Q_HEADS * HEAD_DIM
    akv = A_KV_HEADS * HEAD_DIM
    bw = B_Q_HEADS * HEAD_DIM
    bkv = B_KV_HEADS * HEAD_DIM
    sizes = [aw, akv, akv, bw] + [bkv] * 6 + [N_BRANCH * B_Q_HEADS]
    offs = np.concatenate([[0], np.cumsum(sizes)]).tolist()
    wt = jnp.swapaxes(w_in, 0, 1)
    tn = 512
    order_a = [0, 1, 2, 3, 6, 7, 8, 9]
    n_gate = sizes[10]
    wg_t = jnp.pad(wt[offs[10]:offs[10] + n_gate], ((0, QB - n_gate), (0, 0)))
    one = jnp.ones((HEAD_DIM,), F32)
    seg_gain = {0: a_q_norm_g, 1: a_k_norm_g * KEY_SCALE, 3: b_q_norm_g,
                6: b_k_norm_g[1] * KEY_SCALE, 8: b_k_norm_g[2] * KEY_SCALE}
    kinds, tiles, gains, col = [], [], [], {}
    c0 = 0
    for k in order_a:
        col[k] = c0
        for t in range(sizes[k] // tn):
            kinds.append(1 if k in seg_gain else 0)
            tiles.append(offs[k] // tn + t)
            gains.append(seg_gain.get(k, one))
        c0 += sizes[k]
    kinds = jnp.asarray(kinds, jnp.int32)
    tiles = jnp.asarray(tiles, jnp.int32)
    gains = jnp.stack(gains).astype(F32).reshape(len(gains), 1, HEAD_DIM)

    hn, kv32, gates = _inproj_b(x, norm_mix_g, wt, offs[4] // tn, 2 * bkv // tn, wg_t)
    proj = _inproj_a(hn, wt, kinds, tiles, gains, tn=tn)

    tab_a = rel_bias[:, :A_Q_HEADS]
    tab_b = rel_bias[:, A_Q_HEADS:]
    near_a = _stack_heads(_near_bias_tables(tab_a), A_KV_HEADS)
    near_b = _stack_heads(_near_bias_tables(tab_b), B_KV_HEADS)
    head_row = lambda v, G: _stack_heads(jnp.broadcast_to(v.astype(F32)[:, None, None],
                                                          (v.shape[0], 1, QB)), G)
    c31_a = head_row(tab_a[N_BUCKETS - 1], A_KV_HEADS)
    c31_b = head_row(tab_b[N_BUCKETS - 1], B_KV_HEADS)

    qcol = lambda k: col[k] // (GROUP * HEAD_DIM)
    hcol = lambda k: col[k] // HEAD_DIM
    o_a = _band_attention(proj, A_KV_HEADS, qcol(0), hcol(1), hcol(2), near_a, c31_a,
                          head_row(a_sinks, A_KV_HEADS), A_WINDOW, transposed_out=False)

    NC = S // CMP_STRIDE
    n_cmp = (S - CMP_BLOCK) // CMP_STRIDE + 1
    w1 = cmp_w1.reshape(2, CMP_BLOCK, HEAD_DIM, cmp_w1.shape[-1]).astype(BF16)
    kc, vct = _compress(kv32, B_KV_HEADS, cmp_pos_emb.astype(F32), w1, cmp_b1[:, None, :].astype(F32),
                        cmp_w2.astype(BF16), cmp_b2[:, None, :].astype(F32),
                        (b_k_norm_g[0] * KEY_SCALE).reshape(1, HEAD_DIM).astype(F32))

    NB = S // SLC_BLOCK
    ii = np.arange(NC)[None, :]
    jj = np.arange(NB)[:, None]
    ovl_t = ((ii * CMP_STRIDE <= jj * SLC_BLOCK + SLC_BLOCK - 1)
             & (ii * CMP_STRIDE + CMP_BLOCK - 1 >= jj * SLC_BLOCK) & (ii < n_cmp))
    ovl_t = jnp.asarray(ovl_t.astype(np.float32), BF16)
    ocmp_t, sel_t = _cmp_select(proj, qcol(3), kc, vct,
                                _stack_heads(_cmp_bias_table(tab_b), B_KV_HEADS), c31_b, ovl_t, n_cmp)

    owin_t = _band_attention(proj, B_KV_HEADS, qcol(3), hcol(8), hcol(9), near_b, c31_b, None,
                             B_WINDOW, transposed_out=True)
    gates_s = gates[:, :n_gate].reshape(S // QB, QB, B_KV_HEADS, GROUP, N_BRANCH)
    gates_s = gates_s.transpose(2, 0, 4, 3, 1).reshape(B_KV_HEADS, S // QB, N_BRANCH, QW)
    o_b = _sel_attention(proj, qcol(3), hcol(6), hcol(7), near_b, c31_b, sel_t, gates_s,
                         ocmp_t, owin_t)

    on = _rmsnorm([o_a, o_b], out_norm_g)
    x2 = _outproj(on, w_out, x)

    hf = _rmsnorm([x2], norm_ffn_g)
    hmid = _ffn_a(hf, w_gate, w_up, conv_w.astype(F32), conv_b[None, :].astype(F32))
    return _ffn_b(hmid, w_down.astype(BF16), x2)


def kernel(x, rel_bias, norm_mix_g, w_in, a_q_norm_g, a_k_norm_g, a_sinks, b_q_norm_g, b_k_norm_g,
           cmp_pos_emb, cmp_w1, cmp_b1, cmp_w2, cmp_b2, out_norm_g, w_out, norm_ffn_g, w_gate, w_up,
           conv_w, conv_b, w_down):
    depth = w_in.shape[0]
    batch = x.shape[0]
    outs = []
    for b in range(batch):
        h = x[b]
        for l in range(depth):
            h = _layer(h, rel_bias, norm_mix_g[l], w_in[l], a_q_norm_g[l], a_k_norm_g[l], a_sinks[l],
                       b_q_norm_g[l], b_k_norm_g[l], cmp_pos_emb[l], cmp_w1[l], cmp_b1[l], cmp_w2[l],
                       cmp_b2[l], out_norm_g[l], w_out[l], norm_ffn_g[l], w_gate[l], w_up[l],
                       conv_w[l], conv_b[l], w_down[l])
        outs.append(h)
    return jnp.stack(outs)
```

```python
import functools
import math

import numpy as np
import jax
import jax.numpy as jnp
from jax import lax
from jax.experimental import pallas as pl
from jax.experimental.pallas import tpu as pltpu

F32 = jnp.float32
BF16 = jnp.bfloat16

HEAD_DIM = 128
A_Q_HEADS = 16
A_KV_HEADS = 4
B_Q_HEADS = 16
B_KV_HEADS = 4
GROUP = 4
A_WINDOW = 128
B_WINDOW = 512
CMP_BLOCK = 32
CMP_STRIDE = 16
SLC_BLOCK = 64
SLC_TOPK = 16
N_BRANCH = 3
N_BUCKETS = 32
MAX_DISTANCE = 128
EPS = 1e-6
NEG = -1e30
FORCE_SCORE = 1e6
SCALE = HEAD_DIM ** -0.5

QB = 128
CMP_WIN = 24
CMP_TAB = 40
VMEM_LIMIT = 56 * 1024 * 1024


def _cparams(sem, **kw):
    return pltpu.CompilerParams(dimension_semantics=sem, vmem_limit_bytes=VMEM_LIMIT, **kw)


def _t5_bucket_np(dist):
    n = np.maximum(dist, 0)
    max_exact = N_BUCKETS // 2
    nf = np.maximum(n, 1).astype(np.float32)
    large = max_exact + (np.log(nf / max_exact) / math.log(MAX_DISTANCE / max_exact)
                         * (N_BUCKETS - max_exact)).astype(np.int32)
    large = np.minimum(large, N_BUCKETS - 1)
    return np.where(n < max_exact, n, large).astype(np.int32)


def _bias_from_buckets(tab, idx):
    onehot = (idx[None] == np.arange(N_BUCKETS).reshape((-1,) + (1,) * idx.ndim)).astype(np.float32)
    return jnp.einsum('bh,b...->h...', tab.astype(F32), jnp.asarray(onehot),
                      precision=lax.Precision.HIGHEST)


def _near_bias_tables(tab):
    kk = np.arange(QB)[:, None]
    qq = np.arange(QB)[None, :]
    return _bias_from_buckets(tab, np.stack([_t5_bucket_np(qq - kk), _t5_bucket_np(QB + qq - kk)]))


def _cmp_bias_table(tab):
    npr = np.arange(CMP_TAB)[:, None] - 16
    qq = np.arange(QB)[None, :]
    return _bias_from_buckets(tab, _t5_bucket_np(qq - CMP_STRIDE * npr - (CMP_BLOCK - 1)))


def _rmsnorm_kernel(*refs, n_in):
    x_refs, g_ref, o_ref = refs[:n_in], refs[n_in], refs[n_in + 1]
    off = 0
    for x_ref in x_refs:
        x = x_ref[...]
        w = x.shape[-1]
        y = x * lax.rsqrt(jnp.mean(x * x, axis=-1, keepdims=True) + EPS)
        o_ref[:, off:off + w] = (y * g_ref[:, off:off + w]).astype(o_ref.dtype)
        off += w


def _rmsnorm(xs, gain, tr=256):
    S = xs[0].shape[0]
    widths = [x.shape[1] for x in xs]
    n = sum(widths)
    tr = min(tr, S)
    return pl.pallas_call(
        functools.partial(_rmsnorm_kernel, n_in=len(xs)),
        grid=(S // tr,),
        in_specs=[pl.BlockSpec((tr, w), lambda i: (i, 0)) for w in widths]
        + [pl.BlockSpec((1, n), lambda i: (0, 0))],
        out_specs=pl.BlockSpec((tr, n), lambda i: (i, 0)),
        out_shape=jax.ShapeDtypeStruct((S, n), BF16),
        compiler_params=_cparams(("arbitrary",)),
        name="rmsnorm",
    )(*xs, gain.reshape(1, n).astype(F32))


NORM_ROW_CHUNKS = 2


def _dot_wt(a, wt):
    return lax.dot_general(a, wt.astype(BF16), (((1,), (1,)), ((), ())), preferred_element_type=F32)


def _inproj_a_kernel(kind_ref, tile_ref, a_ref, w_ref, g_ref, o_ref, w_s):
    j = pl.program_id(0)

    @pl.when(pl.program_id(1) == 0)
    def _():
        w_s[...] = w_ref[...].astype(BF16)

    @pl.when(kind_ref[j] == 0)
    def _():
        o_ref[...] = _dot_wt(a_ref[...], w_s[...]).astype(o_ref.dtype)

    @pl.when(kind_ref[j] == 1)
    def _():
        g = g_ref[0]
        hm = a_ref.shape[0] // NORM_ROW_CHUNKS
        for c in range(NORM_ROW_CHUNKS):
            rows = slice(c * hm, (c + 1) * hm)
            acc = _dot_wt(a_ref[rows, :], w_s[...])
            for h in range(acc.shape[1] // HEAD_DIM):
                sl = acc[:, h * HEAD_DIM:(h + 1) * HEAD_DIM]
                y = sl * lax.rsqrt(jnp.mean(sl * sl, axis=-1, keepdims=True) + EPS)
                o_ref[rows, h * HEAD_DIM:(h + 1) * HEAD_DIM] = (y * g).astype(o_ref.dtype)


def _inproj_a(hn, wt, kinds, tiles, gains, tm=1024, tn=512):
    S, D = hn.shape
    n_tiles = kinds.shape[0]
    tm = min(tm, S)
    grid_spec = pltpu.PrefetchScalarGridSpec(
        num_scalar_prefetch=2,
        grid=(n_tiles, S // tm),
        in_specs=[pl.BlockSpec((tm, D), lambda j, i, k, t: (i, 0)),
                  pl.BlockSpec((tn, D), lambda j, i, k, t: (t[j], 0)),
                  pl.BlockSpec((1, 1, HEAD_DIM), lambda j, i, k, t: (j, 0, 0))],
        out_specs=pl.BlockSpec((tm, tn), lambda j, i, k, t: (i, j)),
        scratch_shapes=[pltpu.VMEM((tn, D), BF16)],
    )
    return pl.pallas_call(
        _inproj_a_kernel,
        grid_spec=grid_spec,
        out_shape=jax.ShapeDtypeStruct((S, n_tiles * tn), BF16),
        compiler_params=_cparams(("arbitrary", "arbitrary")),
        name="inproj_heads",
    )(kinds, tiles, hn, wt, gains)


def _inproj_b_kernel(a_ref, w_ref, wg_ref, o_ref, gate_ref):
    a = a_ref[...]
    o_ref[...] = _dot_wt(a, w_ref[...])

    @pl.when(pl.program_id(1) == 0)
    def _():
        gate_ref[...] = 1.0 / (1.0 + jnp.exp(-_dot_wt(a, wg_ref[...])))


def _inproj_b(hn, wt, tile0, n_tiles, wg_t, tm=1024, tn=512):
    S, D = hn.shape
    tm = min(tm, S)
    ng = wg_t.shape[0]
    return pl.pallas_call(
        _inproj_b_kernel,
        grid=(S // tm, n_tiles),
        in_specs=[pl.BlockSpec((tm, D), lambda i, j: (i, 0)),
                  pl.BlockSpec((tn, D), lambda i, j: (tile0 + j, 0)),
                  pl.BlockSpec((ng, D), lambda i, j: (0, 0))],
        out_specs=[pl.BlockSpec((tm, tn), lambda i, j: (i, j)),
                   pl.BlockSpec((tm, ng), lambda i, j: (i, 0))],
        out_shape=[jax.ShapeDtypeStruct((S, n_tiles * tn), F32),
                   jax.ShapeDtypeStruct((S, ng), F32)],
        compiler_params=_cparams(("arbitrary", "arbitrary")),
        name="inproj_cmp_gates",
    )(hn, wt, wg_t)


def _gelu_tanh(x):
    return 0.5 * x * (1.0 + jnp.tanh(math.sqrt(2.0 / math.pi) * (x + 0.044715 * (x * x * x))))


def _compress_kernel(tk_ref, tv_ref, pos_ref, w1_ref, b1_ref, w2_ref, b2_ref, gk_ref,
                     kc_ref, vct_ref):
    nc = kc_ref.shape[1]

    def mlp(t_ref, kv):
        a0 = jnp.zeros((nc, w1_ref.shape[-1]), F32)
        a1 = jnp.zeros((nc, w1_ref.shape[-1]), F32)
        for b in range(CMP_STRIDE):
            t = t_ref[pl.ds(b, nc, stride=CMP_STRIDE), :]
            lo, hi = b, CMP_STRIDE + b
            a0 = a0 + jnp.dot((t + pos_ref[kv, lo:lo + 1, :]).astype(BF16), w1_ref[kv, lo],
                              preferred_element_type=F32)
            a1 = a1 + jnp.dot((t + pos_ref[kv, hi:hi + 1, :]).astype(BF16), w1_ref[kv, hi],
                              preferred_element_type=F32)
        h = a0 + pltpu.roll(a1, nc - 1, 0) + b1_ref[kv]
        h = _gelu_tanh(h)
        return jnp.dot(h.astype(BF16), w2_ref[kv], preferred_element_type=F32) + b2_ref[kv]

    ck = mlp(tk_ref, 0)
    ck = ck * lax.rsqrt(jnp.mean(ck * ck, axis=-1, keepdims=True) + EPS) * gk_ref[...]
    kc_ref[0] = ck.astype(kc_ref.dtype)
    cv = mlp(tv_ref, 1)
    vct_ref[0] = cv.T.astype(vct_ref.dtype)


def _compress(kv32, G, pos, w1, b1, w2, b2, gk):
    S = kv32.shape[0]
    NC = S // CMP_STRIDE
    hid = w1.shape[-1]
    return pl.pallas_call(
        _compress_kernel,
        grid=(G,),
        in_specs=[pl.BlockSpec((S, HEAD_DIM), lambda g: (0, g)),
                  pl.BlockSpec((S, HEAD_DIM), lambda g: (0, G + g)),
                  pl.BlockSpec((2, CMP_BLOCK, HEAD_DIM), lambda g: (0, 0, 0)),
                  pl.BlockSpec((2, CMP_BLOCK, HEAD_DIM, hid), lambda g: (0, 0, 0, 0)),
                  pl.BlockSpec((2, 1, hid), lambda g: (0, 0, 0)),
                  pl.BlockSpec((2, hid, HEAD_DIM), lambda g: (0, 0, 0)),
                  pl.BlockSpec((2, 1, HEAD_DIM), lambda g: (0, 0, 0)),
                  pl.BlockSpec((1, HEAD_DIM), lambda g: (0, 0))],
        out_specs=[pl.BlockSpec((1, NC, HEAD_DIM), lambda g: (g, 0, 0)),
                   pl.BlockSpec((1, HEAD_DIM, NC), lambda g: (g, 0, 0))],
        out_shape=[jax.ShapeDtypeStruct((G, NC, HEAD_DIM), BF16),
                   jax.ShapeDtypeStruct((G, HEAD_DIM, NC), BF16)],
        compiler_params=_cparams(("arbitrary",)),
        name="compress",
    )(kv32, kv32, pos, w1, b1, w2, b2, gk)


QW = GROUP * QB


def _stacked_queries(q_ref):
    return jnp.concatenate([q_ref[:, r * HEAD_DIM:(r + 1) * HEAD_DIM] for r in range(GROUP)], axis=0)


LOG2E = math.log2(math.e)
KEY_SCALE = SCALE * LOG2E
QSUB = 4


def _scores(k_t, qs):
    return lax.dot_general(k_t, qs, (((1,), (1,)), ((), ())), preferred_element_type=F32)


def _rel_bias(tab, c31):
    return (tab - c31) * LOG2E


def _init_state(m_ref, l_ref, acc_ref):
    m_ref[...] = jnp.full(m_ref.shape, NEG, F32)
    l_ref[...] = jnp.zeros(l_ref.shape, F32)
    acc_ref[...] = jnp.zeros(acc_ref.shape, F32)


def _online_update(m_ref, l_ref, acc_ref, lanes, t, v_t):
    m_old = m_ref[:, lanes]
    m_new = jnp.maximum(m_old, jnp.max(t, axis=0, keepdims=True))
    alpha = jnp.exp2(m_old - m_new)
    p = jnp.exp2(t - m_new)
    l_ref[:, lanes] = alpha * l_ref[:, lanes] + jnp.sum(p, axis=0, keepdims=True)
    pv = lax.dot_general(v_t, p.astype(BF16), (((0,), (0,)), ((), ())), preferred_element_type=F32)
    acc_ref[:, lanes] = acc_ref[:, lanes] * alpha + pv
    m_ref[:, lanes] = m_new


def _key_query_iotas():
    kk = lax.broadcasted_iota(jnp.int32, (QB, QW), 0)
    qq = lax.broadcasted_iota(jnp.int32, (QB, QW), 1) & (QB - 1)
    return kk, qq


def _load_kv(k_ref, v_ref, kb0, n):
    rows = pl.ds(pl.multiple_of(kb0 * QB, QB), n * QB)
    return k_ref[rows, :], v_ref[rows, :]


def _store_heads(o_ref, o_t):
    for r in range(GROUP):
        o_ref[:, r * HEAD_DIM:(r + 1) * HEAD_DIM] = o_t[:, r * QB:(r + 1) * QB].T


def _stack_heads(a, G):
    a = a.reshape((G, GROUP) + a.shape[1:])
    a = jnp.moveaxis(a, 1, -2)
    return a.reshape(a.shape[:-2] + (QW,))


def _band_kernel(q_ref, k_ref, v_ref, tab_ref, c31_ref, sink_ref, o_ref, m_ref, l_ref, acc_ref,
                 *, nback, use_sinks, transposed_out, qsub):
    c0 = pl.program_id(1) * qsub
    _init_state(m_ref, l_ref, acc_ref)
    kk, qq = _key_query_iotas()
    c31 = c31_ref[0]

    def masked_scores(sub, kb0, deltas):
        k_t, v_t = _load_kv(k_ref, v_ref, kb0, len(deltas))
        s = _scores(k_t, _stacked_queries(q_ref.at[sub * QB:(sub + 1) * QB]))
        parts = []
        for t, delta in enumerate(deltas):
            sb = s[t * QB:(t + 1) * QB]
            if delta <= 1:
                sb = sb + _rel_bias(tab_ref[0, delta], c31)
            if delta == 0:
                sb = jnp.where(qq >= kk, sb, NEG)
            elif delta == nback:
                sb = jnp.where(qq < kk, sb, NEG)
            parts.append(sb)
        return (parts[0] if len(parts) == 1 else jnp.concatenate(parts, axis=0)), v_t

    def update(sub, s, v_t):
        _online_update(m_ref.at[sub], l_ref.at[sub], acc_ref.at[sub], slice(None), s, v_t)

    @pl.when(c0 >= nback)
    def _():
        band = list(range(nback, -1, -1))
        tiles = [masked_scores(sub, c0 + sub - nback, band) for sub in range(qsub)]
        for sub, (s, v_t) in enumerate(tiles):
            update(sub, s, v_t)

    @pl.when(c0 < nback)
    def _():
        for sub in range(qsub):
            for delta in range(nback, -1, -1):
                @pl.when(c0 + sub - delta >= 0)
                def _(sub=sub, delta=delta):
                    update(sub, *masked_scores(sub, c0 + sub - delta, [delta]))

    for sub in range(qsub):
        l = l_ref[sub]
        acc = acc_ref[sub]
        if use_sinks:
            m = m_ref[sub]
            sk = _rel_bias(sink_ref[0], c31)
            m_f = jnp.maximum(m, sk)
            a = jnp.exp2(m - m_f)
            l = l * a + jnp.exp2(sk - m_f)
            acc = acc * a
        o_t = acc * (1.0 / l)
        if transposed_out:
            o_ref[0, sub] = o_t
        else:
            _store_heads(o_ref.at[sub * QB:(sub + 1) * QB], o_t)


def _qkv_specs(S, q_col, k_col, v_col, q_rows=QB):
    return [pl.BlockSpec((q_rows, GROUP * HEAD_DIM), lambda g, c: (c, q_col + g)),
            pl.BlockSpec((S, HEAD_DIM), lambda g, c: (0, k_col + g)),
            pl.BlockSpec((S, HEAD_DIM), lambda g, c: (0, v_col + g))]


def _band_attention(proj, G, q_col, k_col, v_col, tabs, c31, sinks, window, transposed_out):
    S = proj.shape[0]
    nkb = S // QB
    nback = -(-(window - 1) // QB)
    use_sinks = sinks is not None
    if sinks is None:
        sinks = jnp.zeros_like(c31)
    qsub = min(nkb, QSUB * (2 if nback == 1 else 1))
    if transposed_out:
        out_shape = jax.ShapeDtypeStruct((G, nkb, HEAD_DIM, QW), F32)
        out_spec = pl.BlockSpec((1, qsub, HEAD_DIM, QW), lambda g, c: (g, c, 0, 0))
    else:
        out_shape = jax.ShapeDtypeStruct((S, G * GROUP * HEAD_DIM), F32)
        out_spec = pl.BlockSpec((qsub * QB, GROUP * HEAD_DIM), lambda g, c: (c, g))
    row = pl.BlockSpec((1, 1, QW), lambda g, c: (g, 0, 0))
    return pl.pallas_call(
        functools.partial(_band_kernel, nback=nback, use_sinks=use_sinks,
                          transposed_out=transposed_out, qsub=qsub),
        grid=(G, nkb // qsub),
        in_specs=_qkv_specs(S, q_col, k_col, v_col, qsub * QB)
        + [pl.BlockSpec((1, 2, QB, QW), lambda g, c: (g, 0, 0, 0)), row, row],
        out_specs=out_spec,
        out_shape=out_shape,
        scratch_shapes=[pltpu.VMEM((qsub, 1, QW), F32), pltpu.VMEM((qsub, 1, QW), F32),
                        pltpu.VMEM((qsub, HEAD_DIM, QW), F32)],
        compiler_params=_cparams(("arbitrary", "arbitrary")),
        name="band_attention_w%d" % window,
    )(proj, proj, proj, tabs, c31, sinks)


def _cmp_select_kernel(q_ref, kc_ref, vct_ref, tab_ref, c31_ref, ovl_ref, ot_ref, sel_ref, s_ref,
                       *, n_valid, topk):
    c0 = pl.program_id(1) * QSUB
    nc = kc_ref.shape[1]
    nb = ovl_ref.shape[0]
    rown = lax.broadcasted_iota(jnp.int32, (nc, QW), 0)
    wrow = lax.broadcasted_iota(jnp.int32, (CMP_WIN, QW), 0)
    wq = lax.broadcasted_iota(jnp.int32, (CMP_WIN, QW), 1) & (QB - 1)
    lane_q = lax.broadcasted_iota(jnp.int32, (1, QW), 1) & (QB - 1)

    for sub in range(QSUB):
        c = c0 + sub
        qs = _stacked_queries(q_ref.at[sub * QB:(sub + 1) * QB])
        w0 = pl.multiple_of(jnp.maximum(8 * c - 16, 0), 8)
        toff = pl.multiple_of(w0 - 8 * c + 16, 8)
        n_abs = w0 + wrow
        dist = (c * QB + wq) - (n_abs * CMP_STRIDE + (CMP_BLOCK - 1))
        valid_w = (dist >= 0) & (n_abs < n_valid)
        s_ref[sub] = jnp.where(rown < w0, _scores(kc_ref[0], qs), NEG)
        bias_w = _rel_bias(tab_ref[0, pl.ds(toff, CMP_WIN), :], c31_ref[0])
        s_w = _scores(kc_ref[0, pl.ds(w0, CMP_WIN), :], qs) + bias_w
        s_ref[sub, pl.ds(w0, CMP_WIN), :] = jnp.where(valid_w, s_w, NEG)

    p_groups = []
    for sub in range(QSUB):
        has_any = ((c0 + sub) * QB + lane_q >= CMP_BLOCK - 1).astype(F32)
        s = s_ref[sub]
        m = jnp.max(s, axis=0, keepdims=True)
        e = jnp.exp2(s - m)
        p = e * (has_any / jnp.sum(e, axis=0, keepdims=True))
        ot_ref[0, sub] = jnp.dot(vct_ref[0], p.astype(BF16), preferred_element_type=F32)
        p_grp = p[:, 0:QB]
        for r in range(1, GROUP):
            p_grp = p_grp + p[:, r * QB:(r + 1) * QB]
        p_groups.append(p_grp.astype(BF16))

    nq = QSUB * QB
    scores = jnp.dot(ovl_ref[...], jnp.concatenate(p_groups, axis=1), preferred_element_type=F32)
    blk = lax.broadcasted_iota(jnp.int32, (nb, nq), 0)
    cur = lax.shift_right_logical(c0 * QB + lax.broadcasted_iota(jnp.int32, (nb, nq), 1),
                                  int(math.log2(SLC_BLOCK)))
    forced = (blk == 0) | (blk == cur) | (blk == cur - 1)
    work = jnp.where(forced, FORCE_SCORE, jnp.where(blk <= cur, scores, -1.0))
    blk_f = blk.astype(F32)
    sel = jnp.zeros((nb, nq), F32)
    for _ in range(topk):
        mx = jnp.max(work, axis=0, keepdims=True)
        first = jnp.min(jnp.where(work == mx, blk_f, float(nb)), axis=0, keepdims=True)
        hit = blk_f == first
        sel = jnp.where(hit, 1.0, sel)
        work = jnp.where(hit, -3e38, work)
    sel_ref[0] = sel


def _cmp_select(proj, q_col, kc, vct, tab, c31, ovl_t, n_valid):
    nkb = proj.shape[0] // QB
    G, NC = kc.shape[0], kc.shape[1]
    NB = ovl_t.shape[0]
    return pl.pallas_call(
        functools.partial(_cmp_select_kernel, n_valid=n_valid, topk=min(SLC_TOPK, NB)),
        grid=(G, nkb // QSUB),
        in_specs=[pl.BlockSpec((QSUB * QB, GROUP * HEAD_DIM), lambda g, c: (c, q_col + g)),
                  pl.BlockSpec((1, NC, HEAD_DIM), lambda g, c: (g, 0, 0)),
                  pl.BlockSpec((1, HEAD_DIM, NC), lambda g, c: (g, 0, 0)),
                  pl.BlockSpec((1, CMP_TAB, QW), lambda g, c: (g, 0, 0)),
                  pl.BlockSpec((1, 1, QW), lambda g, c: (g, 0, 0)),
                  pl.BlockSpec((NB, NC), lambda g, c: (0, 0))],
        out_specs=[pl.BlockSpec((1, QSUB, HEAD_DIM, QW), lambda g, c: (g, c, 0, 0)),
                   pl.BlockSpec((1, NB, QSUB * QB), lambda g, c: (g, 0, c))],
        out_shape=[jax.ShapeDtypeStruct((G, nkb, HEAD_DIM, QW), F32),
                   jax.ShapeDtypeStruct((G, NB, nkb * QB), F32)],
        scratch_shapes=[pltpu.VMEM((QSUB, NC, QW), F32)],
        compiler_params=_cparams(("arbitrary", "arbitrary")),
        name="cmp_attention_select",
    )(proj, kc, vct, tab, c31, ovl_t)


FAR_BLOCKS = 4


def _sel_block(c, q_ref, sel_row, gate_ref, ocmp_ref, owin_ref, o_ref, k_ref, v_ref, tab_ref, c31,
               m_ref, l_ref, acc_ref, s_a, s_b, p_a, p_b, al_a, al_b, s_n):
    _init_state(m_ref, l_ref, acc_ref)
    kk, qq = _key_query_iotas()
    qs = _stacked_queries(q_ref)
    spb = QB // SLC_BLOCK
    tk = FAR_BLOCKS * QB
    last_tile = k_ref.shape[0] // tk - 1
    n_far = jnp.maximum(c - 1, 0)
    n_tiles = (n_far + FAR_BLOCKS - 1) // FAR_BLOCKS

    def head_row(r, row_limit=None, r_load=None):
        row = sel_row(r if r_load is None else r_load)
        if row_limit is not None:
            row = jnp.where(r < row_limit, row, 0.0)
        return jnp.concatenate([row] * GROUP, axis=1)

    def tile_rows(i):
        return pl.ds(pl.multiple_of(jnp.minimum(i, last_tile) * tk, tk), tk)

    def far_scores(i, s_buf):
        s = _scores(k_ref[tile_rows(i), :], qs)
        r0 = i * (FAR_BLOCKS * spb)
        r0_load = jnp.minimum(i, last_tile) * (FAR_BLOCKS * spb)
        for t in range(FAR_BLOCKS * spb):
            row = head_row(r0 + t, n_far * spb, r0_load + t)
            blk = slice(t * SLC_BLOCK, (t + 1) * SLC_BLOCK)
            s_buf[blk, :] = jnp.where(row > 0.5, s[blk], NEG)

    def far_softmax(s_buf, p_buf, al_buf):
        blocks = [slice(t * SLC_BLOCK, (t + 1) * SLC_BLOCK) for t in range(FAR_BLOCKS * spb)]
        sub8 = lambda a: a.reshape(SLC_BLOCK // 8, 8, QW)
        m_part = jnp.full((8, QW), NEG, F32)
        for blk in blocks:
            m_part = jnp.maximum(m_part, jnp.max(sub8(s_buf[blk, :]), axis=0))
        m_old = m_ref[...]
        m_new = jnp.maximum(m_old, jnp.max(m_part, axis=0, keepdims=True))
        alpha = jnp.exp2(m_old - m_new)
        l_part = jnp.zeros((8, QW), F32)
        for blk in blocks:
            p = jnp.exp2(s_buf[blk, :] - m_new)
            l_part = l_part + jnp.sum(sub8(p), axis=0)
            p_buf[blk, :] = p.astype(BF16)
        l_ref[...] = alpha * l_ref[...] + jnp.sum(l_part, axis=0, keepdims=True)
        m_ref[...] = m_new
        al_buf[...] = alpha

    def far_values(i, p_buf, al_buf):
        pv = lax.dot_general(v_ref[tile_rows(i), :], p_buf[...], (((0,), (0,)), ((), ())),
                             preferred_element_type=F32)
        acc_ref[...] = acc_ref[...] * al_buf[...] + pv

    far_scores(0, s_a)
    far_scores(1, s_b)
    for p_buf, al_buf in ((p_a, al_a), (p_b, al_b)):
        p_buf[...] = jnp.zeros(p_buf.shape, BF16)
        al_buf[...] = jnp.ones(al_buf.shape, F32)

    blocks = [(jnp.maximum(c - 1, 0), 1, c * spb), (c, 0, None)]
    near_rows = [pl.ds(pl.multiple_of(kb * QB, QB), QB) for kb, _, _ in blocks]
    s = _scores(jnp.concatenate([k_ref[r, :] for r in near_rows], axis=0), qs)
    for t, (kb, delta, limit) in enumerate(blocks):
        msk = jnp.concatenate(
            [jnp.broadcast_to(head_row(kb * spb + u, limit), (SLC_BLOCK, QW)) for u in range(spb)],
            axis=0) > 0.5
        if delta == 0:
            msk = msk & (qq >= kk)
        sb = s[t * QB:(t + 1) * QB] + _rel_bias(tab_ref[0, delta], c31)
        s_n[t * QB:(t + 1) * QB, :] = jnp.where(msk, sb, NEG)

    def pair_body(j, carry):
        i = 2 * j
        far_values(jnp.maximum(i - 2, 0), p_a, al_a)
        far_values(jnp.maximum(i - 1, 0), p_b, al_b)
        far_softmax(s_a, p_a, al_a)
        far_softmax(s_b, p_b, al_b)
        far_scores(i + 2, s_a)
        far_scores(i + 3, s_b)
        return carry

    n_pairs = (n_tiles + 1) // 2
    lax.fori_loop(0, n_pairs, pair_body, 0)
    far_values(jnp.maximum(2 * n_pairs - 2, 0), p_a, al_a)
    far_values(jnp.maximum(2 * n_pairs - 1, 0), p_b, al_b)

    v_near = jnp.concatenate([v_ref[r, :] for r in near_rows], axis=0)
    _online_update(m_ref, l_ref, acc_ref, slice(None), s_n[...], v_near)

    o_slc = acc_ref[...] * (1.0 / l_ref[...])
    o_t = gate_ref[0:1, :] * ocmp_ref[...] + gate_ref[1:2, :] * o_slc + gate_ref[2:3, :] * owin_ref[...]
    _store_heads(o_ref, o_t)


def _sel_kernel(q_ref, k_ref, v_ref, tab_ref, c31_ref, sel_ref, gate_ref, ocmp_ref, owin_ref, o_ref,
                *scratch):
    _sel_block(pl.program_id(1), q_ref, lambda r: sel_ref[0, pl.ds(r, 1), :],
               gate_ref.at[0, 0], ocmp_ref.at[0, 0], owin_ref.at[0, 0], o_ref,
               k_ref, v_ref, tab_ref, c31_ref[0], *scratch)


def _sel_attention(proj, q_col, k_col, v_col, tabs, c31, sel_t, gates_s, ocmp_t, owin_t):
    S = proj.shape[0]
    G, NB = sel_t.shape[0], sel_t.shape[1]
    nkb = S // QB
    nq = 1
    tile = pl.BlockSpec((1, nq, HEAD_DIM, QW), lambda g, c: (g, c, 0, 0))
    return pl.pallas_call(
        _sel_kernel,
        grid=(G, nkb // nq),
        in_specs=_qkv_specs(S, q_col, k_col, v_col, nq * QB)
        + [pl.BlockSpec((1, 2, QB, QW), lambda g, c: (g, 0, 0, 0)),
           pl.BlockSpec((1, 1, QW), lambda g, c: (g, 0, 0)),
           pl.BlockSpec((1, NB, nq * QB), lambda g, c: (g, 0, c)),
           pl.BlockSpec((1, nq, N_BRANCH, QW), lambda g, c: (g, c, 0, 0)),
           tile, tile],
        out_specs=pl.BlockSpec((nq * QB, GROUP * HEAD_DIM), lambda g, c: (c, g)),
        out_shape=jax.ShapeDtypeStruct((S, G * GROUP * HEAD_DIM), F32),
        scratch_shapes=[pltpu.VMEM((1, QW), F32), pltpu.VMEM((1, QW), F32),
                        pltpu.VMEM((HEAD_DIM, QW), F32)]
        + [pltpu.VMEM((FAR_BLOCKS * QB, QW), F32)] * 2
        + [pltpu.VMEM((FAR_BLOCKS * QB, QW), BF16)] * 2
        + [pltpu.VMEM((1, QW), F32)] * 2
        + [pltpu.VMEM((2 * QB, QW), F32)],
        compiler_params=_cparams(("arbitrary", "arbitrary")),
        name="selected_attention_combine",
    )(proj, proj, proj, tabs, c31, sel_t, gates_s, ocmp_t, owin_t)


def _outproj_kernel(a_ref, w_ref, x_ref, o_ref, w_s):
    @pl.when(pl.program_id(1) == 0)
    def _():
        w_s[...] = w_ref[...].astype(BF16)

    o_ref[...] = x_ref[...] + jnp.dot(a_ref[...], w_s[...], preferred_element_type=F32)


def _outproj(a, w, x, tm=1024, tn=512):
    S, D = a.shape
    N = w.shape[1]
    tm = min(tm, S)
    return pl.pallas_call(
        _outproj_kernel,
        grid=(N // tn, S // tm),
        in_specs=[pl.BlockSpec((tm, D), lambda j, i: (i, 0)),
                  pl.BlockSpec((D, tn), lambda j, i: (0, j)),
                  pl.BlockSpec((tm, tn), lambda j, i: (i, j))],
        out_specs=pl.BlockSpec((tm, tn), lambda j, i: (i, j)),
        out_shape=jax.ShapeDtypeStruct((S, N), F32),
        scratch_shapes=[pltpu.VMEM((D, tn), BF16)],
        compiler_params=_cparams(("arbitrary", "arbitrary")),
        name="outproj_residual",
    )(a, w, x)


HALO = 16
FFN_ROW_CHUNKS = 2


def _ffn_a_kernel(halo_ref, a_ref, wg_ref, wu_ref, cw_ref, cb_ref, o_ref, g_s):
    i = pl.program_id(0)
    tm = a_ref.shape[0]
    halo = halo_ref[...]
    halo = jnp.where(i > 0, halo, jnp.zeros_like(halo))
    wg = wg_ref[...].astype(BF16)
    wu = wu_ref[...].astype(BF16)
    hm = tm // FFN_ROW_CHUNKS
    for h in range(FFN_ROW_CHUNKS):
        a = a_ref[h * hm:(h + 1) * hm, :]
        g = g_s.at[h]
        if h == 0:
            g[...] = jnp.dot(jnp.concatenate([halo, a], axis=0), wg, preferred_element_type=F32)
        else:
            g[0:HALO, :] = g_s[h - 1, hm:hm + HALO, :]
            g[HALO:, :] = jnp.dot(a, wg, preferred_element_type=F32)
        up = jnp.dot(a, wu, preferred_element_type=F32)
        gext = g[...]
        g1 = pltpu.roll(gext, 1, 0)
        g2 = pltpu.roll(g1, 1, 0)
        y = cb_ref[...] + g2[HALO:] * cw_ref[0:1, :]
        y = y + g1[HALO:] * cw_ref[1:2, :]
        y = y + gext[HALO:] * cw_ref[2:3, :]
        act = y * (1.0 / (1.0 + jnp.exp(-y)))
        o_ref[h * hm:(h + 1) * hm, :] = (act * up).astype(o_ref.dtype)


def _ffn_a(hf, wg, wu, cw, cb, tm=2048, tf=256):
    S, D = hf.shape
    Fp = wg.shape[1]
    assert Fp % tf == 0
    tm = min(tm, S)
    hb = tm // HALO
    return pl.pallas_call(
        _ffn_a_kernel,
        grid=(S // tm, Fp // tf),
        in_specs=[pl.BlockSpec((HALO, D), lambda i, f: (jnp.maximum(i * hb - 1, 0), 0)),
                  pl.BlockSpec((tm, D), lambda i, f: (i, 0), pipeline_mode=pl.Buffered(1)),
                  pl.BlockSpec((D, tf), lambda i, f: (0, f)),
                  pl.BlockSpec((D, tf), lambda i, f: (0, f)),
                  pl.BlockSpec((cw.shape[0], tf), lambda i, f: (0, f)),
                  pl.BlockSpec((1, tf), lambda i, f: (0, f))],
        out_specs=pl.BlockSpec((tm, tf), lambda i, f: (i, f)),
        out_shape=jax.ShapeDtypeStruct((S, Fp), BF16),
        scratch_shapes=[pltpu.VMEM((FFN_ROW_CHUNKS, HALO + tm // FFN_ROW_CHUNKS, tf), F32)],
        compiler_params=_cparams(("arbitrary", "arbitrary")),
        name="ffn_gate_up",
    )(hf, hf, wg, wu, cw, cb)


def _ffn_b_kernel(a_ref, w_ref, x_ref, o_ref):
    o_ref[...] = x_ref[...] + jnp.dot(a_ref[...], w_ref[...], preferred_element_type=F32)


def _ffn_b(h, w, x, tm=512, tn=512):
    S, F = h.shape
    N = w.shape[1]
    tm = min(tm, S)
    return pl.pallas_call(
        _ffn_b_kernel,
        grid=(S // tm, N // tn),
        in_specs=[pl.BlockSpec((tm, F), lambda i, j: (i, 0)),
                  pl.BlockSpec((F, tn), lambda i, j: (0, j)),
                  pl.BlockSpec((tm, tn), lambda i, j: (i, j))],
        out_specs=pl.BlockSpec((tm, tn), lambda i, j: (i, j)),
        out_shape=jax.ShapeDtypeStruct((S, N), F32),
        compiler_params=_cparams(("arbitrary", "arbitrary")),
        name="ffn_down_residual",
    )(h, w, x)


def _layer(x, rel_bias, norm_mix_g, w_in, a_q_norm_g, a_k_norm_g, a_sinks, b_q_norm_g, b_k_norm_g,
           cmp_pos_emb, cmp_w1, cmp_b1, cmp_w2, cmp_b2, out_norm_g, w_out, norm_ffn_g, w_gate,
           w_up, conv_w, conv_b, w_down):
    S, D = x.shape
    aw = A_Q_HEADS * HEAD_DIM
    akv = A_KV_HEADS * HEAD_DIM
    bw = B_Q_HEADS * HEAD_DIM
    bkv = B_KV_HEADS * HEAD_DIM
    sizes = [aw, akv, akv, bw] + [bkv] * 6 + [N_BRANCH * B_Q_HEADS]
    offs = np.concatenate([[0], np.cumsum(sizes)]).tolist()
    wt = jnp.swapaxes(w_in, 0, 1)
    tn = 512
    order_a = [0, 1, 2, 3, 6, 7, 8, 9]
    n_gate = sizes[10]
    wg_t = jnp.pad(wt[offs[10]:offs[10] + n_gate], ((0, QB - n_gate), (0, 0)))
    one = jnp.ones((HEAD_DIM,), F32)
    seg_gain = {0: a_q_norm_g, 1: a_k_norm_g * KEY_SCALE, 3: b_q_norm_g,
                6: b_k_norm_g[1] * KEY_SCALE, 8: b_k_norm_g[2] * KEY_SCALE}
    kinds, tiles, gains, col = [], [], [], {}
    c0 = 0
    for k in order_a:
        col[k] = c0
        for t in range(sizes[k] // tn):
            kinds.append(1 if k in seg_gain else 0)
            tiles.append(offs[k] // tn + t)
            gains.append(seg_gain.get(k, one))
        c0 += sizes[k]
    kinds = jnp.asarray(kinds, jnp.int32)
    tiles = jnp.asarray(tiles, jnp.int32)
    gains = jnp.stack(gains).astype(F32).reshape(len(gains), 1, HEAD_DIM)

    hn = _rmsnorm([x], norm_mix_g)
    proj = _inproj_a(hn, wt, kinds, tiles, gains, tn=tn)
    kv32, gates = _inproj_b(hn, wt, offs[4] // tn, 2 * bkv // tn, wg_t)

    tab_a = rel_bias[:, :A_Q_HEADS]
    tab_b = rel_bias[:, A_Q_HEADS:]
    near_a = _stack_heads(_near_bias_tables(tab_a), A_KV_HEADS)
    near_b = _stack_heads(_near_bias_tables(tab_b), B_KV_HEADS)
    head_row = lambda v, G: _stack_heads(jnp.broadcast_to(v.astype(F32)[:, None, None],
                                                          (v.shape[0], 1, QB)), G)
    c31_a = head_row(tab_a[N_BUCKETS - 1], A_KV_HEADS)
    c31_b = head_row(tab_b[N_BUCKETS - 1], B_KV_HEADS)

    qcol = lambda k: col[k] // (GROUP * HEAD_DIM)
    hcol = lambda k: col[k] // HEAD_DIM
    o_a = _band_attention(proj, A_KV_HEADS, qcol(0), hcol(1), hcol(2), near_a, c31_a,
                          head_row(a_sinks, A_KV_HEADS), A_WINDOW, transposed_out=False)

    NC = S // CMP_STRIDE
    n_cmp = (S - CMP_BLOCK) // CMP_STRIDE + 1
    w1 = cmp_w1.reshape(2, CMP_BLOCK, HEAD_DIM, cmp_w1.shape[-1]).astype(BF16)
    kc, vct = _compress(kv32, B_KV_HEADS, cmp_pos_emb.astype(F32), w1, cmp_b1[:, None, :].astype(F32),
                        cmp_w2.astype(BF16), cmp_b2[:, None, :].astype(F32),
                        (b_k_norm_g[0] * KEY_SCALE).reshape(1, HEAD_DIM).astype(F32))

    NB = S // SLC_BLOCK
    ii = np.arange(NC)[None, :]
    jj = np.arange(NB)[:, None]
    ovl_t = ((ii * CMP_STRIDE <= jj * SLC_BLOCK + SLC_BLOCK - 1)
             & (ii * CMP_STRIDE + CMP_BLOCK - 1 >= jj * SLC_BLOCK) & (ii < n_cmp))
    ovl_t = jnp.asarray(ovl_t.astype(np.float32), BF16)
    ocmp_t, sel_t = _cmp_select(proj, qcol(3), kc, vct,
                                _stack_heads(_cmp_bias_table(tab_b), B_KV_HEADS), c31_b, ovl_t, n_cmp)

    owin_t = _band_attention(proj, B_KV_HEADS, qcol(3), hcol(8), hcol(9), near_b, c31_b, None,
                             B_WINDOW, transposed_out=True)
    gates_s = gates[:, :n_gate].reshape(S // QB, QB, B_KV_HEADS, GROUP, N_BRANCH)
    gates_s = gates_s.transpose(2, 0, 4, 3, 1).reshape(B_KV_HEADS, S // QB, N_BRANCH, QW)
    o_b = _sel_attention(proj, qcol(3), hcol(6), hcol(7), near_b, c31_b, sel_t, gates_s,
                         ocmp_t, owin_t)

    on = _rmsnorm([o_a, o_b], out_norm_g)
    x2 = _outproj(on, w_out, x)

    hf = _rmsnorm([x2], norm_ffn_g)
    hmid = _ffn_a(hf, w_gate, w_up, conv_w.astype(F32), conv_b[None, :].astype(F32))
    return _ffn_b(hmid, w_down.astype(BF16), x2)


def kernel(x, rel_bias, norm_mix_g, w_in, a_q_norm_g, a_k_norm_g, a_sinks, b_q_norm_g, b_k_norm_g,
           cmp_pos_emb, cmp_w1, cmp_b1, cmp_w2, cmp_b2, out_norm_g, w_out, norm_ffn_g, w_gate, w_up,
           conv_w, conv_b, w_down):
    depth = w_in.shape[0]
    batch = x.shape[0]
    outs = []
    for b in range(batch):
        h = x[b]
        for l in range(depth):
            h = _layer(h, rel_bias, norm_mix_g[l], w_in[l], a_q_norm_g[l], a_k_norm_g[l], a_sinks[l],
                       b_q_norm_g[l], b_k_norm_g[l], cmp_pos_emb[l], cmp_w1[l], cmp_b1[l], cmp_w2[l],
                       cmp_b2[l], out_norm_g[l], w_out[l], norm_ffn_g[l], w_gate[l], w_up[l],
                       conv_w[l], conv_b[l], w_down[l])
        outs.append(h)
    return jnp.stack(outs)
```

```python
import functools
import math

import numpy as np
import jax
import jax.numpy as jnp
from jax import lax
from jax.experimental import pallas as pl
from jax.experimental.pallas import tpu as pltpu

F32 = jnp.float32
BF16 = jnp.bfloat16

HEAD_DIM = 128
A_Q_HEADS = 16
A_KV_HEADS = 4
B_Q_HEADS = 16
B_KV_HEADS = 4
GROUP = 4
A_WINDOW = 128
B_WINDOW = 512
CMP_BLOCK = 32
CMP_STRIDE = 16
SLC_BLOCK = 64
SLC_TOPK = 16
N_BRANCH = 3
N_BUCKETS = 32
MAX_DISTANCE = 128
EPS = 1e-6
NEG = -1e30
FORCE_SCORE = 1e6
SCALE = HEAD_DIM ** -0.5

QB = 128
CMP_WIN = 24
CMP_TAB = 40
VMEM_LIMIT = 56 * 1024 * 1024


def _cparams(sem, **kw):
    return pltpu.CompilerParams(dimension_semantics=sem, vmem_limit_bytes=VMEM_LIMIT, **kw)


def _t5_bucket_np(dist):
    n = np.maximum(dist, 0)
    max_exact = N_BUCKETS // 2
    nf = np.maximum(n, 1).astype(np.float32)
    large = max_exact + (np.log(nf / max_exact) / math.log(MAX_DISTANCE / max_exact)
                         * (N_BUCKETS - max_exact)).astype(np.int32)
    large = np.minimum(large, N_BUCKETS - 1)
    return np.where(n < max_exact, n, large).astype(np.int32)


def _bias_from_buckets(tab, idx):
    onehot = (idx[None] == np.arange(N_BUCKETS).reshape((-1,) + (1,) * idx.ndim)).astype(np.float32)
    return jnp.einsum('bh,b...->h...', tab.astype(F32), jnp.asarray(onehot),
                      precision=lax.Precision.HIGHEST)


def _near_bias_tables(tab):
    kk = np.arange(QB)[:, None]
    qq = np.arange(QB)[None, :]
    return _bias_from_buckets(tab, np.stack([_t5_bucket_np(qq - kk), _t5_bucket_np(QB + qq - kk)]))


def _cmp_bias_table(tab):
    npr = np.arange(CMP_TAB)[:, None] - 16
    qq = np.arange(QB)[None, :]
    return _bias_from_buckets(tab, _t5_bucket_np(qq - CMP_STRIDE * npr - (CMP_BLOCK - 1)))


def _rmsnorm_kernel(*refs, n_in):
    x_refs, g_ref, o_ref = refs[:n_in], refs[n_in], refs[n_in + 1]
    off = 0
    for x_ref in x_refs:
        x = x_ref[...]
        w = x.shape[-1]
        y = x * lax.rsqrt(jnp.mean(x * x, axis=-1, keepdims=True) + EPS)
        o_ref[:, off:off + w] = (y * g_ref[:, off:off + w]).astype(o_ref.dtype)
        off += w


def _rmsnorm(xs, gain, tr=256):
    S = xs[0].shape[0]
    widths = [x.shape[1] for x in xs]
    n = sum(widths)
    tr = min(tr, S)
    return pl.pallas_call(
        functools.partial(_rmsnorm_kernel, n_in=len(xs)),
        grid=(S // tr,),
        in_specs=[pl.BlockSpec((tr, w), lambda i: (i, 0)) for w in widths]
        + [pl.BlockSpec((1, n), lambda i: (0, 0))],
        out_specs=pl.BlockSpec((tr, n), lambda i: (i, 0)),
        out_shape=jax.ShapeDtypeStruct((S, n), BF16),
        compiler_params=_cparams(("arbitrary",)),
        name="rmsnorm",
    )(*xs, gain.reshape(1, n).astype(F32))


NORM_ROW_CHUNKS = 2


def _dot_wt(a, wt):
    return lax.dot_general(a, wt.astype(BF16), (((1,), (1,)), ((), ())), preferred_element_type=F32)


def _inproj_a_kernel(kind_ref, tile_ref, a_ref, w_ref, g_ref, o_ref, w_s):
    j = pl.program_id(0)

    @pl.when(pl.program_id(1) == 0)
    def _():
        w_s[...] = w_ref[...].astype(BF16)

    @pl.when(kind_ref[j] == 0)
    def _():
        o_ref[...] = _dot_wt(a_ref[...], w_s[...]).astype(o_ref.dtype)

    @pl.when(kind_ref[j] == 1)
    def _():
        g = g_ref[0]
        hm = a_ref.shape[0] // NORM_ROW_CHUNKS
        for c in range(NORM_ROW_CHUNKS):
            rows = slice(c * hm, (c + 1) * hm)
            acc = _dot_wt(a_ref[rows, :], w_s[...])
            for h in range(acc.shape[1] // HEAD_DIM):
                sl = acc[:, h * HEAD_DIM:(h + 1) * HEAD_DIM]
                y = sl * lax.rsqrt(jnp.mean(sl * sl, axis=-1, keepdims=True) + EPS)
                o_ref[rows, h * HEAD_DIM:(h + 1) * HEAD_DIM] = (y * g).astype(o_ref.dtype)


def _inproj_a(hn, wt, kinds, tiles, gains, tm=1024, tn=512):
    S, D = hn.shape
    n_tiles = kinds.shape[0]
    tm = min(tm, S)
    grid_spec = pltpu.PrefetchScalarGridSpec(
        num_scalar_prefetch=2,
        grid=(n_tiles, S // tm),
        in_specs=[pl.BlockSpec((tm, D), lambda j, i, k, t: (i, 0)),
                  pl.BlockSpec((tn, D), lambda j, i, k, t: (t[j], 0)),
                  pl.BlockSpec((1, 1, HEAD_DIM), lambda j, i, k, t: (j, 0, 0))],
        out_specs=pl.BlockSpec((tm, tn), lambda j, i, k, t: (i, j)),
        scratch_shapes=[pltpu.VMEM((tn, D), BF16)],
    )
    return pl.pallas_call(
        _inproj_a_kernel,
        grid_spec=grid_spec,
        out_shape=jax.ShapeDtypeStruct((S, n_tiles * tn), BF16),
        compiler_params=_cparams(("arbitrary", "arbitrary")),
        name="inproj_heads",
    )(kinds, tiles, hn, wt, gains)


def _inproj_b_kernel(a_ref, w_ref, wg_ref, o_ref, gate_ref):
    a = a_ref[...]
    o_ref[...] = _dot_wt(a, w_ref[...])

    @pl.when(pl.program_id(1) == 0)
    def _():
        gate_ref[...] = 1.0 / (1.0 + jnp.exp(-_dot_wt(a, wg_ref[...])))


def _inproj_b(hn, wt, tile0, n_tiles, wg_t, tm=1024, tn=512):
    S, D = hn.shape
    tm = min(tm, S)
    ng = wg_t.shape[0]
    return pl.pallas_call(
        _inproj_b_kernel,
        grid=(S // tm, n_tiles),
        in_specs=[pl.BlockSpec((tm, D), lambda i, j: (i, 0)),
                  pl.BlockSpec((tn, D), lambda i, j: (tile0 + j, 0)),
                  pl.BlockSpec((ng, D), lambda i, j: (0, 0))],
        out_specs=[pl.BlockSpec((tm, tn), lambda i, j: (i, j)),
                   pl.BlockSpec((tm, ng), lambda i, j: (i, 0))],
        out_shape=[jax.ShapeDtypeStruct((S, n_tiles * tn), F32),
                   jax.ShapeDtypeStruct((S, ng), F32)],
        compiler_params=_cparams(("arbitrary", "arbitrary")),
        name="inproj_cmp_gates",
    )(hn, wt, wg_t)


def _gelu_tanh(x):
    return 0.5 * x * (1.0 + jnp.tanh(math.sqrt(2.0 / math.pi) * (x + 0.044715 * (x * x * x))))


def _compress_kernel(tk_ref, tv_ref, pos_ref, w1_ref, b1_ref, w2_ref, b2_ref, gk_ref,
                     kc_ref, vct_ref):
    nc = kc_ref.shape[1]

    def mlp(t_ref, kv):
        a0 = jnp.zeros((nc, w1_ref.shape[-1]), F32)
        a1 = jnp.zeros((nc, w1_ref.shape[-1]), F32)
        for b in range(CMP_STRIDE):
            t = t_ref[pl.ds(b, nc, stride=CMP_STRIDE), :]
            lo, hi = b, CMP_STRIDE + b
            a0 = a0 + jnp.dot((t + pos_ref[kv, lo:lo + 1, :]).astype(BF16), w1_ref[kv, lo],
                              preferred_element_type=F32)
            a1 = a1 + jnp.dot((t + pos_ref[kv, hi:hi + 1, :]).astype(BF16), w1_ref[kv, hi],
                              preferred_element_type=F32)
        h = a0 + pltpu.roll(a1, nc - 1, 0) + b1_ref[kv]
        h = _gelu_tanh(h)
        return jnp.dot(h.astype(BF16), w2_ref[kv], preferred_element_type=F32) + b2_ref[kv]

    ck = mlp(tk_ref, 0)
    ck = ck * lax.rsqrt(jnp.mean(ck * ck, axis=-1, keepdims=True) + EPS) * gk_ref[...]
    kc_ref[0] = ck.astype(kc_ref.dtype)
    cv = mlp(tv_ref, 1)
    vct_ref[0] = cv.T.astype(vct_ref.dtype)


def _compress(kv32, G, pos, w1, b1, w2, b2, gk):
    S = kv32.shape[0]
    NC = S // CMP_STRIDE
    hid = w1.shape[-1]
    return pl.pallas_call(
        _compress_kernel,
        grid=(G,),
        in_specs=[pl.BlockSpec((S, HEAD_DIM), lambda g: (0, g)),
                  pl.BlockSpec((S, HEAD_DIM), lambda g: (0, G + g)),
                  pl.BlockSpec((2, CMP_BLOCK, HEAD_DIM), lambda g: (0, 0, 0)),
                  pl.BlockSpec((2, CMP_BLOCK, HEAD_DIM, hid), lambda g: (0, 0, 0, 0)),
                  pl.BlockSpec((2, 1, hid), lambda g: (0, 0, 0)),
                  pl.BlockSpec((2, hid, HEAD_DIM), lambda g: (0, 0, 0)),
                  pl.BlockSpec((2, 1, HEAD_DIM), lambda g: (0, 0, 0)),
                  pl.BlockSpec((1, HEAD_DIM), lambda g: (0, 0))],
        out_specs=[pl.BlockSpec((1, NC, HEAD_DIM), lambda g: (g, 0, 0)),
                   pl.BlockSpec((1, HEAD_DIM, NC), lambda g: (g, 0, 0))],
        out_shape=[jax.ShapeDtypeStruct((G, NC, HEAD_DIM), BF16),
                   jax.ShapeDtypeStruct((G, HEAD_DIM, NC), BF16)],
        compiler_params=_cparams(("arbitrary",)),
        name="compress",
    )(kv32, kv32, pos, w1, b1, w2, b2, gk)


QW = GROUP * QB


def _stacked_queries(q_ref):
    return jnp.concatenate([q_ref[:, r * HEAD_DIM:(r + 1) * HEAD_DIM] for r in range(GROUP)], axis=0)


LOG2E = math.log2(math.e)
KEY_SCALE = SCALE * LOG2E
QSUB = 8


def _scores(k_t, qs):
    return lax.dot_general(k_t, qs, (((1,), (1,)), ((), ())), preferred_element_type=F32)


def _rel_bias(tab, c31):
    return (tab - c31) * LOG2E


def _init_state(m_ref, l_ref, acc_ref):
    m_ref[...] = jnp.full(m_ref.shape, NEG, F32)
    l_ref[...] = jnp.zeros(l_ref.shape, F32)
    acc_ref[...] = jnp.zeros(acc_ref.shape, F32)


def _online_update(m_ref, l_ref, acc_ref, lanes, t, v_t):
    m_old = m_ref[:, lanes]
    m_new = jnp.maximum(m_old, jnp.max(t, axis=0, keepdims=True))
    alpha = jnp.exp2(m_old - m_new)
    p = jnp.exp2(t - m_new)
    l_ref[:, lanes] = alpha * l_ref[:, lanes] + jnp.sum(p, axis=0, keepdims=True)
    pv = lax.dot_general(v_t, p.astype(BF16), (((0,), (0,)), ((), ())), preferred_element_type=F32)
    acc_ref[:, lanes] = acc_ref[:, lanes] * alpha + pv
    m_ref[:, lanes] = m_new


def _key_query_iotas():
    kk = lax.broadcasted_iota(jnp.int32, (QB, QW), 0)
    qq = lax.broadcasted_iota(jnp.int32, (QB, QW), 1) & (QB - 1)
    return kk, qq


def _load_kv(k_ref, v_ref, kb0, n):
    rows = pl.ds(pl.multiple_of(kb0 * QB, QB), n * QB)
    return k_ref[rows, :], v_ref[rows, :]


def _store_heads(o_ref, o_t):
    for r in range(GROUP):
        o_ref[:, r * HEAD_DIM:(r + 1) * HEAD_DIM] = o_t[:, r * QB:(r + 1) * QB].T


def _stack_heads(a, G):
    a = a.reshape((G, GROUP) + a.shape[1:])
    a = jnp.moveaxis(a, 1, -2)
    return a.reshape(a.shape[:-2] + (QW,))


def _band_kernel(q_ref, k_ref, v_ref, tab_ref, c31_ref, sink_ref, o_ref, m_ref, l_ref, acc_ref,
                 *, nback, use_sinks, transposed_out, qsub):
    c0 = pl.program_id(1) * qsub
    _init_state(m_ref, l_ref, acc_ref)
    kk, qq = _key_query_iotas()
    c31 = c31_ref[0]

    def masked_scores(sub, kb0, deltas):
        k_t, v_t = _load_kv(k_ref, v_ref, kb0, len(deltas))
        s = _scores(k_t, _stacked_queries(q_ref.at[sub * QB:(sub + 1) * QB]))
        parts = []
        for t, delta in enumerate(deltas):
            sb = s[t * QB:(t + 1) * QB]
            if delta <= 1:
                sb = sb + _rel_bias(tab_ref[0, delta], c31)
            if delta == 0:
                sb = jnp.where(qq >= kk, sb, NEG)
            elif delta == nback:
                sb = jnp.where(qq < kk, sb, NEG)
            parts.append(sb)
        return (parts[0] if len(parts) == 1 else jnp.concatenate(parts, axis=0)), v_t

    def update(sub, s, v_t):
        _online_update(m_ref.at[sub], l_ref.at[sub], acc_ref.at[sub], slice(None), s, v_t)

    @pl.when(c0 >= nback)
    def _():
        band = list(range(nback, -1, -1))
        tiles = [masked_scores(sub, c0 + sub - nback, band) for sub in range(qsub)]
        for sub, (s, v_t) in enumerate(tiles):
            update(sub, s, v_t)

    @pl.when(c0 < nback)
    def _():
        for sub in range(qsub):
            for delta in range(nback, -1, -1):
                @pl.when(c0 + sub - delta >= 0)
                def _(sub=sub, delta=delta):
                    update(sub, *masked_scores(sub, c0 + sub - delta, [delta]))

    for sub in range(qsub):
        l = l_ref[sub]
        acc = acc_ref[sub]
        if use_sinks:
            m = m_ref[sub]
            sk = _rel_bias(sink_ref[0], c31)
            m_f = jnp.maximum(m, sk)
            a = jnp.exp2(m - m_f)
            l = l * a + jnp.exp2(sk - m_f)
            acc = acc * a
        o_t = acc * (1.0 / l)
        if transposed_out:
            o_ref[0, sub] = o_t
        else:
            _store_heads(o_ref.at[sub * QB:(sub + 1) * QB], o_t)


def _qkv_specs(S, q_col, k_col, v_col, q_rows=QB):
    return [pl.BlockSpec((q_rows, GROUP * HEAD_DIM), lambda g, c: (c, q_col + g)),
            pl.BlockSpec((S, HEAD_DIM), lambda g, c: (0, k_col + g)),
            pl.BlockSpec((S, HEAD_DIM), lambda g, c: (0, v_col + g))]


def _band_attention(proj, G, q_col, k_col, v_col, tabs, c31, sinks, window, transposed_out):
    S = proj.shape[0]
    nkb = S // QB
    nback = -(-(window - 1) // QB)
    use_sinks = sinks is not None
    if sinks is None:
        sinks = jnp.zeros_like(c31)
    qsub = min(nkb, QSUB * (2 if nback == 1 else 1))
    if transposed_out:
        out_shape = jax.ShapeDtypeStruct((G, nkb, HEAD_DIM, QW), F32)
        out_spec = pl.BlockSpec((1, qsub, HEAD_DIM, QW), lambda g, c: (g, c, 0, 0))
    else:
        out_shape = jax.ShapeDtypeStruct((S, G * GROUP * HEAD_DIM), F32)
        out_spec = pl.BlockSpec((qsub * QB, GROUP * HEAD_DIM), lambda g, c: (c, g))
    row = pl.BlockSpec((1, 1, QW), lambda g, c: (g, 0, 0))
    return pl.pallas_call(
        functools.partial(_band_kernel, nback=nback, use_sinks=use_sinks,
                          transposed_out=transposed_out, qsub=qsub),
        grid=(G, nkb // qsub),
        in_specs=_qkv_specs(S, q_col, k_col, v_col, qsub * QB)
        + [pl.BlockSpec((1, 2, QB, QW), lambda g, c: (g, 0, 0, 0)), row, row],
        out_specs=out_spec,
        out_shape=out_shape,
        scratch_shapes=[pltpu.VMEM((qsub, 1, QW), F32), pltpu.VMEM((qsub, 1, QW), F32),
                        pltpu.VMEM((qsub, HEAD_DIM, QW), F32)],
        compiler_params=_cparams(("arbitrary", "arbitrary")),
        name="band_attention_w%d" % window,
    )(proj, proj, proj, tabs, c31, sinks)


def _cmp_select_kernel(q_ref, kc_ref, vct_ref, tab_ref, c31_ref, ovl_ref, ot_ref, sel_ref, s_ref,
                       *, n_valid, topk):
    c0 = pl.program_id(1) * QSUB
    nc = kc_ref.shape[1]
    nb = ovl_ref.shape[0]
    rown = lax.broadcasted_iota(jnp.int32, (nc, QW), 0)
    wrow = lax.broadcasted_iota(jnp.int32, (CMP_WIN, QW), 0)
    wq = lax.broadcasted_iota(jnp.int32, (CMP_WIN, QW), 1) & (QB - 1)
    lane_q = lax.broadcasted_iota(jnp.int32, (1, QW), 1) & (QB - 1)

    for sub in range(QSUB):
        c = c0 + sub
        qs = _stacked_queries(q_ref.at[sub * QB:(sub + 1) * QB])
        w0 = pl.multiple_of(jnp.maximum(8 * c - 16, 0), 8)
        toff = pl.multiple_of(w0 - 8 * c + 16, 8)
        n_abs = w0 + wrow
        dist = (c * QB + wq) - (n_abs * CMP_STRIDE + (CMP_BLOCK - 1))
        valid_w = (dist >= 0) & (n_abs < n_valid)
        s_ref[sub] = jnp.where(rown < w0, _scores(kc_ref[0], qs), NEG)
        bias_w = _rel_bias(tab_ref[0, pl.ds(toff, CMP_WIN), :], c31_ref[0])
        s_w = _scores(kc_ref[0, pl.ds(w0, CMP_WIN), :], qs) + bias_w
        s_ref[sub, pl.ds(w0, CMP_WIN), :] = jnp.where(valid_w, s_w, NEG)

    p_groups = []
    for sub in range(QSUB):
        has_any = ((c0 + sub) * QB + lane_q >= CMP_BLOCK - 1).astype(F32)
        s = s_ref[sub]
        m = jnp.max(s, axis=0, keepdims=True)
        e = jnp.exp2(s - m)
        p = e * (has_any / jnp.sum(e, axis=0, keepdims=True))
        ot_ref[0, sub] = jnp.dot(vct_ref[0], p.astype(BF16), preferred_element_type=F32)
        p_grp = p[:, 0:QB]
        for r in range(1, GROUP):
            p_grp = p_grp + p[:, r * QB:(r + 1) * QB]
        p_groups.append(p_grp.astype(BF16))

    nq = QSUB * QB
    scores = jnp.dot(ovl_ref[...], jnp.concatenate(p_groups, axis=1), preferred_element_type=F32)
    blk = lax.broadcasted_iota(jnp.int32, (nb, nq), 0)
    cur = lax.shift_right_logical(c0 * QB + lax.broadcasted_iota(jnp.int32, (nb, nq), 1),
                                  int(math.log2(SLC_BLOCK)))
    forced = (blk == 0) | (blk == cur) | (blk == cur - 1)
    work = jnp.where(forced, FORCE_SCORE, jnp.where(blk <= cur, scores, -1.0))
    blk_f = blk.astype(F32)
    sel = jnp.zeros((nb, nq), F32)
    for _ in range(topk):
        mx = jnp.max(work, axis=0, keepdims=True)
        first = jnp.min(jnp.where(work == mx, blk_f, float(nb)), axis=0, keepdims=True)
        hit = blk_f == first
        sel = jnp.where(hit, 1.0, sel)
        work = jnp.where(hit, -3e38, work)
    sel_ref[0] = sel


def _cmp_select(proj, q_col, kc, vct, tab, c31, ovl_t, n_valid):
    nkb = proj.shape[0] // QB
    G, NC = kc.shape[0], kc.shape[1]
    NB = ovl_t.shape[0]
    return pl.pallas_call(
        functools.partial(_cmp_select_kernel, n_valid=n_valid, topk=min(SLC_TOPK, NB)),
        grid=(G, nkb // QSUB),
        in_specs=[pl.BlockSpec((QSUB * QB, GROUP * HEAD_DIM), lambda g, c: (c, q_col + g)),
                  pl.BlockSpec((1, NC, HEAD_DIM), lambda g, c: (g, 0, 0)),
                  pl.BlockSpec((1, HEAD_DIM, NC), lambda g, c: (g, 0, 0)),
                  pl.BlockSpec((1, CMP_TAB, QW), lambda g, c: (g, 0, 0)),
                  pl.BlockSpec((1, 1, QW), lambda g, c: (g, 0, 0)),
                  pl.BlockSpec((NB, NC), lambda g, c: (0, 0))],
        out_specs=[pl.BlockSpec((1, QSUB, HEAD_DIM, QW), lambda g, c: (g, c, 0, 0)),
                   pl.BlockSpec((1, NB, QSUB * QB), lambda g, c: (g, 0, c))],
        out_shape=[jax.ShapeDtypeStruct((G, nkb, HEAD_DIM, QW), F32),
                   jax.ShapeDtypeStruct((G, NB, nkb * QB), F32)],
        scratch_shapes=[pltpu.VMEM((QSUB, NC, QW), F32)],
        compiler_params=_cparams(("arbitrary", "arbitrary")),
        name="cmp_attention_select",
    )(proj, kc, vct, tab, c31, ovl_t)


FAR_BLOCKS = 4


def _sel_block(c, q_ref, sel_row, gate_ref, ocmp_ref, owin_ref, o_ref, k_ref, v_ref, tab_ref, c31,
               m_ref, l_ref, acc_ref, s_a, s_b, p_a, p_b, al_a, al_b, s_n):
    _init_state(m_ref, l_ref, acc_ref)
    kk, qq = _key_query_iotas()
    qs = _stacked_queries(q_ref)
    spb = QB // SLC_BLOCK
    tk = FAR_BLOCKS * QB
    last_tile = k_ref.shape[0] // tk - 1
    n_far = jnp.maximum(c - 1, 0)
    n_tiles = (n_far + FAR_BLOCKS - 1) // FAR_BLOCKS

    def head_row(r, row_limit=None, r_load=None):
        row = sel_row(r if r_load is None else r_load)
        if row_limit is not None:
            row = jnp.where(r < row_limit, row, 0.0)
        return jnp.concatenate([row] * GROUP, axis=1)

    def tile_rows(i):
        return pl.ds(pl.multiple_of(jnp.minimum(i, last_tile) * tk, tk), tk)

    def far_scores(i, s_buf):
        s = _scores(k_ref[tile_rows(i), :], qs)
        r0 = i * (FAR_BLOCKS * spb)
        r0_load = jnp.minimum(i, last_tile) * (FAR_BLOCKS * spb)
        for t in range(FAR_BLOCKS * spb):
            row = head_row(r0 + t, n_far * spb, r0_load + t)
            blk = slice(t * SLC_BLOCK, (t + 1) * SLC_BLOCK)
            s_buf[blk, :] = jnp.where(row > 0.5, s[blk], NEG)

    def far_softmax(s_buf, p_buf, al_buf):
        blocks = [slice(t * SLC_BLOCK, (t + 1) * SLC_BLOCK) for t in range(FAR_BLOCKS * spb)]
        sub8 = lambda a: a.reshape(SLC_BLOCK // 8, 8, QW)
        m_part = jnp.full((8, QW), NEG, F32)
        for blk in blocks:
            m_part = jnp.maximum(m_part, jnp.max(sub8(s_buf[blk, :]), axis=0))
        m_old = m_ref[...]
        m_new = jnp.maximum(m_old, jnp.max(m_part, axis=0, keepdims=True))
        alpha = jnp.exp2(m_old - m_new)
        l_part = jnp.zeros((8, QW), F32)
        for blk in blocks:
            p = jnp.exp2(s_buf[blk, :] - m_new)
            l_part = l_part + jnp.sum(sub8(p), axis=0)
            p_buf[blk, :] = p.astype(BF16)
        l_ref[...] = alpha * l_ref[...] + jnp.sum(l_part, axis=0, keepdims=True)
        m_ref[...] = m_new
        al_buf[...] = alpha

    def far_values(i, p_buf, al_buf):
        pv = lax.dot_general(v_ref[tile_rows(i), :], p_buf[...], (((0,), (0,)), ((), ())),
                             preferred_element_type=F32)
        acc_ref[...] = acc_ref[...] * al_buf[...] + pv

    far_scores(0, s_a)
    far_scores(1, s_b)
    for p_buf, al_buf in ((p_a, al_a), (p_b, al_b)):
        p_buf[...] = jnp.zeros(p_buf.shape, BF16)
        al_buf[...] = jnp.ones(al_buf.shape, F32)

    blocks = [(jnp.maximum(c - 1, 0), 1, c * spb), (c, 0, None)]
    near_rows = [pl.ds(pl.multiple_of(kb * QB, QB), QB) for kb, _, _ in blocks]
    s = _scores(jnp.concatenate([k_ref[r, :] for r in near_rows], axis=0), qs)
    for t, (kb, delta, limit) in enumerate(blocks):
        msk = jnp.concatenate(
            [jnp.broadcast_to(head_row(kb * spb + u, limit), (SLC_BLOCK, QW)) for u in range(spb)],
            axis=0) > 0.5
        if delta == 0:
            msk = msk & (qq >= kk)
        sb = s[t * QB:(t + 1) * QB] + _rel_bias(tab_ref[0, delta], c31)
        s_n[t * QB:(t + 1) * QB, :] = jnp.where(msk, sb, NEG)

    def pair_body(j, carry):
        i = 2 * j
        far_values(jnp.maximum(i - 2, 0), p_a, al_a)
        far_values(jnp.maximum(i - 1, 0), p_b, al_b)
        far_softmax(s_a, p_a, al_a)
        far_softmax(s_b, p_b, al_b)
        far_scores(i + 2, s_a)
        far_scores(i + 3, s_b)
        return carry

    n_pairs = (n_tiles + 1) // 2
    lax.fori_loop(0, n_pairs, pair_body, 0)
    far_values(jnp.maximum(2 * n_pairs - 2, 0), p_a, al_a)
    far_values(jnp.maximum(2 * n_pairs - 1, 0), p_b, al_b)

    v_near = jnp.concatenate([v_ref[r, :] for r in near_rows], axis=0)
    _online_update(m_ref, l_ref, acc_ref, slice(None), s_n[...], v_near)

    o_slc = acc_ref[...] * (1.0 / l_ref[...])
    o_t = gate_ref[0:1, :] * ocmp_ref[...] + gate_ref[1:2, :] * o_slc + gate_ref[2:3, :] * owin_ref[...]
    _store_heads(o_ref, o_t)


def _sel_kernel(q_ref, k_ref, v_ref, tab_ref, c31_ref, sel_ref, gate_ref, ocmp_ref, owin_ref, o_ref,
                *scratch):
    _sel_block(pl.program_id(1), q_ref, lambda r: sel_ref[0, pl.ds(r, 1), :],
               gate_ref.at[0, 0], ocmp_ref.at[0, 0], owin_ref.at[0, 0], o_ref,
               k_ref, v_ref, tab_ref, c31_ref[0], *scratch)


def _sel_attention(proj, q_col, k_col, v_col, tabs, c31, sel_t, gates_s, ocmp_t, owin_t):
    S = proj.shape[0]
    G, NB = sel_t.shape[0], sel_t.shape[1]
    nkb = S // QB
    nq = 1
    tile = pl.BlockSpec((1, nq, HEAD_DIM, QW), lambda g, c: (g, c, 0, 0))
    return pl.pallas_call(
        _sel_kernel,
        grid=(G, nkb // nq),
        in_specs=_qkv_specs(S, q_col, k_col, v_col, nq * QB)
        + [pl.BlockSpec((1, 2, QB, QW), lambda g, c: (g, 0, 0, 0)),
           pl.BlockSpec((1, 1, QW), lambda g, c: (g, 0, 0)),
           pl.BlockSpec((1, NB, nq * QB), lambda g, c: (g, 0, c)),
           pl.BlockSpec((1, nq, N_BRANCH, QW), lambda g, c: (g, c, 0, 0)),
           tile, tile],
        out_specs=pl.BlockSpec((nq * QB, GROUP * HEAD_DIM), lambda g, c: (c, g)),
        out_shape=jax.ShapeDtypeStruct((S, G * GROUP * HEAD_DIM), F32),
        scratch_shapes=[pltpu.VMEM((1, QW), F32), pltpu.VMEM((1, QW), F32),
                        pltpu.VMEM((HEAD_DIM, QW), F32)]
        + [pltpu.VMEM((FAR_BLOCKS * QB, QW), F32)] * 2
        + [pltpu.VMEM((FAR_BLOCKS * QB, QW), BF16)] * 2
        + [pltpu.VMEM((1, QW), F32)] * 2
        + [pltpu.VMEM((2 * QB, QW), F32)],
        compiler_params=_cparams(("arbitrary", "arbitrary")),
        name="selected_attention_combine",
    )(proj, proj, proj, tabs, c31, sel_t, gates_s, ocmp_t, owin_t)


def _outproj_kernel(a_ref, w_ref, x_ref, o_ref, w_s):
    @pl.when(pl.program_id(1) == 0)
    def _():
        w_s[...] = w_ref[...].astype(BF16)

    o_ref[...] = x_ref[...] + jnp.dot(a_ref[...], w_s[...], preferred_element_type=F32)


def _outproj(a, w, x, tm=1024, tn=512):
    S, D = a.shape
    N = w.shape[1]
    tm = min(tm, S)
    return pl.pallas_call(
        _outproj_kernel,
        grid=(N // tn, S // tm),
        in_specs=[pl.BlockSpec((tm, D), lambda j, i: (i, 0)),
                  pl.BlockSpec((D, tn), lambda j, i: (0, j)),
                  pl.BlockSpec((tm, tn), lambda j, i: (i, j))],
        out_specs=pl.BlockSpec((tm, tn), lambda j, i: (i, j)),
        out_shape=jax.ShapeDtypeStruct((S, N), F32),
        scratch_shapes=[pltpu.VMEM((D, tn), BF16)],
        compiler_params=_cparams(("arbitrary", "arbitrary")),
        name="outproj_residual",
    )(a, w, x)


HALO = 16
FFN_ROW_CHUNKS = 2


def _ffn_a_kernel(halo_ref, a_ref, wg_ref, wu_ref, cw_ref, cb_ref, o_ref, g_s):
    i = pl.program_id(0)
    tm = a_ref.shape[0]
    halo = halo_ref[...]
    halo = jnp.where(i > 0, halo, jnp.zeros_like(halo))
    wg = wg_ref[...].astype(BF16)
    wu = wu_ref[...].astype(BF16)
    hm = tm // FFN_ROW_CHUNKS
    for h in range(FFN_ROW_CHUNKS):
        a = a_ref[h * hm:(h + 1) * hm, :]
        g = g_s.at[h]
        if h == 0:
            g[...] = jnp.dot(jnp.concatenate([halo, a], axis=0), wg, preferred_element_type=F32)
        else:
            g[0:HALO, :] = g_s[h - 1, hm:hm + HALO, :]
            g[HALO:, :] = jnp.dot(a, wg, preferred_element_type=F32)
        up = jnp.dot(a, wu, preferred_element_type=F32)
        gext = g[...]
        g1 = pltpu.roll(gext, 1, 0)
        g2 = pltpu.roll(g1, 1, 0)
        y = cb_ref[...] + g2[HALO:] * cw_ref[0:1, :]
        y = y + g1[HALO:] * cw_ref[1:2, :]
        y = y + gext[HALO:] * cw_ref[2:3, :]
        act = y * (1.0 / (1.0 + jnp.exp(-y)))
        o_ref[h * hm:(h + 1) * hm, :] = (act * up).astype(o_ref.dtype)


def _ffn_a(hf, wg, wu, cw, cb, tm=2048, tf=256):
    S, D = hf.shape
    Fp = wg.shape[1]
    assert Fp % tf == 0
    tm = min(tm, S)
    hb = tm // HALO
    return pl.pallas_call(
        _ffn_a_kernel,
        grid=(S // tm, Fp // tf),
        in_specs=[pl.BlockSpec((HALO, D), lambda i, f: (jnp.maximum(i * hb - 1, 0), 0)),
                  pl.BlockSpec((tm, D), lambda i, f: (i, 0), pipeline_mode=pl.Buffered(1)),
                  pl.BlockSpec((D, tf), lambda i, f: (0, f)),
                  pl.BlockSpec((D, tf), lambda i, f: (0, f)),
                  pl.BlockSpec((cw.shape[0], tf), lambda i, f: (0, f)),
                  pl.BlockSpec((1, tf), lambda i, f: (0, f))],
        out_specs=pl.BlockSpec((tm, tf), lambda i, f: (i, f)),
        out_shape=jax.ShapeDtypeStruct((S, Fp), BF16),
        scratch_shapes=[pltpu.VMEM((FFN_ROW_CHUNKS, HALO + tm // FFN_ROW_CHUNKS, tf), F32)],
        compiler_params=_cparams(("arbitrary", "arbitrary")),
        name="ffn_gate_up",
    )(hf, hf, wg, wu, cw, cb)


def _ffn_b_kernel(a_ref, w_ref, x_ref, o_ref):
    o_ref[...] = x_ref[...] + jnp.dot(a_ref[...], w_ref[...], preferred_element_type=F32)


def _ffn_b(h, w, x, tm=512, tn=512):
    S, F = h.shape
    N = w.shape[1]
    tm = min(tm, S)
    return pl.pallas_call(
        _ffn_b_kernel,
        grid=(S // tm, N // tn),
        in_specs=[pl.BlockSpec((tm, F), lambda i, j: (i, 0)),
                  pl.BlockSpec((F, tn), lambda i, j: (0, j)),
                  pl.BlockSpec((tm, tn), lambda i, j: (i, j))],
        out_specs=pl.BlockSpec((tm, tn), lambda i, j: (i, j)),
        out_shape=jax.ShapeDtypeStruct((S, N), F32),
        compiler_params=_cparams(("arbitrary", "arbitrary")),
        name="ffn_down_residual",
    )(h, w, x)


def _layer(x, rel_bias, norm_mix_g, w_in, a_q_norm_g, a_k_norm_g, a_sinks, b_q_norm_g, b_k_norm_g,
           cmp_pos_emb, cmp_w1, cmp_b1, cmp_w2, cmp_b2, out_norm_g, w_out, norm_ffn_g, w_gate,
           w_up, conv_w, conv_b, w_down):
    S, D = x.shape
    aw = A_Q_HEADS * HEAD_DIM
    akv = A_KV_HEADS * HEAD_DIM
    bw = B_Q_HEADS * HEAD_DIM
    bkv = B_KV_HEADS * HEAD_DIM
    sizes = [aw, akv, akv, bw] + [bkv] * 6 + [N_BRANCH * B_Q_HEADS]
    offs = np.concatenate([[0], np.cumsum(sizes)]).tolist()
    wt = jnp.swapaxes(w_in, 0, 1)
    tn = 512
    order_a = [0, 1, 2, 3, 6, 7, 8, 9]
    n_gate = sizes[10]
    wg_t = jnp.pad(wt[offs[10]:offs[10] + n_gate], ((0, QB - n_gate), (0, 0)))
    one = jnp.ones((HEAD_DIM,), F32)
    seg_gain = {0: a_q_norm_g, 1: a_k_norm_g * KEY_SCALE, 3: b_q_norm_g,
                6: b_k_norm_g[1] * KEY_SCALE, 8: b_k_norm_g[2] * KEY_SCALE}
    kinds, tiles, gains, col = [], [], [], {}
    c0 = 0
    for k in order_a:
        col[k] = c0
        for t in range(sizes[k] // tn):
            kinds.append(1 if k in seg_gain else 0)
            tiles.append(offs[k] // tn + t)
            gains.append(seg_gain.get(k, one))
        c0 += sizes[k]
    kinds = jnp.asarray(kinds, jnp.int32)
    tiles = jnp.asarray(tiles, jnp.int32)
    gains = jnp.stack(gains).astype(F32).reshape(len(gains), 1, HEAD_DIM)

    hn = _rmsnorm([x], norm_mix_g)
    proj = _inproj_a(hn, wt, kinds, tiles, gains, tn=tn)
    kv32, gates = _inproj_b(hn, wt, offs[4] // tn, 2 * bkv // tn, wg_t)

    tab_a = rel_bias[:, :A_Q_HEADS]
    tab_b = rel_bias[:, A_Q_HEADS:]
    near_a = _stack_heads(_near_bias_tables(tab_a), A_KV_HEADS)
    near_b = _stack_heads(_near_bias_tables(tab_b), B_KV_HEADS)
    head_row = lambda v, G: _stack_heads(jnp.broadcast_to(v.astype(F32)[:, None, None],
                                                          (v.shape[0], 1, QB)), G)
    c31_a = head_row(tab_a[N_BUCKETS - 1], A_KV_HEADS)
    c31_b = head_row(tab_b[N_BUCKETS - 1], B_KV_HEADS)

    qcol = lambda k: col[k] // (GROUP * HEAD_DIM)
    hcol = lambda k: col[k] // HEAD_DIM
    o_a = _band_attention(proj, A_KV_HEADS, qcol(0), hcol(1), hcol(2), near_a, c31_a,
                          head_row(a_sinks, A_KV_HEADS), A_WINDOW, transposed_out=False)

    NC = S // CMP_STRIDE
    n_cmp = (S - CMP_BLOCK) // CMP_STRIDE + 1
    w1 = cmp_w1.reshape(2, CMP_BLOCK, HEAD_DIM, cmp_w1.shape[-1]).astype(BF16)
    kc, vct = _compress(kv32, B_KV_HEADS, cmp_pos_emb.astype(F32), w1, cmp_b1[:, None, :].astype(F32),
                        cmp_w2.astype(BF16), cmp_b2[:, None, :].astype(F32),
                        (b_k_norm_g[0] * KEY_SCALE).reshape(1, HEAD_DIM).astype(F32))

    NB = S // SLC_BLOCK
    ii = np.arange(NC)[None, :]
    jj = np.arange(NB)[:, None]
    ovl_t = ((ii * CMP_STRIDE <= jj * SLC_BLOCK + SLC_BLOCK - 1)
             & (ii * CMP_STRIDE + CMP_BLOCK - 1 >= jj * SLC_BLOCK) & (ii < n_cmp))
    ovl_t = jnp.asarray(ovl_t.astype(np.float32), BF16)
    ocmp_t, sel_t = _cmp_select(proj, qcol(3), kc, vct,
                                _stack_heads(_cmp_bias_table(tab_b), B_KV_HEADS), c31_b, ovl_t, n_cmp)

    owin_t = _band_attention(proj, B_KV_HEADS, qcol(3), hcol(8), hcol(9), near_b, c31_b, None,
                             B_WINDOW, transposed_out=True)
    gates_s = gates[:, :n_gate].reshape(S // QB, QB, B_KV_HEADS, GROUP, N_BRANCH)
    gates_s = gates_s.transpose(2, 0, 4, 3, 1).reshape(B_KV_HEADS, S // QB, N_BRANCH, QW)
    o_b = _sel_attention(proj, qcol(3), hcol(6), hcol(7), near_b, c31_b, sel_t, gates_s,
                         ocmp_t, owin_t)

    on = _rmsnorm([o_a, o_b], out_norm_g)
    x2 = _outproj(on, w_out, x)

    hf = _rmsnorm([x2], norm_ffn_g)
    hmid = _ffn_a(hf, w_gate, w_up, conv_w.astype(F32), conv_b[None, :].astype(F32))
    return _ffn_b(hmid, w_down.astype(BF16), x2)


def kernel(x, rel_bias, norm_mix_g, w_in, a_q_norm_g, a_k_norm_g, a_sinks, b_q_norm_g, b_k_norm_g,
           cmp_pos_emb, cmp_w1, cmp_b1, cmp_w2, cmp_b2, out_norm_g, w_out, norm_ffn_g, w_gate, w_up,
           conv_w, conv_b, w_down):
    depth = w_in.shape[0]
    batch = x.shape[0]
    outs = []
    for b in range(batch):
        h = x[b]
        for l in range(depth):
            h = _layer(h, rel_bias, norm_mix_g[l], w_in[l], a_q_norm_g[l], a_k_norm_g[l], a_sinks[l],
                       b_q_norm_g[l], b_k_norm_g[l], cmp_pos_emb[l], cmp_w1[l], cmp_b1[l], cmp_w2[l],
                       cmp_b2[l], out_norm_g[l], w_out[l], norm_ffn_g[l], w_gate[l], w_up[l],
                       conv_w[l], conv_b[l], w_down[l])
        outs.append(h)
    return jnp.stack(outs)
```

```python
import functools
import math

import numpy as np
import jax
import jax.numpy as jnp
from jax import lax
from jax.experimental import pallas as pl
from jax.experimental.pallas import tpu as pltpu

F32 = jnp.float32
BF16 = jnp.bfloat16

HEAD_DIM = 128
A_Q_HEADS = 16
A_KV_HEADS = 4
B_Q_HEADS = 16
B_KV_HEADS = 4
GROUP = 4
A_WINDOW = 128
B_WINDOW = 512
CMP_BLOCK = 32
CMP_STRIDE = 16
SLC_BLOCK = 64
SLC_TOPK = 16
N_BRANCH = 3
N_BUCKETS = 32
MAX_DISTANCE = 128
EPS = 1e-6
NEG = -1e30
FORCE_SCORE = 1e6
SCALE = HEAD_DIM ** -0.5

QB = 128
CMP_WIN = 24
CMP_TAB = 40
VMEM_LIMIT = 56 * 1024 * 1024


def _cparams(sem, **kw):
    return pltpu.CompilerParams(dimension_semantics=sem, vmem_limit_bytes=VMEM_LIMIT, **kw)


def _t5_bucket_np(dist):
    n = np.maximum(dist, 0)
    max_exact = N_BUCKETS // 2
    nf = np.maximum(n, 1).astype(np.float32)
    large = max_exact + (np.log(nf / max_exact) / math.log(MAX_DISTANCE / max_exact)
                         * (N_BUCKETS - max_exact)).astype(np.int32)
    large = np.minimum(large, N_BUCKETS - 1)
    return np.where(n < max_exact, n, large).astype(np.int32)


def _bias_from_buckets(tab, idx):
    onehot = (idx[None] == np.arange(N_BUCKETS).reshape((-1,) + (1,) * idx.ndim)).astype(np.float32)
    return jnp.einsum('bh,b...->h...', tab.astype(F32), jnp.asarray(onehot),
                      precision=lax.Precision.HIGHEST)


def _near_bias_tables(tab):
    kk = np.arange(QB)[:, None]
    qq = np.arange(QB)[None, :]
    return _bias_from_buckets(tab, np.stack([_t5_bucket_np(qq - kk), _t5_bucket_np(QB + qq - kk)]))


def _cmp_bias_table(tab):
    npr = np.arange(CMP_TAB)[:, None] - 16
    qq = np.arange(QB)[None, :]
    return _bias_from_buckets(tab, _t5_bucket_np(qq - CMP_STRIDE * npr - (CMP_BLOCK - 1)))


def _rmsnorm_kernel(*refs, n_in):
    x_refs, g_ref, o_ref = refs[:n_in], refs[n_in], refs[n_in + 1]
    off = 0
    for x_ref in x_refs:
        x = x_ref[...]
        w = x.shape[-1]
        y = x * lax.rsqrt(jnp.mean(x * x, axis=-1, keepdims=True) + EPS)
        o_ref[:, off:off + w] = (y * g_ref[:, off:off + w]).astype(o_ref.dtype)
        off += w


def _rmsnorm(xs, gain, tr=256):
    S = xs[0].shape[0]
    widths = [x.shape[1] for x in xs]
    n = sum(widths)
    tr = min(tr, S)
    return pl.pallas_call(
        functools.partial(_rmsnorm_kernel, n_in=len(xs)),
        grid=(S // tr,),
        in_specs=[pl.BlockSpec((tr, w), lambda i: (i, 0)) for w in widths]
        + [pl.BlockSpec((1, n), lambda i: (0, 0))],
        out_specs=pl.BlockSpec((tr, n), lambda i: (i, 0)),
        out_shape=jax.ShapeDtypeStruct((S, n), BF16),
        compiler_params=_cparams(("arbitrary",)),
        name="rmsnorm",
    )(*xs, gain.reshape(1, n).astype(F32))


NORM_ROW_CHUNKS = 2


def _dot_wt(a, wt):
    return lax.dot_general(a, wt.astype(BF16), (((1,), (1,)), ((), ())), preferred_element_type=F32)


def _inproj_a_kernel(kind_ref, tile_ref, a_ref, w_ref, g_ref, o_ref, w_s):
    j = pl.program_id(0)

    @pl.when(pl.program_id(1) == 0)
    def _():
        w_s[...] = w_ref[...].astype(BF16)

    @pl.when(kind_ref[j] == 0)
    def _():
        o_ref[...] = _dot_wt(a_ref[...], w_s[...]).astype(o_ref.dtype)

    @pl.when(kind_ref[j] == 1)
    def _():
        g = g_ref[0]
        hm = a_ref.shape[0] // NORM_ROW_CHUNKS
        for c in range(NORM_ROW_CHUNKS):
            rows = slice(c * hm, (c + 1) * hm)
            acc = _dot_wt(a_ref[rows, :], w_s[...])
            for h in range(acc.shape[1] // HEAD_DIM):
                sl = acc[:, h * HEAD_DIM:(h + 1) * HEAD_DIM]
                y = sl * lax.rsqrt(jnp.mean(sl * sl, axis=-1, keepdims=True) + EPS)
                o_ref[rows, h * HEAD_DIM:(h + 1) * HEAD_DIM] = (y * g).astype(o_ref.dtype)


def _inproj_a(hn, wt, kinds, tiles, gains, tm=1024, tn=512):
    S, D = hn.shape
    n_tiles = kinds.shape[0]
    tm = min(tm, S)
    grid_spec = pltpu.PrefetchScalarGridSpec(
        num_scalar_prefetch=2,
        grid=(n_tiles, S // tm),
        in_specs=[pl.BlockSpec((tm, D), lambda j, i, k, t: (i, 0)),
                  pl.BlockSpec((tn, D), lambda j, i, k, t: (t[j], 0)),
                  pl.BlockSpec((1, 1, HEAD_DIM), lambda j, i, k, t: (j, 0, 0))],
        out_specs=pl.BlockSpec((tm, tn), lambda j, i, k, t: (i, j)),
        scratch_shapes=[pltpu.VMEM((tn, D), BF16)],
    )
    return pl.pallas_call(
        _inproj_a_kernel,
        grid_spec=grid_spec,
        out_shape=jax.ShapeDtypeStruct((S, n_tiles * tn), BF16),
        compiler_params=_cparams(("arbitrary", "arbitrary")),
        name="inproj_heads",
    )(kinds, tiles, hn, wt, gains)


def _inproj_b_kernel(a_ref, w_ref, wg_ref, o_ref, gate_ref):
    a = a_ref[...]
    o_ref[...] = _dot_wt(a, w_ref[...])

    @pl.when(pl.program_id(1) == 0)
    def _():
        gate_ref[...] = 1.0 / (1.0 + jnp.exp(-_dot_wt(a, wg_ref[...])))


def _inproj_b(hn, wt, tile0, n_tiles, wg_t, tm=1024, tn=512):
    S, D = hn.shape
    tm = min(tm, S)
    ng = wg_t.shape[0]
    return pl.pallas_call(
        _inproj_b_kernel,
        grid=(S // tm, n_tiles),
        in_specs=[pl.BlockSpec((tm, D), lambda i, j: (i, 0)),
                  pl.BlockSpec((tn, D), lambda i, j: (tile0 + j, 0)),
                  pl.BlockSpec((ng, D), lambda i, j: (0, 0))],
        out_specs=[pl.BlockSpec((tm, tn), lambda i, j: (i, j)),
                   pl.BlockSpec((tm, ng), lambda i, j: (i, 0))],
        out_shape=[jax.ShapeDtypeStruct((S, n_tiles * tn), F32),
                   jax.ShapeDtypeStruct((S, ng), F32)],
        compiler_params=_cparams(("arbitrary", "arbitrary")),
        name="inproj_cmp_gates",
    )(hn, wt, wg_t)


def _gelu_tanh(x):
    return 0.5 * x * (1.0 + jnp.tanh(math.sqrt(2.0 / math.pi) * (x + 0.044715 * (x * x * x))))


def _compress_kernel(tk_ref, tv_ref, pos_ref, w1_ref, b1_ref, w2_ref, b2_ref, gk_ref,
                     kc_ref, vct_ref):
    nc = kc_ref.shape[1]

    def mlp(t_ref, kv):
        a0 = jnp.zeros((nc, w1_ref.shape[-1]), F32)
        a1 = jnp.zeros((nc, w1_ref.shape[-1]), F32)
        for b in range(CMP_STRIDE):
            t = t_ref[pl.ds(b, nc, stride=CMP_STRIDE), :]
            lo, hi = b, CMP_STRIDE + b
            a0 = a0 + jnp.dot((t + pos_ref[kv, lo:lo + 1, :]).astype(BF16), w1_ref[kv, lo],
                              preferred_element_type=F32)
            a1 = a1 + jnp.dot((t + pos_ref[kv, hi:hi + 1, :]).astype(BF16), w1_ref[kv, hi],
                              preferred_element_type=F32)
        h = a0 + pltpu.roll(a1, nc - 1, 0) + b1_ref[kv]
        h = _gelu_tanh(h)
        return jnp.dot(h.astype(BF16), w2_ref[kv], preferred_element_type=F32) + b2_ref[kv]

    ck = mlp(tk_ref, 0)
    ck = ck * lax.rsqrt(jnp.mean(ck * ck, axis=-1, keepdims=True) + EPS) * gk_ref[...]
    kc_ref[0] = ck.astype(kc_ref.dtype)
    cv = mlp(tv_ref, 1)
    vct_ref[0] = cv.T.astype(vct_ref.dtype)


def _compress(kv32, G, pos, w1, b1, w2, b2, gk):
    S = kv32.shape[0]
    NC = S // CMP_STRIDE
    hid = w1.shape[-1]
    return pl.pallas_call(
        _compress_kernel,
        grid=(G,),
        in_specs=[pl.BlockSpec((S, HEAD_DIM), lambda g: (0, g)),
                  pl.BlockSpec((S, HEAD_DIM), lambda g: (0, G + g)),
                  pl.BlockSpec((2, CMP_BLOCK, HEAD_DIM), lambda g: (0, 0, 0)),
                  pl.BlockSpec((2, CMP_BLOCK, HEAD_DIM, hid), lambda g: (0, 0, 0, 0)),
                  pl.BlockSpec((2, 1, hid), lambda g: (0, 0, 0)),
                  pl.BlockSpec((2, hid, HEAD_DIM), lambda g: (0, 0, 0)),
                  pl.BlockSpec((2, 1, HEAD_DIM), lambda g: (0, 0, 0)),
                  pl.BlockSpec((1, HEAD_DIM), lambda g: (0, 0))],
        out_specs=[pl.BlockSpec((1, NC, HEAD_DIM), lambda g: (g, 0, 0)),
                   pl.BlockSpec((1, HEAD_DIM, NC), lambda g: (g, 0, 0))],
        out_shape=[jax.ShapeDtypeStruct((G, NC, HEAD_DIM), BF16),
                   jax.ShapeDtypeStruct((G, HEAD_DIM, NC), BF16)],
        compiler_params=_cparams(("arbitrary",)),
        name="compress",
    )(kv32, kv32, pos, w1, b1, w2, b2, gk)


QW = GROUP * QB


def _stacked_queries(q_ref):
    return jnp.concatenate([q_ref[:, r * HEAD_DIM:(r + 1) * HEAD_DIM] for r in range(GROUP)], axis=0)


LOG2E = math.log2(math.e)
KEY_SCALE = SCALE * LOG2E
QSUB = 8
BAND_QSUB = 4


def _scores(k_t, qs):
    return lax.dot_general(k_t, qs, (((1,), (1,)), ((), ())), preferred_element_type=F32)


def _rel_bias(tab, c31):
    return (tab - c31) * LOG2E


def _init_state(m_ref, l_ref, acc_ref):
    m_ref[...] = jnp.full(m_ref.shape, NEG, F32)
    l_ref[...] = jnp.zeros(l_ref.shape, F32)
    acc_ref[...] = jnp.zeros(acc_ref.shape, F32)


def _online_update(m_ref, l_ref, acc_ref, lanes, t, v_t):
    m_old = m_ref[:, lanes]
    m_new = jnp.maximum(m_old, jnp.max(t, axis=0, keepdims=True))
    alpha = jnp.exp2(m_old - m_new)
    p = jnp.exp2(t - m_new)
    l_ref[:, lanes] = alpha * l_ref[:, lanes] + jnp.sum(p, axis=0, keepdims=True)
    pv = lax.dot_general(v_t, p.astype(BF16), (((0,), (0,)), ((), ())), preferred_element_type=F32)
    acc_ref[:, lanes] = acc_ref[:, lanes] * alpha + pv
    m_ref[:, lanes] = m_new


def _key_query_iotas():
    kk = lax.broadcasted_iota(jnp.int32, (QB, QW), 0)
    qq = lax.broadcasted_iota(jnp.int32, (QB, QW), 1) & (QB - 1)
    return kk, qq


def _load_kv(k_ref, v_ref, kb0, n):
    rows = pl.ds(pl.multiple_of(kb0 * QB, QB), n * QB)
    return k_ref[rows, :], v_ref[rows, :]


def _store_heads(o_ref, o_t):
    for r in range(GROUP):
        o_ref[:, r * HEAD_DIM:(r + 1) * HEAD_DIM] = o_t[:, r * QB:(r + 1) * QB].T


def _stack_heads(a, G):
    a = a.reshape((G, GROUP) + a.shape[1:])
    a = jnp.moveaxis(a, 1, -2)
    return a.reshape(a.shape[:-2] + (QW,))


def _band_kernel(q_ref, k_ref, v_ref, tab_ref, c31_ref, sink_ref, o_ref, m_ref, l_ref, acc_ref,
                 *, nback, use_sinks, transposed_out, qsub):
    c0 = pl.program_id(1) * qsub
    _init_state(m_ref, l_ref, acc_ref)
    kk, qq = _key_query_iotas()
    c31 = c31_ref[0]

    def masked_scores(sub, kb0, deltas):
        k_t, v_t = _load_kv(k_ref, v_ref, kb0, len(deltas))
        s = _scores(k_t, _stacked_queries(q_ref.at[sub * QB:(sub + 1) * QB]))
        parts = []
        for t, delta in enumerate(deltas):
            sb = s[t * QB:(t + 1) * QB]
            if delta <= 1:
                sb = sb + _rel_bias(tab_ref[0, delta], c31)
            if delta == 0:
                sb = jnp.where(qq >= kk, sb, NEG)
            elif delta == nback:
                sb = jnp.where(qq < kk, sb, NEG)
            parts.append(sb)
        return (parts[0] if len(parts) == 1 else jnp.concatenate(parts, axis=0)), v_t

    def update(sub, s, v_t):
        _online_update(m_ref.at[sub], l_ref.at[sub], acc_ref.at[sub], slice(None), s, v_t)

    @pl.when(c0 >= nback)
    def _():
        band = list(range(nback, -1, -1))
        tiles = [masked_scores(sub, c0 + sub - nback, band) for sub in range(qsub)]
        for sub, (s, v_t) in enumerate(tiles):
            update(sub, s, v_t)

    @pl.when(c0 < nback)
    def _():
        for sub in range(qsub):
            for delta in range(nback, -1, -1):
                @pl.when(c0 + sub - delta >= 0)
                def _(sub=sub, delta=delta):
                    update(sub, *masked_scores(sub, c0 + sub - delta, [delta]))

    for sub in range(qsub):
        l = l_ref[sub]
        acc = acc_ref[sub]
        if use_sinks:
            m = m_ref[sub]
            sk = _rel_bias(sink_ref[0], c31)
            m_f = jnp.maximum(m, sk)
            a = jnp.exp2(m - m_f)
            l = l * a + jnp.exp2(sk - m_f)
            acc = acc * a
        o_t = acc * (1.0 / l)
        if transposed_out:
            o_ref[0, sub] = o_t
        else:
            _store_heads(o_ref.at[sub * QB:(sub + 1) * QB], o_t)


def _qkv_specs(S, q_col, k_col, v_col, q_rows=QB):
    return [pl.BlockSpec((q_rows, GROUP * HEAD_DIM), lambda g, c: (c, q_col + g)),
            pl.BlockSpec((S, HEAD_DIM), lambda g, c: (0, k_col + g)),
            pl.BlockSpec((S, HEAD_DIM), lambda g, c: (0, v_col + g))]


def _band_attention(proj, G, q_col, k_col, v_col, tabs, c31, sinks, window, transposed_out):
    S = proj.shape[0]
    nkb = S // QB
    nback = -(-(window - 1) // QB)
    use_sinks = sinks is not None
    if sinks is None:
        sinks = jnp.zeros_like(c31)
    qsub = min(nkb, BAND_QSUB * (2 if nback == 1 else 1))
    if transposed_out:
        out_shape = jax.ShapeDtypeStruct((G, nkb, HEAD_DIM, QW), F32)
        out_spec = pl.BlockSpec((1, qsub, HEAD_DIM, QW), lambda g, c: (g, c, 0, 0))
    else:
        out_shape = jax.ShapeDtypeStruct((S, G * GROUP * HEAD_DIM), F32)
        out_spec = pl.BlockSpec((qsub * QB, GROUP * HEAD_DIM), lambda g, c: (c, g))
    row = pl.BlockSpec((1, 1, QW), lambda g, c: (g, 0, 0))
    return pl.pallas_call(
        functools.partial(_band_kernel, nback=nback, use_sinks=use_sinks,
                          transposed_out=transposed_out, qsub=qsub),
        grid=(G, nkb // qsub),
        in_specs=_qkv_specs(S, q_col, k_col, v_col, qsub * QB)
        + [pl.BlockSpec((1, 2, QB, QW), lambda g, c: (g, 0, 0, 0)), row, row],
        out_specs=out_spec,
        out_shape=out_shape,
        scratch_shapes=[pltpu.VMEM((qsub, 1, QW), F32), pltpu.VMEM((qsub, 1, QW), F32),
                        pltpu.VMEM((qsub, HEAD_DIM, QW), F32)],
        compiler_params=_cparams(("arbitrary", "arbitrary")),
        name="band_attention_w%d" % window,
    )(proj, proj, proj, tabs, c31, sinks)


def _cmp_select_kernel(q_ref, kc_ref, vct_ref, tab_ref, c31_ref, ovl_ref, ot_ref, sel_ref, s_ref,
                       *, n_valid, topk):
    c0 = pl.program_id(1) * QSUB
    nc = kc_ref.shape[1]
    nb = ovl_ref.shape[0]
    rown = lax.broadcasted_iota(jnp.int32, (nc, QW), 0)
    wrow = lax.broadcasted_iota(jnp.int32, (CMP_WIN, QW), 0)
    wq = lax.broadcasted_iota(jnp.int32, (CMP_WIN, QW), 1) & (QB - 1)
    lane_q = lax.broadcasted_iota(jnp.int32, (1, QW), 1) & (QB - 1)

    for sub in range(QSUB):
        c = c0 + sub
        qs = _stacked_queries(q_ref.at[sub * QB:(sub + 1) * QB])
        w0 = pl.multiple_of(jnp.maximum(8 * c - 16, 0), 8)
        toff = pl.multiple_of(w0 - 8 * c + 16, 8)
        n_abs = w0 + wrow
        dist = (c * QB + wq) - (n_abs * CMP_STRIDE + (CMP_BLOCK - 1))
        valid_w = (dist >= 0) & (n_abs < n_valid)
        s_ref[sub] = jnp.where(rown < w0, _scores(kc_ref[0], qs), NEG)
        bias_w = _rel_bias(tab_ref[0, pl.ds(toff, CMP_WIN), :], c31_ref[0])
        s_w = _scores(kc_ref[0, pl.ds(w0, CMP_WIN), :], qs) + bias_w
        s_ref[sub, pl.ds(w0, CMP_WIN), :] = jnp.where(valid_w, s_w, NEG)

    p_groups = []
    for sub in range(QSUB):
        has_any = ((c0 + sub) * QB + lane_q >= CMP_BLOCK - 1).astype(F32)
        s = s_ref[sub]
        m = jnp.max(s, axis=0, keepdims=True)
        e = jnp.exp2(s - m)
        p = e * (has_any / jnp.sum(e, axis=0, keepdims=True))
        ot_ref[0, sub] = jnp.dot(vct_ref[0], p.astype(BF16), preferred_element_type=F32)
        p_grp = p[:, 0:QB]
        for r in range(1, GROUP):
            p_grp = p_grp + p[:, r * QB:(r + 1) * QB]
        p_groups.append(p_grp.astype(BF16))

    nq = QSUB * QB
    scores = jnp.dot(ovl_ref[...], jnp.concatenate(p_groups, axis=1), preferred_element_type=F32)
    blk = lax.broadcasted_iota(jnp.int32, (nb, nq), 0)
    cur = lax.shift_right_logical(c0 * QB + lax.broadcasted_iota(jnp.int32, (nb, nq), 1),
                                  int(math.log2(SLC_BLOCK)))
    forced = (blk == 0) | (blk == cur) | (blk == cur - 1)
    work = jnp.where(forced, FORCE_SCORE, jnp.where(blk <= cur, scores, -1.0))
    blk_f = blk.astype(F32)
    sel = jnp.zeros((nb, nq), F32)
    for _ in range(topk):
        mx = jnp.max(work, axis=0, keepdims=True)
        first = jnp.min(jnp.where(work == mx, blk_f, float(nb)), axis=0, keepdims=True)
        hit = blk_f == first
        sel = jnp.where(hit, 1.0, sel)
        work = jnp.where(hit, -3e38, work)
    sel_ref[0] = sel


def _cmp_select(proj, q_col, kc, vct, tab, c31, ovl_t, n_valid):
    nkb = proj.shape[0] // QB
    G, NC = kc.shape[0], kc.shape[1]
    NB = ovl_t.shape[0]
    return pl.pallas_call(
        functools.partial(_cmp_select_kernel, n_valid=n_valid, topk=min(SLC_TOPK, NB)),
        grid=(G, nkb // QSUB),
        in_specs=[pl.BlockSpec((QSUB * QB, GROUP * HEAD_DIM), lambda g, c: (c, q_col + g)),
                  pl.BlockSpec((1, NC, HEAD_DIM), lambda g, c: (g, 0, 0)),
                  pl.BlockSpec((1, HEAD_DIM, NC), lambda g, c: (g, 0, 0)),
                  pl.BlockSpec((1, CMP_TAB, QW), lambda g, c: (g, 0, 0)),
                  pl.BlockSpec((1, 1, QW), lambda g, c: (g, 0, 0)),
                  pl.BlockSpec((NB, NC), lambda g, c: (0, 0))],
        out_specs=[pl.BlockSpec((1, QSUB, HEAD_DIM, QW), lambda g, c: (g, c, 0, 0)),
                   pl.BlockSpec((1, NB, QSUB * QB), lambda g, c: (g, 0, c))],
        out_shape=[jax.ShapeDtypeStruct((G, nkb, HEAD_DIM, QW), F32),
                   jax.ShapeDtypeStruct((G, NB, nkb * QB), F32)],
        scratch_shapes=[pltpu.VMEM((QSUB, NC, QW), F32)],
        compiler_params=_cparams(("arbitrary", "arbitrary")),
        name="cmp_attention_select",
    )(proj, kc, vct, tab, c31, ovl_t)


FAR_BLOCKS = 4


def _sel_block(c, q_ref, sel_row, gate_ref, ocmp_ref, owin_ref, o_ref, k_ref, v_ref, tab_ref, c31,
               m_ref, l_ref, acc_ref, s_a, s_b, p_a, p_b, al_a, al_b, s_n):
    _init_state(m_ref, l_ref, acc_ref)
    kk, qq = _key_query_iotas()
    qs = _stacked_queries(q_ref)
    spb = QB // SLC_BLOCK
    tk = FAR_BLOCKS * QB
    last_tile = k_ref.shape[0] // tk - 1
    n_far = jnp.maximum(c - 1, 0)
    n_tiles = (n_far + FAR_BLOCKS - 1) // FAR_BLOCKS

    def head_row(r, row_limit=None, r_load=None):
        row = sel_row(r if r_load is None else r_load)
        if row_limit is not None:
            row = jnp.where(r < row_limit, row, 0.0)
        return jnp.concatenate([row] * GROUP, axis=1)

    def tile_rows(i):
        return pl.ds(pl.multiple_of(jnp.minimum(i, last_tile) * tk, tk), tk)

    def far_scores(i, s_buf):
        s = _scores(k_ref[tile_rows(i), :], qs)
        r0 = i * (FAR_BLOCKS * spb)
        r0_load = jnp.minimum(i, last_tile) * (FAR_BLOCKS * spb)
        for t in range(FAR_BLOCKS * spb):
            row = head_row(r0 + t, n_far * spb, r0_load + t)
            blk = slice(t * SLC_BLOCK, (t + 1) * SLC_BLOCK)
            s_buf[blk, :] = jnp.where(row > 0.5, s[blk], NEG)

    def far_softmax(s_buf, p_buf, al_buf):
        blocks = [slice(t * SLC_BLOCK, (t + 1) * SLC_BLOCK) for t in range(FAR_BLOCKS * spb)]
        sub8 = lambda a: a.reshape(SLC_BLOCK // 8, 8, QW)
        m_part = jnp.full((8, QW), NEG, F32)
        for blk in blocks:
            m_part = jnp.maximum(m_part, jnp.max(sub8(s_buf[blk, :]), axis=0))
        m_old = m_ref[...]
        m_new = jnp.maximum(m_old, jnp.max(m_part, axis=0, keepdims=True))
        alpha = jnp.exp2(m_old - m_new)
        l_part = jnp.zeros((8, QW), F32)
        for blk in blocks:
            p = jnp.exp2(s_buf[blk, :] - m_new)
            l_part = l_part + jnp.sum(sub8(p), axis=0)
            p_buf[blk, :] = p.astype(BF16)
        l_ref[...] = alpha * l_ref[...] + jnp.sum(l_part, axis=0, keepdims=True)
        m_ref[...] = m_new
        al_buf[...] = alpha

    def far_values(i, p_buf, al_buf):
        pv = lax.dot_general(v_ref[tile_rows(i), :], p_buf[...], (((0,), (0,)), ((), ())),
                             preferred_element_type=F32)
        acc_ref[...] = acc_ref[...] * al_buf[...] + pv

    far_scores(0, s_a)
    far_scores(1, s_b)
    for p_buf, al_buf in ((p_a, al_a), (p_b, al_b)):
        p_buf[...] = jnp.zeros(p_buf.shape, BF16)
        al_buf[...] = jnp.ones(al_buf.shape, F32)

    blocks = [(jnp.maximum(c - 1, 0), 1, c * spb), (c, 0, None)]
    near_rows = [pl.ds(pl.multiple_of(kb * QB, QB), QB) for kb, _, _ in blocks]
    s = _scores(jnp.concatenate([k_ref[r, :] for r in near_rows], axis=0), qs)
    for t, (kb, delta, limit) in enumerate(blocks):
        msk = jnp.concatenate(
            [jnp.broadcast_to(head_row(kb * spb + u, limit), (SLC_BLOCK, QW)) for u in range(spb)],
            axis=0) > 0.5
        if delta == 0:
            msk = msk & (qq >= kk)
        sb = s[t * QB:(t + 1) * QB] + _rel_bias(tab_ref[0, delta], c31)
        s_n[t * QB:(t + 1) * QB, :] = jnp.where(msk, sb, NEG)

    def pair_body(j, carry):
        i = 2 * j
        far_values(jnp.maximum(i - 2, 0), p_a, al_a)
        far_values(jnp.maximum(i - 1, 0), p_b, al_b)
        far_softmax(s_a, p_a, al_a)
        far_softmax(s_b, p_b, al_b)
        far_scores(i + 2, s_a)
        far_scores(i + 3, s_b)
        return carry

    n_pairs = (n_tiles + 1) // 2
    lax.fori_loop(0, n_pairs, pair_body, 0)
    far_values(jnp.maximum(2 * n_pairs - 2, 0), p_a, al_a)
    far_values(jnp.maximum(2 * n_pairs - 1, 0), p_b, al_b)

    v_near = jnp.concatenate([v_ref[r, :] for r in near_rows], axis=0)
    _online_update(m_ref, l_ref, acc_ref, slice(None), s_n[...], v_near)

    o_slc = acc_ref[...] * (1.0 / l_ref[...])
    o_t = gate_ref[0:1, :] * ocmp_ref[...] + gate_ref[1:2, :] * o_slc + gate_ref[2:3, :] * owin_ref[...]
    _store_heads(o_ref, o_t)


def _sel_kernel(q_ref, k_ref, v_ref, tab_ref, c31_ref, sel_ref, gate_ref, ocmp_ref, owin_ref, o_ref,
                *scratch):
    _sel_block(pl.program_id(1), q_ref, lambda r: sel_ref[0, pl.ds(r, 1), :],
               gate_ref.at[0, 0], ocmp_ref.at[0, 0], owin_ref.at[0, 0], o_ref,
               k_ref, v_ref, tab_ref, c31_ref[0], *scratch)


def _sel_attention(proj, q_col, k_col, v_col, tabs, c31, sel_t, gates_s, ocmp_t, owin_t):
    S = proj.shape[0]
    G, NB = sel_t.shape[0], sel_t.shape[1]
    nkb = S // QB
    nq = 1
    tile = pl.BlockSpec((1, nq, HEAD_DIM, QW), lambda g, c: (g, c, 0, 0))
    return pl.pallas_call(
        _sel_kernel,
        grid=(G, nkb // nq),
        in_specs=_qkv_specs(S, q_col, k_col, v_col, nq * QB)
        + [pl.BlockSpec((1, 2, QB, QW), lambda g, c: (g, 0, 0, 0)),
           pl.BlockSpec((1, 1, QW), lambda g, c: (g, 0, 0)),
           pl.BlockSpec((1, NB, nq * QB), lambda g, c: (g, 0, c)),
           pl.BlockSpec((1, nq, N_BRANCH, QW), lambda g, c: (g, c, 0, 0)),
           tile, tile],
        out_specs=pl.BlockSpec((nq * QB, GROUP * HEAD_DIM), lambda g, c: (c, g)),
        out_shape=jax.ShapeDtypeStruct((S, G * GROUP * HEAD_DIM), F32),
        scratch_shapes=[pltpu.VMEM((1, QW), F32), pltpu.VMEM((1, QW), F32),
                        pltpu.VMEM((HEAD_DIM, QW), F32)]
        + [pltpu.VMEM((FAR_BLOCKS * QB, QW), F32)] * 2
        + [pltpu.VMEM((FAR_BLOCKS * QB, QW), BF16)] * 2
        + [pltpu.VMEM((1, QW), F32)] * 2
        + [pltpu.VMEM((2 * QB, QW), F32)],
        compiler_params=_cparams(("arbitrary", "arbitrary")),
        name="selected_attention_combine",
    )(proj, proj, proj, tabs, c31, sel_t, gates_s, ocmp_t, owin_t)


def _outproj_kernel(a_ref, w_ref, x_ref, o_ref, w_s):
    @pl.when(pl.program_id(1) == 0)
    def _():
        w_s[...] = w_ref[...].astype(BF16)

    o_ref[...] = x_ref[...] + jnp.dot(a_ref[...], w_s[...], preferred_element_type=F32)


def _outproj(a, w, x, tm=1024, tn=512):
    S, D = a.shape
    N = w.shape[1]
    tm = min(tm, S)
    return pl.pallas_call(
        _outproj_kernel,
        grid=(N // tn, S // tm),
        in_specs=[pl.BlockSpec((tm, D), lambda j, i: (i, 0)),
                  pl.BlockSpec((D, tn), lambda j, i: (0, j)),
                  pl.BlockSpec((tm, tn), lambda j, i: (i, j))],
        out_specs=pl.BlockSpec((tm, tn), lambda j, i: (i, j)),
        out_shape=jax.ShapeDtypeStruct((S, N), F32),
        scratch_shapes=[pltpu.VMEM((D, tn), BF16)],
        compiler_params=_cparams(("arbitrary", "arbitrary")),
        name="outproj_residual",
    )(a, w, x)


HALO = 16
FFN_ROW_CHUNKS = 2


def _ffn_a_kernel(halo_ref, a_ref, wg_ref, wu_ref, cw_ref, cb_ref, o_ref, g_s):
    i = pl.program_id(0)
    tm = a_ref.shape[0]
    halo = halo_ref[...]
    halo = jnp.where(i > 0, halo, jnp.zeros_like(halo))
    wg = wg_ref[...].astype(BF16)
    wu = wu_ref[...].astype(BF16)
    hm = tm // FFN_ROW_CHUNKS
    for h in range(FFN_ROW_CHUNKS):
        a = a_ref[h * hm:(h + 1) * hm, :]
        g = g_s.at[h]
        if h == 0:
            g[...] = jnp.dot(jnp.concatenate([halo, a], axis=0), wg, preferred_element_type=F32)
        else:
            g[0:HALO, :] = g_s[h - 1, hm:hm + HALO, :]
            g[HALO:, :] = jnp.dot(a, wg, preferred_element_type=F32)
        up = jnp.dot(a, wu, preferred_element_type=F32)
        gext = g[...]
        g1 = pltpu.roll(gext, 1, 0)
        g2 = pltpu.roll(g1, 1, 0)
        y = cb_ref[...] + g2[HALO:] * cw_ref[0:1, :]
        y = y + g1[HALO:] * cw_ref[1:2, :]
        y = y + gext[HALO:] * cw_ref[2:3, :]
        act = y * (1.0 / (1.0 + jnp.exp(-y)))
        o_ref[h * hm:(h + 1) * hm, :] = (act * up).astype(o_ref.dtype)


def _ffn_a(hf, wg, wu, cw, cb, tm=2048, tf=256):
    S, D = hf.shape
    Fp = wg.shape[1]
    assert Fp % tf == 0
    tm = min(tm, S)
    hb = tm // HALO
    return pl.pallas_call(
        _ffn_a_kernel,
        grid=(S // tm, Fp // tf),
        in_specs=[pl.BlockSpec((HALO, D), lambda i, f: (jnp.maximum(i * hb - 1, 0), 0)),
                  pl.BlockSpec((tm, D), lambda i, f: (i, 0), pipeline_mode=pl.Buffered(1)),
                  pl.BlockSpec((D, tf), lambda i, f: (0, f)),
                  pl.BlockSpec((D, tf), lambda i, f: (0, f)),
                  pl.BlockSpec((cw.shape[0], tf), lambda i, f: (0, f)),
                  pl.BlockSpec((1, tf), lambda i, f: (0, f))],
        out_specs=pl.BlockSpec((tm, tf), lambda i, f: (i, f)),
        out_shape=jax.ShapeDtypeStruct((S, Fp), BF16),
        scratch_shapes=[pltpu.VMEM((FFN_ROW_CHUNKS, HALO + tm // FFN_ROW_CHUNKS, tf), F32)],
        compiler_params=_cparams(("arbitrary", "arbitrary")),
        name="ffn_gate_up",
    )(hf, hf, wg, wu, cw, cb)


def _ffn_b_kernel(a_ref, w_ref, x_ref, o_ref):
    o_ref[...] = x_ref[...] + jnp.dot(a_ref[...], w_ref[...], preferred_element_type=F32)


def _ffn_b(h, w, x, tm=512, tn=512):
    S, F = h.shape
    N = w.shape[1]
    tm = min(tm, S)
    return pl.pallas_call(
        _ffn_b_kernel,
        grid=(S // tm, N // tn),
        in_specs=[pl.BlockSpec((tm, F), lambda i, j: (i, 0)),
                  pl.BlockSpec((F, tn), lambda i, j: (0, j)),
                  pl.BlockSpec((tm, tn), lambda i, j: (i, j))],
        out_specs=pl.BlockSpec((tm, tn), lambda i, j: (i, j)),
        out_shape=jax.ShapeDtypeStruct((S, N), F32),
        compiler_params=_cparams(("arbitrary", "arbitrary")),
        name="ffn_down_residual",
    )(h, w, x)


def _layer(x, rel_bias, norm_mix_g, w_in, a_q_norm_g, a_k_norm_g, a_sinks, b_q_norm_g, b_k_norm_g,
           cmp_pos_emb, cmp_w1, cmp_b1, cmp_w2, cmp_b2, out_norm_g, w_out, norm_ffn_g, w_gate,
           w_up, conv_w, conv_b, w_down):
    S, D = x.shape
    aw = A_Q_HEADS * HEAD_DIM
    akv = A_KV_HEADS * HEAD_DIM
    bw = B_Q_HEADS * HEAD_DIM
    bkv = B_KV_HEADS * HEAD_DIM
    sizes = [aw, akv, akv, bw] + [bkv] * 6 + [N_BRANCH * B_Q_HEADS]
    offs = np.concatenate([[0], np.cumsum(sizes)]).tolist()
    wt = jnp.swapaxes(w_in, 0, 1)
    tn = 512
    order_a = [0, 1, 2, 3, 6, 7, 8, 9]
    n_gate = sizes[10]
    wg_t = jnp.pad(wt[offs[10]:offs[10] + n_gate], ((0, QB - n_gate), (0, 0)))
    one = jnp.ones((HEAD_DIM,), F32)
    seg_gain = {0: a_q_norm_g, 1: a_k_norm_g * KEY_SCALE, 3: b_q_norm_g,
                6: b_k_norm_g[1] * KEY_SCALE, 8: b_k_norm_g[2] * KEY_SCALE}
    kinds, tiles, gains, col = [], [], [], {}
    c0 = 0
    for k in order_a:
        col[k] = c0
        for t in range(sizes[k] // tn):
            kinds.append(1 if k in seg_gain else 0)
            tiles.append(offs[k] // tn + t)
            gains.append(seg_gain.get(k, one))
        c0 += sizes[k]
    kinds = jnp.asarray(kinds, jnp.int32)
    tiles = jnp.asarray(tiles, jnp.int32)
    gains = jnp.stack(gains).astype(F32).reshape(len(gains), 1, HEAD_DIM)

    hn = _rmsnorm([x], norm_mix_g)
    proj = _inproj_a(hn, wt, kinds, tiles, gains, tn=tn)
    kv32, gates = _inproj_b(hn, wt, offs[4] // tn, 2 * bkv // tn, wg_t)

    tab_a = rel_bias[:, :A_Q_HEADS]
    tab_b = rel_bias[:, A_Q_HEADS:]
    near_a = _stack_heads(_near_bias_tables(tab_a), A_KV_HEADS)
    near_b = _stack_heads(_near_bias_tables(tab_b), B_KV_HEADS)
    head_row = lambda v, G: _stack_heads(jnp.broadcast_to(v.astype(F32)[:, None, None],
                                                          (v.shape[0], 1, QB)), G)
    c31_a = head_row(tab_a[N_BUCKETS - 1], A_KV_HEADS)
    c31_b = head_row(tab_b[N_BUCKETS - 1], B_KV_HEADS)

    qcol = lambda k: col[k] // (GROUP * HEAD_DIM)
    hcol = lambda k: col[k] // HEAD_DIM
    o_a = _band_attention(proj, A_KV_HEADS, qcol(0), hcol(1), hcol(2), near_a, c31_a,
                          head_row(a_sinks, A_KV_HEADS), A_WINDOW, transposed_out=False)

    NC = S // CMP_STRIDE
    n_cmp = (S - CMP_BLOCK) // CMP_STRIDE + 1
    w1 = cmp_w1.reshape(2, CMP_BLOCK, HEAD_DIM, cmp_w1.shape[-1]).astype(BF16)
    kc, vct = _compress(kv32, B_KV_HEADS, cmp_pos_emb.astype(F32), w1, cmp_b1[:, None, :].astype(F32),
                        cmp_w2.astype(BF16), cmp_b2[:, None, :].astype(F32),
                        (b_k_norm_g[0] * KEY_SCALE).reshape(1, HEAD_DIM).astype(F32))

    NB = S // SLC_BLOCK
    ii = np.arange(NC)[None, :]
    jj = np.arange(NB)[:, None]
    ovl_t = ((ii * CMP_STRIDE <= jj * SLC_BLOCK + SLC_BLOCK - 1)
             & (ii * CMP_STRIDE + CMP_BLOCK - 1 >= jj * SLC_BLOCK) & (ii < n_cmp))
    ovl_t = jnp.asarray(ovl_t.astype(np.float32), BF16)
    ocmp_t, sel_t = _cmp_select(proj, qcol(3), kc, vct,
                                _stack_heads(_cmp_bias_table(tab_b), B_KV_HEADS), c31_b, ovl_t, n_cmp)

    owin_t = _band_attention(proj, B_KV_HEADS, qcol(3), hcol(8), hcol(9), near_b, c31_b, None,
                             B_WINDOW, transposed_out=True)
    gates_s = gates[:, :n_gate].reshape(S // QB, QB, B_KV_HEADS, GROUP, N_BRANCH)
    gates_s = gates_s.transpose(2, 0, 4, 3, 1).reshape(B_KV_HEADS, S // QB, N_BRANCH, QW)
    o_b = _sel_attention(proj, qcol(3), hcol(6), hcol(7), near_b, c31_b, sel_t, gates_s,
                         ocmp_t, owin_t)

    on = _rmsnorm([o_a, o_b], out_norm_g)
    x2 = _outproj(on, w_out, x)

    hf = _rmsnorm([x2], norm_ffn_g)
    hmid = _ffn_a(hf, w_gate, w_up, conv_w.astype(F32), conv_b[None, :].astype(F32))
    return _ffn_b(hmid, w_down.astype(BF16), x2)


def kernel(x, rel_bias, norm_mix_g, w_in, a_q_norm_g, a_k_norm_g, a_sinks, b_q_norm_g, b_k_norm_g,
           cmp_pos_emb, cmp_w1, cmp_b1, cmp_w2, cmp_b2, out_norm_g, w_out, norm_ffn_g, w_gate, w_up,
           conv_w, conv_b, w_down):
    depth = w_in.shape[0]
    batch = x.shape[0]
    outs = []
    for b in range(batch):
        h = x[b]
        for l in range(depth):
            h = _layer(h, rel_bias, norm_mix_g[l], w_in[l], a_q_norm_g[l], a_k_norm_g[l], a_sinks[l],
                       b_q_norm_g[l], b_k_norm_g[l], cmp_pos_emb[l], cmp_w1[l], cmp_b1[l], cmp_w2[l],
                       cmp_b2[l], out_norm_g[l], w_out[l], norm_ffn_g[l], w_gate[l], w_up[l],
                       conv_w[l], conv_b[l], w_down[l])
        outs.append(h)
    return jnp.stack(outs)
```

```python
import functools
import math

import numpy as np
import jax
import jax.numpy as jnp
from jax import lax
from jax.experimental import pallas as pl
from jax.experimental.pallas import tpu as pltpu

F32 = jnp.float32
BF16 = jnp.bfloat16

HEAD_DIM = 128
A_Q_HEADS = 16
A_KV_HEADS = 4
B_Q_HEADS = 16
B_KV_HEADS = 4
GROUP = 4
A_WINDOW = 128
B_WINDOW = 512
CMP_BLOCK = 32
CMP_STRIDE = 16
SLC_BLOCK = 64
SLC_TOPK = 16
N_BRANCH = 3
N_BUCKETS = 32
MAX_DISTANCE = 128
EPS = 1e-6
NEG = -1e30
FORCE_SCORE = 1e6
SCALE = HEAD_DIM ** -0.5

QB = 128
CMP_WIN = 24
CMP_TAB = 40
VMEM_LIMIT = 56 * 1024 * 1024


def _cparams(sem, **kw):
    return pltpu.CompilerParams(dimension_semantics=sem, vmem_limit_bytes=VMEM_LIMIT, **kw)


def _t5_bucket_np(dist):
    n = np.maximum(dist, 0)
    max_exact = N_BUCKETS // 2
    nf = np.maximum(n, 1).astype(np.float32)
    large = max_exact + (np.log(nf / max_exact) / math.log(MAX_DISTANCE / max_exact)
                         * (N_BUCKETS - max_exact)).astype(np.int32)
    large = np.minimum(large, N_BUCKETS - 1)
    return np.where(n < max_exact, n, large).astype(np.int32)


def _bias_from_buckets(tab, idx):
    onehot = (idx[None] == np.arange(N_BUCKETS).reshape((-1,) + (1,) * idx.ndim)).astype(np.float32)
    return jnp.einsum('bh,b...->h...', tab.astype(F32), jnp.asarray(onehot),
                      precision=lax.Precision.HIGHEST)


def _near_bias_tables(tab):
    kk = np.arange(QB)[:, None]
    qq = np.arange(QB)[None, :]
    return _bias_from_buckets(tab, np.stack([_t5_bucket_np(qq - kk), _t5_bucket_np(QB + qq - kk)]))


def _cmp_bias_table(tab):
    npr = np.arange(CMP_TAB)[:, None] - 16
    qq = np.arange(QB)[None, :]
    return _bias_from_buckets(tab, _t5_bucket_np(qq - CMP_STRIDE * npr - (CMP_BLOCK - 1)))


def _rmsnorm_kernel(*refs, n_in):
    x_refs, g_ref, o_ref = refs[:n_in], refs[n_in], refs[n_in + 1]
    off = 0
    for x_ref in x_refs:
        x = x_ref[...]
        w = x.shape[-1]
        y = x * lax.rsqrt(jnp.mean(x * x, axis=-1, keepdims=True) + EPS)
        o_ref[:, off:off + w] = (y * g_ref[:, off:off + w]).astype(o_ref.dtype)
        off += w


def _rmsnorm(xs, gain, tr=256):
    S = xs[0].shape[0]
    widths = [x.shape[1] for x in xs]
    n = sum(widths)
    tr = min(tr, S)
    return pl.pallas_call(
        functools.partial(_rmsnorm_kernel, n_in=len(xs)),
        grid=(S // tr,),
        in_specs=[pl.BlockSpec((tr, w), lambda i: (i, 0)) for w in widths]
        + [pl.BlockSpec((1, n), lambda i: (0, 0))],
        out_specs=pl.BlockSpec((tr, n), lambda i: (i, 0)),
        out_shape=jax.ShapeDtypeStruct((S, n), BF16),
        compiler_params=_cparams(("arbitrary",)),
        name="rmsnorm",
    )(*xs, gain.reshape(1, n).astype(F32))


NORM_ROW_CHUNKS = 2


def _dot_wt(a, wt):
    return lax.dot_general(a, wt.astype(BF16), (((1,), (1,)), ((), ())), preferred_element_type=F32)


def _inproj_a_kernel(kind_ref, tile_ref, a_ref, w_ref, g_ref, o_ref, w_s):
    j = pl.program_id(0)

    @pl.when(pl.program_id(1) == 0)
    def _():
        w_s[...] = w_ref[...].astype(BF16)

    @pl.when(kind_ref[j] == 0)
    def _():
        o_ref[...] = _dot_wt(a_ref[...], w_s[...]).astype(o_ref.dtype)

    @pl.when(kind_ref[j] == 1)
    def _():
        g = g_ref[0]
        hm = a_ref.shape[0] // NORM_ROW_CHUNKS
        for c in range(NORM_ROW_CHUNKS):
            rows = slice(c * hm, (c + 1) * hm)
            acc = _dot_wt(a_ref[rows, :], w_s[...])
            for h in range(acc.shape[1] // HEAD_DIM):
                sl = acc[:, h * HEAD_DIM:(h + 1) * HEAD_DIM]
                y = sl * lax.rsqrt(jnp.mean(sl * sl, axis=-1, keepdims=True) + EPS)
                o_ref[rows, h * HEAD_DIM:(h + 1) * HEAD_DIM] = (y * g).astype(o_ref.dtype)


def _inproj_a(hn, wt, kinds, tiles, gains, tm=1024, tn=512, out_dtype=BF16, name="inproj_heads"):
    S, D = hn.shape
    n_tiles = kinds.shape[0]
    tm = min(tm, S)
    grid_spec = pltpu.PrefetchScalarGridSpec(
        num_scalar_prefetch=2,
        grid=(n_tiles, S // tm),
        in_specs=[pl.BlockSpec((tm, D), lambda j, i, k, t: (i, 0)),
                  pl.BlockSpec((tn, D), lambda j, i, k, t: (t[j], 0)),
                  pl.BlockSpec((1, 1, HEAD_DIM), lambda j, i, k, t: (j, 0, 0))],
        out_specs=pl.BlockSpec((tm, tn), lambda j, i, k, t: (i, j)),
        scratch_shapes=[pltpu.VMEM((tn, D), BF16)],
    )
    return pl.pallas_call(
        _inproj_a_kernel,
        grid_spec=grid_spec,
        out_shape=jax.ShapeDtypeStruct((S, n_tiles * tn), out_dtype),
        compiler_params=_cparams(("arbitrary", "arbitrary")),
        name=name,
    )(kinds, tiles, hn, wt, gains)


def _gates_kernel(a_ref, wg_ref, o_ref):
    o_ref[...] = 1.0 / (1.0 + jnp.exp(-_dot_wt(a_ref[...], wg_ref[...])))


def _gates(hn, wg_t, tm=1024):
    S, D = hn.shape
    tm = min(tm, S)
    ng = wg_t.shape[0]
    return pl.pallas_call(
        _gates_kernel,
        grid=(S // tm,),
        in_specs=[pl.BlockSpec((tm, D), lambda i: (i, 0)),
                  pl.BlockSpec((ng, D), lambda i: (0, 0))],
        out_specs=pl.BlockSpec((tm, ng), lambda i: (i, 0)),
        out_shape=jax.ShapeDtypeStruct((S, ng), F32),
        compiler_params=_cparams(("arbitrary",)),
        name="inproj_gates",
    )(hn, wg_t)


def _gelu_tanh(x):
    return 0.5 * x * (1.0 + jnp.tanh(math.sqrt(2.0 / math.pi) * (x + 0.044715 * (x * x * x))))


def _compress_kernel(tk_ref, tv_ref, pos_ref, w1_ref, b1_ref, w2_ref, b2_ref, gk_ref,
                     kc_ref, vct_ref):
    nc = kc_ref.shape[1]

    def mlp(t_ref, kv):
        a0 = jnp.zeros((nc, w1_ref.shape[-1]), F32)
        a1 = jnp.zeros((nc, w1_ref.shape[-1]), F32)
        for b in range(CMP_STRIDE):
            t = t_ref[pl.ds(b, nc, stride=CMP_STRIDE), :]
            lo, hi = b, CMP_STRIDE + b
            a0 = a0 + jnp.dot((t + pos_ref[kv, lo:lo + 1, :]).astype(BF16), w1_ref[kv, lo],
                              preferred_element_type=F32)
            a1 = a1 + jnp.dot((t + pos_ref[kv, hi:hi + 1, :]).astype(BF16), w1_ref[kv, hi],
                              preferred_element_type=F32)
        h = a0 + pltpu.roll(a1, nc - 1, 0) + b1_ref[kv]
        h = _gelu_tanh(h)
        return jnp.dot(h.astype(BF16), w2_ref[kv], preferred_element_type=F32) + b2_ref[kv]

    ck = mlp(tk_ref, 0)
    ck = ck * lax.rsqrt(jnp.mean(ck * ck, axis=-1, keepdims=True) + EPS) * gk_ref[...]
    kc_ref[0] = ck.astype(kc_ref.dtype)
    cv = mlp(tv_ref, 1)
    vct_ref[0] = cv.T.astype(vct_ref.dtype)


def _compress(kv32, G, pos, w1, b1, w2, b2, gk):
    S = kv32.shape[0]
    NC = S // CMP_STRIDE
    hid = w1.shape[-1]
    return pl.pallas_call(
        _compress_kernel,
        grid=(G,),
        in_specs=[pl.BlockSpec((S, HEAD_DIM), lambda g: (0, g)),
                  pl.BlockSpec((S, HEAD_DIM), lambda g: (0, G + g)),
                  pl.BlockSpec((2, CMP_BLOCK, HEAD_DIM), lambda g: (0, 0, 0)),
                  pl.BlockSpec((2, CMP_BLOCK, HEAD_DIM, hid), lambda g: (0, 0, 0, 0)),
                  pl.BlockSpec((2, 1, hid), lambda g: (0, 0, 0)),
                  pl.BlockSpec((2, hid, HEAD_DIM), lambda g: (0, 0, 0)),
                  pl.BlockSpec((2, 1, HEAD_DIM), lambda g: (0, 0, 0)),
                  pl.BlockSpec((1, HEAD_DIM), lambda g: (0, 0))],
        out_specs=[pl.BlockSpec((1, NC, HEAD_DIM), lambda g: (g, 0, 0)),
                   pl.BlockSpec((1, HEAD_DIM, NC), lambda g: (g, 0, 0))],
        out_shape=[jax.ShapeDtypeStruct((G, NC, HEAD_DIM), BF16),
                   jax.ShapeDtypeStruct((G, HEAD_DIM, NC), BF16)],
        compiler_params=_cparams(("arbitrary",)),
        name="compress",
    )(kv32, kv32, pos, w1, b1, w2, b2, gk)


QW = GROUP * QB


def _stacked_queries(q_ref):
    return jnp.concatenate([q_ref[:, r * HEAD_DIM:(r + 1) * HEAD_DIM] for r in range(GROUP)], axis=0)


LOG2E = math.log2(math.e)
KEY_SCALE = SCALE * LOG2E
QSUB = 8
BAND_QSUB = 4


def _scores(k_t, qs):
    return lax.dot_general(k_t, qs, (((1,), (1,)), ((), ())), preferred_element_type=F32)


def _rel_bias(tab, c31):
    return (tab - c31) * LOG2E


def _init_state(m_ref, l_ref, acc_ref):
    m_ref[...] = jnp.full(m_ref.shape, NEG, F32)
    l_ref[...] = jnp.zeros(l_ref.shape, F32)
    acc_ref[...] = jnp.zeros(acc_ref.shape, F32)


def _online_update(m_ref, l_ref, acc_ref, lanes, t, v_t):
    m_old = m_ref[:, lanes]
    m_new = jnp.maximum(m_old, jnp.max(t, axis=0, keepdims=True))
    alpha = jnp.exp2(m_old - m_new)
    p = jnp.exp2(t - m_new)
    l_ref[:, lanes] = alpha * l_ref[:, lanes] + jnp.sum(p, axis=0, keepdims=True)
    pv = lax.dot_general(v_t, p.astype(BF16), (((0,), (0,)), ((), ())), preferred_element_type=F32)
    acc_ref[:, lanes] = acc_ref[:, lanes] * alpha + pv
    m_ref[:, lanes] = m_new


def _key_query_iotas():
    kk = lax.broadcasted_iota(jnp.int32, (QB, QW), 0)
    qq = lax.broadcasted_iota(jnp.int32, (QB, QW), 1) & (QB - 1)
    return kk, qq


def _load_kv(k_ref, v_ref, kb0, n):
    rows = pl.ds(pl.multiple_of(kb0 * QB, QB), n * QB)
    return k_ref[rows, :], v_ref[rows, :]


def _store_heads(o_ref, o_t):
    for r in range(GROUP):
        o_ref[:, r * HEAD_DIM:(r + 1) * HEAD_DIM] = o_t[:, r * QB:(r + 1) * QB].T


def _stack_heads(a, G):
    a = a.reshape((G, GROUP) + a.shape[1:])
    a = jnp.moveaxis(a, 1, -2)
    return a.reshape(a.shape[:-2] + (QW,))


def _band_kernel(q_ref, k_ref, v_ref, tab_ref, c31_ref, sink_ref, o_ref, m_ref, l_ref, acc_ref,
                 *, nback, use_sinks, transposed_out, qsub):
    c0 = pl.program_id(1) * qsub
    _init_state(m_ref, l_ref, acc_ref)
    kk, qq = _key_query_iotas()
    c31 = c31_ref[0]

    def masked_scores(sub, kb0, deltas):
        k_t, v_t = _load_kv(k_ref, v_ref, kb0, len(deltas))
        s = _scores(k_t, _stacked_queries(q_ref.at[sub * QB:(sub + 1) * QB]))
        parts = []
        for t, delta in enumerate(deltas):
            sb = s[t * QB:(t + 1) * QB]
            if delta <= 1:
                sb = sb + _rel_bias(tab_ref[0, delta], c31)
            if delta == 0:
                sb = jnp.where(qq >= kk, sb, NEG)
            elif delta == nback:
                sb = jnp.where(qq < kk, sb, NEG)
            parts.append(sb)
        return (parts[0] if len(parts) == 1 else jnp.concatenate(parts, axis=0)), v_t

    def update(sub, s, v_t):
        _online_update(m_ref.at[sub], l_ref.at[sub], acc_ref.at[sub], slice(None), s, v_t)

    @pl.when(c0 >= nback)
    def _():
        band = list(range(nback, -1, -1))
        tiles = [masked_scores(sub, c0 + sub - nback, band) for sub in range(qsub)]
        for sub, (s, v_t) in enumerate(tiles):
            update(sub, s, v_t)

    @pl.when(c0 < nback)
    def _():
        for sub in range(qsub):
            for delta in range(nback, -1, -1):
                @pl.when(c0 + sub - delta >= 0)
                def _(sub=sub, delta=delta):
                    update(sub, *masked_scores(sub, c0 + sub - delta, [delta]))

    for sub in range(qsub):
        l = l_ref[sub]
        acc = acc_ref[sub]
        if use_sinks:
            m = m_ref[sub]
            sk = _rel_bias(sink_ref[0], c31)
            m_f = jnp.maximum(m, sk)
            a = jnp.exp2(m - m_f)
            l = l * a + jnp.exp2(sk - m_f)
            acc = acc * a
        o_t = acc * (1.0 / l)
        if transposed_out:
            o_ref[0, sub] = o_t
        else:
            _store_heads(o_ref.at[sub * QB:(sub + 1) * QB], o_t)


def _qkv_specs(S, q_col, k_col, v_col, q_rows=QB):
    return [pl.BlockSpec((q_rows, GROUP * HEAD_DIM), lambda g, c: (c, q_col + g)),
            pl.BlockSpec((S, HEAD_DIM), lambda g, c: (0, k_col + g)),
            pl.BlockSpec((S, HEAD_DIM), lambda g, c: (0, v_col + g))]


def _band_attention(proj, G, q_col, k_col, v_col, tabs, c31, sinks, window, transposed_out):
    S = proj.shape[0]
    nkb = S // QB
    nback = -(-(window - 1) // QB)
    use_sinks = sinks is not None
    if sinks is None:
        sinks = jnp.zeros_like(c31)
    qsub = min(nkb, BAND_QSUB * (2 if nback == 1 else 1))
    if transposed_out:
        out_shape = jax.ShapeDtypeStruct((G, nkb, HEAD_DIM, QW), F32)
        out_spec = pl.BlockSpec((1, qsub, HEAD_DIM, QW), lambda g, c: (g, c, 0, 0))
    else:
        out_shape = jax.ShapeDtypeStruct((S, G * GROUP * HEAD_DIM), F32)
        out_spec = pl.BlockSpec((qsub * QB, GROUP * HEAD_DIM), lambda g, c: (c, g))
    row = pl.BlockSpec((1, 1, QW), lambda g, c: (g, 0, 0))
    return pl.pallas_call(
        functools.partial(_band_kernel, nback=nback, use_sinks=use_sinks,
                          transposed_out=transposed_out, qsub=qsub),
        grid=(G, nkb // qsub),
        in_specs=_qkv_specs(S, q_col, k_col, v_col, qsub * QB)
        + [pl.BlockSpec((1, 2, QB, QW), lambda g, c: (g, 0, 0, 0)), row, row],
        out_specs=out_spec,
        out_shape=out_shape,
        scratch_shapes=[pltpu.VMEM((qsub, 1, QW), F32), pltpu.VMEM((qsub, 1, QW), F32),
                        pltpu.VMEM((qsub, HEAD_DIM, QW), F32)],
        compiler_params=_cparams(("arbitrary", "arbitrary")),
        name="band_attention_w%d" % window,
    )(proj, proj, proj, tabs, c31, sinks)


def _cmp_select_kernel(q_ref, kc_ref, vct_ref, tab_ref, c31_ref, ovl_ref, ot_ref, sel_ref, s_ref,
                       *, n_valid, topk):
    c0 = pl.program_id(1) * QSUB
    nc = kc_ref.shape[1]
    nb = ovl_ref.shape[0]
    rown = lax.broadcasted_iota(jnp.int32, (nc, QW), 0)
    wrow = lax.broadcasted_iota(jnp.int32, (CMP_WIN, QW), 0)
    wq = lax.broadcasted_iota(jnp.int32, (CMP_WIN, QW), 1) & (QB - 1)
    lane_q = lax.broadcasted_iota(jnp.int32, (1, QW), 1) & (QB - 1)

    for sub in range(QSUB):
        c = c0 + sub
        qs = _stacked_queries(q_ref.at[sub * QB:(sub + 1) * QB])
        w0 = pl.multiple_of(jnp.maximum(8 * c - 16, 0), 8)
        toff = pl.multiple_of(w0 - 8 * c + 16, 8)
        n_abs = w0 + wrow
        dist = (c * QB + wq) - (n_abs * CMP_STRIDE + (CMP_BLOCK - 1))
        valid_w = (dist >= 0) & (n_abs < n_valid)
        s_ref[sub] = jnp.where(rown < w0, _scores(kc_ref[0], qs), NEG)
        bias_w = _rel_bias(tab_ref[0, pl.ds(toff, CMP_WIN), :], c31_ref[0])
        s_w = _scores(kc_ref[0, pl.ds(w0, CMP_WIN), :], qs) + bias_w
        s_ref[sub, pl.ds(w0, CMP_WIN), :] = jnp.where(valid_w, s_w, NEG)

    p_groups = []
    for sub in range(QSUB):
        has_any = ((c0 + sub) * QB + lane_q >= CMP_BLOCK - 1).astype(F32)
        s = s_ref[sub]
        m = jnp.max(s, axis=0, keepdims=True)
        e = jnp.exp2(s - m)
        p = e * (has_any / jnp.sum(e, axis=0, keepdims=True))
        ot_ref[0, sub] = jnp.dot(vct_ref[0], p.astype(BF16), preferred_element_type=F32)
        p_grp = p[:, 0:QB]
        for r in range(1, GROUP):
            p_grp = p_grp + p[:, r * QB:(r + 1) * QB]
        p_groups.append(p_grp.astype(BF16))

    nq = QSUB * QB
    scores = jnp.dot(ovl_ref[...], jnp.concatenate(p_groups, axis=1), preferred_element_type=F32)
    blk = lax.broadcasted_iota(jnp.int32, (nb, nq), 0)
    cur = lax.shift_right_logical(c0 * QB + lax.broadcasted_iota(jnp.int32, (nb, nq), 1),
                                  int(math.log2(SLC_BLOCK)))
    forced = (blk == 0) | (blk == cur) | (blk == cur - 1)
    work = jnp.where(forced, FORCE_SCORE, jnp.where(blk <= cur, scores, -1.0))
    blk_f = blk.astype(F32)
    sel = jnp.zeros((nb, nq), F32)
    for _ in range(topk):
        mx = jnp.max(work, axis=0, keepdims=True)
        first = jnp.min(jnp.where(work == mx, blk_f, float(nb)), axis=0, keepdims=True)
        hit = blk_f == first
        sel = jnp.where(hit, 1.0, sel)
        work = jnp.where(hit, -3e38, work)
    sel_ref[0] = sel


def _cmp_select(proj, q_col, kc, vct, tab, c31, ovl_t, n_valid):
    nkb = proj.shape[0] // QB
    G, NC = kc.shape[0], kc.shape[1]
    NB = ovl_t.shape[0]
    return pl.pallas_call(
        functools.partial(_cmp_select_kernel, n_valid=n_valid, topk=min(SLC_TOPK, NB)),
        grid=(G, nkb // QSUB),
        in_specs=[pl.BlockSpec((QSUB * QB, GROUP * HEAD_DIM), lambda g, c: (c, q_col + g)),
                  pl.BlockSpec((1, NC, HEAD_DIM), lambda g, c: (g, 0, 0)),
                  pl.BlockSpec((1, HEAD_DIM, NC), lambda g, c: (g, 0, 0)),
                  pl.BlockSpec((1, CMP_TAB, QW), lambda g, c: (g, 0, 0)),
                  pl.BlockSpec((1, 1, QW), lambda g, c: (g, 0, 0)),
                  pl.BlockSpec((NB, NC), lambda g, c: (0, 0))],
        out_specs=[pl.BlockSpec((1, QSUB, HEAD_DIM, QW), lambda g, c: (g, c, 0, 0)),
                   pl.BlockSpec((1, NB, QSUB * QB), lambda g, c: (g, 0, c))],
        out_shape=[jax.ShapeDtypeStruct((G, nkb, HEAD_DIM, QW), F32),
                   jax.ShapeDtypeStruct((G, NB, nkb * QB), F32)],
        scratch_shapes=[pltpu.VMEM((QSUB, NC, QW), F32)],
        compiler_params=_cparams(("arbitrary", "arbitrary")),
        name="cmp_attention_select",
    )(proj, kc, vct, tab, c31, ovl_t)


FAR_BLOCKS = 4


def _sel_block(c, q_ref, sel_row, gate_ref, ocmp_ref, owin_ref, o_ref, k_ref, v_ref, tab_ref, c31,
               m_ref, l_ref, acc_ref, s_a, s_b, p_a, p_b, al_a, al_b, s_n):
    _init_state(m_ref, l_ref, acc_ref)
    kk, qq = _key_query_iotas()
    qs = _stacked_queries(q_ref)
    spb = QB // SLC_BLOCK
    tk = FAR_BLOCKS * QB
    last_tile = k_ref.shape[0] // tk - 1
    n_far = jnp.maximum(c - 1, 0)
    n_tiles = (n_far + FAR_BLOCKS - 1) // FAR_BLOCKS

    def head_row(r, row_limit=None, r_load=None):
        row = sel_row(r if r_load is None else r_load)
        if row_limit is not None:
            row = jnp.where(r < row_limit, row, 0.0)
        return jnp.concatenate([row] * GROUP, axis=1)

    def tile_rows(i):
        return pl.ds(pl.multiple_of(jnp.minimum(i, last_tile) * tk, tk), tk)

    def far_scores(i, s_buf):
        s = _scores(k_ref[tile_rows(i), :], qs)
        r0 = i * (FAR_BLOCKS * spb)
        r0_load = jnp.minimum(i, last_tile) * (FAR_BLOCKS * spb)
        for t in range(FAR_BLOCKS * spb):
            row = head_row(r0 + t, n_far * spb, r0_load + t)
            blk = slice(t * SLC_BLOCK, (t + 1) * SLC_BLOCK)
            s_buf[blk, :] = jnp.where(row > 0.5, s[blk], NEG)

    def far_softmax(s_buf, p_buf, al_buf):
        blocks = [slice(t * SLC_BLOCK, (t + 1) * SLC_BLOCK) for t in range(FAR_BLOCKS * spb)]
        sub8 = lambda a: a.reshape(SLC_BLOCK // 8, 8, QW)
        m_part = jnp.full((8, QW), NEG, F32)
        for blk in blocks:
            m_part = jnp.maximum(m_part, jnp.max(sub8(s_buf[blk, :]), axis=0))
        m_old = m_ref[...]
        m_new = jnp.maximum(m_old, jnp.max(m_part, axis=0, keepdims=True))
        alpha = jnp.exp2(m_old - m_new)
        l_part = jnp.zeros((8, QW), F32)
        for blk in blocks:
            p = jnp.exp2(s_buf[blk, :] - m_new)
            l_part = l_part + jnp.sum(sub8(p), axis=0)
            p_buf[blk, :] = p.astype(BF16)
        l_ref[...] = alpha * l_ref[...] + jnp.sum(l_part, axis=0, keepdims=True)
        m_ref[...] = m_new
        al_buf[...] = alpha

    def far_values(i, p_buf, al_buf):
        pv = lax.dot_general(v_ref[tile_rows(i), :], p_buf[...], (((0,), (0,)), ((), ())),
                             preferred_element_type=F32)
        acc_ref[...] = acc_ref[...] * al_buf[...] + pv

    far_scores(0, s_a)
    far_scores(1, s_b)
    for p_buf, al_buf in ((p_a, al_a), (p_b, al_b)):
        p_buf[...] = jnp.zeros(p_buf.shape, BF16)
        al_buf[...] = jnp.ones(al_buf.shape, F32)

    blocks = [(jnp.maximum(c - 1, 0), 1, c * spb), (c, 0, None)]
    near_rows = [pl.ds(pl.multiple_of(kb * QB, QB), QB) for kb, _, _ in blocks]
    s = _scores(jnp.concatenate([k_ref[r, :] for r in near_rows], axis=0), qs)
    for t, (kb, delta, limit) in enumerate(blocks):
        msk = jnp.concatenate(
            [jnp.broadcast_to(head_row(kb * spb + u, limit), (SLC_BLOCK, QW)) for u in range(spb)],
            axis=0) > 0.5
        if delta == 0:
            msk = msk & (qq >= kk)
        sb = s[t * QB:(t + 1) * QB] + _rel_bias(tab_ref[0, delta], c31)
        s_n[t * QB:(t + 1) * QB, :] = jnp.where(msk, sb, NEG)

    def pair_body(j, carry):
        i = 2 * j
        far_values(jnp.maximum(i - 2, 0), p_a, al_a)
        far_values(jnp.maximum(i - 1, 0), p_b, al_b)
        far_softmax(s_a, p_a, al_a)
        far_softmax(s_b, p_b, al_b)
        far_scores(i + 2, s_a)
        far_scores(i + 3, s_b)
        return carry

    n_pairs = (n_tiles + 1) // 2
    lax.fori_loop(0, n_pairs, pair_body, 0)
    far_values(jnp.maximum(2 * n_pairs - 2, 0), p_a, al_a)
    far_values(jnp.maximum(2 * n_pairs - 1, 0), p_b, al_b)

    v_near = jnp.concatenate([v_ref[r, :] for r in near_rows], axis=0)
    _online_update(m_ref, l_ref, acc_ref, slice(None), s_n[...], v_near)

    o_slc = acc_ref[...] * (1.0 / l_ref[...])
    o_t = gate_ref[0:1, :] * ocmp_ref[...] + gate_ref[1:2, :] * o_slc + gate_ref[2:3, :] * owin_ref[...]
    _store_heads(o_ref, o_t)


def _sel_kernel(q_ref, k_ref, v_ref, tab_ref, c31_ref, sel_ref, gate_ref, ocmp_ref, owin_ref, o_ref,
                *scratch):
    _sel_block(pl.program_id(1), q_ref, lambda r: sel_ref[0, pl.ds(r, 1), :],
               gate_ref.at[0, 0], ocmp_ref.at[0, 0], owin_ref.at[0, 0], o_ref,
               k_ref, v_ref, tab_ref, c31_ref[0], *scratch)


def _sel_attention(proj, q_col, k_col, v_col, tabs, c31, sel_t, gates_s, ocmp_t, owin_t):
    S = proj.shape[0]
    G, NB = sel_t.shape[0], sel_t.shape[1]
    nkb = S // QB
    nq = 1
    tile = pl.BlockSpec((1, nq, HEAD_DIM, QW), lambda g, c: (g, c, 0, 0))
    return pl.pallas_call(
        _sel_kernel,
        grid=(G, nkb // nq),
        in_specs=_qkv_specs(S, q_col, k_col, v_col, nq * QB)
        + [pl.BlockSpec((1, 2, QB, QW), lambda g, c: (g, 0, 0, 0)),
           pl.BlockSpec((1, 1, QW), lambda g, c: (g, 0, 0)),
           pl.BlockSpec((1, NB, nq * QB), lambda g, c: (g, 0, c)),
           pl.BlockSpec((1, nq, N_BRANCH, QW), lambda g, c: (g, c, 0, 0)),
           tile, tile],
        out_specs=pl.BlockSpec((nq * QB, GROUP * HEAD_DIM), lambda g, c: (c, g)),
        out_shape=jax.ShapeDtypeStruct((S, G * GROUP * HEAD_DIM), F32),
        scratch_shapes=[pltpu.VMEM((1, QW), F32), pltpu.VMEM((1, QW), F32),
                        pltpu.VMEM((HEAD_DIM, QW), F32)]
        + [pltpu.VMEM((FAR_BLOCKS * QB, QW), F32)] * 2
        + [pltpu.VMEM((FAR_BLOCKS * QB, QW), BF16)] * 2
        + [pltpu.VMEM((1, QW), F32)] * 2
        + [pltpu.VMEM((2 * QB, QW), F32)],
        compiler_params=_cparams(("arbitrary", "arbitrary")),
        name="selected_attention_combine",
    )(proj, proj, proj, tabs, c31, sel_t, gates_s, ocmp_t, owin_t)


def _outproj_kernel(a_ref, w_ref, x_ref, o_ref, w_s):
    @pl.when(pl.program_id(1) == 0)
    def _():
        w_s[...] = w_ref[...].astype(BF16)

    o_ref[...] = x_ref[...] + jnp.dot(a_ref[...], w_s[...], preferred_element_type=F32)


def _outproj(a, w, x, tm=1024, tn=512):
    S, D = a.shape
    N = w.shape[1]
    tm = min(tm, S)
    return pl.pallas_call(
        _outproj_kernel,
        grid=(N // tn, S // tm),
        in_specs=[pl.BlockSpec((tm, D), lambda j, i: (i, 0)),
                  pl.BlockSpec((D, tn), lambda j, i: (0, j)),
                  pl.BlockSpec((tm, tn), lambda j, i: (i, j))],
        out_specs=pl.BlockSpec((tm, tn), lambda j, i: (i, j)),
        out_shape=jax.ShapeDtypeStruct((S, N), F32),
        scratch_shapes=[pltpu.VMEM((D, tn), BF16)],
        compiler_params=_cparams(("arbitrary", "arbitrary")),
        name="outproj_residual",
    )(a, w, x)


HALO = 16
FFN_ROW_CHUNKS = 2


def _ffn_a_kernel(halo_ref, a_ref, wg_ref, wu_ref, cw_ref, cb_ref, o_ref, g_s):
    i = pl.program_id(0)
    tm = a_ref.shape[0]
    halo = halo_ref[...]
    halo = jnp.where(i > 0, halo, jnp.zeros_like(halo))
    wg = wg_ref[...].astype(BF16)
    wu = wu_ref[...].astype(BF16)
    hm = tm // FFN_ROW_CHUNKS
    for h in range(FFN_ROW_CHUNKS):
        a = a_ref[h * hm:(h + 1) * hm, :]
        g = g_s.at[h]
        if h == 0:
            g[...] = jnp.dot(jnp.concatenate([halo, a], axis=0), wg, preferred_element_type=F32)
        else:
            g[0:HALO, :] = g_s[h - 1, hm:hm + HALO, :]
            g[HALO:, :] = jnp.dot(a, wg, preferred_element_type=F32)
        up = jnp.dot(a, wu, preferred_element_type=F32)
        gext = g[...]
        g1 = pltpu.roll(gext, 1, 0)
        g2 = pltpu.roll(g1, 1, 0)
        y = cb_ref[...] + g2[HALO:] * cw_ref[0:1, :]
        y = y + g1[HALO:] * cw_ref[1:2, :]
        y = y + gext[HALO:] * cw_ref[2:3, :]
        act = y * (1.0 / (1.0 + jnp.exp(-y)))
        o_ref[h * hm:(h + 1) * hm, :] = (act * up).astype(o_ref.dtype)


def _ffn_a(hf, wg, wu, cw, cb, tm=2048, tf=256):
    S, D = hf.shape
    Fp = wg.shape[1]
    assert Fp % tf == 0
    tm = min(tm, S)
    hb = tm // HALO
    return pl.pallas_call(
        _ffn_a_kernel,
        grid=(S // tm, Fp // tf),
        in_specs=[pl.BlockSpec((HALO, D), lambda i, f: (jnp.maximum(i * hb - 1, 0), 0)),
                  pl.BlockSpec((tm, D), lambda i, f: (i, 0), pipeline_mode=pl.Buffered(1)),
                  pl.BlockSpec((D, tf), lambda i, f: (0, f)),
                  pl.BlockSpec((D, tf), lambda i, f: (0, f)),
                  pl.BlockSpec((cw.shape[0], tf), lambda i, f: (0, f)),
                  pl.BlockSpec((1, tf), lambda i, f: (0, f))],
        out_specs=pl.BlockSpec((tm, tf), lambda i, f: (i, f)),
        out_shape=jax.ShapeDtypeStruct((S, Fp), BF16),
        scratch_shapes=[pltpu.VMEM((FFN_ROW_CHUNKS, HALO + tm // FFN_ROW_CHUNKS, tf), F32)],
        compiler_params=_cparams(("arbitrary", "arbitrary")),
        name="ffn_gate_up",
    )(hf, hf, wg, wu, cw, cb)


def _ffn_b_kernel(a_ref, w_ref, x_ref, o_ref):
    o_ref[...] = x_ref[...] + jnp.dot(a_ref[...], w_ref[...], preferred_element_type=F32)


def _ffn_b(h, w, x, tm=512, tn=512):
    S, F = h.shape
    N = w.shape[1]
    tm = min(tm, S)
    return pl.pallas_call(
        _ffn_b_kernel,
        grid=(S // tm, N // tn),
        in_specs=[pl.BlockSpec((tm, F), lambda i, j: (i, 0)),
                  pl.BlockSpec((F, tn), lambda i, j: (0, j)),
                  pl.BlockSpec((tm, tn), lambda i, j: (i, j))],
        out_specs=pl.BlockSpec((tm, tn), lambda i, j: (i, j)),
        out_shape=jax.ShapeDtypeStruct((S, N), F32),
        compiler_params=_cparams(("arbitrary", "arbitrary")),
        name="ffn_down_residual",
    )(h, w, x)


def _layer(x, rel_bias, norm_mix_g, w_in, a_q_norm_g, a_k_norm_g, a_sinks, b_q_norm_g, b_k_norm_g,
           cmp_pos_emb, cmp_w1, cmp_b1, cmp_w2, cmp_b2, out_norm_g, w_out, norm_ffn_g, w_gate,
           w_up, conv_w, conv_b, w_down):
    S, D = x.shape
    aw = A_Q_HEADS * HEAD_DIM
    akv = A_KV_HEADS * HEAD_DIM
    bw = B_Q_HEADS * HEAD_DIM
    bkv = B_KV_HEADS * HEAD_DIM
    sizes = [aw, akv, akv, bw] + [bkv] * 6 + [N_BRANCH * B_Q_HEADS]
    offs = np.concatenate([[0], np.cumsum(sizes)]).tolist()
    wt = jnp.swapaxes(w_in, 0, 1)
    tn = 512
    order_a = [0, 1, 2, 3, 6, 7, 8, 9]
    n_gate = sizes[10]
    wg_t = jnp.pad(wt[offs[10]:offs[10] + n_gate], ((0, QB - n_gate), (0, 0)))
    one = jnp.ones((HEAD_DIM,), F32)
    seg_gain = {0: a_q_norm_g, 1: a_k_norm_g * KEY_SCALE, 3: b_q_norm_g,
                6: b_k_norm_g[1] * KEY_SCALE, 8: b_k_norm_g[2] * KEY_SCALE}
    kinds, tiles, gains, col = [], [], [], {}
    c0 = 0
    for k in order_a:
        col[k] = c0
        for t in range(sizes[k] // tn):
            kinds.append(1 if k in seg_gain else 0)
            tiles.append(offs[k] // tn + t)
            gains.append(seg_gain.get(k, one))
        c0 += sizes[k]
    kinds = jnp.asarray(kinds, jnp.int32)
    tiles = jnp.asarray(tiles, jnp.int32)
    gains = jnp.stack(gains).astype(F32).reshape(len(gains), 1, HEAD_DIM)

    hn = _rmsnorm([x], norm_mix_g)
    proj = _inproj_a(hn, wt, kinds, tiles, gains, tn=tn)
    n_kv = 2 * bkv // tn
    kv32 = _inproj_a(hn, wt, jnp.zeros((n_kv,), jnp.int32),
                     jnp.asarray([offs[4] // tn + t for t in range(n_kv)], jnp.int32),
                     jnp.ones((n_kv, 1, HEAD_DIM), F32), tn=tn, out_dtype=F32,
                     name="inproj_cmp")
    gates = _gates(hn, wg_t)

    tab_a = rel_bias[:, :A_Q_HEADS]
    tab_b = rel_bias[:, A_Q_HEADS:]
    near_a = _stack_heads(_near_bias_tables(tab_a), A_KV_HEADS)
    near_b = _stack_heads(_near_bias_tables(tab_b), B_KV_HEADS)
    head_row = lambda v, G: _stack_heads(jnp.broadcast_to(v.astype(F32)[:, None, None],
                                                          (v.shape[0], 1, QB)), G)
    c31_a = head_row(tab_a[N_BUCKETS - 1], A_KV_HEADS)
    c31_b = head_row(tab_b[N_BUCKETS - 1], B_KV_HEADS)

    qcol = lambda k: col[k] // (GROUP * HEAD_DIM)
    hcol = lambda k: col[k] // HEAD_DIM
    o_a = _band_attention(proj, A_KV_HEADS, qcol(0), hcol(1), hcol(2), near_a, c31_a,
                          head_row(a_sinks, A_KV_HEADS), A_WINDOW, transposed_out=False)

    NC = S // CMP_STRIDE
    n_cmp = (S - CMP_BLOCK) // CMP_STRIDE + 1
    w1 = cmp_w1.reshape(2, CMP_BLOCK, HEAD_DIM, cmp_w1.shape[-1]).astype(BF16)
    kc, vct = _compress(kv32, B_KV_HEADS, cmp_pos_emb.astype(F32), w1, cmp_b1[:, None, :].astype(F32),
                        cmp_w2.astype(BF16), cmp_b2[:, None, :].astype(F32),
                        (b_k_norm_g[0] * KEY_SCALE).reshape(1, HEAD_DIM).astype(F32))

    NB = S // SLC_BLOCK
    ii = np.arange(NC)[None, :]
    jj = np.arange(NB)[:, None]
    ovl_t = ((ii * CMP_STRIDE <= jj * SLC_BLOCK + SLC_BLOCK - 1)
             & (ii * CMP_STRIDE + CMP_BLOCK - 1 >= jj * SLC_BLOCK) & (ii < n_cmp))
    ovl_t = jnp.asarray(ovl_t.astype(np.float32), BF16)
    ocmp_t, sel_t = _cmp_select(proj, qcol(3), kc, vct,
                                _stack_heads(_cmp_bias_table(tab_b), B_KV_HEADS), c31_b, ovl_t, n_cmp)

    owin_t = _band_attention(proj, B_KV_HEADS, qcol(3), hcol(8), hcol(9), near_b, c31_b, None,
                             B_WINDOW, transposed_out=True)
    gates_s = gates[:, :n_gate].reshape(S // QB, QB, B_KV_HEADS, GROUP, N_BRANCH)
    gates_s = gates_s.transpose(2, 0, 4, 3, 1).reshape(B_KV_HEADS, S // QB, N_BRANCH, QW)
    o_b = _sel_attention(proj, qcol(3), hcol(6), hcol(7), near_b, c31_b, sel_t, gates_s,
                         ocmp_t, owin_t)

    on = _rmsnorm([o_a, o_b], out_norm_g)
    x2 = _outproj(on, w_out, x)

    hf = _rmsnorm([x2], norm_ffn_g)
    hmid = _ffn_a(hf, w_gate, w_up, conv_w.astype(F32), conv_b[None, :].astype(F32))
    return _ffn_b(hmid, w_down.astype(BF16), x2)


def kernel(x, rel_bias, norm_mix_g, w_in, a_q_norm_g, a_k_norm_g, a_sinks, b_q_norm_g, b_k_norm_g,
           cmp_pos_emb, cmp_w1, cmp_b1, cmp_w2, cmp_b2, out_norm_g, w_out, norm_ffn_g, w_gate, w_up,
           conv_w, conv_b, w_down):
    depth = w_in.shape[0]
    batch = x.shape[0]
    outs = []
    for b in range(batch):
        h = x[b]
        for l in range(depth):
            h = _layer(h, rel_bias, norm_mix_g[l], w_in[l], a_q_norm_g[l], a_k_norm_g[l], a_sinks[l],
                       b_q_norm_g[l], b_k_norm_g[l], cmp_pos_emb[l], cmp_w1[l], cmp_b1[l], cmp_w2[l],
                       cmp_b2[l], out_norm_g[l], w_out[l], norm_ffn_g[l], w_gate[l], w_up[l],
                       conv_w[l], conv_b[l], w_down[l])
        outs.append(h)
    return jnp.stack(outs)
```

```python
import functools
import math

import numpy as np
import jax
import jax.numpy as jnp
from jax import lax
from jax.experimental import pallas as pl
from jax.experimental.pallas import tpu as pltpu

F32 = jnp.float32
BF16 = jnp.bfloat16

HEAD_DIM = 128
A_Q_HEADS = 16
A_KV_HEADS = 4
B_Q_HEADS = 16
B_KV_HEADS = 4
GROUP = 4
A_WINDOW = 128
B_WINDOW = 512
CMP_BLOCK = 32
CMP_STRIDE = 16
SLC_BLOCK = 64
SLC_TOPK = 16
N_BRANCH = 3
N_BUCKETS = 32
MAX_DISTANCE = 128
EPS = 1e-6
NEG = -1e30
FORCE_SCORE = 1e6
SCALE = HEAD_DIM ** -0.5

QB = 128
CMP_WIN = 24
CMP_TAB = 40
VMEM_LIMIT = 56 * 1024 * 1024


def _cparams(sem, **kw):
    return pltpu.CompilerParams(dimension_semantics=sem, vmem_limit_bytes=VMEM_LIMIT, **kw)


def _t5_bucket_np(dist):
    n = np.maximum(dist, 0)
    max_exact = N_BUCKETS // 2
    nf = np.maximum(n, 1).astype(np.float32)
    large = max_exact + (np.log(nf / max_exact) / math.log(MAX_DISTANCE / max_exact)
                         * (N_BUCKETS - max_exact)).astype(np.int32)
    large = np.minimum(large, N_BUCKETS - 1)
    return np.where(n < max_exact, n, large).astype(np.int32)


def _bias_from_buckets(tab, idx):
    onehot = (idx[None] == np.arange(N_BUCKETS).reshape((-1,) + (1,) * idx.ndim)).astype(np.float32)
    return jnp.einsum('bh,b...->h...', tab.astype(F32), jnp.asarray(onehot),
                      precision=lax.Precision.HIGHEST)


def _near_bias_tables(tab):
    kk = np.arange(QB)[:, None]
    qq = np.arange(QB)[None, :]
    return _bias_from_buckets(tab, np.stack([_t5_bucket_np(qq - kk), _t5_bucket_np(QB + qq - kk)]))


def _cmp_bias_table(tab):
    npr = np.arange(CMP_TAB)[:, None] - 16
    qq = np.arange(QB)[None, :]
    return _bias_from_buckets(tab, _t5_bucket_np(qq - CMP_STRIDE * npr - (CMP_BLOCK - 1)))


def _rmsnorm_kernel(*refs, n_in):
    x_refs, g_ref, o_ref = refs[:n_in], refs[n_in], refs[n_in + 1]
    off = 0
    for x_ref in x_refs:
        x = x_ref[...]
        w = x.shape[-1]
        y = x * lax.rsqrt(jnp.mean(x * x, axis=-1, keepdims=True) + EPS)
        o_ref[:, off:off + w] = (y * g_ref[:, off:off + w]).astype(o_ref.dtype)
        off += w


def _rmsnorm(xs, gain, tr=512):
    S = xs[0].shape[0]
    widths = [x.shape[1] for x in xs]
    n = sum(widths)
    tr = min(tr, S)
    return pl.pallas_call(
        functools.partial(_rmsnorm_kernel, n_in=len(xs)),
        grid=(S // tr,),
        in_specs=[pl.BlockSpec((tr, w), lambda i: (i, 0)) for w in widths]
        + [pl.BlockSpec((1, n), lambda i: (0, 0))],
        out_specs=pl.BlockSpec((tr, n), lambda i: (i, 0)),
        out_shape=jax.ShapeDtypeStruct((S, n), BF16),
        compiler_params=_cparams(("arbitrary",)),
        name="rmsnorm",
    )(*xs, gain.reshape(1, n).astype(F32))


NORM_ROW_CHUNKS = 2


def _dot_wt(a, wt):
    return lax.dot_general(a, wt.astype(BF16), (((1,), (1,)), ((), ())), preferred_element_type=F32)


def _inproj_a_kernel(kind_ref, tile_ref, a_ref, w_ref, g_ref, o_ref, w_s):
    j = pl.program_id(0)

    @pl.when(pl.program_id(1) == 0)
    def _():
        w_s[...] = w_ref[...].astype(BF16)

    @pl.when(kind_ref[j] == 0)
    def _():
        o_ref[...] = _dot_wt(a_ref[...], w_s[...]).astype(o_ref.dtype)

    @pl.when(kind_ref[j] == 1)
    def _():
        g = g_ref[0]
        hm = a_ref.shape[0] // NORM_ROW_CHUNKS
        for c in range(NORM_ROW_CHUNKS):
            rows = slice(c * hm, (c + 1) * hm)
            acc = _dot_wt(a_ref[rows, :], w_s[...])
            for h in range(acc.shape[1] // HEAD_DIM):
                sl = acc[:, h * HEAD_DIM:(h + 1) * HEAD_DIM]
                y = sl * lax.rsqrt(jnp.mean(sl * sl, axis=-1, keepdims=True) + EPS)
                o_ref[rows, h * HEAD_DIM:(h + 1) * HEAD_DIM] = (y * g).astype(o_ref.dtype)


def _inproj_a(hn, wt, kinds, tiles, gains, tm=1024, tn=512):
    S, D = hn.shape
    n_tiles = kinds.shape[0]
    tm = min(tm, S)
    grid_spec = pltpu.PrefetchScalarGridSpec(
        num_scalar_prefetch=2,
        grid=(n_tiles, S // tm),
        in_specs=[pl.BlockSpec((tm, D), lambda j, i, k, t: (i, 0)),
                  pl.BlockSpec((tn, D), lambda j, i, k, t: (t[j], 0)),
                  pl.BlockSpec((1, 1, HEAD_DIM), lambda j, i, k, t: (j, 0, 0))],
        out_specs=pl.BlockSpec((tm, tn), lambda j, i, k, t: (i, j)),
        scratch_shapes=[pltpu.VMEM((tn, D), BF16)],
    )
    return pl.pallas_call(
        _inproj_a_kernel,
        grid_spec=grid_spec,
        out_shape=jax.ShapeDtypeStruct((S, n_tiles * tn), BF16),
        compiler_params=_cparams(("arbitrary", "arbitrary")),
        name="inproj_heads",
    )(kinds, tiles, hn, wt, gains)


def _inproj_b_kernel(a_ref, w_ref, wg_ref, o_ref, gate_ref):
    a = a_ref[...]
    o_ref[...] = _dot_wt(a, w_ref[...])

    @pl.when(pl.program_id(1) == 0)
    def _():
        gate_ref[...] = 1.0 / (1.0 + jnp.exp(-_dot_wt(a, wg_ref[...])))


def _inproj_b(hn, wt, tile0, n_tiles, wg_t, tm=1024, tn=512):
    S, D = hn.shape
    tm = min(tm, S)
    ng = wg_t.shape[0]
    return pl.pallas_call(
        _inproj_b_kernel,
        grid=(S // tm, n_tiles),
        in_specs=[pl.BlockSpec((tm, D), lambda i, j: (i, 0)),
                  pl.BlockSpec((tn, D), lambda i, j: (tile0 + j, 0)),
                  pl.BlockSpec((ng, D), lambda i, j: (0, 0))],
        out_specs=[pl.BlockSpec((tm, tn), lambda i, j: (i, j)),
                   pl.BlockSpec((tm, ng), lambda i, j: (i, 0))],
        out_shape=[jax.ShapeDtypeStruct((S, n_tiles * tn), F32),
                   jax.ShapeDtypeStruct((S, ng), F32)],
        compiler_params=_cparams(("arbitrary", "arbitrary")),
        name="inproj_cmp_gates",
    )(hn, wt, wg_t)


def _gelu_tanh(x):
    return 0.5 * x * (1.0 + jnp.tanh(math.sqrt(2.0 / math.pi) * (x + 0.044715 * (x * x * x))))


def _compress_kernel(tk_ref, tv_ref, pos_ref, w1_ref, b1_ref, w2_ref, b2_ref, gk_ref,
                     kc_ref, vct_ref):
    nc = kc_ref.shape[1]

    def mlp(t_ref, kv):
        a0 = jnp.zeros((nc, w1_ref.shape[-1]), F32)
        a1 = jnp.zeros((nc, w1_ref.shape[-1]), F32)
        for b in range(CMP_STRIDE):
            t = t_ref[pl.ds(b, nc, stride=CMP_STRIDE), :]
            lo, hi = b, CMP_STRIDE + b
            a0 = a0 + jnp.dot((t + pos_ref[kv, lo:lo + 1, :]).astype(BF16), w1_ref[kv, lo],
                              preferred_element_type=F32)
            a1 = a1 + jnp.dot((t + pos_ref[kv, hi:hi + 1, :]).astype(BF16), w1_ref[kv, hi],
                              preferred_element_type=F32)
        h = a0 + pltpu.roll(a1, nc - 1, 0) + b1_ref[kv]
        h = _gelu_tanh(h)
        return jnp.dot(h.astype(BF16), w2_ref[kv], preferred_element_type=F32) + b2_ref[kv]

    ck = mlp(tk_ref, 0)
    ck = ck * lax.rsqrt(jnp.mean(ck * ck, axis=-1, keepdims=True) + EPS) * gk_ref[...]
    kc_ref[0] = ck.astype(kc_ref.dtype)
    cv = mlp(tv_ref, 1)
    vct_ref[0] = cv.T.astype(vct_ref.dtype)


def _compress(kv32, G, pos, w1, b1, w2, b2, gk):
    S = kv32.shape[0]
    NC = S // CMP_STRIDE
    hid = w1.shape[-1]
    return pl.pallas_call(
        _compress_kernel,
        grid=(G,),
        in_specs=[pl.BlockSpec((S, HEAD_DIM), lambda g: (0, g)),
                  pl.BlockSpec((S, HEAD_DIM), lambda g: (0, G + g)),
                  pl.BlockSpec((2, CMP_BLOCK, HEAD_DIM), lambda g: (0, 0, 0)),
                  pl.BlockSpec((2, CMP_BLOCK, HEAD_DIM, hid), lambda g: (0, 0, 0, 0)),
                  pl.BlockSpec((2, 1, hid), lambda g: (0, 0, 0)),
                  pl.BlockSpec((2, hid, HEAD_DIM), lambda g: (0, 0, 0)),
                  pl.BlockSpec((2, 1, HEAD_DIM), lambda g: (0, 0, 0)),
                  pl.BlockSpec((1, HEAD_DIM), lambda g: (0, 0))],
        out_specs=[pl.BlockSpec((1, NC, HEAD_DIM), lambda g: (g, 0, 0)),
                   pl.BlockSpec((1, HEAD_DIM, NC), lambda g: (g, 0, 0))],
        out_shape=[jax.ShapeDtypeStruct((G, NC, HEAD_DIM), BF16),
                   jax.ShapeDtypeStruct((G, HEAD_DIM, NC), BF16)],
        compiler_params=_cparams(("arbitrary",)),
        name="compress",
    )(kv32, kv32, pos, w1, b1, w2, b2, gk)


QW = GROUP * QB


def _stacked_queries(q_ref):
    return jnp.concatenate([q_ref[:, r * HEAD_DIM:(r + 1) * HEAD_DIM] for r in range(GROUP)], axis=0)


LOG2E = math.log2(math.e)
KEY_SCALE = SCALE * LOG2E
QSUB = 8
BAND_QSUB = 4


def _scores(k_t, qs):
    return lax.dot_general(k_t, qs, (((1,), (1,)), ((), ())), preferred_element_type=F32)


def _rel_bias(tab, c31):
    return (tab - c31) * LOG2E


def _init_state(m_ref, l_ref, acc_ref):
    m_ref[...] = jnp.full(m_ref.shape, NEG, F32)
    l_ref[...] = jnp.zeros(l_ref.shape, F32)
    acc_ref[...] = jnp.zeros(acc_ref.shape, F32)


def _online_update(m_ref, l_ref, acc_ref, lanes, t, v_t):
    m_old = m_ref[:, lanes]
    m_new = jnp.maximum(m_old, jnp.max(t, axis=0, keepdims=True))
    alpha = jnp.exp2(m_old - m_new)
    p = jnp.exp2(t - m_new)
    l_ref[:, lanes] = alpha * l_ref[:, lanes] + jnp.sum(p, axis=0, keepdims=True)
    pv = lax.dot_general(v_t, p.astype(BF16), (((0,), (0,)), ((), ())), preferred_element_type=F32)
    acc_ref[:, lanes] = acc_ref[:, lanes] * alpha + pv
    m_ref[:, lanes] = m_new


def _key_query_iotas():
    kk = lax.broadcasted_iota(jnp.int32, (QB, QW), 0)
    qq = lax.broadcasted_iota(jnp.int32, (QB, QW), 1) & (QB - 1)
    return kk, qq


def _load_kv(k_ref, v_ref, kb0, n):
    rows = pl.ds(pl.multiple_of(kb0 * QB, QB), n * QB)
    return k_ref[rows, :], v_ref[rows, :]


def _store_heads(o_ref, o_t):
    for r in range(GROUP):
        o_ref[:, r * HEAD_DIM:(r + 1) * HEAD_DIM] = o_t[:, r * QB:(r + 1) * QB].T


def _stack_heads(a, G):
    a = a.reshape((G, GROUP) + a.shape[1:])
    a = jnp.moveaxis(a, 1, -2)
    return a.reshape(a.shape[:-2] + (QW,))


def _band_kernel(q_ref, k_ref, v_ref, tab_ref, c31_ref, sink_ref, o_ref, m_ref, l_ref, acc_ref,
                 *, nback, use_sinks, transposed_out, qsub):
    c0 = pl.program_id(1) * qsub
    _init_state(m_ref, l_ref, acc_ref)
    kk, qq = _key_query_iotas()
    c31 = c31_ref[0]

    def masked_scores(sub, kb0, deltas):
        k_t, v_t = _load_kv(k_ref, v_ref, kb0, len(deltas))
        s = _scores(k_t, _stacked_queries(q_ref.at[sub * QB:(sub + 1) * QB]))
        parts = []
        for t, delta in enumerate(deltas):
            sb = s[t * QB:(t + 1) * QB]
            if delta <= 1:
                sb = sb + _rel_bias(tab_ref[0, delta], c31)
            if delta == 0:
                sb = jnp.where(qq >= kk, sb, NEG)
            elif delta == nback:
                sb = jnp.where(qq < kk, sb, NEG)
            parts.append(sb)
        return (parts[0] if len(parts) == 1 else jnp.concatenate(parts, axis=0)), v_t

    def update(sub, s, v_t):
        _online_update(m_ref.at[sub], l_ref.at[sub], acc_ref.at[sub], slice(None), s, v_t)

    @pl.when(c0 >= nback)
    def _():
        band = list(range(nback, -1, -1))
        tiles = [masked_scores(sub, c0 + sub - nback, band) for sub in range(qsub)]
        for sub, (s, v_t) in enumerate(tiles):
            update(sub, s, v_t)

    @pl.when(c0 < nback)
    def _():
        for sub in range(qsub):
            for delta in range(nback, -1, -1):
                @pl.when(c0 + sub - delta >= 0)
                def _(sub=sub, delta=delta):
                    update(sub, *masked_scores(sub, c0 + sub - delta, [delta]))

    for sub in range(qsub):
        l = l_ref[sub]
        acc = acc_ref[sub]
        if use_sinks:
            m = m_ref[sub]
            sk = _rel_bias(sink_ref[0], c31)
            m_f = jnp.maximum(m, sk)
            a = jnp.exp2(m - m_f)
            l = l * a + jnp.exp2(sk - m_f)
            acc = acc * a
        o_t = acc * (1.0 / l)
        if transposed_out:
            o_ref[0, sub] = o_t
        else:
            _store_heads(o_ref.at[sub * QB:(sub + 1) * QB], o_t)


def _qkv_specs(S, q_col, k_col, v_col, q_rows=QB):
    return [pl.BlockSpec((q_rows, GROUP * HEAD_DIM), lambda g, c: (c, q_col + g)),
            pl.BlockSpec((S, HEAD_DIM), lambda g, c: (0, k_col + g)),
            pl.BlockSpec((S, HEAD_DIM), lambda g, c: (0, v_col + g))]


def _band_attention(proj, G, q_col, k_col, v_col, tabs, c31, sinks, window, transposed_out):
    S = proj.shape[0]
    nkb = S // QB
    nback = -(-(window - 1) // QB)
    use_sinks = sinks is not None
    if sinks is None:
        sinks = jnp.zeros_like(c31)
    qsub = min(nkb, BAND_QSUB * (2 if nback == 1 else 1))
    if transposed_out:
        out_shape = jax.ShapeDtypeStruct((G, nkb, HEAD_DIM, QW), F32)
        out_spec = pl.BlockSpec((1, qsub, HEAD_DIM, QW), lambda g, c: (g, c, 0, 0))
    else:
        out_shape = jax.ShapeDtypeStruct((S, G * GROUP * HEAD_DIM), F32)
        out_spec = pl.BlockSpec((qsub * QB, GROUP * HEAD_DIM), lambda g, c: (c, g))
    row = pl.BlockSpec((1, 1, QW), lambda g, c: (g, 0, 0))
    return pl.pallas_call(
        functools.partial(_band_kernel, nback=nback, use_sinks=use_sinks,
                          transposed_out=transposed_out, qsub=qsub),
        grid=(G, nkb // qsub),
        in_specs=_qkv_specs(S, q_col, k_col, v_col, qsub * QB)
        + [pl.BlockSpec((1, 2, QB, QW), lambda g, c: (g, 0, 0, 0)), row, row],
        out_specs=out_spec,
        out_shape=out_shape,
        scratch_shapes=[pltpu.VMEM((qsub, 1, QW), F32), pltpu.VMEM((qsub, 1, QW), F32),
                        pltpu.VMEM((qsub, HEAD_DIM, QW), F32)],
        compiler_params=_cparams(("arbitrary", "arbitrary")),
        name="band_attention_w%d" % window,
    )(proj, proj, proj, tabs, c31, sinks)


def _cmp_select_kernel(q_ref, kc_ref, vct_ref, tab_ref, c31_ref, ovl_ref, ot_ref, sel_ref, s_ref,
                       *, n_valid, topk):
    c0 = pl.program_id(1) * QSUB
    nc = kc_ref.shape[1]
    nb = ovl_ref.shape[0]
    rown = lax.broadcasted_iota(jnp.int32, (nc, QW), 0)
    wrow = lax.broadcasted_iota(jnp.int32, (CMP_WIN, QW), 0)
    wq = lax.broadcasted_iota(jnp.int32, (CMP_WIN, QW), 1) & (QB - 1)
    lane_q = lax.broadcasted_iota(jnp.int32, (1, QW), 1) & (QB - 1)

    for sub in range(QSUB):
        c = c0 + sub
        qs = _stacked_queries(q_ref.at[sub * QB:(sub + 1) * QB])
        w0 = pl.multiple_of(jnp.maximum(8 * c - 16, 0), 8)
        toff = pl.multiple_of(w0 - 8 * c + 16, 8)
        n_abs = w0 + wrow
        dist = (c * QB + wq) - (n_abs * CMP_STRIDE + (CMP_BLOCK - 1))
        valid_w = (dist >= 0) & (n_abs < n_valid)
        s_ref[sub] = jnp.where(rown < w0, _scores(kc_ref[0], qs), NEG)
        bias_w = _rel_bias(tab_ref[0, pl.ds(toff, CMP_WIN), :], c31_ref[0])
        s_w = _scores(kc_ref[0, pl.ds(w0, CMP_WIN), :], qs) + bias_w
        s_ref[sub, pl.ds(w0, CMP_WIN), :] = jnp.where(valid_w, s_w, NEG)

    p_groups = []
    for sub in range(QSUB):
        has_any = ((c0 + sub) * QB + lane_q >= CMP_BLOCK - 1).astype(F32)
        s = s_ref[sub]
        m = jnp.max(s, axis=0, keepdims=True)
        e = jnp.exp2(s - m)
        p = e * (has_any / jnp.sum(e, axis=0, keepdims=True))
        ot_ref[0, sub] = jnp.dot(vct_ref[0], p.astype(BF16), preferred_element_type=F32)
        p_grp = p[:, 0:QB]
        for r in range(1, GROUP):
            p_grp = p_grp + p[:, r * QB:(r + 1) * QB]
        p_groups.append(p_grp.astype(BF16))

    nq = QSUB * QB
    scores = jnp.dot(ovl_ref[...], jnp.concatenate(p_groups, axis=1), preferred_element_type=F32)
    blk = lax.broadcasted_iota(jnp.int32, (nb, nq), 0)
    cur = lax.shift_right_logical(c0 * QB + lax.broadcasted_iota(jnp.int32, (nb, nq), 1),
                                  int(math.log2(SLC_BLOCK)))
    forced = (blk == 0) | (blk == cur) | (blk == cur - 1)
    work = jnp.where(forced, FORCE_SCORE, jnp.where(blk <= cur, scores, -1.0))
    blk_f = blk.astype(F32)
    sel = jnp.zeros((nb, nq), F32)
    for _ in range(topk):
        mx = jnp.max(work, axis=0, keepdims=True)
        first = jnp.min(jnp.where(work == mx, blk_f, float(nb)), axis=0, keepdims=True)
        hit = blk_f == first
        sel = jnp.where(hit, 1.0, sel)
        work = jnp.where(hit, -3e38, work)
    sel_ref[0] = sel


def _cmp_select(proj, q_col, kc, vct, tab, c31, ovl_t, n_valid):
    nkb = proj.shape[0] // QB
    G, NC = kc.shape[0], kc.shape[1]
    NB = ovl_t.shape[0]
    return pl.pallas_call(
        functools.partial(_cmp_select_kernel, n_valid=n_valid, topk=min(SLC_TOPK, NB)),
        grid=(G, nkb // QSUB),
        in_specs=[pl.BlockSpec((QSUB * QB, GROUP * HEAD_DIM), lambda g, c: (c, q_col + g)),
                  pl.BlockSpec((1, NC, HEAD_DIM), lambda g, c: (g, 0, 0)),
                  pl.BlockSpec((1, HEAD_DIM, NC), lambda g, c: (g, 0, 0)),
                  pl.BlockSpec((1, CMP_TAB, QW), lambda g, c: (g, 0, 0)),
                  pl.BlockSpec((1, 1, QW), lambda g, c: (g, 0, 0)),
                  pl.BlockSpec((NB, NC), lambda g, c: (0, 0))],
        out_specs=[pl.BlockSpec((1, QSUB, HEAD_DIM, QW), lambda g, c: (g, c, 0, 0)),
                   pl.BlockSpec((1, NB, QSUB * QB), lambda g, c: (g, 0, c))],
        out_shape=[jax.ShapeDtypeStruct((G, nkb, HEAD_DIM, QW), F32),
                   jax.ShapeDtypeStruct((G, NB, nkb * QB), F32)],
        scratch_shapes=[pltpu.VMEM((QSUB, NC, QW), F32)],
        compiler_params=_cparams(("arbitrary", "arbitrary")),
        name="cmp_attention_select",
    )(proj, kc, vct, tab, c31, ovl_t)


FAR_BLOCKS = 4


def _sel_block(c, q_ref, sel_row, gate_ref, ocmp_ref, owin_ref, o_ref, k_ref, v_ref, tab_ref, c31,
               m_ref, l_ref, acc_ref, s_a, s_b, p_a, p_b, al_a, al_b, s_n):
    _init_state(m_ref, l_ref, acc_ref)
    kk, qq = _key_query_iotas()
    qs = _stacked_queries(q_ref)
    spb = QB // SLC_BLOCK
    tk = FAR_BLOCKS * QB
    last_tile = k_ref.shape[0] // tk - 1
    n_far = jnp.maximum(c - 1, 0)
    n_tiles = (n_far + FAR_BLOCKS - 1) // FAR_BLOCKS

    def head_row(r, row_limit=None, r_load=None):
        row = sel_row(r if r_load is None else r_load)
        if row_limit is not None:
            row = jnp.where(r < row_limit, row, 0.0)
        return jnp.concatenate([row] * GROUP, axis=1)

    def tile_rows(i):
        return pl.ds(pl.multiple_of(jnp.minimum(i, last_tile) * tk, tk), tk)

    def far_scores(i, s_buf):
        s = _scores(k_ref[tile_rows(i), :], qs)
        r0 = i * (FAR_BLOCKS * spb)
        r0_load = jnp.minimum(i, last_tile) * (FAR_BLOCKS * spb)
        for t in range(FAR_BLOCKS * spb):
            row = head_row(r0 + t, n_far * spb, r0_load + t)
            blk = slice(t * SLC_BLOCK, (t + 1) * SLC_BLOCK)
            s_buf[blk, :] = jnp.where(row > 0.5, s[blk], NEG)

    def far_softmax(s_buf, p_buf, al_buf):
        blocks = [slice(t * SLC_BLOCK, (t + 1) * SLC_BLOCK) for t in range(FAR_BLOCKS * spb)]
        sub8 = lambda a: a.reshape(SLC_BLOCK // 8, 8, QW)
        m_part = jnp.full((8, QW), NEG, F32)
        for blk in blocks:
            m_part = jnp.maximum(m_part, jnp.max(sub8(s_buf[blk, :]), axis=0))
        m_old = m_ref[...]
        m_new = jnp.maximum(m_old, jnp.max(m_part, axis=0, keepdims=True))
        alpha = jnp.exp2(m_old - m_new)
        l_part = jnp.zeros((8, QW), F32)
        for blk in blocks:
            p = jnp.exp2(s_buf[blk, :] - m_new)
            l_part = l_part + jnp.sum(sub8(p), axis=0)
            p_buf[blk, :] = p.astype(BF16)
        l_ref[...] = alpha * l_ref[...] + jnp.sum(l_part, axis=0, keepdims=True)
        m_ref[...] = m_new
        al_buf[...] = alpha

    def far_values(i, p_buf, al_buf):
        pv = lax.dot_general(v_ref[tile_rows(i), :], p_buf[...], (((0,), (0,)), ((), ())),
                             preferred_element_type=F32)
        acc_ref[...] = acc_ref[...] * al_buf[...] + pv

    far_scores(0, s_a)
    far_scores(1, s_b)
    for p_buf, al_buf in ((p_a, al_a), (p_b, al_b)):
        p_buf[...] = jnp.zeros(p_buf.shape, BF16)
        al_buf[...] = jnp.ones(al_buf.shape, F32)

    blocks = [(jnp.maximum(c - 1, 0), 1, c * spb), (c, 0, None)]
    near_rows = [pl.ds(pl.multiple_of(kb * QB, QB), QB) for kb, _, _ in blocks]
    s = _scores(jnp.concatenate([k_ref[r, :] for r in near_rows], axis=0), qs)
    for t, (kb, delta, limit) in enumerate(blocks):
        msk = jnp.concatenate(
            [jnp.broadcast_to(head_row(kb * spb + u, limit), (SLC_BLOCK, QW)) for u in range(spb)],
            axis=0) > 0.5
        if delta == 0:
            msk = msk & (qq >= kk)
        sb = s[t * QB:(t + 1) * QB] + _rel_bias(tab_ref[0, delta], c31)
        s_n[t * QB:(t + 1) * QB, :] = jnp.where(msk, sb, NEG)

    def pair_body(j, carry):
        i = 2 * j
        far_values(jnp.maximum(i - 2, 0), p_a, al_a)
        far_values(jnp.maximum(i - 1, 0), p_b, al_b)
        far_softmax(s_a, p_a, al_a)
        far_softmax(s_b, p_b, al_b)
        far_scores(i + 2, s_a)
        far_scores(i + 3, s_b)
        return carry

    n_pairs = (n_tiles + 1) // 2
    lax.fori_loop(0, n_pairs, pair_body, 0)
    far_values(jnp.maximum(2 * n_pairs - 2, 0), p_a, al_a)
    far_values(jnp.maximum(2 * n_pairs - 1, 0), p_b, al_b)

    v_near = jnp.concatenate([v_ref[r, :] for r in near_rows], axis=0)
    _online_update(m_ref, l_ref, acc_ref, slice(None), s_n[...], v_near)

    o_slc = acc_ref[...] * (1.0 / l_ref[...])
    o_t = gate_ref[0:1, :] * ocmp_ref[...] + gate_ref[1:2, :] * o_slc + gate_ref[2:3, :] * owin_ref[...]
    _store_heads(o_ref, o_t)


def _sel_kernel(q_ref, k_ref, v_ref, tab_ref, c31_ref, sel_ref, gate_ref, ocmp_ref, owin_ref, o_ref,
                *scratch):
    _sel_block(pl.program_id(1), q_ref, lambda r: sel_ref[0, pl.ds(r, 1), :],
               gate_ref.at[0, 0], ocmp_ref.at[0, 0], owin_ref.at[0, 0], o_ref,
               k_ref, v_ref, tab_ref, c31_ref[0], *scratch)


def _sel_attention(proj, q_col, k_col, v_col, tabs, c31, sel_t, gates_s, ocmp_t, owin_t):
    S = proj.shape[0]
    G, NB = sel_t.shape[0], sel_t.shape[1]
    nkb = S // QB
    nq = 1
    tile = pl.BlockSpec((1, nq, HEAD_DIM, QW), lambda g, c: (g, c, 0, 0))
    return pl.pallas_call(
        _sel_kernel,
        grid=(G, nkb // nq),
        in_specs=_qkv_specs(S, q_col, k_col, v_col, nq * QB)
        + [pl.BlockSpec((1, 2, QB, QW), lambda g, c: (g, 0, 0, 0)),
           pl.BlockSpec((1, 1, QW), lambda g, c: (g, 0, 0)),
           pl.BlockSpec((1, NB, nq * QB), lambda g, c: (g, 0, c)),
           pl.BlockSpec((1, nq, N_BRANCH, QW), lambda g, c: (g, c, 0, 0)),
           tile, tile],
        out_specs=pl.BlockSpec((nq * QB, GROUP * HEAD_DIM), lambda g, c: (c, g)),
        out_shape=jax.ShapeDtypeStruct((S, G * GROUP * HEAD_DIM), F32),
        scratch_shapes=[pltpu.VMEM((1, QW), F32), pltpu.VMEM((1, QW), F32),
                        pltpu.VMEM((HEAD_DIM, QW), F32)]
        + [pltpu.VMEM((FAR_BLOCKS * QB, QW), F32)] * 2
        + [pltpu.VMEM((FAR_BLOCKS * QB, QW), BF16)] * 2
        + [pltpu.VMEM((1, QW), F32)] * 2
        + [pltpu.VMEM((2 * QB, QW), F32)],
        compiler_params=_cparams(("arbitrary", "arbitrary")),
        name="selected_attention_combine",
    )(proj, proj, proj, tabs, c31, sel_t, gates_s, ocmp_t, owin_t)


def _outproj_kernel(a_ref, w_ref, x_ref, o_ref, w_s):
    @pl.when(pl.program_id(1) == 0)
    def _():
        w_s[...] = w_ref[...].astype(BF16)

    o_ref[...] = x_ref[...] + jnp.dot(a_ref[...], w_s[...], preferred_element_type=F32)


def _outproj(a, w, x, tm=1024, tn=512):
    S, D = a.shape
    N = w.shape[1]
    tm = min(tm, S)
    return pl.pallas_call(
        _outproj_kernel,
        grid=(N // tn, S // tm),
        in_specs=[pl.BlockSpec((tm, D), lambda j, i: (i, 0)),
                  pl.BlockSpec((D, tn), lambda j, i: (0, j)),
                  pl.BlockSpec((tm, tn), lambda j, i: (i, j))],
        out_specs=pl.BlockSpec((tm, tn), lambda j, i: (i, j)),
        out_shape=jax.ShapeDtypeStruct((S, N), F32),
        scratch_shapes=[pltpu.VMEM((D, tn), BF16)],
        compiler_params=_cparams(("arbitrary", "arbitrary")),
        name="outproj_residual",
    )(a, w, x)


HALO = 16
FFN_ROW_CHUNKS = 2


def _ffn_a_kernel(halo_ref, a_ref, wg_ref, wu_ref, cw_ref, cb_ref, o_ref, g_s):
    i = pl.program_id(0)
    tm = a_ref.shape[0]
    halo = halo_ref[...]
    halo = jnp.where(i > 0, halo, jnp.zeros_like(halo))
    wg = wg_ref[...].astype(BF16)
    wu = wu_ref[...].astype(BF16)
    hm = tm // FFN_ROW_CHUNKS
    for h in range(FFN_ROW_CHUNKS):
        a = a_ref[h * hm:(h + 1) * hm, :]
        g = g_s.at[h]
        if h == 0:
            g[...] = jnp.dot(jnp.concatenate([halo, a], axis=0), wg, preferred_element_type=F32)
        else:
            g[0:HALO, :] = g_s[h - 1, hm:hm + HALO, :]
            g[HALO:, :] = jnp.dot(a, wg, preferred_element_type=F32)
        up = jnp.dot(a, wu, preferred_element_type=F32)
        gext = g[...]
        g1 = pltpu.roll(gext, 1, 0)
        g2 = pltpu.roll(g1, 1, 0)
        y = cb_ref[...] + g2[HALO:] * cw_ref[0:1, :]
        y = y + g1[HALO:] * cw_ref[1:2, :]
        y = y + gext[HALO:] * cw_ref[2:3, :]
        act = y * (1.0 / (1.0 + jnp.exp(-y)))
        o_ref[h * hm:(h + 1) * hm, :] = (act * up).astype(o_ref.dtype)


def _ffn_a(hf, wg, wu, cw, cb, tm=2048, tf=256):
    S, D = hf.shape
    Fp = wg.shape[1]
    assert Fp % tf == 0
    tm = min(tm, S)
    hb = tm // HALO
    return pl.pallas_call(
        _ffn_a_kernel,
        grid=(S // tm, Fp // tf),
        in_specs=[pl.BlockSpec((HALO, D), lambda i, f: (jnp.maximum(i * hb - 1, 0), 0)),
                  pl.BlockSpec((tm, D), lambda i, f: (i, 0), pipeline_mode=pl.Buffered(1)),
                  pl.BlockSpec((D, tf), lambda i, f: (0, f)),
                  pl.BlockSpec((D, tf), lambda i, f: (0, f)),
                  pl.BlockSpec((cw.shape[0], tf), lambda i, f: (0, f)),
                  pl.BlockSpec((1, tf), lambda i, f: (0, f))],
        out_specs=pl.BlockSpec((tm, tf), lambda i, f: (i, f)),
        out_shape=jax.ShapeDtypeStruct((S, Fp), BF16),
        scratch_shapes=[pltpu.VMEM((FFN_ROW_CHUNKS, HALO + tm // FFN_ROW_CHUNKS, tf), F32)],
        compiler_params=_cparams(("arbitrary", "arbitrary")),
        name="ffn_gate_up",
    )(hf, hf, wg, wu, cw, cb)


def _ffn_b_kernel(a_ref, w_ref, x_ref, o_ref):
    o_ref[...] = x_ref[...] + jnp.dot(a_ref[...], w_ref[...], preferred_element_type=F32)


def _ffn_b(h, w, x, tm=512, tn=512):
    S, F = h.shape
    N = w.shape[1]
    tm = min(tm, S)
    return pl.pallas_call(
        _ffn_b_kernel,
        grid=(S // tm, N // tn),
        in_specs=[pl.BlockSpec((tm, F), lambda i, j: (i, 0)),
                  pl.BlockSpec((F, tn), lambda i, j: (0, j)),
                  pl.BlockSpec((tm, tn), lambda i, j: (i, j))],
        out_specs=pl.BlockSpec((tm, tn), lambda i, j: (i, j)),
        out_shape=jax.ShapeDtypeStruct((S, N), F32),
        compiler_params=_cparams(("arbitrary", "arbitrary")),
        name="ffn_down_residual",
    )(h, w, x)


def _layer(x, rel_bias, norm_mix_g, w_in, a_q_norm_g, a_k_norm_g, a_sinks, b_q_norm_g, b_k_norm_g,
           cmp_pos_emb, cmp_w1, cmp_b1, cmp_w2, cmp_b2, out_norm_g, w_out, norm_ffn_g, w_gate,
           w_up, conv_w, conv_b, w_down):
    S, D = x.shape
    aw = A_Q_HEADS * HEAD_DIM
    akv = A_KV_HEADS * HEAD_DIM
    bw = B_Q_HEADS * HEAD_DIM
    bkv = B_KV_HEADS * HEAD_DIM
    sizes = [aw, akv, akv, bw] + [bkv] * 6 + [N_BRANCH * B_Q_HEADS]
    offs = np.concatenate([[0], np.cumsum(sizes)]).tolist()
    wt = jnp.swapaxes(w_in, 0, 1)
    tn = 512
    order_a = [0, 1, 2, 3, 6, 7, 8, 9]
    n_gate = sizes[10]
    wg_t = jnp.pad(wt[offs[10]:offs[10] + n_gate], ((0, QB - n_gate), (0, 0)))
    one = jnp.ones((HEAD_DIM,), F32)
    seg_gain = {0: a_q_norm_g, 1: a_k_norm_g * KEY_SCALE, 3: b_q_norm_g,
                6: b_k_norm_g[1] * KEY_SCALE, 8: b_k_norm_g[2] * KEY_SCALE}
    kinds, tiles, gains, col = [], [], [], {}
    c0 = 0
    for k in order_a:
        col[k] = c0
        for t in range(sizes[k] // tn):
            kinds.append(1 if k in seg_gain else 0)
            tiles.append(offs[k] // tn + t)
            gains.append(seg_gain.get(k, one))
        c0 += sizes[k]
    kinds = jnp.asarray(kinds, jnp.int32)
    tiles = jnp.asarray(tiles, jnp.int32)
    gains = jnp.stack(gains).astype(F32).reshape(len(gains), 1, HEAD_DIM)

    hn = _rmsnorm([x], norm_mix_g)
    proj = _inproj_a(hn, wt, kinds, tiles, gains, tn=tn)
    kv32, gates = _inproj_b(hn, wt, offs[4] // tn, 2 * bkv // tn, wg_t)

    tab_a = rel_bias[:, :A_Q_HEADS]
    tab_b = rel_bias[:, A_Q_HEADS:]
    near_a = _stack_heads(_near_bias_tables(tab_a), A_KV_HEADS)
    near_b = _stack_heads(_near_bias_tables(tab_b), B_KV_HEADS)
    head_row = lambda v, G: _stack_heads(jnp.broadcast_to(v.astype(F32)[:, None, None],
                                                          (v.shape[0], 1, QB)), G)
    c31_a = head_row(tab_a[N_BUCKETS - 1], A_KV_HEADS)
    c31_b = head_row(tab_b[N_BUCKETS - 1], B_KV_HEADS)

    qcol = lambda k: col[k] // (GROUP * HEAD_DIM)
    hcol = lambda k: col[k] // HEAD_DIM
    o_a = _band_attention(proj, A_KV_HEADS, qcol(0), hcol(1), hcol(2), near_a, c31_a,
                          head_row(a_sinks, A_KV_HEADS), A_WINDOW, transposed_out=False)

    NC = S // CMP_STRIDE
    n_cmp = (S - CMP_BLOCK) // CMP_STRIDE + 1
    w1 = cmp_w1.reshape(2, CMP_BLOCK, HEAD_DIM, cmp_w1.shape[-1]).astype(BF16)
    kc, vct = _compress(kv32, B_KV_HEADS, cmp_pos_emb.astype(F32), w1, cmp_b1[:, None, :].astype(F32),
                        cmp_w2.astype(BF16), cmp_b2[:, None, :].astype(F32),
                        (b_k_norm_g[0] * KEY_SCALE).reshape(1, HEAD_DIM).astype(F32))

    NB = S // SLC_BLOCK
    ii = np.arange(NC)[None, :]
    jj = np.arange(NB)[:, None]
    ovl_t = ((ii * CMP_STRIDE <= jj * SLC_BLOCK + SLC_BLOCK - 1)
             & (ii * CMP_STRIDE + CMP_BLOCK - 1 >= jj * SLC_BLOCK) & (ii < n_cmp))
    ovl_t = jnp.asarray(ovl_t.astype(np.float32), BF16)
    ocmp_t, sel_t = _cmp_select(proj, qcol(3), kc, vct,
                                _stack_heads(_cmp_bias_table(tab_b), B_KV_HEADS), c31_b, ovl_t, n_cmp)

    owin_t = _band_attention(proj, B_KV_HEADS, qcol(3), hcol(8), hcol(9), near_b, c31_b, None,
                             B_WINDOW, transposed_out=True)
    gates_s = gates[:, :n_gate].reshape(S // QB, QB, B_KV_HEADS, GROUP, N_BRANCH)
    gates_s = gates_s.transpose(2, 0, 4, 3, 1).reshape(B_KV_HEADS, S // QB, N_BRANCH, QW)
    o_b = _sel_attention(proj, qcol(3), hcol(6), hcol(7), near_b, c31_b, sel_t, gates_s,
                         ocmp_t, owin_t)

    on = _rmsnorm([o_a, o_b], out_norm_g)
    x2 = _outproj(on, w_out, x)

    hf = _rmsnorm([x2], norm_ffn_g)
    hmid = _ffn_a(hf, w_gate, w_up, conv_w.astype(F32), conv_b[None, :].astype(F32))
    return _ffn_b(hmid, w_down.astype(BF16), x2)


def kernel(x, rel_bias, norm_mix_g, w_in, a_q_norm_g, a_k_norm_g, a_sinks, b_q_norm_g, b_k_norm_g,
           cmp_pos_emb, cmp_w1, cmp_b1, cmp_w2, cmp_b2, out_norm_g, w_out, norm_ffn_g, w_gate, w_up,
           conv_w, conv_b, w_down):
    depth = w_in.shape[0]
    batch = x.shape[0]
    outs = []
    for b in range(batch):
        h = x[b]
        for l in range(depth):
            h = _layer(h, rel_bias, norm_mix_g[l], w_in[l], a_q_norm_g[l], a_k_norm_g[l], a_sinks[l],
                       b_q_norm_g[l], b_k_norm_g[l], cmp_pos_emb[l], cmp_w1[l], cmp_b1[l], cmp_w2[l],
                       cmp_b2[l], out_norm_g[l], w_out[l], norm_ffn_g[l], w_gate[l], w_up[l],
                       conv_w[l], conv_b[l], w_down[l])
        outs.append(h)
    return jnp.stack(outs)
```
